```python
import math
import jax
import jax.numpy as jnp
from jax import lax
import numpy as np

D_MODEL = 1024
BATCH = 16
SEQ = 2048
DEPTH = 4
DEC_BATCH = 4
DEC_SEQ = 8192
PAST_LEN = 128

N_MIXERS = 3
N_LAYERS_A = (DEPTH + 2) // 3
N_LAYERS_B = (DEPTH + 1) // 3
N_LAYERS_C = DEPTH // 3

A_GROUPS = ((128, 1), (512, 4), (2048, 16))
A_N_GROUPS = len(A_GROUPS)
A_HEADS = 16
A_HEAD_DIM = D_MODEL // A_HEADS
A_WIDTH = A_HEADS * A_HEAD_DIM
ROPE_THETA = 10000.0

HY_WIDTH = D_MODEL
HY_SHORT = 3
HY_EMB = 33
HY_BANDS = (HY_EMB - 1) // 2
HY_FILTER_HIDDEN = 64
HY_DECAY_TARGET = 1e-2
HY_DECAY_STRONG_PCT = 0.3
HY_DECAY_WEAK_PCT = 1.5

C_CHUNK = 128
C_WIDTH = D_MODEL
C_GROUPS = 8
C_GROUP_DIM = C_WIDTH // C_GROUPS

MEM_LEN = 256
X_HEADS = 4
X_HEAD_DIM = D_MODEL // X_HEADS

D_FF = -(-8 * D_MODEL // (3 * 256)) * 256

EPS = 1e-6
NEG_INF = -1e30

kernel_name = 'hybrid_dilated_hyena_sgu_encoder'


def rms_norm(x, g):
    xf = x.astype(jnp.float32)
    y = xf * lax.rsqrt(jnp.mean(xf * xf, axis=-1, keepdims=True) + EPS)
    return (y * g.astype(jnp.float32)).astype(x.dtype)


def layer_norm(x, g, b):
    xf = x.astype(jnp.float32)
    mu = jnp.mean(xf, axis=-1, keepdims=True)
    xc = xf - mu
    y = xc * lax.rsqrt(jnp.mean(xc * xc, axis=-1, keepdims=True) + EPS)
    return (y * g.astype(jnp.float32) + b.astype(jnp.float32)).astype(x.dtype)


def rope_tables(seq_len, dim):
    inv = ROPE_THETA ** (-jnp.arange(0, dim, 2, dtype=jnp.float32) / dim)
    ang = jnp.arange(seq_len, dtype=jnp.float32)[:, None] * inv[None, :]
    return jnp.cos(ang), jnp.sin(ang)


def apply_rope(x, cos, sin):
    xf = x.astype(jnp.float32)
    half = xf.shape[-1] // 2
    x1, x2 = xf[..., :half], xf[..., half:]
    c, s = cos[None, :, None, :], sin[None, :, None, :]
    return jnp.concatenate([x1 * c - x2 * s, x2 * c + x1 * s], axis=-1).astype(x.dtype)


def dilated_window_attention(q, k, v, window, dilation):
    b, s, h, dh = q.shape
    half = window // (2 * dilation)
    blk = half
    n_sub = s // dilation
    n_blk = -(-n_sub // blk)
    pad = n_blk * blk - n_sub
    n = b * dilation

    def to_sub(t):
        t = t.reshape(b, n_sub, dilation, h, dh).transpose(0, 2, 1, 3, 4)
        return t.reshape(n, n_sub, h, dh)

    def band(t):
        tp = jnp.pad(t, ((0, 0), (blk, blk + pad), (0, 0), (0, 0))).reshape(n, n_blk + 2, blk, h, dh)
        return jnp.concatenate([tp[:, :-2], tp[:, 1:-1], tp[:, 2:]], axis=2)

    qb = jnp.pad(to_sub(q), ((0, 0), (0, pad), (0, 0), (0, 0))).reshape(n, n_blk, blk, h, dh)
    kb = band(to_sub(k))
    vb = band(to_sub(v))
    scores = jnp.einsum('nbqhd,nbkhd->nbhqk', qb, kb).astype(jnp.float32) * (dh ** -0.5)
    qi = jnp.arange(n_blk)[:, None, None] * blk + jnp.arange(blk)[None, :, None]
    kj = jnp.arange(n_blk)[:, None, None] * blk - blk + jnp.arange(3 * blk)[None, None, :]
    valid = (jnp.abs(qi - kj) <= half) & (kj >= 0) & (kj < n_sub)
    scores = jnp.where(valid[None, :, None], scores, NEG_INF)
    lse = jax.nn.logsumexp(scores, axis=-1)
    p = jnp.exp(scores - lse[..., None])
    out = jnp.einsum('nbhqk,nbkhd->nbqhd', p, vb.astype(jnp.float32))
    out = out.reshape(n, n_blk * blk, h, dh)[:, :n_sub]
    lse = lse.transpose(0, 1, 3, 2).reshape(n, n_blk * blk, h)[:, :n_sub]

    def from_sub(t):
        t = t.reshape((b, dilation, n_sub) + t.shape[2:])
        t = jnp.swapaxes(t, 1, 2)
        return t.reshape((b, s) + t.shape[3:])

    return from_sub(out), from_sub(lse)


def mixer_dilated(xn, w_in, w_out):
    b, s, _ = xn.shape
    qkv = (xn @ w_in).reshape(b, s, A_N_GROUPS, 3, A_HEADS, A_HEAD_DIM)
    cos, sin = rope_tables(s, A_HEAD_DIM)
    outs, lses = [], []
    for g, (window, dilation) in enumerate(A_GROUPS):
        q = apply_rope(qkv[:, :, g, 0], cos, sin)
        k = apply_rope(qkv[:, :, g, 1], cos, sin)
        o, l = dilated_window_attention(q, k, qkv[:, :, g, 2], window, dilation)
        outs.append(o)
        lses.append(l)
    wts = jax.nn.softmax(jnp.stack(lses), axis=0)
    o = jnp.einsum('gbsh,gbshd->bshd', wts, jnp.stack(outs))
    return o.reshape(b, s, A_WIDTH).astype(xn.dtype) @ w_out


def short_conv(u, w, bias):
    up = jnp.pad(u, ((0, 0), (1, 1), (0, 0)))
    return up[:, :-2] * w[0] + up[:, 1:-1] * w[1] + up[:, 2:] * w[2] + bias


def hyena_position_features(seq_len):
    t = jnp.linspace(0.0, 1.0, seq_len, dtype=jnp.float32)[:, None]
    w = 2.0 * math.pi * jnp.arange(seq_len, dtype=jnp.float32)[:, None] / seq_len
    f = jnp.linspace(1e-4, HY_BANDS - 1, HY_BANDS, dtype=jnp.float32)[None, :]
    return jnp.concatenate([t, jnp.cos(f * w), -jnp.sin(f * w)], axis=-1)


def hyena_decay(seq_len):
    t = jnp.linspace(0.0, 1.0, seq_len, dtype=jnp.float32)[:, None]
    max_decay = math.log(HY_DECAY_TARGET) / HY_DECAY_STRONG_PCT
    min_decay = math.log(HY_DECAY_TARGET) / HY_DECAY_WEAK_PCT
    deltas = jnp.linspace(min_decay, max_decay, HY_WIDTH, dtype=jnp.float32)
    return jnp.exp(-t * jnp.abs(deltas)[None, :])


def hyena_filters(seq_len, f_w1, f_b1, f_w2, f_b2, f_w3, f_b3, f_wout, f_freq):
    f32 = jnp.float32
    freq = f_freq.astype(f32)
    z = hyena_position_features(seq_len)
    hid = jnp.sin(freq * (z @ f_w1.astype(f32) + f_b1.astype(f32)))
    hid = jnp.sin(freq * (hid @ f_w2.astype(f32) + f_b2.astype(f32)))
    hid = jnp.sin(freq * (hid @ f_w3.astype(f32) + f_b3.astype(f32)))
    h = (hid @ f_wout.astype(f32)).reshape(seq_len, 2, HY_WIDTH) * hyena_decay(seq_len)[:, None, :]
    h_fwd, h_bwd = h[:, 0], h[:, 1]
    h_full = jnp.concatenate(
        [h_fwd[:1] + h_bwd[:1], h_fwd[1:], jnp.zeros((1, HY_WIDTH), f32), h_bwd[1:][::-1]], axis=0)
    return h_full / jnp.sum(jnp.abs(h_full), axis=0, keepdims=True)


def long_conv(v, h_full):
    seq_len = v.shape[1]
    vf = jnp.fft.rfft(v.astype(jnp.float32), n=2 * seq_len, axis=1)
    hf = jnp.fft.rfft(h_full, n=2 * seq_len, axis=0)
    return jnp.fft.irfft(vf * hf[None], n=2 * seq_len, axis=1)[:, :seq_len]


def mixer_hyena(xn, w_in, conv_w, conv_b, f_w1, f_b1, f_w2, f_b2, f_w3, f_b3, f_wout, f_freq, bias_d, w_out):
    b, s, _ = xn.shape
    u = short_conv(xn @ w_in, conv_w, conv_b)
    x0, x1, v = jnp.split(u, 3, axis=-1)
    h_full = hyena_filters(s, f_w1, f_b1, f_w2, f_b2, f_w3, f_b3, f_wout, f_freq)
    v = v * x1
    v = (long_conv(v, h_full) + bias_d.astype(jnp.float32) * v.astype(jnp.float32)).astype(xn.dtype)
    return (v * x0) @ w_out


def mixer_sgu(xn, w_in, ln_g, ln_b, w_s, b_s, w_out):
    b, s, _ = xn.shape
    z = jax.nn.gelu(xn @ w_in, approximate=False)
    zu, zv = jnp.split(z, 2, axis=-1)
    zv = layer_norm(zv, ln_g, ln_b).reshape(b, s // C_CHUNK, C_CHUNK, C_GROUPS, C_GROUP_DIM)
    sv = jnp.einsum('hpq,bnqhc->bnphc', w_s, zv) + b_s.T[:, :, None]
    return (zu * sv.reshape(b, s, C_WIDTH)) @ w_out


def cross_attention(hn, mem, w_q, w_kv, w_o):
    b, s, _ = hn.shape
    q = (hn @ w_q).reshape(b, s, X_HEADS, X_HEAD_DIM)
    kv = (mem @ w_kv).reshape(b, mem.shape[1], 2, X_HEADS, X_HEAD_DIM)
    sc = jnp.einsum('bshd,bmhd->bhsm', q, kv[:, :, 0]).astype(jnp.float32) * (X_HEAD_DIM ** -0.5)
    p = jax.nn.softmax(sc, axis=-1).astype(hn.dtype)
    o = jnp.einsum('bhsm,bmhd->bshd', p, kv[:, :, 1]).reshape(b, s, X_HEADS * X_HEAD_DIM)
    return o @ w_o


def swiglu(hn, w_gu, w_down):
    gate, up = jnp.split(hn @ w_gu, 2, axis=-1)
    return (jax.nn.silu(gate) * up) @ w_down


def trunk(x, mem, g_mix, g_cross, g_ffn, g_final, a_w_in, a_w_out, b_w_in, b_conv_w, b_conv_b,
          b_f_w1, b_f_b1, b_f_w2, b_f_b2, b_f_w3, b_f_b3, b_f_wout, b_f_freq, b_bias_d, b_w_out,
          c_w_in, c_ln_g, c_ln_b, c_w_s, c_b_s, c_w_out, x_w_q, x_w_kv, x_w_o, f_w_gu, f_w_down):
    for i in range(DEPTH):
        kind, j = i % N_MIXERS, i // N_MIXERS
        xn = rms_norm(x, g_mix[i])
        if kind == 0:
            m = mixer_dilated(xn, a_w_in[j], a_w_out[j])
        elif kind == 1:
            m = mixer_hyena(xn, b_w_in[j], b_conv_w[j], b_conv_b[j], b_f_w1[j], b_f_b1[j], b_f_w2[j],
                            b_f_b2[j], b_f_w3[j], b_f_b3[j], b_f_wout[j], b_f_freq[j], b_bias_d[j], b_w_out[j])
        else:
            m = mixer_sgu(xn, c_w_in[j], c_ln_g[j], c_ln_b[j], c_w_s[j], c_b_s[j], c_w_out[j])
        x = x + m
        x = x + cross_attention(rms_norm(x, g_cross[i]), mem, x_w_q[i], x_w_kv[i], x_w_o[i])
        x = x + swiglu(rms_norm(x, g_ffn[i]), f_w_gu[i], f_w_down[i])
    return rms_norm(x, g_final)


def setup_inputs(seed: int = 0) -> dict:
    key = jax.random.key(seed)
    ks = iter(jax.random.split(key, 40))
    d = D_MODEL
    nA, nB, nC = N_LAYERS_A, N_LAYERS_B, N_LAYERS_C

    def nrm(shape, scale):
        return jax.random.normal(next(ks), shape, jnp.float32) * scale

    def gain(shape):
        return 1.0 + nrm(shape, 0.1)

    return {
        'x_prompt': nrm((BATCH, SEQ, d), 1.0),
        'x_sample': nrm((DEC_BATCH, DEC_SEQ, d), 1.0),
        'mem_prompt': nrm((BATCH, MEM_LEN, d), 1.0),
        'mem_sample': nrm((DEC_BATCH, MEM_LEN, d), 1.0),
        'g_mix': gain((DEPTH, d)),
        'g_cross': gain((DEPTH, d)),
        'g_ffn': gain((DEPTH, d)),
        'g_final': gain((d,)),
        'a_w_in': nrm((nA, d, A_N_GROUPS * 3 * A_WIDTH), d ** -0.5),
        'a_w_out': nrm((nA, A_WIDTH, d), A_WIDTH ** -0.5),
        'b_w_in': nrm((nB, d, 3 * HY_WIDTH), d ** -0.5),
        'b_conv_w': nrm((nB, HY_SHORT, 3 * HY_WIDTH), HY_SHORT ** -0.5),
        'b_conv_b': nrm((nB, 3 * HY_WIDTH), 0.02),
        'b_f_w1': nrm((nB, HY_EMB, HY_FILTER_HIDDEN), HY_EMB ** -0.5),
        'b_f_b1': nrm((nB, HY_FILTER_HIDDEN), 0.02),
        'b_f_w2': nrm((nB, HY_FILTER_HIDDEN, HY_FILTER_HIDDEN), HY_FILTER_HIDDEN ** -0.5),
        'b_f_b2': nrm((nB, HY_FILTER_HIDDEN), 0.02),
        'b_f_w3': nrm((nB, HY_FILTER_HIDDEN, HY_FILTER_HIDDEN), HY_FILTER_HIDDEN ** -0.5),
        'b_f_b3': nrm((nB, HY_FILTER_HIDDEN), 0.02),
        'b_f_wout': nrm((nB, HY_FILTER_HIDDEN, 2 * HY_WIDTH), HY_FILTER_HIDDEN ** -0.5),
        'b_f_freq': gain((nB, HY_FILTER_HIDDEN)),
        'b_bias_d': nrm((nB, HY_WIDTH), 0.5),
        'b_w_out': nrm((nB, HY_WIDTH, d), HY_WIDTH ** -0.5),
        'c_w_in': nrm((nC, d, 2 * C_WIDTH), d ** -0.5),
        'c_ln_g': gain((nC, C_WIDTH)),
        'c_ln_b': nrm((nC, C_WIDTH), 0.02),
        'c_w_s': nrm((nC, C_GROUPS, C_CHUNK, C_CHUNK), C_CHUNK ** -0.5),
        'c_b_s': gain((nC, C_GROUPS, C_CHUNK)),
        'c_w_out': nrm((nC, C_WIDTH, d), C_WIDTH ** -0.5),
        'x_w_q': nrm((DEPTH, d, X_HEADS * X_HEAD_DIM), d ** -0.5),
        'x_w_kv': nrm((DEPTH, d, 2 * X_HEADS * X_HEAD_DIM), d ** -0.5),
        'x_w_o': nrm((DEPTH, X_HEADS * X_HEAD_DIM, d), (X_HEADS * X_HEAD_DIM) ** -0.5),
        'f_w_gu': nrm((DEPTH, d, 2 * D_FF), d ** -0.5),
        'f_w_down': nrm((DEPTH, D_FF, d), D_FF ** -0.5),
    }


def reference(x_prompt, x_sample, mem_prompt, mem_sample, g_mix, g_cross, g_ffn, g_final,
              a_w_in, a_w_out, b_w_in, b_conv_w, b_conv_b, b_f_w1, b_f_b1, b_f_w2, b_f_b2,
              b_f_w3, b_f_b3, b_f_wout, b_f_freq, b_bias_d, b_w_out, c_w_in, c_ln_g, c_ln_b,
              c_w_s, c_b_s, c_w_out, x_w_q, x_w_kv, x_w_o, f_w_gu, f_w_down):
    weights = dict(g_mix=g_mix, g_cross=g_cross, g_ffn=g_ffn, g_final=g_final,
                   a_w_in=a_w_in, a_w_out=a_w_out, b_w_in=b_w_in, b_conv_w=b_conv_w, b_conv_b=b_conv_b,
                   b_f_w1=b_f_w1, b_f_b1=b_f_b1, b_f_w2=b_f_w2, b_f_b2=b_f_b2, b_f_w3=b_f_w3,
                   b_f_b3=b_f_b3, b_f_wout=b_f_wout, b_f_freq=b_f_freq, b_bias_d=b_bias_d,
                   b_w_out=b_w_out, c_w_in=c_w_in, c_ln_g=c_ln_g, c_ln_b=c_ln_b, c_w_s=c_w_s,
                   c_b_s=c_b_s, c_w_out=c_w_out, x_w_q=x_w_q, x_w_kv=x_w_kv, x_w_o=x_w_o,
                   f_w_gu=f_w_gu, f_w_down=f_w_down)
    y_prompt = trunk(x_prompt, mem_prompt, **weights)
    y_sample = trunk(x_sample, mem_sample, **weights)
    return (y_prompt, y_sample)
```

```python
import functools
import math

import numpy as np
import jax
import jax.numpy as jnp
from jax import lax
from jax.experimental import pallas as pl
from jax.experimental.pallas import tpu as pltpu

F32 = jnp.float32
BF16 = jnp.bfloat16

D_MODEL = 1024
DEPTH = 4
N_MIXERS = 3

A_GROUPS = ((128, 1), (512, 4), (2048, 16))
A_N_GROUPS = len(A_GROUPS)
A_HEADS = 16
A_HEAD_DIM = D_MODEL // A_HEADS
A_HALF = 64
ROPE_THETA = 10000.0

HY_EMB = 33
HY_BANDS = (HY_EMB - 1) // 2
HY_HID_PAD = 128
HY_DECAY_TARGET = 1e-2
HY_DECAY_STRONG_PCT = 0.3
HY_DECAY_WEAK_PCT = 1.5

C_CHUNK = 128
C_GROUPS = 8

MEM_LEN = 256
X_HEADS = 4
X_HEAD_DIM = D_MODEL // X_HEADS

D_FF = -(-8 * D_MODEL // (3 * 256)) * 256
FF_CHUNK = 256

EPS = 1e-6
NEG_INF = -1e30

LANES = 128
VMEM_LIMIT = 56 * 1024 * 1024


def _params(*sem):
    return pltpu.CompilerParams(dimension_semantics=sem, vmem_limit_bytes=VMEM_LIMIT)


def _resident(shape):
    nd = len(shape)
    return pl.BlockSpec(shape, lambda *_: (0,) * nd, pipeline_mode=pl.Buffered(1))


def _dot(a, b):
    return jnp.dot(a, b, preferred_element_type=F32)


def _dot_nt(a, b):
    return lax.dot_general(a, b, (((1,), (1,)), ((), ())), preferred_element_type=F32)


def _rms(x, g):
    return x * lax.rsqrt(jnp.mean(x * x, axis=-1, keepdims=True) + EPS) * g


def _ffn_body(*refs, final):
    if final:
        x_ref, g_ref, wgu_ref, wd_ref, gf_ref, o_ref, act_ref = refs
    else:
        x_ref, g_ref, wgu_ref, wd_ref, o_ref, act_ref = refs
    x = x_ref[...]
    xn = _rms(x, g_ref[...]).astype(BF16)
    for c in range(D_FF // FF_CHUNK):
        lo = c * FF_CHUNK
        gate = _dot(xn, wgu_ref[:, lo:lo + FF_CHUNK])
        up = _dot(xn, wgu_ref[:, D_FF + lo:D_FF + lo + FF_CHUNK])
        act_ref[:, lo:lo + FF_CHUNK] = (gate / (1.0 + jnp.exp(-gate)) * up).astype(BF16)
    y = x + _dot(act_ref[...], wd_ref[...])
    if final:
        y = _rms(y, gf_ref[...])
    o_ref[...] = y


def _ffn(x2, g, wgu, wd, g_final=None):
    t = x2.shape[0]
    tm = 512
    final = g_final is not None
    tok = pl.BlockSpec((tm, D_MODEL), lambda i: (i, 0))
    row = pl.BlockSpec((1, D_MODEL), lambda i: (0, 0))
    in_specs = [tok, row, _resident(wgu.shape), _resident(wd.shape)]
    args = [x2, g, wgu, wd]
    if final:
        in_specs.append(row)
        args.append(g_final)
    return pl.pallas_call(
        functools.partial(_ffn_body, final=final),
        grid=(t // tm,),
        in_specs=in_specs,
        out_specs=tok,
        out_shape=jax.ShapeDtypeStruct((t, D_MODEL), F32),
        scratch_shapes=[pltpu.VMEM((tm, D_FF), BF16)],
        compiler_params=_params("parallel"),
        name="ffn_final" if final else "ffn",
    )(*args)


def _kv_body(m_ref, w_ref, o_ref):
    o_ref[...] = _dot(m_ref[...].astype(BF16), w_ref[...]).astype(BF16)


def _kv_proj(mem2, wkv):
    r = mem2.shape[0]
    tm = 256
    return pl.pallas_call(
        _kv_body,
        grid=(r // tm,),
        in_specs=[pl.BlockSpec((tm, D_MODEL), lambda i: (i, 0)), _resident(wkv.shape)],
        out_specs=pl.BlockSpec((tm, 2 * D_MODEL), lambda i: (i, 0)),
        out_shape=jax.ShapeDtypeStruct((r, 2 * D_MODEL), BF16),
        compiler_params=_params("parallel"),
        name="kv_proj",
    )(mem2, wkv)


def _xattn_body(x_ref, g_ref, wq_ref, k_ref, v_ref, wo_ref, o_ref, y_ref):
    x = x_ref[...]
    xn = _rms(x, g_ref[...]).astype(BF16)
    q = (_dot(xn, wq_ref[...]) * (X_HEAD_DIM ** -0.5)).astype(BF16)
    for h in range(X_HEADS):
        sl = slice(h * X_HEAD_DIM, (h + 1) * X_HEAD_DIM)
        s = _dot_nt(q[:, sl], k_ref[:, sl])
        p = jnp.exp(s - jnp.max(s, axis=-1, keepdims=True))
        inv = 1.0 / jnp.sum(p, axis=-1, keepdims=True)
        y_ref[:, sl] = (_dot(p.astype(BF16), v_ref[:, sl]) * inv).astype(BF16)
    o_ref[...] = x + _dot(y_ref[...], wo_ref[...])


def _xattn(x, g, wq, kv, boff, wo):
    b, s, _ = x.shape
    tm = 512
    tok = pl.BlockSpec((None, tm, D_MODEL), lambda bi, i: (bi, i, 0))
    return pl.pallas_call(
        _xattn_body,
        grid=(b, s // tm),
        in_specs=[
            tok,
            pl.BlockSpec((1, D_MODEL), lambda bi, i: (0, 0)),
            _resident(wq.shape),
            pl.BlockSpec((None, MEM_LEN, D_MODEL), lambda bi, i: (bi + boff, 0, 0)),
            pl.BlockSpec((None, MEM_LEN, D_MODEL), lambda bi, i: (bi + boff, 0, 1)),
            _resident(wo.shape),
        ],
        out_specs=tok,
        out_shape=jax.ShapeDtypeStruct(x.shape, F32),
        scratch_shapes=[pltpu.VMEM((tm, D_MODEL), BF16)],
        compiler_params=_params("parallel", "parallel"),
        name="xattn",
    )(x, g, wq, kv, kv, wo)


def _a_in_body(x_ref, g_ref, w_ref, cos_ref, sin_ref, o_ref):
    tm = x_ref.shape[0]
    xn = _rms(x_ref[...], g_ref[...]).astype(BF16)
    reps = D_MODEL // LANES
    cos_t = jnp.concatenate([cos_ref[...]] * reps, axis=1)
    sin_t = jnp.concatenate([sin_ref[...]] * reps, axis=1)
    lane = lax.broadcasted_iota(jnp.int32, (tm, D_MODEL), 1)
    low_half = (lane & (A_HEAD_DIM // 2)) == 0
    for c in range(3 * A_N_GROUPS):
        lo = c * D_MODEL
        y = _dot(xn, w_ref[:, lo:lo + D_MODEL])
        part = c % 3
        if part < 2:
            if part == 0:
                y = y * (A_HEAD_DIM ** -0.5)
            partner = jnp.where(low_half,
                                pltpu.roll(y, D_MODEL - A_HEAD_DIM // 2, 1),
                                pltpu.roll(y, A_HEAD_DIM // 2, 1))
            y = y * cos_t + partner * sin_t
        o_ref[:, lo:lo + D_MODEL] = y.astype(BF16)


def _a_in(x, g, w, cos_t, sin_t):
    b, s, _ = x.shape
    tm = 256
    n_out = w.shape[1]
    return pl.pallas_call(
        _a_in_body,
        grid=(b, s // tm),
        in_specs=[
            pl.BlockSpec((None, tm, D_MODEL), lambda bi, i: (bi, i, 0)),
            pl.BlockSpec((1, D_MODEL), lambda bi, i: (0, 0)),
            _resident(w.shape),
            pl.BlockSpec((tm, LANES), lambda bi, i: (i, 0)),
            pl.BlockSpec((tm, LANES), lambda bi, i: (i, 0)),
        ],
        out_specs=pl.BlockSpec((None, tm, n_out), lambda bi, i: (bi, i, 0)),
        out_shape=jax.ShapeDtypeStruct((b, s, n_out), BF16),
        compiler_params=_params("parallel", "parallel"),
        name="a_in",
    )(x, g, w, cos_t, sin_t)


def _attn_body(q_ref, kp_ref, k_ref, kn_ref, vp_ref, v_ref, vn_ref, o_ref, lse_ref, *, n_tiles):
    tq = q_ref.shape[0]
    win = tq + 2 * A_HALF
    i = pl.program_id(2)
    kw = jnp.concatenate([kp_ref[...], k_ref[...], kn_ref[...]], axis=0)
    vw = jnp.concatenate([vp_ref[...], v_ref[...], vn_ref[...]], axis=0)
    qi = lax.broadcasted_iota(jnp.int32, (tq, win), 0)
    kj = lax.broadcasted_iota(jnp.int32, (tq, win), 1)
    first = jnp.where(i == 0, A_HALF, 0)
    last = jnp.where(i == n_tiles - 1, tq + A_HALF, win)
    mask = (kj >= qi) & (kj <= qi + 2 * A_HALF) & (kj >= first) & (kj < last)
    lane = lax.broadcasted_iota(jnp.int32, (tq, LANES), 1)
    left = lane < A_HEAD_DIM
    lse_all = jnp.zeros((tq, LANES), F32)
    for hp in range(A_HEADS // 2):
        sl = slice(hp * LANES, (hp + 1) * LANES)
        q2 = q_ref[:, sl]
        k2 = kw[:, sl]
        v2 = vw[:, sl]
        halves = []
        for side in range(2):
            keep = left if side == 0 else jnp.logical_not(left)
            s = _dot_nt(jnp.where(keep, q2, jnp.zeros_like(q2)), k2)
            s = jnp.where(mask, s, NEG_INF)
            m = jnp.max(s, axis=-1, keepdims=True)
            p = jnp.exp(s - m)
            l = jnp.sum(p, axis=-1, keepdims=True)
            halves.append(_dot(p.astype(BF16), v2) * (1.0 / l))
            lse_all = jnp.where(lane == 2 * hp + side, m + jnp.log(l), lse_all)
        o_ref[:, sl] = jnp.where(left, halves[0], halves[1]).astype(BF16)
    lse_ref[...] = lse_all


def _attn_group(qkv, gidx, dil):
    b, s, ncol = qkv.shape
    n_sub = s // dil
    tq = 128
    n_tiles = n_sub // tq
    halo_per_tile = tq // A_HALF
    n_halo = n_sub // A_HALF
    blocks_per_row = ncol // D_MODEL
    view = qkv.reshape(b, n_sub, dil * ncol)

    def col(part):
        return lambda r: r * blocks_per_row + gidx * 3 + part

    def own(part):
        c = col(part)
        return pl.BlockSpec((None, tq, D_MODEL), lambda bi, r, i: (bi, i, c(r)))

    def prev(part):
        c = col(part)
        return pl.BlockSpec((None, A_HALF, D_MODEL),
                            lambda bi, r, i: (bi, jnp.maximum(i * halo_per_tile - 1, 0), c(r)))

    def nxt(part):
        c = col(part)
        return pl.BlockSpec((None, A_HALF, D_MODEL),
                            lambda bi, r, i: (bi, jnp.minimum((i + 1) * halo_per_tile, n_halo - 1), c(r)))

    o, lse = pl.pallas_call(
        functools.partial(_attn_body, n_tiles=n_tiles),
        grid=(b, dil, n_tiles),
        in_specs=[own(0), prev(1), own(1), nxt(1), prev(2), own(2), nxt(2)],
        out_specs=[
            pl.BlockSpec((None, tq, D_MODEL), lambda bi, r, i: (bi, i, r)),
            pl.BlockSpec((None, tq, LANES), lambda bi, r, i: (bi, i, r)),
        ],
        out_shape=[
            jax.ShapeDtypeStruct((b, n_sub, dil * D_MODEL), BF16),
            jax.ShapeDtypeStruct((b, n_sub, dil * LANES), F32),
        ],
        compiler_params=_params("parallel", "parallel", "parallel"),
        name=f"attn_d{dil}",
    )(view, view, view, view, view, view, view)
    return o.reshape(b, s, D_MODEL), lse.reshape(b, s, LANES)


def _a_out_body(o1_ref, o2_ref, o3_ref, l1_ref, l2_ref, l3_ref, x_ref, w_ref, out_ref, y_ref):
    tm = x_ref.shape[0]
    l1, l2, l3 = l1_ref[...], l2_ref[...], l3_ref[...]
    m = jnp.maximum(jnp.maximum(l1, l2), l3)
    e1, e2, e3 = jnp.exp(l1 - m), jnp.exp(l2 - m), jnp.exp(l3 - m)
    inv = 1.0 / (e1 + e2 + e3)
    w1, w2, w3 = e1 * inv, e2 * inv, e3 * inv
    left = lax.broadcasted_iota(jnp.int32, (tm, LANES), 1) < A_HEAD_DIM
    for hp in range(A_HEADS // 2):
        sl = slice(hp * LANES, (hp + 1) * LANES)

        def spread(w):
            return jnp.where(left, w[:, 2 * hp:2 * hp + 1], w[:, 2 * hp + 1:2 * hp + 2])

        y = (spread(w1) * o1_ref[:, sl].astype(F32) + spread(w2) * o2_ref[:, sl].astype(F32)
             + spread(w3) * o3_ref[:, sl].astype(F32))
        y_ref[:, sl] = y.astype(BF16)
    out_ref[...] = x_ref[...] + _dot(y_ref[...], w_ref[...])


def _a_out(outs, lses, x2, w):
    t = x2.shape[0]
    tm = 512
    tok = pl.BlockSpec((tm, D_MODEL), lambda i: (i, 0))
    st = pl.BlockSpec((tm, LANES), lambda i: (i, 0))
    return pl.pallas_call(
        _a_out_body,
        grid=(t // tm,),
        in_specs=[tok, tok, tok, st, st, st, tok, _resident(w.shape)],
        out_specs=tok,
        out_shape=jax.ShapeDtypeStruct((t, D_MODEL), F32),
        scratch_shapes=[pltpu.VMEM((tm, D_MODEL), BF16)],
        compiler_params=_params("parallel"),
        name="a_out",
    )(*[o.reshape(t, D_MODEL) for o in outs], *[l.reshape(t, LANES) for l in lses], x2, w)


def _rope_tables(seq_len):
    inv = ROPE_THETA ** (-jnp.arange(0, A_HEAD_DIM, 2, dtype=F32) / A_HEAD_DIM)
    ang = jnp.arange(seq_len, dtype=F32)[:, None] * inv[None, :]
    cos, sin = jnp.cos(ang), jnp.sin(ang)
    reps = LANES // A_HEAD_DIM
    return (jnp.concatenate([cos, cos] * reps, axis=1),
            jnp.concatenate([-sin, sin] * reps, axis=1))


def _mixer_a(x, g, w_in, w_out):
    b, s, _ = x.shape
    cos_t, sin_t = _rope_tables(s)
    qkv = _a_in(x, g, w_in, cos_t, sin_t)
    outs, lses = [], []
    for gidx, (_, dil) in enumerate(A_GROUPS):
        o, l = _attn_group(qkv, gidx, dil)
        outs.append(o)
        lses.append(l)
    return _a_out(outs, lses, x.reshape(b * s, D_MODEL), w_out).reshape(x.shape)


HALO = 8


def _hy_in_body(xp_ref, x_ref, xn_ref, g_ref, w_ref, cw_ref, cb_ref, x0_ref, vv_ref, u_ref, *, n_tiles):
    tm = x_ref.shape[0]
    i = pl.program_id(1)
    xe = jnp.concatenate([xp_ref[...], x_ref[...], xn_ref[...]], axis=0)
    xn = _rms(xe, g_ref[...]).astype(BF16)
    row = lax.broadcasted_iota(jnp.int32, (tm + 2 * HALO, 1), 0)
    inside = ((row >= HALO) | (i > 0)) & ((row < tm + HALO) | (i < n_tiles - 1))
    parts = []
    for c in range(3):
        lo = c * D_MODEL
        u_ref[...] = jnp.where(inside, _dot(xn, w_ref[:, lo:lo + D_MODEL]), 0.0)
        conv = (u_ref[pl.ds(HALO - 1, tm), :] * cw_ref[0:1, lo:lo + D_MODEL]
                + u_ref[pl.ds(HALO, tm), :] * cw_ref[1:2, lo:lo + D_MODEL]
                + u_ref[pl.ds(HALO + 1, tm), :] * cw_ref[2:3, lo:lo + D_MODEL]
                + cb_ref[:, lo:lo + D_MODEL])
        parts.append(conv)
    x0_ref[...] = parts[0]
    vv_ref[...] = parts[2] * parts[1]


def _hy_in(x, g, w, conv_w, conv_b):
    b, s, _ = x.shape
    tm = 512
    n_tiles = s // tm
    per = tm // HALO
    n_halo = s // HALO
    tok = pl.BlockSpec((None, tm, D_MODEL), lambda bi, i: (bi, i, 0))
    return pl.pallas_call(
        functools.partial(_hy_in_body, n_tiles=n_tiles),
        grid=(b, n_tiles),
        in_specs=[
            pl.BlockSpec((None, HALO, D_MODEL), lambda bi, i: (bi, jnp.maximum(i * per - 1, 0), 0)),
            tok,
            pl.BlockSpec((None, HALO, D_MODEL), lambda bi, i: (bi, jnp.minimum((i + 1) * per, n_halo - 1), 0)),
            pl.BlockSpec((1, D_MODEL), lambda bi, i: (0, 0)),
            _resident(w.shape),
            pl.BlockSpec(conv_w.shape, lambda bi, i: (0, 0)),
            pl.BlockSpec(conv_b.shape, lambda bi, i: (0, 0)),
        ],
        out_specs=[tok, tok],
        out_shape=[jax.ShapeDtypeStruct(x.shape, F32), jax.ShapeDtypeStruct(x.shape, F32)],
        scratch_shapes=[pltpu.VMEM((tm + 2 * HALO, D_MODEL), F32)],
        compiler_params=_params("parallel", "parallel"),
        name="hy_in",
    )(x, x, x, g, w, conv_w, conv_b)


def _hdot(a, b):
    return jnp.dot(a, b, precision=lax.Precision.HIGHEST, preferred_element_type=F32)


def _hy_filter_body(z_ref, t_ref, a_ref, b_ref, w1_ref, b1_ref, w2_ref, b2_ref, w3_ref, b3_ref,
                    wo_ref, fr_ref, dl_ref, h_ref, sum_ref):
    fr = fr_ref[...]
    hid = jnp.sin(fr * (_hdot(z_ref[...], w1_ref[...]) + b1_ref[...]))
    hid = jnp.sin(fr * (_hdot(hid, w2_ref[...]) + b2_ref[...]))
    hid = jnp.sin(fr * (_hdot(hid, w3_ref[...]) + b3_ref[...]))
    decay = jnp.exp(-t_ref[...] * dl_ref[...])
    h_fwd = _hdot(hid, wo_ref[:, :D_MODEL]) * decay
    h_bwd = _hdot(hid, wo_ref[:, D_MODEL:]) * decay
    h = a_ref[...] * h_fwd + b_ref[...] * h_bwd
    h_ref[...] = h

    @pl.when(pl.program_id(0) == 0)
    def _():
        sum_ref[...] = jnp.zeros_like(sum_ref)

    sum_ref[...] += jnp.sum(jnp.abs(h), axis=0, keepdims=True)


def _hy_filter(seq_len, f_w1, f_b1, f_w2, f_b2, f_w3, f_b3, f_wout, f_freq):
    n = 2 * seq_len
    t = jnp.linspace(0.0, 1.0, seq_len, dtype=F32)[:, None]
    w = 2.0 * math.pi * jnp.arange(seq_len, dtype=F32)[:, None] / seq_len
    f = jnp.linspace(1e-4, HY_BANDS - 1, HY_BANDS, dtype=F32)[None, :]
    z = jnp.concatenate([t, jnp.cos(f * w), -jnp.sin(f * w)], axis=-1)
    src = np.concatenate([np.arange(seq_len), [0], np.arange(seq_len - 1, 0, -1)])
    pos = np.arange(n)
    use_fwd = (pos < seq_len).astype(np.float32)[:, None]
    use_bwd = ((pos == 0) | (pos > seq_len)).astype(np.float32)[:, None]
    z2 = jnp.pad(z[src], ((0, 0), (0, HY_HID_PAD - HY_EMB)))
    t2 = t[src]
    max_decay = math.log(HY_DECAY_TARGET) / HY_DECAY_STRONG_PCT
    min_decay = math.log(HY_DECAY_TARGET) / HY_DECAY_WEAK_PCT
    deltas = jnp.abs(jnp.linspace(min_decay, max_decay, D_MODEL, dtype=F32))[None, :]

    def pad2(m, rows):
        return jnp.pad(m, ((0, rows - m.shape[0]), (0, HY_HID_PAD - m.shape[1])))

    def padv(v):
        return jnp.pad(v, (0, HY_HID_PAD - v.shape[0]))[None, :]

    wo = jnp.pad(f_wout, ((0, HY_HID_PAD - f_wout.shape[0]), (0, 0)))
    tr = 512
    rowblk = lambda width: pl.BlockSpec((tr, width), lambda i: (i, 0))
    full = lambda shape: pl.BlockSpec(shape, lambda i: (0, 0))
    sq = (HY_HID_PAD, HY_HID_PAD)
    vec = (1, HY_HID_PAD)
    return pl.pallas_call(
        _hy_filter_body,
        grid=(n // tr,),
        in_specs=[rowblk(HY_HID_PAD), rowblk(1), rowblk(1), rowblk(1),
                  full(sq), full(vec), full(sq), full(vec), full(sq), full(vec),
                  full(wo.shape), full(vec), full((1, D_MODEL))],
        out_specs=[rowblk(D_MODEL), full((1, D_MODEL))],
        out_shape=[jax.ShapeDtypeStruct((n, D_MODEL), F32), jax.ShapeDtypeStruct((1, D_MODEL), F32)],
        compiler_params=_params("arbitrary"),
        name="hy_filter",
    )(z2, t2, jnp.asarray(use_fwd), jnp.asarray(use_bwd),
      pad2(f_w1, HY_HID_PAD), padv(f_b1), pad2(f_w2, HY_HID_PAD), padv(f_b2),
      pad2(f_w3, HY_HID_PAD), padv(f_b3), wo, padv(f_freq), deltas)


def _fft_split(n):
    n1 = 1 << ((n.bit_length() - 1 + 1) // 2)
    return n1, n // n1


def _fft_tables(n1, n2):
    n = n1 * n2
    h = n1 // 2
    idx = np.arange(n1)
    ang = -2.0 * np.pi * ((idx[:, None] * idx[None, :]) % n1) / n1
    fr, fi = np.cos(ang), np.sin(ang)
    m1_data = np.block([[fr[:, :h], -fi[:, :h]], [fi[:, :h], fr[:, :h]]])
    m1_filt = np.concatenate([fr, fi], axis=0)
    ifr, ifi = fr.T[:h] / n, -fi.T[:h] / n
    m3 = np.block([[ifr, -ifi], [ifi, ifr]])
    k1 = jnp.arange(n1, dtype=jnp.int32)[:, None, None]
    k2 = jnp.arange(n2, dtype=jnp.int32)[None, :, None]
    i2 = jnp.arange(n2, dtype=jnp.int32)[None, None, :]
    phase = (i2 * k1 + n1 * i2 * k2) % n
    ga = (-2.0 * math.pi / n) * phase.astype(F32)
    gr, gi = jnp.cos(ga), jnp.sin(ga)
    g_fwd = jnp.concatenate([jnp.concatenate([gr, -gi], axis=2), jnp.concatenate([gi, gr], axis=2)], axis=1)
    grt, git = jnp.swapaxes(gr, 1, 2), jnp.swapaxes(gi, 1, 2)
    g_inv = jnp.concatenate([jnp.concatenate([grt, git], axis=2), jnp.concatenate([-git, grt], axis=2)], axis=1)
    as_bf = lambda m: jnp.asarray(m, dtype=F32).astype(BF16)
    return as_bf(m1_data), as_bf(m1_filt), as_bf(m3), g_fwd.astype(BF16), g_inv.astype(BF16)


def _lmul_body(m_ref, z_ref, o_ref):
    kk = m_ref.shape[1]
    cb = z_ref.shape[-1]
    z = z_ref[...].reshape(kk, cb).astype(BF16)
    o_ref[...] = _dot(m_ref[...], z).reshape(o_ref.shape).astype(o_ref.dtype)


def _lmul(mat, z4, out_dtype, cb, name):
    p, _, kh, cols = z4.shape
    r = mat.shape[0]
    return pl.pallas_call(
        _lmul_body,
        grid=(p, cols // cb),
        in_specs=[pl.BlockSpec(mat.shape, lambda pi, j: (0, 0)),
                  pl.BlockSpec((None, 2, kh, cb), lambda pi, j: (pi, 0, 0, j))],
        out_specs=pl.BlockSpec((None, 2, r // 2, cb), lambda pi, j: (pi, 0, 0, j)),
        out_shape=jax.ShapeDtypeStruct((p, 2, r // 2, cols), out_dtype),
        compiler_params=_params("parallel", "parallel"),
        name=name,
    )(mat, z4)


def _spec_filter_body(g_ref, a_ref, sc_ref, h_ref):
    n2, ct = a_ref.shape[1], a_ref.shape[2]
    spec = _dot(g_ref[...], a_ref[...].reshape(2 * n2, ct)) * sc_ref[...]
    h_ref[...] = spec.reshape(2, n2, ct)


def _spec_filter(g_fwd, a5, scale):
    _, _, n1, n2, c = a5.shape
    return pl.pallas_call(
        _spec_filter_body,
        grid=(n1,),
        in_specs=[pl.BlockSpec((None, 2 * n2, 2 * n2), lambda k: (k, 0, 0)),
                  pl.BlockSpec((None, 2, None, n2, c), lambda k: (0, 0, k, 0, 0)),
                  pl.BlockSpec((1, c), lambda k: (0, 0))],
        out_specs=pl.BlockSpec((None, 2, n2, c), lambda k: (k, 0, 0, 0)),
        out_shape=jax.ShapeDtypeStruct((n1, 2, n2, c), F32),
        compiler_params=_params("parallel"),
        name="hy_spec_filter",
    )(g_fwd, a5, scale)


def _spec_body(gf_ref, gi_ref, h_ref, a_ref, o_ref):
    n2, ct = a_ref.shape[1], a_ref.shape[2]
    spec = _dot(gf_ref[...], a_ref[...].reshape(2 * n2, ct))
    xr, xi = spec[:n2], spec[n2:]
    hr, hi = h_ref[0], h_ref[1]
    y = jnp.concatenate([xr * hr - xi * hi, xr * hi + xi * hr], axis=0).astype(BF16)
    o_ref[...] = _dot(gi_ref[...], y).reshape(2, n2, ct).astype(BF16)


def _spec(g_fwd, g_inv, hspec, a5):
    p, _, n1, n2, c = a5.shape
    blk = pl.BlockSpec((None, 2, None, n2, c), lambda k, pi: (pi, 0, k, 0, 0))
    mat = pl.BlockSpec((None, 2 * n2, 2 * n2), lambda k, pi: (k, 0, 0))
    return pl.pallas_call(
        _spec_body,
        grid=(n1, p),
        in_specs=[mat, mat, pl.BlockSpec((None, 2, n2, c), lambda k, pi: (k, 0, 0, 0)), blk],
        out_specs=blk,
        out_shape=jax.ShapeDtypeStruct(a5.shape, BF16),
        compiler_params=_params("parallel", "parallel"),
        name="hy_spec",
    )(g_fwd, g_inv, hspec, a5)


def _long_conv(vv, h_raw, h_norm):
    b, l, c = vv.shape
    n1, n2 = _fft_split(2 * l)
    m1_data, m1_filt, m3, g_fwd, g_inv = _fft_tables(n1, n2)
    cb = 2048
    a_h = _lmul(m1_filt, h_raw.reshape(1, 2, n1 // 2, n2 * c), BF16, cb, "hy_dft1_filter")
    hspec = _spec_filter(g_fwd, a_h.reshape(1, 2, n1, n2, c), 1.0 / h_norm)
    a = _lmul(m1_data, vv.reshape(b // 2, 2, n1 // 2, n2 * c), BF16, cb, "hy_dft1")
    bk = _spec(g_fwd, g_inv, hspec, a.reshape(b // 2, 2, n1, n2, c))
    y = _lmul(m3, bk.reshape(b // 2, 2, n1, n2 * c), F32, cb, "hy_idft")
    return y.reshape(b, l, c)


def _hy_out_body(cv_ref, vv_ref, x0_ref, bd_ref, x_ref, w_ref, o_ref):
    y = ((cv_ref[...] + bd_ref[...] * vv_ref[...]) * x0_ref[...]).astype(BF16)
    o_ref[...] = x_ref[...] + _dot(y, w_ref[...])


def _hy_out(conv, vv, x0, bias_d, x2, w):
    t = x2.shape[0]
    tm = 512
    tok = pl.BlockSpec((tm, D_MODEL), lambda i: (i, 0))
    return pl.pallas_call(
        _hy_out_body,
        grid=(t // tm,),
        in_specs=[tok, tok, tok, pl.BlockSpec((1, D_MODEL), lambda i: (0, 0)), tok, _resident(w.shape)],
        out_specs=tok,
        out_shape=jax.ShapeDtypeStruct((t, D_MODEL), F32),
        compiler_params=_params("parallel"),
        name="hy_out",
    )(conv.reshape(t, D_MODEL), vv.reshape(t, D_MODEL), x0.reshape(t, D_MODEL), bias_d, x2, w)


def _mixer_b(x, g, w_in, conv_w, conv_b, filt, bias_d, w_out):
    b, s, _ = x.shape
    x0, vv = _hy_in(x, g, w_in, conv_w, conv_b)
    h_raw, h_norm = _hy_filter(s, *filt)
    conv = _long_conv(vv, h_raw, h_norm)
    return _hy_out(conv, vv, x0, bias_d, x.reshape(b * s, D_MODEL), w_out).reshape(x.shape)


def _gelu(z):
    return 0.5 * z * (1.0 + lax.erf(z * (2.0 ** -0.5)))


def _sgu_body(x_ref, g_ref, win_ref, lng_ref, lnb_ref, ws_ref, bs_ref, wout_ref, o_ref, y_ref):
    tm = x_ref.shape[0]
    x = x_ref[...]
    xn = _rms(x, g_ref[...]).astype(BF16)
    zu = _gelu(_dot(xn, win_ref[:, :D_MODEL]))
    zv = _gelu(_dot(xn, win_ref[:, D_MODEL:]))
    zc = zv - jnp.mean(zv, axis=-1, keepdims=True)
    zv = zc * lax.rsqrt(jnp.mean(zc * zc, axis=-1, keepdims=True) + EPS) * lng_ref[...] + lnb_ref[...]
    zvb = zv.astype(BF16)
    for c in range(tm // C_CHUNK):
        rows = slice(c * C_CHUNK, (c + 1) * C_CHUNK)
        for h in range(C_GROUPS):
            cols = slice(h * LANES, (h + 1) * LANES)
            sv = _dot(ws_ref[h], zvb[rows, cols]) + bs_ref[:, h:h + 1]
            y_ref[rows, cols] = (zu[rows, cols] * sv).astype(BF16)
    o_ref[...] = x + _dot(y_ref[...], wout_ref[...])


def _mixer_c(x, g, w_in, ln_g, ln_b, w_s, b_s_t, w_out):
    b, s, _ = x.shape
    t = b * s
    tm = 256
    tok = pl.BlockSpec((tm, D_MODEL), lambda i: (i, 0))
    row = pl.BlockSpec((1, D_MODEL), lambda i: (0, 0))
    return pl.pallas_call(
        _sgu_body,
        grid=(t // tm,),
        in_specs=[tok, row, _resident(w_in.shape), row, row, _resident(w_s.shape),
                  pl.BlockSpec(b_s_t.shape, lambda i: (0, 0)), _resident(w_out.shape)],
        out_specs=tok,
        out_shape=jax.ShapeDtypeStruct((t, D_MODEL), F32),
        scratch_shapes=[pltpu.VMEM((tm, D_MODEL), BF16)],
        compiler_params=_params("parallel"),
        name="sgu",
    )(x.reshape(t, D_MODEL), g, w_in, ln_g, ln_b, w_s, b_s_t, w_out).reshape(x.shape)


def _trunk(x, kvs, boff, w):
    b, s, _ = x.shape
    t = b * s
    for i in range(DEPTH):
        kind, j = i % N_MIXERS, i // N_MIXERS
        g = w["g_mix"][i][None, :]
        if kind == 0:
            x = _mixer_a(x, g, w["a_w_in"][j], w["a_w_out"][j])
        elif kind == 1:
            filt = tuple(w[k][j] for k in ("b_f_w1", "b_f_b1", "b_f_w2", "b_f_b2", "b_f_w3", "b_f_b3",
                                          "b_f_wout", "b_f_freq"))
            x = _mixer_b(x, g, w["b_w_in"][j], w["b_conv_w"][j], w["b_conv_b"][j][None, :], filt,
                         w["b_bias_d"][j][None, :], w["b_w_out"][j])
        else:
            x = _mixer_c(x, g, w["c_w_in"][j], w["c_ln_g"][j][None, :], w["c_ln_b"][j][None, :],
                         w["c_w_s"][j], w["c_b_s"][j].T, w["c_w_out"][j])
        x = _xattn(x, w["g_cross"][i][None, :], w["x_w_q"][i], kvs[i], boff, w["x_w_o"][i])
        g_final = w["g_final"][None, :] if i == DEPTH - 1 else None
        x = _ffn(x.reshape(t, D_MODEL), w["g_ffn"][i][None, :], w["f_w_gu"][i], w["f_w_down"][i],
                 g_final).reshape(b, s, D_MODEL)
    return x


_BF16_WEIGHTS = ("a_w_in", "a_w_out", "b_w_in", "b_w_out", "c_w_in", "c_w_s", "c_w_out",
                 "x_w_q", "x_w_kv", "x_w_o", "f_w_gu", "f_w_down")


def kernel(x_prompt, x_sample, mem_prompt, mem_sample, g_mix, g_cross, g_ffn, g_final, a_w_in, a_w_out, b_w_in, b_conv_w, b_conv_b, b_f_w1, b_f_b1, b_f_w2, b_f_b2, b_f_w3, b_f_b3, b_f_wout, b_f_freq, b_bias_d, b_w_out, c_w_in, c_ln_g, c_ln_b, c_w_s, c_b_s, c_w_out, x_w_q, x_w_kv, x_w_o, f_w_gu, f_w_down):
    w = dict(g_mix=g_mix, g_cross=g_cross, g_ffn=g_ffn, g_final=g_final,
             a_w_in=a_w_in, a_w_out=a_w_out, b_w_in=b_w_in, b_conv_w=b_conv_w, b_conv_b=b_conv_b,
             b_f_w1=b_f_w1, b_f_b1=b_f_b1, b_f_w2=b_f_w2, b_f_b2=b_f_b2, b_f_w3=b_f_w3,
             b_f_b3=b_f_b3, b_f_wout=b_f_wout, b_f_freq=b_f_freq, b_bias_d=b_bias_d,
             b_w_out=b_w_out, c_w_in=c_w_in, c_ln_g=c_ln_g, c_ln_b=c_ln_b, c_w_s=c_w_s,
             c_b_s=c_b_s, c_w_out=c_w_out, x_w_q=x_w_q, x_w_kv=x_w_kv, x_w_o=x_w_o,
             f_w_gu=f_w_gu, f_w_down=f_w_down)
    for name in _BF16_WEIGHTS:
        w[name] = w[name].astype(BF16)
    nb_prompt = mem_prompt.shape[0]
    mem = jnp.concatenate([mem_prompt, mem_sample], axis=0)
    mem2 = mem.reshape(mem.shape[0] * MEM_LEN, D_MODEL)
    kvs = [_kv_proj(mem2, w["x_w_kv"][i]).reshape(mem.shape[0], MEM_LEN, 2 * D_MODEL) for i in range(DEPTH)]
    y_prompt = _trunk(x_prompt, kvs, 0, w)
    y_sample = _trunk(x_sample, kvs, nb_prompt, w)
    return (y_prompt, y_sample)
```

```python
import functools
import math

import numpy as np
import jax
import jax.numpy as jnp
from jax import lax
from jax.experimental import pallas as pl
from jax.experimental.pallas import tpu as pltpu

F32 = jnp.float32
BF16 = jnp.bfloat16

D_MODEL = 1024
DEPTH = 4
N_MIXERS = 3

A_GROUPS = ((128, 1), (512, 4), (2048, 16))
A_N_GROUPS = len(A_GROUPS)
A_HEADS = 16
A_HEAD_DIM = D_MODEL // A_HEADS
A_HALF = 64
ROPE_THETA = 10000.0

HY_EMB = 33
HY_BANDS = (HY_EMB - 1) // 2
HY_HID_PAD = 128
HY_DECAY_TARGET = 1e-2
HY_DECAY_STRONG_PCT = 0.3
HY_DECAY_WEAK_PCT = 1.5

C_CHUNK = 128
C_GROUPS = 8

MEM_LEN = 256
X_HEADS = 4
X_HEAD_DIM = D_MODEL // X_HEADS

D_FF = -(-8 * D_MODEL // (3 * 256)) * 256
FF_CHUNK = 256

EPS = 1e-6
NEG_INF = -1e30

LANES = 128
VMEM_LIMIT = 56 * 1024 * 1024


def _params(*sem):
    return pltpu.CompilerParams(dimension_semantics=sem, vmem_limit_bytes=VMEM_LIMIT)


def _resident(shape):
    nd = len(shape)
    return pl.BlockSpec(shape, lambda *_: (0,) * nd, pipeline_mode=pl.Buffered(1))


def _dot(a, b):
    return jnp.dot(a, b, preferred_element_type=F32)


def _dot_nt(a, b):
    return lax.dot_general(a, b, (((1,), (1,)), ((), ())), preferred_element_type=F32)


def _rms(x, g):
    return x * lax.rsqrt(jnp.mean(x * x, axis=-1, keepdims=True) + EPS) * g


def _ffn_body(*refs, final):
    if final:
        x_ref, g_ref, wgu_ref, wd_ref, gf_ref, o_ref, act_ref = refs
    else:
        x_ref, g_ref, wgu_ref, wd_ref, o_ref, act_ref = refs
    x = x_ref[...]
    xn = _rms(x, g_ref[...]).astype(BF16)
    for c in range(D_FF // FF_CHUNK):
        lo = c * FF_CHUNK
        gate = _dot(xn, wgu_ref[:, lo:lo + FF_CHUNK])
        up = _dot(xn, wgu_ref[:, D_FF + lo:D_FF + lo + FF_CHUNK])
        act_ref[:, lo:lo + FF_CHUNK] = (gate / (1.0 + jnp.exp(-gate)) * up).astype(BF16)
    y = x + _dot(act_ref[...], wd_ref[...])
    if final:
        y = _rms(y, gf_ref[...])
    o_ref[...] = y


def _ffn(x2, g, wgu, wd, g_final=None):
    t = x2.shape[0]
    tm = 512
    final = g_final is not None
    tok = pl.BlockSpec((tm, D_MODEL), lambda i: (i, 0))
    row = pl.BlockSpec((1, D_MODEL), lambda i: (0, 0))
    in_specs = [tok, row, _resident(wgu.shape), _resident(wd.shape)]
    args = [x2, g, wgu, wd]
    if final:
        in_specs.append(row)
        args.append(g_final)
    return pl.pallas_call(
        functools.partial(_ffn_body, final=final),
        grid=(t // tm,),
        in_specs=in_specs,
        out_specs=tok,
        out_shape=jax.ShapeDtypeStruct((t, D_MODEL), F32),
        scratch_shapes=[pltpu.VMEM((tm, D_FF), BF16)],
        compiler_params=_params("parallel"),
        name="ffn_final" if final else "ffn",
    )(*args)


def _kv_body(m_ref, w_ref, o_ref):
    o_ref[...] = _dot(m_ref[...].astype(BF16), w_ref[...]).astype(BF16)


def _kv_proj(mem2, wkv):
    r = mem2.shape[0]
    tm = 256
    return pl.pallas_call(
        _kv_body,
        grid=(r // tm,),
        in_specs=[pl.BlockSpec((tm, D_MODEL), lambda i: (i, 0)), _resident(wkv.shape)],
        out_specs=pl.BlockSpec((tm, 2 * D_MODEL), lambda i: (i, 0)),
        out_shape=jax.ShapeDtypeStruct((r, 2 * D_MODEL), BF16),
        compiler_params=_params("parallel"),
        name="kv_proj",
    )(mem2, wkv)


def _xattn_body(x_ref, g_ref, wq_ref, k_ref, v_ref, wo_ref, o_ref, y_ref):
    x = x_ref[...]
    xn = _rms(x, g_ref[...]).astype(BF16)
    q = (_dot(xn, wq_ref[...]) * (X_HEAD_DIM ** -0.5)).astype(BF16)
    for h in range(X_HEADS):
        sl = slice(h * X_HEAD_DIM, (h + 1) * X_HEAD_DIM)
        s = _dot_nt(q[:, sl], k_ref[:, sl])
        p = jnp.exp(s - jnp.max(s, axis=-1, keepdims=True))
        inv = 1.0 / jnp.sum(p, axis=-1, keepdims=True)
        y_ref[:, sl] = (_dot(p.astype(BF16), v_ref[:, sl]) * inv).astype(BF16)
    o_ref[...] = x + _dot(y_ref[...], wo_ref[...])


def _xattn(x, g, wq, kv, boff, wo):
    b, s, _ = x.shape
    tm = 512
    tok = pl.BlockSpec((None, tm, D_MODEL), lambda bi, i: (bi, i, 0))
    return pl.pallas_call(
        _xattn_body,
        grid=(b, s // tm),
        in_specs=[
            tok,
            pl.BlockSpec((1, D_MODEL), lambda bi, i: (0, 0)),
            _resident(wq.shape),
            pl.BlockSpec((None, MEM_LEN, D_MODEL), lambda bi, i: (bi + boff, 0, 0)),
            pl.BlockSpec((None, MEM_LEN, D_MODEL), lambda bi, i: (bi + boff, 0, 1)),
            _resident(wo.shape),
        ],
        out_specs=tok,
        out_shape=jax.ShapeDtypeStruct(x.shape, F32),
        scratch_shapes=[pltpu.VMEM((tm, D_MODEL), BF16)],
        compiler_params=_params("parallel", "parallel"),
        name="xattn",
    )(x, g, wq, kv, kv, wo)


def _a_in_body(x_ref, g_ref, w_ref, cos_ref, sin_ref, *rest):
    o_refs, (xs_ref, xp_ref) = rest[:A_N_GROUPS], rest[A_N_GROUPS:]
    tm = x_ref.shape[0]
    xn_f32 = _rms(x_ref[...], g_ref[...])
    reps = D_MODEL // LANES
    for cb in range(reps):
        xs_ref[cb] = xn_f32[:, cb * LANES:(cb + 1) * LANES]
    lane = lax.broadcasted_iota(jnp.int32, (tm, D_MODEL), 1)
    low_half = (lane & (A_HEAD_DIM // 2)) == 0
    for gidx, (o_ref, (_, dil)) in enumerate(zip(o_refs, A_GROUPS)):
        rows = tm // dil
        if dil == 1:
            xn = xn_f32.astype(BF16)
            cos, sin = cos_ref[...], sin_ref[...]
        else:
            for r in range(dil):
                for cb in range(reps):
                    xp_ref[r * rows:(r + 1) * rows, cb * LANES:(cb + 1) * LANES] = (
                        xs_ref[cb, pl.ds(r, rows, stride=dil), :].astype(BF16))
            xn = xp_ref[...]
            cos = jnp.concatenate([cos_ref[pl.ds(r, rows, stride=dil), :] for r in range(dil)], axis=0)
            sin = jnp.concatenate([sin_ref[pl.ds(r, rows, stride=dil), :] for r in range(dil)], axis=0)
        cos_t = jnp.concatenate([cos] * reps, axis=1)
        sin_t = jnp.concatenate([sin] * reps, axis=1)
        for part in range(3):
            lo = (gidx * 3 + part) * D_MODEL
            y = _dot(xn, w_ref[:, lo:lo + D_MODEL])
            if part < 2:
                if part == 0:
                    y = y * (A_HEAD_DIM ** -0.5)
                partner = jnp.where(low_half,
                                    pltpu.roll(y, D_MODEL - A_HEAD_DIM // 2, 1),
                                    pltpu.roll(y, A_HEAD_DIM // 2, 1))
                y = y * cos_t + partner * sin_t
            o_ref[:, :, part * D_MODEL:(part + 1) * D_MODEL] = y.reshape(dil, rows, D_MODEL).astype(BF16)


def _a_in(x, g, w, cos_t, sin_t):
    b, s, _ = x.shape
    tm = 256
    return pl.pallas_call(
        _a_in_body,
        grid=(b, s // tm),
        in_specs=[
            pl.BlockSpec((None, tm, D_MODEL), lambda bi, i: (bi, i, 0)),
            pl.BlockSpec((1, D_MODEL), lambda bi, i: (0, 0)),
            _resident(w.shape),
            pl.BlockSpec((tm, LANES), lambda bi, i: (i, 0)),
            pl.BlockSpec((tm, LANES), lambda bi, i: (i, 0)),
        ],
        out_specs=[pl.BlockSpec((None, dil, tm // dil, 3 * D_MODEL), lambda bi, i: (bi, 0, i, 0))
                   for _, dil in A_GROUPS],
        out_shape=[jax.ShapeDtypeStruct((b, dil, s // dil, 3 * D_MODEL), BF16) for _, dil in A_GROUPS],
        scratch_shapes=[pltpu.VMEM((D_MODEL // LANES, tm, LANES), F32), pltpu.VMEM((tm, D_MODEL), BF16)],
        compiler_params=_params("parallel", "parallel"),
        name="a_in",
    )(x, g, w, cos_t, sin_t)


def _attn_body(q_ref, kp_ref, k_ref, kn_ref, vp_ref, v_ref, vn_ref, o_ref, lse_ref, *, n_tiles):
    tq = q_ref.shape[0]
    win = tq + 2 * A_HALF
    i = pl.program_id(2)
    kw = jnp.concatenate([kp_ref[...], k_ref[...], kn_ref[...]], axis=0)
    vw = jnp.concatenate([vp_ref[...], v_ref[...], vn_ref[...]], axis=0)
    qi = lax.broadcasted_iota(jnp.int32, (tq, win), 0)
    kj = lax.broadcasted_iota(jnp.int32, (tq, win), 1)
    first = jnp.where(i == 0, A_HALF, 0)
    last = jnp.where(i == n_tiles - 1, tq + A_HALF, win)
    mask = (kj >= qi) & (kj <= qi + 2 * A_HALF) & (kj >= first) & (kj < last)
    lane = lax.broadcasted_iota(jnp.int32, (tq, LANES), 1)
    left = lane < A_HEAD_DIM
    lse_all = jnp.zeros((tq, LANES), F32)
    for hp in range(A_HEADS // 2):
        sl = slice(hp * LANES, (hp + 1) * LANES)
        q2 = q_ref[:, sl]
        k2 = kw[:, sl]
        v2 = vw[:, sl]
        halves = []
        for side in range(2):
            keep = left if side == 0 else jnp.logical_not(left)
            s = _dot_nt(jnp.where(keep, q2, jnp.zeros_like(q2)), k2)
            s = jnp.where(mask, s, NEG_INF)
            m = jnp.max(s, axis=-1, keepdims=True)
            p = jnp.exp(s - m)
            l = jnp.sum(p, axis=-1, keepdims=True)
            halves.append(_dot(p.astype(BF16), v2) * (1.0 / l))
            lse_all = jnp.where(lane == 2 * hp + side, m + jnp.log(l), lse_all)
        o_ref[:, sl] = jnp.where(left, halves[0], halves[1]).astype(BF16)
    lse_ref[...] = lse_all


def _attn_group(qkv):
    b, dil, n_sub, _ = qkv.shape
    tq = 128
    n_tiles = n_sub // tq
    halo_per_tile = tq // A_HALF
    n_halo = n_sub // A_HALF

    def own(part):
        return pl.BlockSpec((None, None, tq, D_MODEL), lambda bi, r, i: (bi, r, i, part))

    def prev(part):
        return pl.BlockSpec((None, None, A_HALF, D_MODEL),
                            lambda bi, r, i: (bi, r, jnp.maximum(i * halo_per_tile - 1, 0), part))

    def nxt(part):
        return pl.BlockSpec((None, None, A_HALF, D_MODEL),
                            lambda bi, r, i: (bi, r, jnp.minimum((i + 1) * halo_per_tile, n_halo - 1), part))

    return pl.pallas_call(
        functools.partial(_attn_body, n_tiles=n_tiles),
        grid=(b, dil, n_tiles),
        in_specs=[own(0), prev(1), own(1), nxt(1), prev(2), own(2), nxt(2)],
        out_specs=[
            pl.BlockSpec((None, None, tq, D_MODEL), lambda bi, r, i: (bi, r, i, 0)),
            pl.BlockSpec((None, None, tq, LANES), lambda bi, r, i: (bi, r, i, 0)),
        ],
        out_shape=[
            jax.ShapeDtypeStruct((b, dil, n_sub, D_MODEL), BF16),
            jax.ShapeDtypeStruct((b, dil, n_sub, LANES), F32),
        ],
        compiler_params=_params("parallel", "parallel", "parallel"),
        name=f"attn_d{dil}",
    )(qkv, qkv, qkv, qkv, qkv, qkv, qkv)


def _a_out_body(o1_ref, o2_ref, o3_ref, l1_ref, l2_ref, l3_ref, x_ref, w_ref, out_ref,
                os2_ref, os3_ref, ls2_ref, ls3_ref, y_ref):
    tm = x_ref.shape[0]
    for o_ref, l_ref, os_ref, ls_ref in ((o2_ref, l2_ref, os2_ref, ls2_ref), (o3_ref, l3_ref, os3_ref, ls3_ref)):
        dil = o_ref.shape[0]
        rows = tm // dil
        for r in range(dil):
            for cb in range(D_MODEL // LANES):
                os_ref[cb, pl.ds(r, rows, stride=dil), :] = o_ref[r, :, cb * LANES:(cb + 1) * LANES].astype(F32)
            ls_ref[pl.ds(r, rows, stride=dil), :] = l_ref[r]
    l1, l2, l3 = l1_ref[0], ls2_ref[...], ls3_ref[...]
    m = jnp.maximum(jnp.maximum(l1, l2), l3)
    e1, e2, e3 = jnp.exp(l1 - m), jnp.exp(l2 - m), jnp.exp(l3 - m)
    inv = 1.0 / (e1 + e2 + e3)
    w1, w2, w3 = e1 * inv, e2 * inv, e3 * inv
    left = lax.broadcasted_iota(jnp.int32, (tm, LANES), 1) < A_HEAD_DIM
    for hp in range(A_HEADS // 2):
        sl = slice(hp * LANES, (hp + 1) * LANES)

        def spread(w):
            return jnp.where(left, w[:, 2 * hp:2 * hp + 1], w[:, 2 * hp + 1:2 * hp + 2])

        y = spread(w1) * o1_ref[0, :, sl].astype(F32) + spread(w2) * os2_ref[hp] + spread(w3) * os3_ref[hp]
        y_ref[:, sl] = y.astype(BF16)
    out_ref[...] = x_ref[...] + _dot(y_ref[...], w_ref[...])


def _a_out(outs, lses, x, w):
    b, s, _ = x.shape
    tm = 512
    tok = pl.BlockSpec((None, tm, D_MODEL), lambda bi, i: (bi, i, 0))

    def grouped(arr):
        dil, width = arr.shape[1], arr.shape[3]
        return pl.BlockSpec((None, dil, tm // dil, width), lambda bi, i: (bi, 0, i, 0))

    return pl.pallas_call(
        _a_out_body,
        grid=(b, s // tm),
        in_specs=[grouped(a) for a in outs] + [grouped(a) for a in lses] + [tok, _resident(w.shape)],
        out_specs=tok,
        out_shape=jax.ShapeDtypeStruct(x.shape, F32),
        scratch_shapes=[pltpu.VMEM((D_MODEL // LANES, tm, LANES), F32), pltpu.VMEM((D_MODEL // LANES, tm, LANES), F32),
                        pltpu.VMEM((tm, LANES), F32), pltpu.VMEM((tm, LANES), F32),
                        pltpu.VMEM((tm, D_MODEL), BF16)],
        compiler_params=_params("parallel", "parallel"),
        name="a_out",
    )(*outs, *lses, x, w)


def _rope_tables(seq_len):
    inv = ROPE_THETA ** (-jnp.arange(0, A_HEAD_DIM, 2, dtype=F32) / A_HEAD_DIM)
    ang = jnp.arange(seq_len, dtype=F32)[:, None] * inv[None, :]
    cos, sin = jnp.cos(ang), jnp.sin(ang)
    reps = LANES // A_HEAD_DIM
    return (jnp.concatenate([cos, cos] * reps, axis=1),
            jnp.concatenate([-sin, sin] * reps, axis=1))


def _mixer_a(x, g, w_in, w_out):
    b, s, _ = x.shape
    cos_t, sin_t = _rope_tables(s)
    outs, lses = [], []
    for qkv in _a_in(x, g, w_in, cos_t, sin_t):
        o, l = _attn_group(qkv)
        outs.append(o)
        lses.append(l)
    return _a_out(outs, lses, x, w_out)


HALO = 8


def _hy_in_body(xp_ref, x_ref, xn_ref, g_ref, w_ref, cw_ref, cb_ref, x0_ref, vv_ref, u_ref, *, n_tiles):
    tm = x_ref.shape[0]
    i = pl.program_id(1)
    xe = jnp.concatenate([xp_ref[...], x_ref[...], xn_ref[...]], axis=0)
    xn = _rms(xe, g_ref[...]).astype(BF16)
    row = lax.broadcasted_iota(jnp.int32, (tm + 2 * HALO, 1), 0)
    inside = ((row >= HALO) | (i > 0)) & ((row < tm + HALO) | (i < n_tiles - 1))
    parts = []
    for c in range(3):
        lo = c * D_MODEL
        u_ref[...] = jnp.where(inside, _dot(xn, w_ref[:, lo:lo + D_MODEL]), 0.0)
        conv = (u_ref[pl.ds(HALO - 1, tm), :] * cw_ref[0:1, lo:lo + D_MODEL]
                + u_ref[pl.ds(HALO, tm), :] * cw_ref[1:2, lo:lo + D_MODEL]
                + u_ref[pl.ds(HALO + 1, tm), :] * cw_ref[2:3, lo:lo + D_MODEL]
                + cb_ref[:, lo:lo + D_MODEL])
        parts.append(conv)
    x0_ref[...] = parts[0]
    vv_ref[...] = parts[2] * parts[1]


def _hy_in(x, g, w, conv_w, conv_b):
    b, s, _ = x.shape
    tm = 512
    n_tiles = s // tm
    per = tm // HALO
    n_halo = s // HALO
    tok = pl.BlockSpec((None, tm, D_MODEL), lambda bi, i: (bi, i, 0))
    return pl.pallas_call(
        functools.partial(_hy_in_body, n_tiles=n_tiles),
        grid=(b, n_tiles),
        in_specs=[
            pl.BlockSpec((None, HALO, D_MODEL), lambda bi, i: (bi, jnp.maximum(i * per - 1, 0), 0)),
            tok,
            pl.BlockSpec((None, HALO, D_MODEL), lambda bi, i: (bi, jnp.minimum((i + 1) * per, n_halo - 1), 0)),
            pl.BlockSpec((1, D_MODEL), lambda bi, i: (0, 0)),
            _resident(w.shape),
            pl.BlockSpec(conv_w.shape, lambda bi, i: (0, 0)),
            pl.BlockSpec(conv_b.shape, lambda bi, i: (0, 0)),
        ],
        out_specs=[tok, tok],
        out_shape=[jax.ShapeDtypeStruct(x.shape, F32), jax.ShapeDtypeStruct(x.shape, F32)],
        scratch_shapes=[pltpu.VMEM((tm + 2 * HALO, D_MODEL), F32)],
        compiler_params=_params("parallel", "parallel"),
        name="hy_in",
    )(x, x, x, g, w, conv_w, conv_b)


def _hdot(a, b):
    return jnp.dot(a, b, precision=lax.Precision.HIGHEST, preferred_element_type=F32)


def _hy_filter_body(z_ref, t_ref, a_ref, b_ref, w1_ref, b1_ref, w2_ref, b2_ref, w3_ref, b3_ref,
                    wo_ref, fr_ref, dl_ref, h_ref, sum_ref):
    fr = fr_ref[...]
    hid = jnp.sin(fr * (_hdot(z_ref[...], w1_ref[...]) + b1_ref[...]))
    hid = jnp.sin(fr * (_hdot(hid, w2_ref[...]) + b2_ref[...]))
    hid = jnp.sin(fr * (_hdot(hid, w3_ref[...]) + b3_ref[...]))
    decay = jnp.exp(-t_ref[...] * dl_ref[...])
    h_fwd = _hdot(hid, wo_ref[:, :D_MODEL]) * decay
    h_bwd = _hdot(hid, wo_ref[:, D_MODEL:]) * decay
    h = a_ref[...] * h_fwd + b_ref[...] * h_bwd
    h_ref[...] = h

    @pl.when(pl.program_id(0) == 0)
    def _():
        sum_ref[...] = jnp.zeros_like(sum_ref)

    sum_ref[...] += jnp.sum(jnp.abs(h), axis=0, keepdims=True)


def _hy_filter(seq_len, f_w1, f_b1, f_w2, f_b2, f_w3, f_b3, f_wout, f_freq):
    n = 2 * seq_len
    t = jnp.linspace(0.0, 1.0, seq_len, dtype=F32)[:, None]
    w = 2.0 * math.pi * jnp.arange(seq_len, dtype=F32)[:, None] / seq_len
    f = jnp.linspace(1e-4, HY_BANDS - 1, HY_BANDS, dtype=F32)[None, :]
    z = jnp.concatenate([t, jnp.cos(f * w), -jnp.sin(f * w)], axis=-1)
    src = np.concatenate([np.arange(seq_len), [0], np.arange(seq_len - 1, 0, -1)])
    pos = np.arange(n)
    use_fwd = (pos < seq_len).astype(np.float32)[:, None]
    use_bwd = ((pos == 0) | (pos > seq_len)).astype(np.float32)[:, None]
    z2 = jnp.pad(z[src], ((0, 0), (0, HY_HID_PAD - HY_EMB)))
    t2 = t[src]
    max_decay = math.log(HY_DECAY_TARGET) / HY_DECAY_STRONG_PCT
    min_decay = math.log(HY_DECAY_TARGET) / HY_DECAY_WEAK_PCT
    deltas = jnp.abs(jnp.linspace(min_decay, max_decay, D_MODEL, dtype=F32))[None, :]

    def pad2(m, rows):
        return jnp.pad(m, ((0, rows - m.shape[0]), (0, HY_HID_PAD - m.shape[1])))

    def padv(v):
        return jnp.pad(v, (0, HY_HID_PAD - v.shape[0]))[None, :]

    wo = jnp.pad(f_wout, ((0, HY_HID_PAD - f_wout.shape[0]), (0, 0)))
    tr = 512
    rowblk = lambda width: pl.BlockSpec((tr, width), lambda i: (i, 0))
    full = lambda shape: pl.BlockSpec(shape, lambda i: (0, 0))
    sq = (HY_HID_PAD, HY_HID_PAD)
    vec = (1, HY_HID_PAD)
    return pl.pallas_call(
        _hy_filter_body,
        grid=(n // tr,),
        in_specs=[rowblk(HY_HID_PAD), rowblk(1), rowblk(1), rowblk(1),
                  full(sq), full(vec), full(sq), full(vec), full(sq), full(vec),
                  full(wo.shape), full(vec), full((1, D_MODEL))],
        out_specs=[rowblk(D_MODEL), full((1, D_MODEL))],
        out_shape=[jax.ShapeDtypeStruct((n, D_MODEL), F32), jax.ShapeDtypeStruct((1, D_MODEL), F32)],
        compiler_params=_params("arbitrary"),
        name="hy_filter",
    )(z2, t2, jnp.asarray(use_fwd), jnp.asarray(use_bwd),
      pad2(f_w1, HY_HID_PAD), padv(f_b1), pad2(f_w2, HY_HID_PAD), padv(f_b2),
      pad2(f_w3, HY_HID_PAD), padv(f_b3), wo, padv(f_freq), deltas)


def _fft_split(n):
    n1 = 1 << ((n.bit_length() - 1 + 1) // 2)
    return n1, n // n1


def _fft_tables(n1, n2):
    n = n1 * n2
    h = n1 // 2
    idx = np.arange(n1)
    ang = -2.0 * np.pi * ((idx[:, None] * idx[None, :]) % n1) / n1
    fr, fi = np.cos(ang), np.sin(ang)
    m1_data = np.block([[fr[:, :h], -fi[:, :h]], [fi[:, :h], fr[:, :h]]])
    m1_filt = np.concatenate([fr, fi], axis=0)
    ifr, ifi = fr.T[:h] / n, -fi.T[:h] / n
    m3 = np.block([[ifr, -ifi], [ifi, ifr]])
    k1 = jnp.arange(n1, dtype=jnp.int32)[:, None, None]
    k2 = jnp.arange(n2, dtype=jnp.int32)[None, :, None]
    i2 = jnp.arange(n2, dtype=jnp.int32)[None, None, :]
    phase = (i2 * k1 + n1 * i2 * k2) % n
    ga = (-2.0 * math.pi / n) * phase.astype(F32)
    gr, gi = jnp.cos(ga), jnp.sin(ga)
    g_fwd = jnp.concatenate([jnp.concatenate([gr, -gi], axis=2), jnp.concatenate([gi, gr], axis=2)], axis=1)
    grt, git = jnp.swapaxes(gr, 1, 2), jnp.swapaxes(gi, 1, 2)
    g_inv = jnp.concatenate([jnp.concatenate([grt, git], axis=2), jnp.concatenate([-git, grt], axis=2)], axis=1)
    as_bf = lambda m: jnp.asarray(m, dtype=F32).astype(BF16)
    return as_bf(m1_data), as_bf(m1_filt), as_bf(m3), g_fwd.astype(BF16), g_inv.astype(BF16)


FFT_SUB = 16
FFT_TC = 512


def _dft_outer_body(m_ref, z_ref, o_ref, zs_ref, rs_ref):
    q, sub, tc = z_ref.shape
    r = m_ref.shape[0]
    nlb = tc // LANES
    for cb in range(nlb):
        zs_ref[cb] = z_ref[:, :, cb * LANES:(cb + 1) * LANES].reshape(q * sub, LANES)
    for j in range(sub):
        z = jnp.concatenate([zs_ref[cb, pl.ds(j, q, stride=sub), :] for cb in range(nlb)], axis=1)
        res = _dot(m_ref[...], z.astype(BF16))
        for cb in range(nlb):
            rs_ref[cb, pl.ds(j, r, stride=sub), :] = res[:, cb * LANES:(cb + 1) * LANES]
    for cb in range(nlb):
        o_ref[:, :, :, cb * LANES:(cb + 1) * LANES] = rs_ref[cb].reshape(2, r // 2, sub, LANES).astype(BF16)


def _dft_outer(mat, z5, name):
    p, q, nh, sub, c = z5.shape
    r = mat.shape[0]
    tc = FFT_TC
    return pl.pallas_call(
        _dft_outer_body,
        grid=(p, nh, c // tc),
        in_specs=[pl.BlockSpec(mat.shape, lambda pi, h, ci: (0, 0)),
                  pl.BlockSpec((None, q, None, sub, tc), lambda pi, h, ci: (pi, 0, h, 0, ci))],
        out_specs=pl.BlockSpec((None, None, 2, r // 2, sub, tc), lambda pi, h, ci: (pi, h, 0, 0, 0, ci)),
        out_shape=jax.ShapeDtypeStruct((p, nh, 2, r // 2, sub, c), BF16),
        scratch_shapes=[pltpu.VMEM((tc // LANES, q * sub, LANES), F32),
                        pltpu.VMEM((tc // LANES, r * sub, LANES), F32)],
        compiler_params=_params("parallel", "parallel", "parallel"),
        name=name,
    )(mat, z5)


def _idft_outer_body(m_ref, b_ref, o_ref, bs_ref, ys_ref):
    _, n1, sub, tc = b_ref.shape
    q = m_ref.shape[0]
    nlb = tc // LANES
    for cb in range(nlb):
        bs_ref[cb] = b_ref[:, :, :, cb * LANES:(cb + 1) * LANES].astype(F32).reshape(2 * n1 * sub, LANES)
    for j in range(sub):
        bj = jnp.concatenate([bs_ref[cb, pl.ds(j, 2 * n1, stride=sub), :] for cb in range(nlb)], axis=1)
        res = _dot(m_ref[...], bj.astype(BF16))
        for cb in range(nlb):
            ys_ref[cb, pl.ds(j, q, stride=sub), :] = res[:, cb * LANES:(cb + 1) * LANES]
    for cb in range(nlb):
        o_ref[:, :, cb * LANES:(cb + 1) * LANES] = ys_ref[cb].reshape(q, sub, LANES)


def _idft_outer(mat, b6):
    p, nh, _, n1, sub, c = b6.shape
    q = mat.shape[0]
    tc = FFT_TC
    return pl.pallas_call(
        _idft_outer_body,
        grid=(p, nh, c // tc),
        in_specs=[pl.BlockSpec(mat.shape, lambda pi, h, ci: (0, 0)),
                  pl.BlockSpec((None, None, 2, n1, sub, tc), lambda pi, h, ci: (pi, h, 0, 0, 0, ci))],
        out_specs=pl.BlockSpec((None, q, None, sub, tc), lambda pi, h, ci: (pi, 0, h, 0, ci)),
        out_shape=jax.ShapeDtypeStruct((p, q, nh, sub, c), F32),
        scratch_shapes=[pltpu.VMEM((tc // LANES, 2 * n1 * sub, LANES), F32),
                        pltpu.VMEM((tc // LANES, q * sub, LANES), F32)],
        compiler_params=_params("parallel", "parallel", "parallel"),
        name="hy_idft",
    )(mat, b6)


def _stack_re_im(a_ref):
    nh, _, sub, ct = a_ref.shape
    return jnp.concatenate([a_ref[:, 0].reshape(nh * sub, ct), a_ref[:, 1].reshape(nh * sub, ct)], axis=0)


def _spec_filter_body(g_ref, a_ref, sc_ref, h_ref):
    n2 = g_ref.shape[0] // 2
    spec = _dot(g_ref[...], _stack_re_im(a_ref)) * sc_ref[...]
    h_ref[...] = spec.reshape(2, n2, spec.shape[1])


def _spec_filter(g_fwd, a6, scale):
    _, nh, _, n1, sub, c = a6.shape
    n2 = nh * sub
    return pl.pallas_call(
        _spec_filter_body,
        grid=(n1,),
        in_specs=[pl.BlockSpec((None, 2 * n2, 2 * n2), lambda k: (k, 0, 0)),
                  pl.BlockSpec((None, nh, 2, None, sub, c), lambda k: (0, 0, 0, k, 0, 0)),
                  pl.BlockSpec((1, c), lambda k: (0, 0))],
        out_specs=pl.BlockSpec((None, 2, n2, c), lambda k: (k, 0, 0, 0)),
        out_shape=jax.ShapeDtypeStruct((n1, 2, n2, c), F32),
        compiler_params=_params("parallel"),
        name="hy_spec_filter",
    )(g_fwd, a6, scale)


def _spec_body(gf_ref, gi_ref, h_ref, a_ref, o_ref):
    nh, _, sub, ct = a_ref.shape
    n2 = nh * sub
    spec = _dot(gf_ref[...], _stack_re_im(a_ref))
    xr, xi = spec[:n2], spec[n2:]
    hr, hi = h_ref[0], h_ref[1]
    y = jnp.concatenate([xr * hr - xi * hi, xr * hi + xi * hr], axis=0).astype(BF16)
    back = _dot(gi_ref[...], y)
    o_ref[:, 0] = back[:n2].reshape(nh, sub, ct).astype(BF16)
    o_ref[:, 1] = back[n2:].reshape(nh, sub, ct).astype(BF16)


def _spec(g_fwd, g_inv, hspec, a6):
    p, nh, _, n1, sub, c = a6.shape
    n2 = nh * sub
    blk = pl.BlockSpec((None, nh, 2, None, sub, c), lambda k, pi: (pi, 0, 0, k, 0, 0))
    mat = pl.BlockSpec((None, 2 * n2, 2 * n2), lambda k, pi: (k, 0, 0))
    return pl.pallas_call(
        _spec_body,
        grid=(n1, p),
        in_specs=[mat, mat, pl.BlockSpec((None, 2, n2, c), lambda k, pi: (k, 0, 0, 0)), blk],
        out_specs=blk,
        out_shape=jax.ShapeDtypeStruct(a6.shape, BF16),
        compiler_params=_params("parallel", "parallel"),
        name="hy_spec",
    )(g_fwd, g_inv, hspec, a6)


def _long_conv(vv, h_raw, h_norm):
    b, l, c = vv.shape
    n1, n2 = _fft_split(2 * l)
    nh = n2 // FFT_SUB
    m1_data, m1_filt, m3, g_fwd, g_inv = _fft_tables(n1, n2)
    a_h = _dft_outer(m1_filt, h_raw.reshape(1, n1, nh, FFT_SUB, c), "hy_dft1_filter")
    hspec = _spec_filter(g_fwd, a_h, 1.0 / h_norm)
    a = _dft_outer(m1_data, vv.reshape(b // 2, n1, nh, FFT_SUB, c), "hy_dft1")
    y = _idft_outer(m3, _spec(g_fwd, g_inv, hspec, a))
    return y.reshape(b, l, c)


def _hy_out_body(cv_ref, vv_ref, x0_ref, bd_ref, x_ref, w_ref, o_ref):
    y = ((cv_ref[...] + bd_ref[...] * vv_ref[...]) * x0_ref[...]).astype(BF16)
    o_ref[...] = x_ref[...] + _dot(y, w_ref[...])


def _hy_out(conv, vv, x0, bias_d, x2, w):
    t = x2.shape[0]
    tm = 512
    tok = pl.BlockSpec((tm, D_MODEL), lambda i: (i, 0))
    return pl.pallas_call(
        _hy_out_body,
        grid=(t // tm,),
        in_specs=[tok, tok, tok, pl.BlockSpec((1, D_MODEL), lambda i: (0, 0)), tok, _resident(w.shape)],
        out_specs=tok,
        out_shape=jax.ShapeDtypeStruct((t, D_MODEL), F32),
        compiler_params=_params("parallel"),
        name="hy_out",
    )(conv.reshape(t, D_MODEL), vv.reshape(t, D_MODEL), x0.reshape(t, D_MODEL), bias_d, x2, w)


def _mixer_b(x, g, w_in, conv_w, conv_b, filt, bias_d, w_out):
    b, s, _ = x.shape
    x0, vv = _hy_in(x, g, w_in, conv_w, conv_b)
    h_raw, h_norm = _hy_filter(s, *filt)
    conv = _long_conv(vv, h_raw, h_norm)
    return _hy_out(conv, vv, x0, bias_d, x.reshape(b * s, D_MODEL), w_out).reshape(x.shape)


def _gelu(z):
    return 0.5 * z * (1.0 + lax.erf(z * (2.0 ** -0.5)))


def _sgu_body(x_ref, g_ref, win_ref, lng_ref, lnb_ref, ws_ref, bs_ref, wout_ref, o_ref, y_ref):
    tm = x_ref.shape[0]
    x = x_ref[...]
    xn = _rms(x, g_ref[...]).astype(BF16)
    zu = _gelu(_dot(xn, win_ref[:, :D_MODEL]))
    zv = _gelu(_dot(xn, win_ref[:, D_MODEL:]))
    zc = zv - jnp.mean(zv, axis=-1, keepdims=True)
    zv = zc * lax.rsqrt(jnp.mean(zc * zc, axis=-1, keepdims=True) + EPS) * lng_ref[...] + lnb_ref[...]
    zvb = zv.astype(BF16)
    for c in range(tm // C_CHUNK):
        rows = slice(c * C_CHUNK, (c + 1) * C_CHUNK)
        for h in range(C_GROUPS):
            cols = slice(h * LANES, (h + 1) * LANES)
            sv = _dot(ws_ref[h], zvb[rows, cols]) + bs_ref[:, h:h + 1]
            y_ref[rows, cols] = (zu[rows, cols] * sv).astype(BF16)
    o_ref[...] = x + _dot(y_ref[...], wout_ref[...])


def _mixer_c(x, g, w_in, ln_g, ln_b, w_s, b_s_t, w_out):
    b, s, _ = x.shape
    t = b * s
    tm = 256
    tok = pl.BlockSpec((tm, D_MODEL), lambda i: (i, 0))
    row = pl.BlockSpec((1, D_MODEL), lambda i: (0, 0))
    return pl.pallas_call(
        _sgu_body,
        grid=(t // tm,),
        in_specs=[tok, row, _resident(w_in.shape), row, row, _resident(w_s.shape),
                  pl.BlockSpec(b_s_t.shape, lambda i: (0, 0)), _resident(w_out.shape)],
        out_specs=tok,
        out_shape=jax.ShapeDtypeStruct((t, D_MODEL), F32),
        scratch_shapes=[pltpu.VMEM((tm, D_MODEL), BF16)],
        compiler_params=_params("parallel"),
        name="sgu",
    )(x.reshape(t, D_MODEL), g, w_in, ln_g, ln_b, w_s, b_s_t, w_out).reshape(x.shape)


def _trunk(x, kvs, boff, w):
    b, s, _ = x.shape
    t = b * s
    for i in range(DEPTH):
        kind, j = i % N_MIXERS, i // N_MIXERS
        g = w["g_mix"][i][None, :]
        if kind == 0:
            x = _mixer_a(x, g, w["a_w_in"][j], w["a_w_out"][j])
        elif kind == 1:
            filt = tuple(w[k][j] for k in ("b_f_w1", "b_f_b1", "b_f_w2", "b_f_b2", "b_f_w3", "b_f_b3",
                                          "b_f_wout", "b_f_freq"))
            x = _mixer_b(x, g, w["b_w_in"][j], w["b_conv_w"][j], w["b_conv_b"][j][None, :], filt,
                         w["b_bias_d"][j][None, :], w["b_w_out"][j])
        else:
            x = _mixer_c(x, g, w["c_w_in"][j], w["c_ln_g"][j][None, :], w["c_ln_b"][j][None, :],
                         w["c_w_s"][j], w["c_b_s"][j].T, w["c_w_out"][j])
        x = _xattn(x, w["g_cross"][i][None, :], w["x_w_q"][i], kvs[i], boff, w["x_w_o"][i])
        g_final = w["g_final"][None, :] if i == DEPTH - 1 else None
        x = _ffn(x.reshape(t, D_MODEL), w["g_ffn"][i][None, :], w["f_w_gu"][i], w["f_w_down"][i],
                 g_final).reshape(b, s, D_MODEL)
    return x


_BF16_WEIGHTS = ("a_w_in", "a_w_out", "b_w_in", "b_w_out", "c_w_in", "c_w_s", "c_w_out",
                 "x_w_q", "x_w_kv", "x_w_o", "f_w_gu", "f_w_down")


def kernel(x_prompt, x_sample, mem_prompt, mem_sample, g_mix, g_cross, g_ffn, g_final, a_w_in, a_w_out, b_w_in, b_conv_w, b_conv_b, b_f_w1, b_f_b1, b_f_w2, b_f_b2, b_f_w3, b_f_b3, b_f_wout, b_f_freq, b_bias_d, b_w_out, c_w_in, c_ln_g, c_ln_b, c_w_s, c_b_s, c_w_out, x_w_q, x_w_kv, x_w_o, f_w_gu, f_w_down):
    w = dict(g_mix=g_mix, g_cross=g_cross, g_ffn=g_ffn, g_final=g_final,
             a_w_in=a_w_in, a_w_out=a_w_out, b_w_in=b_w_in, b_conv_w=b_conv_w, b_conv_b=b_conv_b,
             b_f_w1=b_f_w1, b_f_b1=b_f_b1, b_f_w2=b_f_w2, b_f_b2=b_f_b2, b_f_w3=b_f_w3,
             b_f_b3=b_f_b3, b_f_wout=b_f_wout, b_f_freq=b_f_freq, b_bias_d=b_bias_d,
             b_w_out=b_w_out, c_w_in=c_w_in, c_ln_g=c_ln_g, c_ln_b=c_ln_b, c_w_s=c_w_s,
             c_b_s=c_b_s, c_w_out=c_w_out, x_w_q=x_w_q, x_w_kv=x_w_kv, x_w_o=x_w_o,
             f_w_gu=f_w_gu, f_w_down=f_w_down)
    for name in _BF16_WEIGHTS:
        w[name] = w[name].astype(BF16)
    nb_prompt = mem_prompt.shape[0]
    mem = jnp.concatenate([mem_prompt, mem_sample], axis=0)
    mem2 = mem.reshape(mem.shape[0] * MEM_LEN, D_MODEL)
    kvs = [_kv_proj(mem2, w["x_w_kv"][i]).reshape(mem.shape[0], MEM_LEN, 2 * D_MODEL) for i in range(DEPTH)]
    y_prompt = _trunk(x_prompt, kvs, 0, w)
    y_sample = _trunk(x_sample, kvs, nb_prompt, w)
    return (y_prompt, y_sample)
```

```python
import functools
import math

import numpy as np
import jax
import jax.numpy as jnp
from jax import lax
from jax.experimental import pallas as pl
from jax.experimental.pallas import tpu as pltpu

F32 = jnp.float32
BF16 = jnp.bfloat16

D_MODEL = 1024
DEPTH = 4
N_MIXERS = 3

A_GROUPS = ((128, 1), (512, 4), (2048, 16))
A_N_GROUPS = len(A_GROUPS)
A_HEADS = 16
A_HEAD_DIM = D_MODEL // A_HEADS
A_HALF = 64
A_SUBTILE = 128
A_Q_SCALE = A_HEAD_DIM ** -0.5 * math.log2(math.e)
ROPE_THETA = 10000.0

HY_EMB = 33
HY_BANDS = (HY_EMB - 1) // 2
HY_HID_PAD = 128
HY_DECAY_TARGET = 1e-2
HY_DECAY_STRONG_PCT = 0.3
HY_DECAY_WEAK_PCT = 1.5

C_CHUNK = 128
C_GROUPS = 8

MEM_LEN = 256
X_HEADS = 4
X_HEAD_DIM = D_MODEL // X_HEADS

D_FF = -(-8 * D_MODEL // (3 * 256)) * 256
FF_CHUNK = 256

EPS = 1e-6
NEG_INF = -1e30

LANES = 128
VMEM_LIMIT = 56 * 1024 * 1024


def _params(*sem):
    return pltpu.CompilerParams(dimension_semantics=sem, vmem_limit_bytes=VMEM_LIMIT)


def _resident(shape):
    nd = len(shape)
    return pl.BlockSpec(shape, lambda *_: (0,) * nd, pipeline_mode=pl.Buffered(1))


def _dot(a, b):
    return jnp.dot(a, b, preferred_element_type=F32)


def _dot_nt(a, b):
    return lax.dot_general(a, b, (((1,), (1,)), ((), ())), preferred_element_type=F32)


def _rms(x, g):
    return x * lax.rsqrt(jnp.mean(x * x, axis=-1, keepdims=True) + EPS) * g


def _ffn_body(*refs, final):
    if final:
        x_ref, g_ref, wgu_ref, wd_ref, gf_ref, o_ref, act_ref = refs
    else:
        x_ref, g_ref, wgu_ref, wd_ref, o_ref, act_ref = refs
    x = x_ref[...]
    xn = _rms(x, g_ref[...]).astype(BF16)
    for c in range(D_FF // FF_CHUNK):
        lo = c * FF_CHUNK
        gate = _dot(xn, wgu_ref[:, lo:lo + FF_CHUNK])
        up = _dot(xn, wgu_ref[:, D_FF + lo:D_FF + lo + FF_CHUNK])
        act_ref[:, lo:lo + FF_CHUNK] = (gate / (1.0 + jnp.exp(-gate)) * up).astype(BF16)
    y = x + _dot(act_ref[...], wd_ref[...])
    if final:
        y = _rms(y, gf_ref[...])
    o_ref[...] = y


def _ffn(x2, g, wgu, wd, g_final=None):
    t = x2.shape[0]
    tm = 512
    final = g_final is not None
    tok = pl.BlockSpec((tm, D_MODEL), lambda i: (i, 0))
    row = pl.BlockSpec((1, D_MODEL), lambda i: (0, 0))
    in_specs = [tok, row, _resident(wgu.shape), _resident(wd.shape)]
    args = [x2, g, wgu, wd]
    if final:
        in_specs.append(row)
        args.append(g_final)
    return pl.pallas_call(
        functools.partial(_ffn_body, final=final),
        grid=(t // tm,),
        in_specs=in_specs,
        out_specs=tok,
        out_shape=jax.ShapeDtypeStruct((t, D_MODEL), F32),
        scratch_shapes=[pltpu.VMEM((tm, D_FF), BF16)],
        compiler_params=_params("parallel"),
        name="ffn_final" if final else "ffn",
    )(*args)


def _kv_body(m_ref, w_ref, o_ref):
    o_ref[...] = _dot(m_ref[...].astype(BF16), w_ref[...]).astype(BF16)


def _kv_proj(mem2, wkv):
    r = mem2.shape[0]
    tm = 256
    return pl.pallas_call(
        _kv_body,
        grid=(r // tm,),
        in_specs=[pl.BlockSpec((tm, D_MODEL), lambda i: (i, 0)), _resident(wkv.shape)],
        out_specs=pl.BlockSpec((tm, 2 * D_MODEL), lambda i: (i, 0)),
        out_shape=jax.ShapeDtypeStruct((r, 2 * D_MODEL), BF16),
        compiler_params=_params("parallel"),
        name="kv_proj",
    )(mem2, wkv)


def _xattn_body(x_ref, g_ref, wq_ref, k_ref, v_ref, wo_ref, o_ref, y_ref):
    x = x_ref[...]
    xn = _rms(x, g_ref[...]).astype(BF16)
    q = (_dot(xn, wq_ref[...]) * (X_HEAD_DIM ** -0.5)).astype(BF16)
    for h in range(X_HEADS):
        sl = slice(h * X_HEAD_DIM, (h + 1) * X_HEAD_DIM)
        s = _dot_nt(q[:, sl], k_ref[:, sl])
        p = jnp.exp(s - jnp.max(s, axis=-1, keepdims=True))
        inv = 1.0 / jnp.sum(p, axis=-1, keepdims=True)
        y_ref[:, sl] = (_dot(p.astype(BF16), v_ref[:, sl]) * inv).astype(BF16)
    o_ref[...] = x + _dot(y_ref[...], wo_ref[...])


def _xattn(x, g, wq, kv, boff, wo):
    b, s, _ = x.shape
    tm = 512
    tok = pl.BlockSpec((None, tm, D_MODEL), lambda bi, i: (bi, i, 0))
    return pl.pallas_call(
        _xattn_body,
        grid=(b, s // tm),
        in_specs=[
            tok,
            pl.BlockSpec((1, D_MODEL), lambda bi, i: (0, 0)),
            _resident(wq.shape),
            pl.BlockSpec((None, MEM_LEN, D_MODEL), lambda bi, i: (bi + boff, 0, 0)),
            pl.BlockSpec((None, MEM_LEN, D_MODEL), lambda bi, i: (bi + boff, 0, 1)),
            _resident(wo.shape),
        ],
        out_specs=tok,
        out_shape=jax.ShapeDtypeStruct(x.shape, F32),
        scratch_shapes=[pltpu.VMEM((tm, D_MODEL), BF16)],
        compiler_params=_params("parallel", "parallel"),
        name="xattn",
    )(x, g, wq, kv, kv, wo)


def _a_in_body(x_ref, g_ref, w_ref, cos_ref, sin_ref, *rest):
    o_refs, (xs_ref, xp_ref) = rest[:A_N_GROUPS], rest[A_N_GROUPS:]
    tm = x_ref.shape[0]
    xn_f32 = _rms(x_ref[...], g_ref[...])
    reps = D_MODEL // LANES
    for cb in range(reps):
        xs_ref[cb] = xn_f32[:, cb * LANES:(cb + 1) * LANES]
    lane = lax.broadcasted_iota(jnp.int32, (tm, D_MODEL), 1)
    low_half = (lane & (A_HEAD_DIM // 2)) == 0
    for gidx, (o_ref, (_, dil)) in enumerate(zip(o_refs, A_GROUPS)):
        rows = tm // dil
        if dil == 1:
            xn = xn_f32.astype(BF16)
            cos, sin = cos_ref[...], sin_ref[...]
        else:
            for r in range(dil):
                for cb in range(reps):
                    xp_ref[r * rows:(r + 1) * rows, cb * LANES:(cb + 1) * LANES] = (
                        xs_ref[cb, pl.ds(r, rows, stride=dil), :].astype(BF16))
            xn = xp_ref[...]
            cos = jnp.concatenate([cos_ref[pl.ds(r, rows, stride=dil), :] for r in range(dil)], axis=0)
            sin = jnp.concatenate([sin_ref[pl.ds(r, rows, stride=dil), :] for r in range(dil)], axis=0)
        cos_t = jnp.concatenate([cos] * reps, axis=1)
        sin_t = jnp.concatenate([sin] * reps, axis=1)
        for part in range(3):
            lo = (gidx * 3 + part) * D_MODEL
            y = _dot(xn, w_ref[:, lo:lo + D_MODEL])
            if part < 2:
                if part == 0:
                    y = y * A_Q_SCALE
                partner = jnp.where(low_half,
                                    pltpu.roll(y, D_MODEL - A_HEAD_DIM // 2, 1),
                                    pltpu.roll(y, A_HEAD_DIM // 2, 1))
                y = y * cos_t + partner * sin_t
            o_ref[:, :, part * D_MODEL:(part + 1) * D_MODEL] = y.reshape(dil, rows, D_MODEL).astype(BF16)


def _a_in(x, g, w, cos_t, sin_t):
    b, s, _ = x.shape
    tm = 256
    return pl.pallas_call(
        _a_in_body,
        grid=(b, s // tm),
        in_specs=[
            pl.BlockSpec((None, tm, D_MODEL), lambda bi, i: (bi, i, 0)),
            pl.BlockSpec((1, D_MODEL), lambda bi, i: (0, 0)),
            _resident(w.shape),
            pl.BlockSpec((tm, LANES), lambda bi, i: (i, 0)),
            pl.BlockSpec((tm, LANES), lambda bi, i: (i, 0)),
        ],
        out_specs=[pl.BlockSpec((None, dil, tm // dil, 3 * D_MODEL), lambda bi, i: (bi, 0, i, 0))
                   for _, dil in A_GROUPS],
        out_shape=[jax.ShapeDtypeStruct((b, dil, s // dil, 3 * D_MODEL), BF16) for _, dil in A_GROUPS],
        scratch_shapes=[pltpu.VMEM((D_MODEL // LANES, tm, LANES), F32), pltpu.VMEM((tm, D_MODEL), BF16)],
        compiler_params=_params("parallel", "parallel"),
        name="a_in",
    )(x, g, w, cos_t, sin_t)


def _attn_body(q_ref, kp_ref, k_ref, kn_ref, vp_ref, v_ref, vn_ref, o_ref, lse_ref, *, n_tiles):
    tq = q_ref.shape[0]
    sq = A_SUBTILE
    win = sq + 2 * A_HALF
    n_sub_tiles = tq // sq
    i = pl.program_id(2)
    kw = jnp.concatenate([kp_ref[...], k_ref[...], kn_ref[...]], axis=0)
    vw = jnp.concatenate([vp_ref[...], v_ref[...], vn_ref[...]], axis=0)
    qi = lax.broadcasted_iota(jnp.int32, (sq, win), 0)
    kj = lax.broadcasted_iota(jnp.int32, (sq, win), 1)
    band = (kj >= qi) & (kj <= qi + 2 * A_HALF)
    left = lax.broadcasted_iota(jnp.int32, (sq, LANES), 1) < A_HEAD_DIM
    left_kv = lax.broadcasted_iota(jnp.int32, (win, LANES), 1) < A_HEAD_DIM
    lane = lax.broadcasted_iota(jnp.int32, (sq, LANES), 1)
    for st in range(n_sub_tiles):
        rows = slice(st * sq, (st + 1) * sq)
        mask = band
        if st == 0:
            mask = mask & (kj >= jnp.where(i == 0, A_HALF, 0))
        if st == n_sub_tiles - 1:
            mask = mask & (kj < jnp.where(i == n_tiles - 1, sq + A_HALF, win))
        lse_all = jnp.zeros((sq, LANES), F32)
        for hp in range(A_HEADS // 2):
            sl = slice(hp * LANES, (hp + 1) * LANES)
            q2 = q_ref[rows, sl]
            k2 = kw[st * sq:st * sq + win, sl]
            v2 = vw[st * sq:st * sq + win, sl]
            halves = []
            for side in range(2):
                keep = left if side == 0 else jnp.logical_not(left)
                s = _dot_nt(jnp.where(keep, q2, jnp.zeros_like(q2)), k2)
                s = jnp.where(mask, s, NEG_INF)
                m = jnp.max(s, axis=-1, keepdims=True)
                p = jnp.exp2(s - m)
                l = jnp.sum(p, axis=-1, keepdims=True)
                halves.append(_dot(p.astype(BF16), v2) * (1.0 / l))
                lse_all = jnp.where(lane == 2 * hp + side, m + jnp.log2(l), lse_all)
            o_ref[rows, sl] = jnp.where(left, halves[0], halves[1]).astype(BF16)
        lse_ref[rows, :] = lse_all * math.log(2.0)


def _attn_group(qkv):
    b, dil, n_sub, _ = qkv.shape
    tq = min(2 * A_SUBTILE, n_sub)
    n_tiles = n_sub // tq
    halo_per_tile = tq // A_HALF
    n_halo = n_sub // A_HALF

    def own(part):
        return pl.BlockSpec((None, None, tq, D_MODEL), lambda bi, r, i: (bi, r, i, part))

    def prev(part):
        return pl.BlockSpec((None, None, A_HALF, D_MODEL),
                            lambda bi, r, i: (bi, r, jnp.maximum(i * halo_per_tile - 1, 0), part))

    def nxt(part):
        return pl.BlockSpec((None, None, A_HALF, D_MODEL),
                            lambda bi, r, i: (bi, r, jnp.minimum((i + 1) * halo_per_tile, n_halo - 1), part))

    return pl.pallas_call(
        functools.partial(_attn_body, n_tiles=n_tiles),
        grid=(b, dil, n_tiles),
        in_specs=[own(0), prev(1), own(1), nxt(1), prev(2), own(2), nxt(2)],
        out_specs=[
            pl.BlockSpec((None, None, tq, D_MODEL), lambda bi, r, i: (bi, r, i, 0)),
            pl.BlockSpec((None, None, tq, LANES), lambda bi, r, i: (bi, r, i, 0)),
        ],
        out_shape=[
            jax.ShapeDtypeStruct((b, dil, n_sub, D_MODEL), BF16),
            jax.ShapeDtypeStruct((b, dil, n_sub, LANES), F32),
        ],
        compiler_params=_params("parallel", "parallel", "parallel"),
        name=f"attn_d{dil}",
    )(qkv, qkv, qkv, qkv, qkv, qkv, qkv)


def _a_out_body(o1_ref, o2_ref, o3_ref, l1_ref, l2_ref, l3_ref, x_ref, w_ref, sp_ref, out_ref,
                os2_ref, os3_ref, ls2_ref, ls3_ref, y_ref):
    tm = x_ref.shape[0]
    for o_ref, l_ref, os_ref, ls_ref in ((o2_ref, l2_ref, os2_ref, ls2_ref), (o3_ref, l3_ref, os3_ref, ls3_ref)):
        dil = o_ref.shape[0]
        rows = tm // dil
        for r in range(dil):
            for cb in range(D_MODEL // LANES):
                os_ref[cb, pl.ds(r, rows, stride=dil), :] = o_ref[r, :, cb * LANES:(cb + 1) * LANES].astype(F32)
            ls_ref[pl.ds(r, rows, stride=dil), :] = l_ref[r]
    l1, l2, l3 = l1_ref[0], ls2_ref[...], ls3_ref[...]
    m = jnp.maximum(jnp.maximum(l1, l2), l3)
    e1, e2, e3 = jnp.exp(l1 - m), jnp.exp(l2 - m), jnp.exp(l3 - m)
    inv = 1.0 / (e1 + e2 + e3)
    def spread(e):
        wgt = e * inv
        hi = wgt.astype(BF16)
        lo = (wgt - hi.astype(F32)).astype(BF16)
        return _dot(hi, sp_ref[...]) + _dot(lo, sp_ref[...])

    w1, w2, w3 = spread(e1), spread(e2), spread(e3)
    for cb in range(D_MODEL // LANES):
        sl = slice(cb * LANES, (cb + 1) * LANES)
        y = w1[:, sl] * o1_ref[0, :, sl].astype(F32) + w2[:, sl] * os2_ref[cb] + w3[:, sl] * os3_ref[cb]
        y_ref[:, sl] = y.astype(BF16)
    out_ref[...] = x_ref[...] + _dot(y_ref[...], w_ref[...])


def _a_out(outs, lses, x, w):
    b, s, _ = x.shape
    tm = 512
    tok = pl.BlockSpec((None, tm, D_MODEL), lambda bi, i: (bi, i, 0))

    def grouped(arr):
        dil, width = arr.shape[1], arr.shape[3]
        return pl.BlockSpec((None, dil, tm // dil, width), lambda bi, i: (bi, 0, i, 0))

    head_of_lane = np.arange(D_MODEL) // A_HEAD_DIM
    spread = jnp.asarray(np.arange(LANES)[:, None] == head_of_lane[None, :], dtype=BF16)
    return pl.pallas_call(
        _a_out_body,
        grid=(b, s // tm),
        in_specs=[grouped(a) for a in outs] + [grouped(a) for a in lses]
                 + [tok, _resident(w.shape), _resident(spread.shape)],
        out_specs=tok,
        out_shape=jax.ShapeDtypeStruct(x.shape, F32),
        scratch_shapes=[pltpu.VMEM((D_MODEL // LANES, tm, LANES), F32), pltpu.VMEM((D_MODEL // LANES, tm, LANES), F32),
                        pltpu.VMEM((tm, LANES), F32), pltpu.VMEM((tm, LANES), F32),
                        pltpu.VMEM((tm, D_MODEL), BF16)],
        compiler_params=_params("parallel", "parallel"),
        name="a_out",
    )(*outs, *lses, x, w, spread)


def _rope_tables(seq_len):
    inv = ROPE_THETA ** (-jnp.arange(0, A_HEAD_DIM, 2, dtype=F32) / A_HEAD_DIM)
    ang = jnp.arange(seq_len, dtype=F32)[:, None] * inv[None, :]
    cos, sin = jnp.cos(ang), jnp.sin(ang)
    reps = LANES // A_HEAD_DIM
    return (jnp.concatenate([cos, cos] * reps, axis=1),
            jnp.concatenate([-sin, sin] * reps, axis=1))


def _mixer_a(x, g, w_in, w_out):
    b, s, _ = x.shape
    cos_t, sin_t = _rope_tables(s)
    outs, lses = [], []
    for qkv in _a_in(x, g, w_in, cos_t, sin_t):
        o, l = _attn_group(qkv)
        outs.append(o)
        lses.append(l)
    return _a_out(outs, lses, x, w_out)


HALO = 8


def _hy_in_body(xp_ref, x_ref, xn_ref, g_ref, w_ref, cw_ref, cb_ref, x0_ref, vv_ref, u_ref, *, n_tiles):
    tm = x_ref.shape[0]
    i = pl.program_id(1)
    xe = jnp.concatenate([xp_ref[...], x_ref[...], xn_ref[...]], axis=0)
    xn = _rms(xe, g_ref[...]).astype(BF16)
    row = lax.broadcasted_iota(jnp.int32, (tm + 2 * HALO, 1), 0)
    inside = ((row >= HALO) | (i > 0)) & ((row < tm + HALO) | (i < n_tiles - 1))
    parts = []
    for c in range(3):
        lo = c * D_MODEL
        u_ref[...] = jnp.where(inside, _dot(xn, w_ref[:, lo:lo + D_MODEL]), 0.0)
        conv = (u_ref[pl.ds(HALO - 1, tm), :] * cw_ref[0:1, lo:lo + D_MODEL]
                + u_ref[pl.ds(HALO, tm), :] * cw_ref[1:2, lo:lo + D_MODEL]
                + u_ref[pl.ds(HALO + 1, tm), :] * cw_ref[2:3, lo:lo + D_MODEL]
                + cb_ref[:, lo:lo + D_MODEL])
        parts.append(conv)
    x0_ref[...] = parts[0]
    vv_ref[...] = parts[2] * parts[1]


def _hy_in(x, g, w, conv_w, conv_b):
    b, s, _ = x.shape
    tm = 512
    n_tiles = s // tm
    per = tm // HALO
    n_halo = s // HALO
    tok = pl.BlockSpec((None, tm, D_MODEL), lambda bi, i: (bi, i, 0))
    return pl.pallas_call(
        functools.partial(_hy_in_body, n_tiles=n_tiles),
        grid=(b, n_tiles),
        in_specs=[
            pl.BlockSpec((None, HALO, D_MODEL), lambda bi, i: (bi, jnp.maximum(i * per - 1, 0), 0)),
            tok,
            pl.BlockSpec((None, HALO, D_MODEL), lambda bi, i: (bi, jnp.minimum((i + 1) * per, n_halo - 1), 0)),
            pl.BlockSpec((1, D_MODEL), lambda bi, i: (0, 0)),
            _resident(w.shape),
            pl.BlockSpec(conv_w.shape, lambda bi, i: (0, 0)),
            pl.BlockSpec(conv_b.shape, lambda bi, i: (0, 0)),
        ],
        out_specs=[tok, tok],
        out_shape=[jax.ShapeDtypeStruct(x.shape, F32), jax.ShapeDtypeStruct(x.shape, F32)],
        scratch_shapes=[pltpu.VMEM((tm + 2 * HALO, D_MODEL), F32)],
        compiler_params=_params("parallel", "parallel"),
        name="hy_in",
    )(x, x, x, g, w, conv_w, conv_b)


def _hdot(a, b):
    return jnp.dot(a, b, precision=lax.Precision.HIGHEST, preferred_element_type=F32)


def _hy_filter_body(z_ref, t_ref, a_ref, b_ref, w1_ref, b1_ref, w2_ref, b2_ref, w3_ref, b3_ref,
                    wo_ref, fr_ref, dl_ref, h_ref, sum_ref):
    fr = fr_ref[...]
    hid = jnp.sin(fr * (_hdot(z_ref[...], w1_ref[...]) + b1_ref[...]))
    hid = jnp.sin(fr * (_hdot(hid, w2_ref[...]) + b2_ref[...]))
    hid = jnp.sin(fr * (_hdot(hid, w3_ref[...]) + b3_ref[...]))
    decay = jnp.exp(-t_ref[...] * dl_ref[...])
    h_fwd = _hdot(hid, wo_ref[:, :D_MODEL]) * decay
    h_bwd = _hdot(hid, wo_ref[:, D_MODEL:]) * decay
    h = a_ref[...] * h_fwd + b_ref[...] * h_bwd
    h_ref[...] = h

    @pl.when(pl.program_id(0) == 0)
    def _():
        sum_ref[...] = jnp.zeros_like(sum_ref)

    sum_ref[...] += jnp.sum(jnp.abs(h), axis=0, keepdims=True)


def _hy_filter(seq_len, f_w1, f_b1, f_w2, f_b2, f_w3, f_b3, f_wout, f_freq):
    n = 2 * seq_len
    t = jnp.linspace(0.0, 1.0, seq_len, dtype=F32)[:, None]
    w = 2.0 * math.pi * jnp.arange(seq_len, dtype=F32)[:, None] / seq_len
    f = jnp.linspace(1e-4, HY_BANDS - 1, HY_BANDS, dtype=F32)[None, :]
    z = jnp.concatenate([t, jnp.cos(f * w), -jnp.sin(f * w)], axis=-1)
    src = np.concatenate([np.arange(seq_len), [0], np.arange(seq_len - 1, 0, -1)])
    pos = np.arange(n)
    use_fwd = (pos < seq_len).astype(np.float32)[:, None]
    use_bwd = ((pos == 0) | (pos > seq_len)).astype(np.float32)[:, None]
    z2 = jnp.pad(z[src], ((0, 0), (0, HY_HID_PAD - HY_EMB)))
    t2 = t[src]
    max_decay = math.log(HY_DECAY_TARGET) / HY_DECAY_STRONG_PCT
    min_decay = math.log(HY_DECAY_TARGET) / HY_DECAY_WEAK_PCT
    deltas = jnp.abs(jnp.linspace(min_decay, max_decay, D_MODEL, dtype=F32))[None, :]

    def pad2(m, rows):
        return jnp.pad(m, ((0, rows - m.shape[0]), (0, HY_HID_PAD - m.shape[1])))

    def padv(v):
        return jnp.pad(v, (0, HY_HID_PAD - v.shape[0]))[None, :]

    wo = jnp.pad(f_wout, ((0, HY_HID_PAD - f_wout.shape[0]), (0, 0)))
    tr = 512
    rowblk = lambda width: pl.BlockSpec((tr, width), lambda i: (i, 0))
    full = lambda shape: pl.BlockSpec(shape, lambda i: (0, 0))
    sq = (HY_HID_PAD, HY_HID_PAD)
    vec = (1, HY_HID_PAD)
    return pl.pallas_call(
        _hy_filter_body,
        grid=(n // tr,),
        in_specs=[rowblk(HY_HID_PAD), rowblk(1), rowblk(1), rowblk(1),
                  full(sq), full(vec), full(sq), full(vec), full(sq), full(vec),
                  full(wo.shape), full(vec), full((1, D_MODEL))],
        out_specs=[rowblk(D_MODEL), full((1, D_MODEL))],
        out_shape=[jax.ShapeDtypeStruct((n, D_MODEL), F32), jax.ShapeDtypeStruct((1, D_MODEL), F32)],
        compiler_params=_params("arbitrary"),
        name="hy_filter",
    )(z2, t2, jnp.asarray(use_fwd), jnp.asarray(use_bwd),
      pad2(f_w1, HY_HID_PAD), padv(f_b1), pad2(f_w2, HY_HID_PAD), padv(f_b2),
      pad2(f_w3, HY_HID_PAD), padv(f_b3), wo, padv(f_freq), deltas)


def _fft_split(n):
    n1 = 1 << ((n.bit_length() - 1 + 1) // 2)
    return n1, n // n1


def _fft_tables(n1, n2):
    n = n1 * n2
    h = n1 // 2
    idx = np.arange(n1)
    ang = -2.0 * np.pi * ((idx[:, None] * idx[None, :]) % n1) / n1
    fr, fi = np.cos(ang), np.sin(ang)
    m1_data = np.block([[fr[:, :h], -fi[:, :h]], [fi[:, :h], fr[:, :h]]])
    m1_filt = np.concatenate([fr, fi], axis=0)
    ifr, ifi = fr.T[:h] / n, -fi.T[:h] / n
    m3 = np.block([[ifr, -ifi], [ifi, ifr]])
    k1 = jnp.arange(n1, dtype=jnp.int32)[:, None, None]
    k2 = jnp.arange(n2, dtype=jnp.int32)[None, :, None]
    i2 = jnp.arange(n2, dtype=jnp.int32)[None, None, :]
    phase = (i2 * k1 + n1 * i2 * k2) % n
    ga = (-2.0 * math.pi / n) * phase.astype(F32)
    gr, gi = jnp.cos(ga), jnp.sin(ga)
    g_fwd = jnp.concatenate([jnp.concatenate([gr, -gi], axis=2), jnp.concatenate([gi, gr], axis=2)], axis=1)
    grt, git = jnp.swapaxes(gr, 1, 2), jnp.swapaxes(gi, 1, 2)
    g_inv = jnp.concatenate([jnp.concatenate([grt, git], axis=2), jnp.concatenate([-git, grt], axis=2)], axis=1)
    as_bf = lambda m: jnp.asarray(m, dtype=F32).astype(BF16)
    return as_bf(m1_data), as_bf(m1_filt), as_bf(m3), g_fwd.astype(BF16), g_inv.astype(BF16)


FFT_SUB = 16
FFT_TC = 512


def _dft_outer_body(m_ref, z_ref, o_ref, zs_ref, rs_ref):
    q, sub, tc = z_ref.shape
    r = m_ref.shape[0]
    nlb = tc // LANES
    for cb in range(nlb):
        zs_ref[cb] = z_ref[:, :, cb * LANES:(cb + 1) * LANES].reshape(q * sub, LANES)
    for j in range(sub):
        z = jnp.concatenate([zs_ref[cb, pl.ds(j, q, stride=sub), :] for cb in range(nlb)], axis=1)
        res = _dot(m_ref[...], z.astype(BF16))
        for cb in range(nlb):
            rs_ref[cb, pl.ds(j, r, stride=sub), :] = res[:, cb * LANES:(cb + 1) * LANES]
    for cb in range(nlb):
        o_ref[:, :, :, cb * LANES:(cb + 1) * LANES] = rs_ref[cb].reshape(2, r // 2, sub, LANES).astype(BF16)


def _dft_outer(mat, z5, name):
    p, q, nh, sub, c = z5.shape
    r = mat.shape[0]
    tc = FFT_TC
    return pl.pallas_call(
        _dft_outer_body,
        grid=(p, nh, c // tc),
        in_specs=[pl.BlockSpec(mat.shape, lambda pi, h, ci: (0, 0)),
                  pl.BlockSpec((None, q, None, sub, tc), lambda pi, h, ci: (pi, 0, h, 0, ci))],
        out_specs=pl.BlockSpec((None, None, 2, r // 2, sub, tc), lambda pi, h, ci: (pi, h, 0, 0, 0, ci)),
        out_shape=jax.ShapeDtypeStruct((p, nh, 2, r // 2, sub, c), BF16),
        scratch_shapes=[pltpu.VMEM((tc // LANES, q * sub, LANES), F32),
                        pltpu.VMEM((tc // LANES, r * sub, LANES), F32)],
        compiler_params=_params("parallel", "parallel", "parallel"),
        name=name,
    )(mat, z5)


def _idft_outer_body(m_ref, b_ref, o_ref, bs_ref, ys_ref):
    _, n1, sub, tc = b_ref.shape
    q = m_ref.shape[0]
    nlb = tc // LANES
    for cb in range(nlb):
        bs_ref[cb] = b_ref[:, :, :, cb * LANES:(cb + 1) * LANES].astype(F32).reshape(2 * n1 * sub, LANES)
    for j in range(sub):
        bj = jnp.concatenate([bs_ref[cb, pl.ds(j, 2 * n1, stride=sub), :] for cb in range(nlb)], axis=1)
        res = _dot(m_ref[...], bj.astype(BF16))
        for cb in range(nlb):
            ys_ref[cb, pl.ds(j, q, stride=sub), :] = res[:, cb * LANES:(cb + 1) * LANES]
    for cb in range(nlb):
        o_ref[:, :, cb * LANES:(cb + 1) * LANES] = ys_ref[cb].reshape(q, sub, LANES)


def _idft_outer(mat, b6):
    p, nh, _, n1, sub, c = b6.shape
    q = mat.shape[0]
    tc = FFT_TC
    return pl.pallas_call(
        _idft_outer_body,
        grid=(p, nh, c // tc),
        in_specs=[pl.BlockSpec(mat.shape, lambda pi, h, ci: (0, 0)),
                  pl.BlockSpec((None, None, 2, n1, sub, tc), lambda pi, h, ci: (pi, h, 0, 0, 0, ci))],
        out_specs=pl.BlockSpec((None, q, None, sub, tc), lambda pi, h, ci: (pi, 0, h, 0, ci)),
        out_shape=jax.ShapeDtypeStruct((p, q, nh, sub, c), F32),
        scratch_shapes=[pltpu.VMEM((tc // LANES, 2 * n1 * sub, LANES), F32),
                        pltpu.VMEM((tc // LANES, q * sub, LANES), F32)],
        compiler_params=_params("parallel", "parallel", "parallel"),
        name="hy_idft",
    )(mat, b6)


FFT_KB = 4


def _stack_re_im(a_ref, kk):
    nh, _, _, sub, ct = a_ref.shape
    return jnp.concatenate([a_ref[:, 0, kk].reshape(nh * sub, ct), a_ref[:, 1, kk].reshape(nh * sub, ct)], axis=0)


def _spec_filter_body(g_ref, a_ref, sc_ref, h_ref):
    n2 = g_ref.shape[1] // 2
    for kk in range(g_ref.shape[0]):
        spec = _dot(g_ref[kk], _stack_re_im(a_ref, kk)) * sc_ref[...]
        h_ref[kk] = spec.reshape(2, n2, spec.shape[1])


def _spec_filter(g_fwd, a6, scale):
    _, nh, _, n1, sub, c = a6.shape
    n2 = nh * sub
    return pl.pallas_call(
        _spec_filter_body,
        grid=(n1 // FFT_KB,),
        in_specs=[pl.BlockSpec((FFT_KB, 2 * n2, 2 * n2), lambda k: (k, 0, 0)),
                  pl.BlockSpec((None, nh, 2, FFT_KB, sub, c), lambda k: (0, 0, 0, k, 0, 0)),
                  pl.BlockSpec((1, c), lambda k: (0, 0))],
        out_specs=pl.BlockSpec((FFT_KB, 2, n2, c), lambda k: (k, 0, 0, 0)),
        out_shape=jax.ShapeDtypeStruct((n1, 2, n2, c), F32),
        compiler_params=_params("parallel"),
        name="hy_spec_filter",
    )(g_fwd, a6, scale)


def _spec_body(gf_ref, gi_ref, h_ref, a_ref, o_ref):
    nh, _, kb, sub, ct = a_ref.shape
    n2 = nh * sub
    for kk in range(kb):
        spec = _dot(gf_ref[kk], _stack_re_im(a_ref, kk))
        xr, xi = spec[:n2], spec[n2:]
        hr, hi = h_ref[kk, 0], h_ref[kk, 1]
        y = jnp.concatenate([xr * hr - xi * hi, xr * hi + xi * hr], axis=0).astype(BF16)
        back = _dot(gi_ref[kk], y)
        o_ref[:, 0, kk] = back[:n2].reshape(nh, sub, ct).astype(BF16)
        o_ref[:, 1, kk] = back[n2:].reshape(nh, sub, ct).astype(BF16)


def _spec(g_fwd, g_inv, hspec, a6):
    p, nh, _, n1, sub, c = a6.shape
    n2 = nh * sub
    blk = pl.BlockSpec((None, nh, 2, FFT_KB, sub, c), lambda k, pi: (pi, 0, 0, k, 0, 0))
    mat = pl.BlockSpec((FFT_KB, 2 * n2, 2 * n2), lambda k, pi: (k, 0, 0))
    return pl.pallas_call(
        _spec_body,
        grid=(n1 // FFT_KB, p),
        in_specs=[mat, mat, pl.BlockSpec((FFT_KB, 2, n2, c), lambda k, pi: (k, 0, 0, 0)), blk],
        out_specs=blk,
        out_shape=jax.ShapeDtypeStruct(a6.shape, BF16),
        compiler_params=_params("parallel", "parallel"),
        name="hy_spec",
    )(g_fwd, g_inv, hspec, a6)


def _long_conv(vv, h_raw, h_norm):
    b, l, c = vv.shape
    n1, n2 = _fft_split(2 * l)
    nh = n2 // FFT_SUB
    m1_data, m1_filt, m3, g_fwd, g_inv = _fft_tables(n1, n2)
    a_h = _dft_outer(m1_filt, h_raw.reshape(1, n1, nh, FFT_SUB, c), "hy_dft1_filter")
    hspec = _spec_filter(g_fwd, a_h, 1.0 / h_norm)
    a = _dft_outer(m1_data, vv.reshape(b // 2, n1, nh, FFT_SUB, c), "hy_dft1")
    y = _idft_outer(m3, _spec(g_fwd, g_inv, hspec, a))
    return y.reshape(b, l, c)


def _hy_out_body(cv_ref, vv_ref, x0_ref, bd_ref, x_ref, w_ref, o_ref):
    y = ((cv_ref[...] + bd_ref[...] * vv_ref[...]) * x0_ref[...]).astype(BF16)
    o_ref[...] = x_ref[...] + _dot(y, w_ref[...])


def _hy_out(conv, vv, x0, bias_d, x2, w):
    t = x2.shape[0]
    tm = 512
    tok = pl.BlockSpec((tm, D_MODEL), lambda i: (i, 0))
    return pl.pallas_call(
        _hy_out_body,
        grid=(t // tm,),
        in_specs=[tok, tok, tok, pl.BlockSpec((1, D_MODEL), lambda i: (0, 0)), tok, _resident(w.shape)],
        out_specs=tok,
        out_shape=jax.ShapeDtypeStruct((t, D_MODEL), F32),
        compiler_params=_params("parallel"),
        name="hy_out",
    )(conv.reshape(t, D_MODEL), vv.reshape(t, D_MODEL), x0.reshape(t, D_MODEL), bias_d, x2, w)


def _mixer_b(x, g, w_in, conv_w, conv_b, filt, bias_d, w_out):
    b, s, _ = x.shape
    x0, vv = _hy_in(x, g, w_in, conv_w, conv_b)
    h_raw, h_norm = _hy_filter(s, *filt)
    conv = _long_conv(vv, h_raw, h_norm)
    return _hy_out(conv, vv, x0, bias_d, x.reshape(b * s, D_MODEL), w_out).reshape(x.shape)


def _gelu(z):
    return 0.5 * z * (1.0 + lax.erf(z * (2.0 ** -0.5)))


def _sgu_body(x_ref, g_ref, win_ref, lng_ref, lnb_ref, ws_ref, bs_ref, wout_ref, o_ref, y_ref):
    tm = x_ref.shape[0]
    x = x_ref[...]
    xn = _rms(x, g_ref[...]).astype(BF16)
    zu = _gelu(_dot(xn, win_ref[:, :D_MODEL]))
    zv = _gelu(_dot(xn, win_ref[:, D_MODEL:]))
    zc = zv - jnp.mean(zv, axis=-1, keepdims=True)
    zv = zc * lax.rsqrt(jnp.mean(zc * zc, axis=-1, keepdims=True) + EPS) * lng_ref[...] + lnb_ref[...]
    zvb = zv.astype(BF16)
    for c in range(tm // C_CHUNK):
        rows = slice(c * C_CHUNK, (c + 1) * C_CHUNK)
        for h in range(C_GROUPS):
            cols = slice(h * LANES, (h + 1) * LANES)
            sv = _dot(ws_ref[h], zvb[rows, cols]) + bs_ref[:, h:h + 1]
            y_ref[rows, cols] = (zu[rows, cols] * sv).astype(BF16)
    o_ref[...] = x + _dot(y_ref[...], wout_ref[...])


def _mixer_c(x, g, w_in, ln_g, ln_b, w_s, b_s_t, w_out):
    b, s, _ = x.shape
    t = b * s
    tm = 256
    tok = pl.BlockSpec((tm, D_MODEL), lambda i: (i, 0))
    row = pl.BlockSpec((1, D_MODEL), lambda i: (0, 0))
    return pl.pallas_call(
        _sgu_body,
        grid=(t // tm,),
        in_specs=[tok, row, _resident(w_in.shape), row, row, _resident(w_s.shape),
                  pl.BlockSpec(b_s_t.shape, lambda i: (0, 0)), _resident(w_out.shape)],
        out_specs=tok,
        out_shape=jax.ShapeDtypeStruct((t, D_MODEL), F32),
        scratch_shapes=[pltpu.VMEM((tm, D_MODEL), BF16)],
        compiler_params=_params("parallel"),
        name="sgu",
    )(x.reshape(t, D_MODEL), g, w_in, ln_g, ln_b, w_s, b_s_t, w_out).reshape(x.shape)


def _trunk(x, kvs, boff, w):
    b, s, _ = x.shape
    t = b * s
    for i in range(DEPTH):
        kind, j = i % N_MIXERS, i // N_MIXERS
        g = w["g_mix"][i][None, :]
        if kind == 0:
            x = _mixer_a(x, g, w["a_w_in"][j], w["a_w_out"][j])
        elif kind == 1:
            filt = tuple(w[k][j] for k in ("b_f_w1", "b_f_b1", "b_f_w2", "b_f_b2", "b_f_w3", "b_f_b3",
                                          "b_f_wout", "b_f_freq"))
            x = _mixer_b(x, g, w["b_w_in"][j], w["b_conv_w"][j], w["b_conv_b"][j][None, :], filt,
                         w["b_bias_d"][j][None, :], w["b_w_out"][j])
        else:
            x = _mixer_c(x, g, w["c_w_in"][j], w["c_ln_g"][j][None, :], w["c_ln_b"][j][None, :],
                         w["c_w_s"][j], w["c_b_s"][j].T, w["c_w_out"][j])
        x = _xattn(x, w["g_cross"][i][None, :], w["x_w_q"][i], kvs[i], boff, w["x_w_o"][i])
        g_final = w["g_final"][None, :] if i == DEPTH - 1 else None
        x = _ffn(x.reshape(t, D_MODEL), w["g_ffn"][i][None, :], w["f_w_gu"][i], w["f_w_down"][i],
                 g_final).reshape(b, s, D_MODEL)
    return x


_BF16_WEIGHTS = ("a_w_in", "a_w_out", "b_w_in", "b_w_out", "c_w_in", "c_w_s", "c_w_out",
                 "x_w_q", "x_w_kv", "x_w_o", "f_w_gu", "f_w_down")


def kernel(x_prompt, x_sample, mem_prompt, mem_sample, g_mix, g_cross, g_ffn, g_final, a_w_in, a_w_out, b_w_in, b_conv_w, b_conv_b, b_f_w1, b_f_b1, b_f_w2, b_f_b2, b_f_w3, b_f_b3, b_f_wout, b_f_freq, b_bias_d, b_w_out, c_w_in, c_ln_g, c_ln_b, c_w_s, c_b_s, c_w_out, x_w_q, x_w_kv, x_w_o, f_w_gu, f_w_down):
    w = dict(g_mix=g_mix, g_cross=g_cross, g_ffn=g_ffn, g_final=g_final,
             a_w_in=a_w_in, a_w_out=a_w_out, b_w_in=b_w_in, b_conv_w=b_conv_w, b_conv_b=b_conv_b,
             b_f_w1=b_f_w1, b_f_b1=b_f_b1, b_f_w2=b_f_w2, b_f_b2=b_f_b2, b_f_w3=b_f_w3,
             b_f_b3=b_f_b3, b_f_wout=b_f_wout, b_f_freq=b_f_freq, b_bias_d=b_bias_d,
             b_w_out=b_w_out, c_w_in=c_w_in, c_ln_g=c_ln_g, c_ln_b=c_ln_b, c_w_s=c_w_s,
             c_b_s=c_b_s, c_w_out=c_w_out, x_w_q=x_w_q, x_w_kv=x_w_kv, x_w_o=x_w_o,
             f_w_gu=f_w_gu, f_w_down=f_w_down)
    for name in _BF16_WEIGHTS:
        w[name] = w[name].astype(BF16)
    nb_prompt = mem_prompt.shape[0]
    mem = jnp.concatenate([mem_prompt, mem_sample], axis=0)
    mem2 = mem.reshape(mem.shape[0] * MEM_LEN, D_MODEL)
    kvs = [_kv_proj(mem2, w["x_w_kv"][i]).reshape(mem.shape[0], MEM_LEN, 2 * D_MODEL) for i in range(DEPTH)]
    y_prompt = _trunk(x_prompt, kvs, 0, w)
    y_sample = _trunk(x_sample, kvs, nb_prompt, w)
    return (y_prompt, y_sample)
```

```python
import functools
import math

import numpy as np
import jax
import jax.numpy as jnp
from jax import lax
from jax.experimental import pallas as pl
from jax.experimental.pallas import tpu as pltpu

F32 = jnp.float32
BF16 = jnp.bfloat16

D_MODEL = 1024
DEPTH = 4
N_MIXERS = 3

A_GROUPS = ((128, 1), (512, 4), (2048, 16))
A_N_GROUPS = len(A_GROUPS)
A_HEADS = 16
A_HEAD_DIM = D_MODEL // A_HEADS
A_HALF = 64
A_SUBTILE = 128
A_Q_SCALE = A_HEAD_DIM ** -0.5 * math.log2(math.e)
ROPE_THETA = 10000.0

HY_EMB = 33
HY_BANDS = (HY_EMB - 1) // 2
HY_HID_PAD = 128
HY_DECAY_TARGET = 1e-2
HY_DECAY_STRONG_PCT = 0.3
HY_DECAY_WEAK_PCT = 1.5

C_CHUNK = 128
C_GROUPS = 8

MEM_LEN = 256
X_HEADS = 4
X_HEAD_DIM = D_MODEL // X_HEADS

D_FF = -(-8 * D_MODEL // (3 * 256)) * 256
FF_CHUNK = 256

EPS = 1e-6
NEG_INF = -1e30

LANES = 128
VMEM_LIMIT = 56 * 1024 * 1024


def _params(*sem):
    return pltpu.CompilerParams(dimension_semantics=sem, vmem_limit_bytes=VMEM_LIMIT)


def _resident(shape):
    nd = len(shape)
    return pl.BlockSpec(shape, lambda *_: (0,) * nd, pipeline_mode=pl.Buffered(1))


def _dot(a, b):
    return jnp.dot(a, b, preferred_element_type=F32)


def _dot_nt(a, b):
    return lax.dot_general(a, b, (((1,), (1,)), ((), ())), preferred_element_type=F32)


def _rms(x, g):
    return x * lax.rsqrt(jnp.mean(x * x, axis=-1, keepdims=True) + EPS) * g


def _ffn_body(*refs, final):
    if final:
        x_ref, g_ref, wgu_ref, wd_ref, gf_ref, o_ref, act_ref = refs
    else:
        x_ref, g_ref, wgu_ref, wd_ref, o_ref, act_ref = refs
    x = x_ref[...]
    xn = _rms(x, g_ref[...]).astype(BF16)
    for c in range(D_FF // FF_CHUNK):
        lo = c * FF_CHUNK
        gate = _dot(xn, wgu_ref[:, lo:lo + FF_CHUNK])
        up = _dot(xn, wgu_ref[:, D_FF + lo:D_FF + lo + FF_CHUNK])
        act_ref[:, lo:lo + FF_CHUNK] = (gate / (1.0 + jnp.exp(-gate)) * up).astype(BF16)
    y = x + _dot(act_ref[...], wd_ref[...])
    if final:
        y = _rms(y, gf_ref[...])
    o_ref[...] = y


def _ffn(x2, g, wgu, wd, g_final=None):
    t = x2.shape[0]
    tm = 512
    final = g_final is not None
    tok = pl.BlockSpec((tm, D_MODEL), lambda i: (i, 0))
    row = pl.BlockSpec((1, D_MODEL), lambda i: (0, 0))
    in_specs = [tok, row, _resident(wgu.shape), _resident(wd.shape)]
    args = [x2, g, wgu, wd]
    if final:
        in_specs.append(row)
        args.append(g_final)
    return pl.pallas_call(
        functools.partial(_ffn_body, final=final),
        grid=(t // tm,),
        in_specs=in_specs,
        out_specs=tok,
        out_shape=jax.ShapeDtypeStruct((t, D_MODEL), F32),
        scratch_shapes=[pltpu.VMEM((tm, D_FF), BF16)],
        compiler_params=_params("parallel"),
        name="ffn_final" if final else "ffn",
    )(*args)


def _kv_body(m_ref, w_ref, o_ref):
    o_ref[...] = _dot(m_ref[...].astype(BF16), w_ref[...]).astype(BF16)


def _kv_proj(mem2, wkv):
    r = mem2.shape[0]
    tm = 256
    return pl.pallas_call(
        _kv_body,
        grid=(r // tm,),
        in_specs=[pl.BlockSpec((tm, D_MODEL), lambda i: (i, 0)), _resident(wkv.shape)],
        out_specs=pl.BlockSpec((tm, 2 * D_MODEL), lambda i: (i, 0)),
        out_shape=jax.ShapeDtypeStruct((r, 2 * D_MODEL), BF16),
        compiler_params=_params("parallel"),
        name="kv_proj",
    )(mem2, wkv)


def _xattn_body(x_ref, g_ref, wq_ref, k_ref, v_ref, wo_ref, o_ref, y_ref):
    x = x_ref[...]
    xn = _rms(x, g_ref[...]).astype(BF16)
    q = (_dot(xn, wq_ref[...]) * (X_HEAD_DIM ** -0.5)).astype(BF16)
    for h in range(X_HEADS):
        sl = slice(h * X_HEAD_DIM, (h + 1) * X_HEAD_DIM)
        s = _dot_nt(q[:, sl], k_ref[:, sl])
        p = jnp.exp(s - jnp.max(s, axis=-1, keepdims=True))
        inv = 1.0 / jnp.sum(p, axis=-1, keepdims=True)
        y_ref[:, sl] = (_dot(p.astype(BF16), v_ref[:, sl]) * inv).astype(BF16)
    o_ref[...] = x + _dot(y_ref[...], wo_ref[...])


def _xattn(x, g, wq, kv, boff, wo):
    b, s, _ = x.shape
    tm = 512
    tok = pl.BlockSpec((None, tm, D_MODEL), lambda bi, i: (bi, i, 0))
    return pl.pallas_call(
        _xattn_body,
        grid=(b, s // tm),
        in_specs=[
            tok,
            pl.BlockSpec((1, D_MODEL), lambda bi, i: (0, 0)),
            _resident(wq.shape),
            pl.BlockSpec((None, MEM_LEN, D_MODEL), lambda bi, i: (bi + boff, 0, 0)),
            pl.BlockSpec((None, MEM_LEN, D_MODEL), lambda bi, i: (bi + boff, 0, 1)),
            _resident(wo.shape),
        ],
        out_specs=tok,
        out_shape=jax.ShapeDtypeStruct(x.shape, F32),
        scratch_shapes=[pltpu.VMEM((tm, D_MODEL), BF16)],
        compiler_params=_params("parallel", "parallel"),
        name="xattn",
    )(x, g, wq, kv, kv, wo)


def _a_in_body(x_ref, g_ref, w_ref, cos_ref, sin_ref, *rest):
    o_refs, (xs_ref, xp_ref) = rest[:A_N_GROUPS], rest[A_N_GROUPS:]
    tm = x_ref.shape[0]
    xn_f32 = _rms(x_ref[...], g_ref[...])
    reps = D_MODEL // LANES
    for cb in range(reps):
        xs_ref[cb] = xn_f32[:, cb * LANES:(cb + 1) * LANES]
    lane = lax.broadcasted_iota(jnp.int32, (tm, D_MODEL), 1)
    low_half = (lane & (A_HEAD_DIM // 2)) == 0
    for gidx, (o_ref, (_, dil)) in enumerate(zip(o_refs, A_GROUPS)):
        rows = tm // dil
        if dil == 1:
            xn = xn_f32.astype(BF16)
            cos, sin = cos_ref[...], sin_ref[...]
        else:
            for r in range(dil):
                for cb in range(reps):
                    xp_ref[r * rows:(r + 1) * rows, cb * LANES:(cb + 1) * LANES] = (
                        xs_ref[cb, pl.ds(r, rows, stride=dil), :].astype(BF16))
            xn = xp_ref[...]
            cos = jnp.concatenate([cos_ref[pl.ds(r, rows, stride=dil), :] for r in range(dil)], axis=0)
            sin = jnp.concatenate([sin_ref[pl.ds(r, rows, stride=dil), :] for r in range(dil)], axis=0)
        cos_t = jnp.concatenate([cos] * reps, axis=1)
        sin_t = jnp.concatenate([sin] * reps, axis=1)
        for part in range(3):
            lo = (gidx * 3 + part) * D_MODEL
            y = _dot(xn, w_ref[:, lo:lo + D_MODEL])
            if part < 2:
                if part == 0:
                    y = y * A_Q_SCALE
                partner = jnp.where(low_half,
                                    pltpu.roll(y, D_MODEL - A_HEAD_DIM // 2, 1),
                                    pltpu.roll(y, A_HEAD_DIM // 2, 1))
                y = y * cos_t + partner * sin_t
            o_ref[:, :, part * D_MODEL:(part + 1) * D_MODEL] = y.reshape(dil, rows, D_MODEL).astype(BF16)


def _a_in(x, g, w, cos_t, sin_t):
    b, s, _ = x.shape
    tm = 256
    return pl.pallas_call(
        _a_in_body,
        grid=(b, s // tm),
        in_specs=[
            pl.BlockSpec((None, tm, D_MODEL), lambda bi, i: (bi, i, 0)),
            pl.BlockSpec((1, D_MODEL), lambda bi, i: (0, 0)),
            _resident(w.shape),
            pl.BlockSpec((tm, LANES), lambda bi, i: (i, 0)),
            pl.BlockSpec((tm, LANES), lambda bi, i: (i, 0)),
        ],
        out_specs=[pl.BlockSpec((None, dil, tm // dil, 3 * D_MODEL), lambda bi, i: (bi, 0, i, 0))
                   for _, dil in A_GROUPS],
        out_shape=[jax.ShapeDtypeStruct((b, dil, s // dil, 3 * D_MODEL), BF16) for _, dil in A_GROUPS],
        scratch_shapes=[pltpu.VMEM((D_MODEL // LANES, tm, LANES), F32), pltpu.VMEM((tm, D_MODEL), BF16)],
        compiler_params=_params("parallel", "parallel"),
        name="a_in",
    )(x, g, w, cos_t, sin_t)


def _attn_body(q_ref, kp_ref, k_ref, kn_ref, vp_ref, v_ref, vn_ref, o_ref, lse_ref, *, n_tiles):
    tq = q_ref.shape[0]
    sq = A_SUBTILE
    win = sq + 2 * A_HALF
    n_sub_tiles = tq // sq
    i = pl.program_id(2)
    kw = jnp.concatenate([kp_ref[...], k_ref[...], kn_ref[...]], axis=0)
    vw = jnp.concatenate([vp_ref[...], v_ref[...], vn_ref[...]], axis=0)
    qi = lax.broadcasted_iota(jnp.int32, (sq, win), 0)
    kj = lax.broadcasted_iota(jnp.int32, (sq, win), 1)
    band = (kj >= qi) & (kj <= qi + 2 * A_HALF)
    left = lax.broadcasted_iota(jnp.int32, (sq, LANES), 1) < A_HEAD_DIM
    left_kv = lax.broadcasted_iota(jnp.int32, (win, LANES), 1) < A_HEAD_DIM
    lane = lax.broadcasted_iota(jnp.int32, (sq, LANES), 1)
    for st in range(n_sub_tiles):
        rows = slice(st * sq, (st + 1) * sq)
        mask = band
        if st == 0:
            mask = mask & (kj >= jnp.where(i == 0, A_HALF, 0))
        if st == n_sub_tiles - 1:
            mask = mask & (kj < jnp.where(i == n_tiles - 1, sq + A_HALF, win))
        lse_all = jnp.zeros((sq, LANES), F32)
        for hp in range(A_HEADS // 2):
            sl = slice(hp * LANES, (hp + 1) * LANES)
            q2 = q_ref[rows, sl]
            k2 = kw[st * sq:st * sq + win, sl]
            v2 = vw[st * sq:st * sq + win, sl]
            halves = []
            for side in range(2):
                keep = left if side == 0 else jnp.logical_not(left)
                s = _dot_nt(jnp.where(keep, q2, jnp.zeros_like(q2)), k2)
                s = jnp.where(mask, s, NEG_INF)
                m = jnp.max(s, axis=-1, keepdims=True)
                p = jnp.exp2(s - m)
                l = jnp.sum(p, axis=-1, keepdims=True)
                halves.append(_dot(p.astype(BF16), v2) * (1.0 / l))
                lse_all = jnp.where(lane == 2 * hp + side, m + jnp.log2(l), lse_all)
            o_ref[rows, sl] = jnp.where(left, halves[0], halves[1]).astype(BF16)
        lse_ref[rows, :] = lse_all * math.log(2.0)


def _attn_group(qkv):
    b, dil, n_sub, _ = qkv.shape
    tq = min(2 * A_SUBTILE, n_sub)
    n_tiles = n_sub // tq
    halo_per_tile = tq // A_HALF
    n_halo = n_sub // A_HALF

    def own(part):
        return pl.BlockSpec((None, None, tq, D_MODEL), lambda bi, r, i: (bi, r, i, part))

    def prev(part):
        return pl.BlockSpec((None, None, A_HALF, D_MODEL),
                            lambda bi, r, i: (bi, r, jnp.maximum(i * halo_per_tile - 1, 0), part))

    def nxt(part):
        return pl.BlockSpec((None, None, A_HALF, D_MODEL),
                            lambda bi, r, i: (bi, r, jnp.minimum((i + 1) * halo_per_tile, n_halo - 1), part))

    return pl.pallas_call(
        functools.partial(_attn_body, n_tiles=n_tiles),
        grid=(b, dil, n_tiles),
        in_specs=[own(0), prev(1), own(1), nxt(1), prev(2), own(2), nxt(2)],
        out_specs=[
            pl.BlockSpec((None, None, tq, D_MODEL), lambda bi, r, i: (bi, r, i, 0)),
            pl.BlockSpec((None, None, tq, LANES), lambda bi, r, i: (bi, r, i, 0)),
        ],
        out_shape=[
            jax.ShapeDtypeStruct((b, dil, n_sub, D_MODEL), BF16),
            jax.ShapeDtypeStruct((b, dil, n_sub, LANES), F32),
        ],
        compiler_params=_params("parallel", "parallel", "parallel"),
        name=f"attn_d{dil}",
    )(qkv, qkv, qkv, qkv, qkv, qkv, qkv)


def _a_out_body(o1_ref, o2_ref, o3_ref, l1_ref, l2_ref, l3_ref, x_ref, w_ref, sp_ref, out_ref,
                os2_ref, os3_ref, ls2_ref, ls3_ref, y_ref):
    tm = x_ref.shape[0]
    for o_ref, l_ref, os_ref, ls_ref in ((o2_ref, l2_ref, os2_ref, ls2_ref), (o3_ref, l3_ref, os3_ref, ls3_ref)):
        dil = o_ref.shape[0]
        rows = tm // dil
        for r in range(dil):
            for cb in range(D_MODEL // LANES):
                os_ref[cb, pl.ds(r, rows, stride=dil), :] = o_ref[r, :, cb * LANES:(cb + 1) * LANES].astype(F32)
            ls_ref[pl.ds(r, rows, stride=dil), :] = l_ref[r]
    l1, l2, l3 = l1_ref[0], ls2_ref[...], ls3_ref[...]
    m = jnp.maximum(jnp.maximum(l1, l2), l3)
    e1, e2, e3 = jnp.exp(l1 - m), jnp.exp(l2 - m), jnp.exp(l3 - m)
    inv = 1.0 / (e1 + e2 + e3)
    valid = lax.broadcasted_iota(jnp.int32, (tm, LANES), 1) < A_HEADS
    packed = jnp.zeros((tm, LANES), F32)
    for gidx, e in enumerate((e1, e2, e3)):
        wgt = jnp.where(valid, e * inv, 0.0)
        hi = wgt.astype(BF16).astype(F32)
        for term, val in enumerate((hi, wgt - hi)):
            shift = 2 * A_HEADS * gidx + A_HEADS * term
            packed = packed + (pltpu.roll(val, shift, 1) if shift else val)
    wall = _dot(packed.astype(BF16), sp_ref[...])
    for cb in range(D_MODEL // LANES):
        sl = slice(cb * LANES, (cb + 1) * LANES)
        y = (wall[:, sl] * o1_ref[0, :, sl].astype(F32)
             + wall[:, D_MODEL + cb * LANES:D_MODEL + (cb + 1) * LANES] * os2_ref[cb]
             + wall[:, 2 * D_MODEL + cb * LANES:2 * D_MODEL + (cb + 1) * LANES] * os3_ref[cb])
        y_ref[:, sl] = y.astype(BF16)
    out_ref[...] = x_ref[...] + _dot(y_ref[...], w_ref[...])


def _a_out(outs, lses, x, w):
    b, s, _ = x.shape
    tm = 512
    tok = pl.BlockSpec((None, tm, D_MODEL), lambda bi, i: (bi, i, 0))

    def grouped(arr):
        dil, width = arr.shape[1], arr.shape[3]
        return pl.BlockSpec((None, dil, tm // dil, width), lambda bi, i: (bi, 0, i, 0))

    col = np.arange(A_N_GROUPS * D_MODEL)
    row = np.arange(LANES)
    spread = jnp.asarray((row[:, None] // (2 * A_HEADS) == col[None, :] // D_MODEL)
                         & (row[:, None] % A_HEADS == (col[None, :] % D_MODEL) // A_HEAD_DIM)
                         & (row[:, None] < 2 * A_HEADS * A_N_GROUPS), dtype=BF16)
    return pl.pallas_call(
        _a_out_body,
        grid=(b, s // tm),
        in_specs=[grouped(a) for a in outs] + [grouped(a) for a in lses]
                 + [tok, _resident(w.shape), _resident(spread.shape)],
        out_specs=tok,
        out_shape=jax.ShapeDtypeStruct(x.shape, F32),
        scratch_shapes=[pltpu.VMEM((D_MODEL // LANES, tm, LANES), F32), pltpu.VMEM((D_MODEL // LANES, tm, LANES), F32),
                        pltpu.VMEM((tm, LANES), F32), pltpu.VMEM((tm, LANES), F32),
                        pltpu.VMEM((tm, D_MODEL), BF16)],
        compiler_params=_params("parallel", "parallel"),
        name="a_out",
    )(*outs, *lses, x, w, spread)


def _rope_tables(seq_len):
    inv = ROPE_THETA ** (-jnp.arange(0, A_HEAD_DIM, 2, dtype=F32) / A_HEAD_DIM)
    ang = jnp.arange(seq_len, dtype=F32)[:, None] * inv[None, :]
    cos, sin = jnp.cos(ang), jnp.sin(ang)
    reps = LANES // A_HEAD_DIM
    return (jnp.concatenate([cos, cos] * reps, axis=1),
            jnp.concatenate([-sin, sin] * reps, axis=1))


def _mixer_a(x, g, w_in, w_out):
    b, s, _ = x.shape
    cos_t, sin_t = _rope_tables(s)
    outs, lses = [], []
    for qkv in _a_in(x, g, w_in, cos_t, sin_t):
        o, l = _attn_group(qkv)
        outs.append(o)
        lses.append(l)
    return _a_out(outs, lses, x, w_out)


HALO = 8


def _hy_in_body(xp_ref, x_ref, xn_ref, g_ref, w_ref, cw_ref, cb_ref, x0_ref, vv_ref, u_ref, *, n_tiles):
    tm = x_ref.shape[0]
    i = pl.program_id(1)
    xe = jnp.concatenate([xp_ref[...], x_ref[...], xn_ref[...]], axis=0)
    xn = _rms(xe, g_ref[...]).astype(BF16)
    row = lax.broadcasted_iota(jnp.int32, (tm + 2 * HALO, 1), 0)
    inside = ((row >= HALO) | (i > 0)) & ((row < tm + HALO) | (i < n_tiles - 1))
    parts = []
    for c in range(3):
        lo = c * D_MODEL
        u_ref[...] = jnp.where(inside, _dot(xn, w_ref[:, lo:lo + D_MODEL]), 0.0)
        conv = (u_ref[pl.ds(HALO - 1, tm), :] * cw_ref[0:1, lo:lo + D_MODEL]
                + u_ref[pl.ds(HALO, tm), :] * cw_ref[1:2, lo:lo + D_MODEL]
                + u_ref[pl.ds(HALO + 1, tm), :] * cw_ref[2:3, lo:lo + D_MODEL]
                + cb_ref[:, lo:lo + D_MODEL])
        parts.append(conv)
    x0_ref[...] = parts[0].astype(BF16)
    vv_ref[...] = (parts[2] * parts[1]).astype(BF16)


def _hy_in(x, g, w, conv_w, conv_b):
    b, s, _ = x.shape
    tm = 512
    n_tiles = s // tm
    per = tm // HALO
    n_halo = s // HALO
    tok = pl.BlockSpec((None, tm, D_MODEL), lambda bi, i: (bi, i, 0))
    return pl.pallas_call(
        functools.partial(_hy_in_body, n_tiles=n_tiles),
        grid=(b, n_tiles),
        in_specs=[
            pl.BlockSpec((None, HALO, D_MODEL), lambda bi, i: (bi, jnp.maximum(i * per - 1, 0), 0)),
            tok,
            pl.BlockSpec((None, HALO, D_MODEL), lambda bi, i: (bi, jnp.minimum((i + 1) * per, n_halo - 1), 0)),
            pl.BlockSpec((1, D_MODEL), lambda bi, i: (0, 0)),
            _resident(w.shape),
            pl.BlockSpec(conv_w.shape, lambda bi, i: (0, 0)),
            pl.BlockSpec(conv_b.shape, lambda bi, i: (0, 0)),
        ],
        out_specs=[tok, tok],
        out_shape=[jax.ShapeDtypeStruct(x.shape, BF16), jax.ShapeDtypeStruct(x.shape, BF16)],
        scratch_shapes=[pltpu.VMEM((tm + 2 * HALO, D_MODEL), F32)],
        compiler_params=_params("parallel", "parallel"),
        name="hy_in",
    )(x, x, x, g, w, conv_w, conv_b)


def _hdot(a, b):
    return jnp.dot(a, b, precision=lax.Precision.HIGHEST, preferred_element_type=F32)


def _hy_filter_body(z_ref, t_ref, a_ref, b_ref, w1_ref, b1_ref, w2_ref, b2_ref, w3_ref, b3_ref,
                    wo_ref, fr_ref, dl_ref, h_ref, sum_ref):
    fr = fr_ref[...]
    hid = jnp.sin(fr * (_hdot(z_ref[...], w1_ref[...]) + b1_ref[...]))
    hid = jnp.sin(fr * (_hdot(hid, w2_ref[...]) + b2_ref[...]))
    hid = jnp.sin(fr * (_hdot(hid, w3_ref[...]) + b3_ref[...]))
    decay = jnp.exp(-t_ref[...] * dl_ref[...])
    hid = hid.astype(BF16)
    h_fwd = _dot(hid, wo_ref[:, :D_MODEL]) * decay
    h_bwd = _dot(hid, wo_ref[:, D_MODEL:]) * decay
    h = a_ref[...] * h_fwd + b_ref[...] * h_bwd
    h_ref[...] = h

    @pl.when(pl.program_id(0) == 0)
    def _():
        sum_ref[...] = jnp.zeros_like(sum_ref)

    sum_ref[...] += jnp.sum(jnp.abs(h), axis=0, keepdims=True)


def _hy_filter(seq_len, f_w1, f_b1, f_w2, f_b2, f_w3, f_b3, f_wout, f_freq):
    n = 2 * seq_len
    src = np.concatenate([np.arange(seq_len), [0], np.arange(seq_len - 1, 0, -1)])
    lag = jnp.asarray(src, dtype=F32)[:, None]
    t2 = lag / (seq_len - 1)
    w = 2.0 * math.pi * lag / seq_len
    f = jnp.linspace(1e-4, HY_BANDS - 1, HY_BANDS, dtype=F32)[None, :]
    z2 = jnp.concatenate([t2, jnp.cos(f * w), -jnp.sin(f * w),
                          jnp.zeros((n, HY_HID_PAD - HY_EMB), F32)], axis=-1)
    pos = np.arange(n)
    use_fwd = (pos < seq_len).astype(np.float32)[:, None]
    use_bwd = ((pos == 0) | (pos > seq_len)).astype(np.float32)[:, None]
    max_decay = math.log(HY_DECAY_TARGET) / HY_DECAY_STRONG_PCT
    min_decay = math.log(HY_DECAY_TARGET) / HY_DECAY_WEAK_PCT
    deltas = jnp.abs(jnp.linspace(min_decay, max_decay, D_MODEL, dtype=F32))[None, :]

    def pad2(m, rows):
        return jnp.pad(m, ((0, rows - m.shape[0]), (0, HY_HID_PAD - m.shape[1])))

    def padv(v):
        return jnp.pad(v, (0, HY_HID_PAD - v.shape[0]))[None, :]

    wo = jnp.pad(f_wout, ((0, HY_HID_PAD - f_wout.shape[0]), (0, 0))).astype(BF16)
    tr = 512
    rowblk = lambda width: pl.BlockSpec((tr, width), lambda i: (i, 0))
    full = lambda shape: pl.BlockSpec(shape, lambda i: (0, 0))
    sq = (HY_HID_PAD, HY_HID_PAD)
    vec = (1, HY_HID_PAD)
    return pl.pallas_call(
        _hy_filter_body,
        grid=(n // tr,),
        in_specs=[rowblk(HY_HID_PAD), rowblk(1), rowblk(1), rowblk(1),
                  full(sq), full(vec), full(sq), full(vec), full(sq), full(vec),
                  full(wo.shape), full(vec), full((1, D_MODEL))],
        out_specs=[rowblk(D_MODEL), full((1, D_MODEL))],
        out_shape=[jax.ShapeDtypeStruct((n, D_MODEL), F32), jax.ShapeDtypeStruct((1, D_MODEL), F32)],
        compiler_params=_params("arbitrary"),
        name="hy_filter",
    )(z2, t2, jnp.asarray(use_fwd), jnp.asarray(use_bwd),
      pad2(f_w1, HY_HID_PAD), padv(f_b1), pad2(f_w2, HY_HID_PAD), padv(f_b2),
      pad2(f_w3, HY_HID_PAD), padv(f_b3), wo, padv(f_freq), deltas)


def _fft_split(n):
    n1 = 1 << ((n.bit_length() - 1 + 1) // 2)
    return n1, n // n1


def _fft_tables(n1, n2):
    n = n1 * n2
    h = n1 // 2
    idx = np.arange(n1)
    ang = -2.0 * np.pi * ((idx[:, None] * idx[None, :]) % n1) / n1
    fr, fi = np.cos(ang), np.sin(ang)
    m1_data = np.block([[fr[:, :h], -fi[:, :h]], [fi[:, :h], fr[:, :h]]])
    m1_filt = np.concatenate([fr, fi], axis=0)
    ifr, ifi = fr.T[:h] / n, -fi.T[:h] / n
    m3 = np.block([[ifr, -ifi], [ifi, ifr]])
    k1 = jnp.arange(n1, dtype=jnp.int32)[:, None, None]
    k2 = jnp.arange(n2, dtype=jnp.int32)[None, :, None]
    i2 = jnp.arange(n2, dtype=jnp.int32)[None, None, :]
    phase = (i2 * k1 + n1 * i2 * k2) % n
    ga = (-2.0 * math.pi / n) * phase.astype(F32)
    gr, gi = jnp.cos(ga), jnp.sin(ga)
    g_fwd = jnp.concatenate([jnp.concatenate([gr, -gi], axis=2), jnp.concatenate([gi, gr], axis=2)], axis=1)
    grt, git = jnp.swapaxes(gr, 1, 2), jnp.swapaxes(gi, 1, 2)
    g_inv = jnp.concatenate([jnp.concatenate([grt, git], axis=2), jnp.concatenate([-git, grt], axis=2)], axis=1)
    as_bf = lambda m: jnp.asarray(m, dtype=F32).astype(BF16)
    return as_bf(m1_data), as_bf(m1_filt), as_bf(m3), g_fwd.astype(BF16), g_inv.astype(BF16)


FFT_SUB = 16
FFT_TC = 512


def _dft_outer_body(m_ref, z_ref, o_ref, zs_ref, rs_ref):
    q, sub, tc = z_ref.shape
    r = m_ref.shape[0]
    nlb = tc // LANES
    for cb in range(nlb):
        zs_ref[cb] = z_ref[:, :, cb * LANES:(cb + 1) * LANES].astype(F32).reshape(q * sub, LANES)
    for j in range(sub):
        z = jnp.concatenate([zs_ref[cb, pl.ds(j, q, stride=sub), :] for cb in range(nlb)], axis=1)
        res = _dot(m_ref[...], z.astype(BF16))
        for cb in range(nlb):
            rs_ref[cb, pl.ds(j, r, stride=sub), :] = res[:, cb * LANES:(cb + 1) * LANES]
    for cb in range(nlb):
        o_ref[:, :, :, cb * LANES:(cb + 1) * LANES] = rs_ref[cb].reshape(2, r // 2, sub, LANES).astype(BF16)


def _dft_outer(mat, z5, name):
    p, q, nh, sub, c = z5.shape
    r = mat.shape[0]
    tc = FFT_TC
    return pl.pallas_call(
        _dft_outer_body,
        grid=(p, nh, c // tc),
        in_specs=[pl.BlockSpec(mat.shape, lambda pi, h, ci: (0, 0)),
                  pl.BlockSpec((None, q, None, sub, tc), lambda pi, h, ci: (pi, 0, h, 0, ci))],
        out_specs=pl.BlockSpec((None, None, 2, r // 2, sub, tc), lambda pi, h, ci: (pi, h, 0, 0, 0, ci)),
        out_shape=jax.ShapeDtypeStruct((p, nh, 2, r // 2, sub, c), BF16),
        scratch_shapes=[pltpu.VMEM((tc // LANES, q * sub, LANES), F32),
                        pltpu.VMEM((tc // LANES, r * sub, LANES), F32)],
        compiler_params=_params("parallel", "parallel", "parallel"),
        name=name,
    )(mat, z5)


def _idft_outer_body(m_ref, b_ref, o_ref, bs_ref, ys_ref):
    _, n1, sub, tc = b_ref.shape
    q = m_ref.shape[0]
    nlb = tc // LANES
    for cb in range(nlb):
        bs_ref[cb] = b_ref[:, :, :, cb * LANES:(cb + 1) * LANES].astype(F32).reshape(2 * n1 * sub, LANES)
    for j in range(sub):
        bj = jnp.concatenate([bs_ref[cb, pl.ds(j, 2 * n1, stride=sub), :] for cb in range(nlb)], axis=1)
        res = _dot(m_ref[...], bj.astype(BF16))
        for cb in range(nlb):
            ys_ref[cb, pl.ds(j, q, stride=sub), :] = res[:, cb * LANES:(cb + 1) * LANES]
    for cb in range(nlb):
        o_ref[:, :, cb * LANES:(cb + 1) * LANES] = ys_ref[cb].reshape(q, sub, LANES)


def _idft_outer(mat, b6):
    p, nh, _, n1, sub, c = b6.shape
    q = mat.shape[0]
    tc = FFT_TC
    return pl.pallas_call(
        _idft_outer_body,
        grid=(p, nh, c // tc),
        in_specs=[pl.BlockSpec(mat.shape, lambda pi, h, ci: (0, 0)),
                  pl.BlockSpec((None, None, 2, n1, sub, tc), lambda pi, h, ci: (pi, h, 0, 0, 0, ci))],
        out_specs=pl.BlockSpec((None, q, None, sub, tc), lambda pi, h, ci: (pi, 0, h, 0, ci)),
        out_shape=jax.ShapeDtypeStruct((p, q, nh, sub, c), F32),
        scratch_shapes=[pltpu.VMEM((tc // LANES, 2 * n1 * sub, LANES), F32),
                        pltpu.VMEM((tc // LANES, q * sub, LANES), F32)],
        compiler_params=_params("parallel", "parallel", "parallel"),
        name="hy_idft",
    )(mat, b6)


FFT_KB = 4


def _stack_re_im(a_ref, kk):
    nh, _, _, sub, ct = a_ref.shape
    return jnp.concatenate([a_ref[:, 0, kk].reshape(nh * sub, ct), a_ref[:, 1, kk].reshape(nh * sub, ct)], axis=0)


def _spec_filter_body(g_ref, a_ref, sc_ref, h_ref):
    n2 = g_ref.shape[1] // 2
    for kk in range(g_ref.shape[0]):
        spec = _dot(g_ref[kk], _stack_re_im(a_ref, kk)) * sc_ref[...]
        h_ref[kk] = spec.reshape(2, n2, spec.shape[1])


def _spec_filter(g_fwd, a6, scale):
    _, nh, _, n1, sub, c = a6.shape
    n2 = nh * sub
    return pl.pallas_call(
        _spec_filter_body,
        grid=(n1 // FFT_KB,),
        in_specs=[pl.BlockSpec((FFT_KB, 2 * n2, 2 * n2), lambda k: (k, 0, 0)),
                  pl.BlockSpec((None, nh, 2, FFT_KB, sub, c), lambda k: (0, 0, 0, k, 0, 0)),
                  pl.BlockSpec((1, c), lambda k: (0, 0))],
        out_specs=pl.BlockSpec((FFT_KB, 2, n2, c), lambda k: (k, 0, 0, 0)),
        out_shape=jax.ShapeDtypeStruct((n1, 2, n2, c), F32),
        compiler_params=_params("parallel"),
        name="hy_spec_filter",
    )(g_fwd, a6, scale)


def _spec_body(gf_ref, gi_ref, h_ref, a_ref, o_ref):
    nh, _, kb, sub, ct = a_ref.shape
    n2 = nh * sub
    for kk in range(kb):
        spec = _dot(gf_ref[kk], _stack_re_im(a_ref, kk))
        xr, xi = spec[:n2], spec[n2:]
        hr, hi = h_ref[kk, 0], h_ref[kk, 1]
        y = jnp.concatenate([xr * hr - xi * hi, xr * hi + xi * hr], axis=0).astype(BF16)
        back = _dot(gi_ref[kk], y)
        o_ref[:, 0, kk] = back[:n2].reshape(nh, sub, ct).astype(BF16)
        o_ref[:, 1, kk] = back[n2:].reshape(nh, sub, ct).astype(BF16)


def _spec(g_fwd, g_inv, hspec, a6):
    p, nh, _, n1, sub, c = a6.shape
    n2 = nh * sub
    blk = pl.BlockSpec((None, nh, 2, FFT_KB, sub, c), lambda k, pi: (pi, 0, 0, k, 0, 0))
    mat = pl.BlockSpec((FFT_KB, 2 * n2, 2 * n2), lambda k, pi: (k, 0, 0))
    return pl.pallas_call(
        _spec_body,
        grid=(n1 // FFT_KB, p),
        in_specs=[mat, mat, pl.BlockSpec((FFT_KB, 2, n2, c), lambda k, pi: (k, 0, 0, 0)), blk],
        out_specs=blk,
        out_shape=jax.ShapeDtypeStruct(a6.shape, BF16),
        compiler_params=_params("parallel", "parallel"),
        name="hy_spec",
    )(g_fwd, g_inv, hspec, a6)


def _long_conv(vv, h_raw, h_norm):
    b, l, c = vv.shape
    n1, n2 = _fft_split(2 * l)
    nh = n2 // FFT_SUB
    m1_data, m1_filt, m3, g_fwd, g_inv = _fft_tables(n1, n2)
    a_h = _dft_outer(m1_filt, h_raw.reshape(1, n1, nh, FFT_SUB, c), "hy_dft1_filter")
    hspec = _spec_filter(g_fwd, a_h, 1.0 / h_norm)
    a = _dft_outer(m1_data, vv.reshape(b // 2, n1, nh, FFT_SUB, c), "hy_dft1")
    y = _idft_outer(m3, _spec(g_fwd, g_inv, hspec, a))
    return y.reshape(b, l, c)


def _hy_out_body(cv_ref, vv_ref, x0_ref, bd_ref, x_ref, w_ref, o_ref):
    y = ((cv_ref[...] + bd_ref[...] * vv_ref[...].astype(F32)) * x0_ref[...].astype(F32)).astype(BF16)
    o_ref[...] = x_ref[...] + _dot(y, w_ref[...])


def _hy_out(conv, vv, x0, bias_d, x2, w):
    t = x2.shape[0]
    tm = 512
    tok = pl.BlockSpec((tm, D_MODEL), lambda i: (i, 0))
    return pl.pallas_call(
        _hy_out_body,
        grid=(t // tm,),
        in_specs=[tok, tok, tok, pl.BlockSpec((1, D_MODEL), lambda i: (0, 0)), tok, _resident(w.shape)],
        out_specs=tok,
        out_shape=jax.ShapeDtypeStruct((t, D_MODEL), F32),
        compiler_params=_params("parallel"),
        name="hy_out",
    )(conv.reshape(t, D_MODEL), vv.reshape(t, D_MODEL), x0.reshape(t, D_MODEL), bias_d, x2, w)


def _mixer_b(x, g, w_in, conv_w, conv_b, filt, bias_d, w_out):
    b, s, _ = x.shape
    x0, vv = _hy_in(x, g, w_in, conv_w, conv_b)
    h_raw, h_norm = _hy_filter(s, *filt)
    conv = _long_conv(vv, h_raw, h_norm)
    return _hy_out(conv, vv, x0, bias_d, x.reshape(b * s, D_MODEL), w_out).reshape(x.shape)


def _gelu(z):
    return 0.5 * z * (1.0 + lax.erf(z * (2.0 ** -0.5)))


def _sgu_body(x_ref, g_ref, win_ref, lng_ref, lnb_ref, ws_ref, bs_ref, wout_ref, o_ref, y_ref):
    tm = x_ref.shape[0]
    x = x_ref[...]
    xn = _rms(x, g_ref[...]).astype(BF16)
    zu = _gelu(_dot(xn, win_ref[:, :D_MODEL]))
    zv = _gelu(_dot(xn, win_ref[:, D_MODEL:]))
    zc = zv - jnp.mean(zv, axis=-1, keepdims=True)
    zv = zc * lax.rsqrt(jnp.mean(zc * zc, axis=-1, keepdims=True) + EPS) * lng_ref[...] + lnb_ref[...]
    zvb = zv.astype(BF16)
    for c in range(tm // C_CHUNK):
        rows = slice(c * C_CHUNK, (c + 1) * C_CHUNK)
        for h in range(C_GROUPS):
            cols = slice(h * LANES, (h + 1) * LANES)
            sv = _dot(ws_ref[h], zvb[rows, cols]) + bs_ref[:, h:h + 1]
            y_ref[rows, cols] = (zu[rows, cols] * sv).astype(BF16)
    o_ref[...] = x + _dot(y_ref[...], wout_ref[...])


def _mixer_c(x, g, w_in, ln_g, ln_b, w_s, b_s_t, w_out):
    b, s, _ = x.shape
    t = b * s
    tm = 256
    tok = pl.BlockSpec((tm, D_MODEL), lambda i: (i, 0))
    row = pl.BlockSpec((1, D_MODEL), lambda i: (0, 0))
    return pl.pallas_call(
        _sgu_body,
        grid=(t // tm,),
        in_specs=[tok, row, _resident(w_in.shape), row, row, _resident(w_s.shape),
                  pl.BlockSpec(b_s_t.shape, lambda i: (0, 0)), _resident(w_out.shape)],
        out_specs=tok,
        out_shape=jax.ShapeDtypeStruct((t, D_MODEL), F32),
        scratch_shapes=[pltpu.VMEM((tm, D_MODEL), BF16)],
        compiler_params=_params("parallel"),
        name="sgu",
    )(x.reshape(t, D_MODEL), g, w_in, ln_g, ln_b, w_s, b_s_t, w_out).reshape(x.shape)


def _trunk(x, kvs, boff, w):
    b, s, _ = x.shape
    t = b * s
    for i in range(DEPTH):
        kind, j = i % N_MIXERS, i // N_MIXERS
        g = w["g_mix"][i][None, :]
        if kind == 0:
            x = _mixer_a(x, g, w["a_w_in"][j], w["a_w_out"][j])
        elif kind == 1:
            filt = tuple(w[k][j] for k in ("b_f_w1", "b_f_b1", "b_f_w2", "b_f_b2", "b_f_w3", "b_f_b3",
                                          "b_f_wout", "b_f_freq"))
            x = _mixer_b(x, g, w["b_w_in"][j], w["b_conv_w"][j], w["b_conv_b"][j][None, :], filt,
                         w["b_bias_d"][j][None, :], w["b_w_out"][j])
        else:
            x = _mixer_c(x, g, w["c_w_in"][j], w["c_ln_g"][j][None, :], w["c_ln_b"][j][None, :],
                         w["c_w_s"][j], w["c_b_s"][j].T, w["c_w_out"][j])
        x = _xattn(x, w["g_cross"][i][None, :], w["x_w_q"][i], kvs[i], boff, w["x_w_o"][i])
        g_final = w["g_final"][None, :] if i == DEPTH - 1 else None
        x = _ffn(x.reshape(t, D_MODEL), w["g_ffn"][i][None, :], w["f_w_gu"][i], w["f_w_down"][i],
                 g_final).reshape(b, s, D_MODEL)
    return x


_BF16_WEIGHTS = ("a_w_in", "a_w_out", "b_w_in", "b_w_out", "c_w_in", "c_w_s", "c_w_out",
                 "x_w_q", "x_w_kv", "x_w_o", "f_w_gu", "f_w_down")


def kernel(x_prompt, x_sample, mem_prompt, mem_sample, g_mix, g_cross, g_ffn, g_final, a_w_in, a_w_out, b_w_in, b_conv_w, b_conv_b, b_f_w1, b_f_b1, b_f_w2, b_f_b2, b_f_w3, b_f_b3, b_f_wout, b_f_freq, b_bias_d, b_w_out, c_w_in, c_ln_g, c_ln_b, c_w_s, c_b_s, c_w_out, x_w_q, x_w_kv, x_w_o, f_w_gu, f_w_down):
    w = dict(g_mix=g_mix, g_cross=g_cross, g_ffn=g_ffn, g_final=g_final,
             a_w_in=a_w_in, a_w_out=a_w_out, b_w_in=b_w_in, b_conv_w=b_conv_w, b_conv_b=b_conv_b,
             b_f_w1=b_f_w1, b_f_b1=b_f_b1, b_f_w2=b_f_w2, b_f_b2=b_f_b2, b_f_w3=b_f_w3,
             b_f_b3=b_f_b3, b_f_wout=b_f_wout, b_f_freq=b_f_freq, b_bias_d=b_bias_d,
             b_w_out=b_w_out, c_w_in=c_w_in, c_ln_g=c_ln_g, c_ln_b=c_ln_b, c_w_s=c_w_s,
             c_b_s=c_b_s, c_w_out=c_w_out, x_w_q=x_w_q, x_w_kv=x_w_kv, x_w_o=x_w_o,
             f_w_gu=f_w_gu, f_w_down=f_w_down)
    for name in _BF16_WEIGHTS:
        w[name] = w[name].astype(BF16)
    nb_prompt = mem_prompt.shape[0]
    mem = jnp.concatenate([mem_prompt, mem_sample], axis=0)
    mem2 = mem.reshape(mem.shape[0] * MEM_LEN, D_MODEL)
    kvs = [_kv_proj(mem2, w["x_w_kv"][i]).reshape(mem.shape[0], MEM_LEN, 2 * D_MODEL) for i in range(DEPTH)]
    y_prompt = _trunk(x_prompt, kvs, 0, w)
    y_sample = _trunk(x_sample, kvs, nb_prompt, w)
    return (y_prompt, y_sample)
```

```python
import functools
import math

import numpy as np
import jax
import jax.numpy as jnp
from jax import lax
from jax.experimental import pallas as pl
from jax.experimental.pallas import tpu as pltpu

F32 = jnp.float32
BF16 = jnp.bfloat16

D_MODEL = 1024
DEPTH = 4
N_MIXERS = 3

A_GROUPS = ((128, 1), (512, 4), (2048, 16))
A_N_GROUPS = len(A_GROUPS)
A_HEADS = 16
A_HEAD_DIM = D_MODEL // A_HEADS
A_HALF = 64
A_SUBTILE = 128
A_Q_SCALE = A_HEAD_DIM ** -0.5 * math.log2(math.e)
ROPE_THETA = 10000.0

HY_EMB = 33
HY_BANDS = (HY_EMB - 1) // 2
HY_HID_PAD = 128
HY_DECAY_TARGET = 1e-2
HY_DECAY_STRONG_PCT = 0.3
HY_DECAY_WEAK_PCT = 1.5

C_CHUNK = 128
C_GROUPS = 8

MEM_LEN = 256
X_HEADS = 4
X_HEAD_DIM = D_MODEL // X_HEADS

D_FF = -(-8 * D_MODEL // (3 * 256)) * 256
FF_CHUNK = 256

EPS = 1e-6
NEG_INF = -1e30

LANES = 128
VMEM_LIMIT = 56 * 1024 * 1024


def _params(*sem):
    return pltpu.CompilerParams(dimension_semantics=sem, vmem_limit_bytes=VMEM_LIMIT)


def _resident(shape):
    nd = len(shape)
    return pl.BlockSpec(shape, lambda *_: (0,) * nd, pipeline_mode=pl.Buffered(1))


def _dot(a, b):
    return jnp.dot(a, b, preferred_element_type=F32)


def _dot_nt(a, b):
    return lax.dot_general(a, b, (((1,), (1,)), ((), ())), preferred_element_type=F32)


def _rms(x, g):
    return x * lax.rsqrt(jnp.mean(x * x, axis=-1, keepdims=True) + EPS) * g


def _ffn_body(*refs, final):
    if final:
        x_ref, g_ref, wgu_ref, wd_ref, gf_ref, o_ref, act_ref = refs
    else:
        x_ref, g_ref, wgu_ref, wd_ref, o_ref, act_ref = refs
    x = x_ref[...]
    xn = _rms(x, g_ref[...]).astype(BF16)
    for c in range(D_FF // FF_CHUNK):
        lo = c * FF_CHUNK
        gate = _dot(xn, wgu_ref[:, lo:lo + FF_CHUNK])
        up = _dot(xn, wgu_ref[:, D_FF + lo:D_FF + lo + FF_CHUNK])
        act_ref[:, lo:lo + FF_CHUNK] = (gate / (1.0 + jnp.exp(-gate)) * up).astype(BF16)
    y = x + _dot(act_ref[...], wd_ref[...])
    if final:
        y = _rms(y, gf_ref[...])
    o_ref[...] = y


def _ffn(x2, g, wgu, wd, g_final=None):
    t = x2.shape[0]
    tm = 512
    final = g_final is not None
    tok = pl.BlockSpec((tm, D_MODEL), lambda i: (i, 0))
    row = pl.BlockSpec((1, D_MODEL), lambda i: (0, 0))
    in_specs = [tok, row, _resident(wgu.shape), _resident(wd.shape)]
    args = [x2, g, wgu, wd]
    if final:
        in_specs.append(row)
        args.append(g_final)
    return pl.pallas_call(
        functools.partial(_ffn_body, final=final),
        grid=(t // tm,),
        in_specs=in_specs,
        out_specs=tok,
        out_shape=jax.ShapeDtypeStruct((t, D_MODEL), F32),
        scratch_shapes=[pltpu.VMEM((tm, D_FF), BF16)],
        compiler_params=_params("parallel"),
        name="ffn_final" if final else "ffn",
    )(*args)


def _kv_body(m_ref, w_ref, o_ref):
    o_ref[...] = _dot(m_ref[...].astype(BF16), w_ref[...]).astype(BF16)


def _kv_proj(mem2, wkv):
    r = mem2.shape[0]
    tm = 256
    return pl.pallas_call(
        _kv_body,
        grid=(r // tm,),
        in_specs=[pl.BlockSpec((tm, D_MODEL), lambda i: (i, 0)), _resident(wkv.shape)],
        out_specs=pl.BlockSpec((tm, 2 * D_MODEL), lambda i: (i, 0)),
        out_shape=jax.ShapeDtypeStruct((r, 2 * D_MODEL), BF16),
        compiler_params=_params("parallel"),
        name="kv_proj",
    )(mem2, wkv)


def _xattn_body(x_ref, g_ref, wq_ref, k_ref, v_ref, wo_ref, o_ref, y_ref):
    x = x_ref[...]
    xn = _rms(x, g_ref[...]).astype(BF16)
    q = (_dot(xn, wq_ref[...]) * (X_HEAD_DIM ** -0.5)).astype(BF16)
    for h in range(X_HEADS):
        sl = slice(h * X_HEAD_DIM, (h + 1) * X_HEAD_DIM)
        s = _dot_nt(q[:, sl], k_ref[:, sl])
        p = jnp.exp(s - jnp.max(s, axis=-1, keepdims=True))
        inv = 1.0 / jnp.sum(p, axis=-1, keepdims=True)
        y_ref[:, sl] = (_dot(p.astype(BF16), v_ref[:, sl]) * inv).astype(BF16)
    o_ref[...] = x + _dot(y_ref[...], wo_ref[...])


def _xattn(x, g, wq, kv, boff, wo):
    b, s, _ = x.shape
    tm = 1024
    tok = pl.BlockSpec((None, tm, D_MODEL), lambda bi, i: (bi, i, 0))
    return pl.pallas_call(
        _xattn_body,
        grid=(b, s // tm),
        in_specs=[
            tok,
            pl.BlockSpec((1, D_MODEL), lambda bi, i: (0, 0)),
            _resident(wq.shape),
            pl.BlockSpec((None, MEM_LEN, D_MODEL), lambda bi, i: (bi + boff, 0, 0)),
            pl.BlockSpec((None, MEM_LEN, D_MODEL), lambda bi, i: (bi + boff, 0, 1)),
            _resident(wo.shape),
        ],
        out_specs=tok,
        out_shape=jax.ShapeDtypeStruct(x.shape, F32),
        scratch_shapes=[pltpu.VMEM((tm, D_MODEL), BF16)],
        compiler_params=_params("parallel", "parallel"),
        name="xattn",
    )(x, g, wq, kv, kv, wo)


def _a_in_body(x_ref, g_ref, w_ref, cos_ref, sin_ref, *rest):
    o_refs, (xs_ref, xp_ref) = rest[:A_N_GROUPS], rest[A_N_GROUPS:]
    tm = x_ref.shape[0]
    xn_f32 = _rms(x_ref[...], g_ref[...])
    reps = D_MODEL // LANES
    for cb in range(reps):
        xs_ref[cb] = xn_f32[:, cb * LANES:(cb + 1) * LANES]
    lane = lax.broadcasted_iota(jnp.int32, (tm, D_MODEL), 1)
    low_half = (lane & (A_HEAD_DIM // 2)) == 0
    for gidx, (o_ref, (_, dil)) in enumerate(zip(o_refs, A_GROUPS)):
        rows = tm // dil
        if dil == 1:
            xn = xn_f32.astype(BF16)
            cos, sin = cos_ref[...], sin_ref[...]
        else:
            for r in range(dil):
                for cb in range(reps):
                    xp_ref[r * rows:(r + 1) * rows, cb * LANES:(cb + 1) * LANES] = (
                        xs_ref[cb, pl.ds(r, rows, stride=dil), :].astype(BF16))
            xn = xp_ref[...]
            cos = jnp.concatenate([cos_ref[pl.ds(r, rows, stride=dil), :] for r in range(dil)], axis=0)
            sin = jnp.concatenate([sin_ref[pl.ds(r, rows, stride=dil), :] for r in range(dil)], axis=0)
        cos_t = jnp.concatenate([cos] * reps, axis=1)
        sin_t = jnp.concatenate([sin] * reps, axis=1)
        for part in range(3):
            lo = (gidx * 3 + part) * D_MODEL
            y = _dot(xn, w_ref[:, lo:lo + D_MODEL])
            if part < 2:
                if part == 0:
                    y = y * A_Q_SCALE
                partner = jnp.where(low_half,
                                    pltpu.roll(y, D_MODEL - A_HEAD_DIM // 2, 1),
                                    pltpu.roll(y, A_HEAD_DIM // 2, 1))
                y = y * cos_t + partner * sin_t
            o_ref[:, :, part * D_MODEL:(part + 1) * D_MODEL] = y.reshape(dil, rows, D_MODEL).astype(BF16)


def _a_in(x, g, w, cos_t, sin_t):
    b, s, _ = x.shape
    tm = 256
    return pl.pallas_call(
        _a_in_body,
        grid=(b, s // tm),
        in_specs=[
            pl.BlockSpec((None, tm, D_MODEL), lambda bi, i: (bi, i, 0)),
            pl.BlockSpec((1, D_MODEL), lambda bi, i: (0, 0)),
            _resident(w.shape),
            pl.BlockSpec((tm, LANES), lambda bi, i: (i, 0)),
            pl.BlockSpec((tm, LANES), lambda bi, i: (i, 0)),
        ],
        out_specs=[pl.BlockSpec((None, dil, tm // dil, 3 * D_MODEL), lambda bi, i: (bi, 0, i, 0))
                   for _, dil in A_GROUPS],
        out_shape=[jax.ShapeDtypeStruct((b, dil, s // dil, 3 * D_MODEL), BF16) for _, dil in A_GROUPS],
        scratch_shapes=[pltpu.VMEM((D_MODEL // LANES, tm, LANES), F32), pltpu.VMEM((tm, D_MODEL), BF16)],
        compiler_params=_params("parallel", "parallel"),
        name="a_in",
    )(x, g, w, cos_t, sin_t)


def _attn_body(q_ref, kp_ref, k_ref, kn_ref, vp_ref, v_ref, vn_ref, o_ref, lse_ref, *, n_tiles):
    tq = q_ref.shape[0]
    sq = A_SUBTILE
    win = sq + 2 * A_HALF
    n_sub_tiles = tq // sq
    i = pl.program_id(2)
    kw = jnp.concatenate([kp_ref[...], k_ref[...], kn_ref[...]], axis=0)
    vw = jnp.concatenate([vp_ref[...], v_ref[...], vn_ref[...]], axis=0)
    qi = lax.broadcasted_iota(jnp.int32, (sq, win), 0)
    kj = lax.broadcasted_iota(jnp.int32, (sq, win), 1)
    band = (kj >= qi) & (kj <= qi + 2 * A_HALF)
    left = lax.broadcasted_iota(jnp.int32, (sq, LANES), 1) < A_HEAD_DIM
    left_kv = lax.broadcasted_iota(jnp.int32, (win, LANES), 1) < A_HEAD_DIM
    lane = lax.broadcasted_iota(jnp.int32, (sq, LANES), 1)
    for st in range(n_sub_tiles):
        rows = slice(st * sq, (st + 1) * sq)
        mask = band
        if st == 0:
            mask = mask & (kj >= jnp.where(i == 0, A_HALF, 0))
        if st == n_sub_tiles - 1:
            mask = mask & (kj < jnp.where(i == n_tiles - 1, sq + A_HALF, win))
        mask2 = jnp.concatenate([mask, mask], axis=0)
        lse_all = jnp.zeros((sq, LANES), F32)

        def scores(hp):
            sl = slice(hp * LANES, (hp + 1) * LANES)
            q2 = q_ref[rows, sl]
            zero = jnp.zeros_like(q2)
            qq = jnp.concatenate([jnp.where(left, q2, zero), jnp.where(left, zero, q2)], axis=0)
            return _dot_nt(qq, kw[st * sq:st * sq + win, sl])

        s_next = scores(0)
        for hp in range(A_HEADS // 2):
            sl = slice(hp * LANES, (hp + 1) * LANES)
            s = jnp.where(mask2, s_next, NEG_INF)
            if hp + 1 < A_HEADS // 2:
                s_next = scores(hp + 1)
            m = jnp.max(s, axis=-1, keepdims=True)
            p = jnp.exp2(s - m)
            l = jnp.sum(p, axis=-1, keepdims=True)
            r = _dot(p.astype(BF16), vw[st * sq:st * sq + win, sl])
            o_ref[rows, sl] = jnp.where(left, r[:sq], r[sq:]).astype(BF16)
            lse_all = jnp.where(lane == 2 * hp, m[:sq], lse_all)
            lse_all = jnp.where(lane == 2 * hp + 1, m[sq:], lse_all)
            lse_all = jnp.where(lane == A_HEADS + 2 * hp, l[:sq], lse_all)
            lse_all = jnp.where(lane == A_HEADS + 2 * hp + 1, l[sq:], lse_all)
        lse_ref[rows, :] = lse_all


def _attn_group(qkv):
    b, dil, n_sub, _ = qkv.shape
    tq = min(2 * A_SUBTILE, n_sub)
    n_tiles = n_sub // tq
    halo_per_tile = tq // A_HALF
    n_halo = n_sub // A_HALF

    def own(part):
        return pl.BlockSpec((None, None, tq, D_MODEL), lambda bi, r, i: (bi, r, i, part))

    def prev(part):
        return pl.BlockSpec((None, None, A_HALF, D_MODEL),
                            lambda bi, r, i: (bi, r, jnp.maximum(i * halo_per_tile - 1, 0), part))

    def nxt(part):
        return pl.BlockSpec((None, None, A_HALF, D_MODEL),
                            lambda bi, r, i: (bi, r, jnp.minimum((i + 1) * halo_per_tile, n_halo - 1), part))

    return pl.pallas_call(
        functools.partial(_attn_body, n_tiles=n_tiles),
        grid=(b, dil, n_tiles),
        in_specs=[own(0), prev(1), own(1), nxt(1), prev(2), own(2), nxt(2)],
        out_specs=[
            pl.BlockSpec((None, None, tq, D_MODEL), lambda bi, r, i: (bi, r, i, 0)),
            pl.BlockSpec((None, None, tq, LANES), lambda bi, r, i: (bi, r, i, 0)),
        ],
        out_shape=[
            jax.ShapeDtypeStruct((b, dil, n_sub, D_MODEL), BF16),
            jax.ShapeDtypeStruct((b, dil, n_sub, LANES), F32),
        ],
        compiler_params=_params("parallel", "parallel", "parallel"),
        name=f"attn_d{dil}",
    )(qkv, qkv, qkv, qkv, qkv, qkv, qkv)


def _a_out_body(o1_ref, o2_ref, o3_ref, l1_ref, l2_ref, l3_ref, x_ref, w_ref, sp_ref, out_ref,
                os2_ref, os3_ref, ls2_ref, ls3_ref, y_ref):
    tm = x_ref.shape[0]
    for o_ref, l_ref, os_ref, ls_ref in ((o2_ref, l2_ref, os2_ref, ls2_ref), (o3_ref, l3_ref, os3_ref, ls3_ref)):
        dil = o_ref.shape[0]
        rows = tm // dil
        for r in range(dil):
            for cb in range(D_MODEL // LANES):
                os_ref[cb, pl.ds(r, rows, stride=dil), :] = o_ref[r, :, cb * LANES:(cb + 1) * LANES].astype(F32)
            ls_ref[pl.ds(r, rows, stride=dil), :] = l_ref[r]
    s1, s2, s3 = l1_ref[0], ls2_ref[...], ls3_ref[...]
    m = jnp.maximum(jnp.maximum(s1, s2), s3)
    e1, e2, e3 = jnp.exp2(s1 - m), jnp.exp2(s2 - m), jnp.exp2(s3 - m)
    den = sum(e * pltpu.roll(st, LANES - A_HEADS, 1) for e, st in ((e1, s1), (e2, s2), (e3, s3)))
    inv = 1.0 / den
    valid = lax.broadcasted_iota(jnp.int32, (tm, LANES), 1) < A_HEADS
    packed = jnp.zeros((tm, LANES), F32)
    for gidx, e in enumerate((e1, e2, e3)):
        wgt = jnp.where(valid, e * inv, 0.0)
        hi = wgt.astype(BF16).astype(F32)
        for term, val in enumerate((hi, wgt - hi)):
            shift = 2 * A_HEADS * gidx + A_HEADS * term
            packed = packed + (pltpu.roll(val, shift, 1) if shift else val)
    wall = _dot(packed.astype(BF16), sp_ref[...])
    for cb in range(D_MODEL // LANES):
        sl = slice(cb * LANES, (cb + 1) * LANES)
        y = (wall[:, sl] * o1_ref[0, :, sl].astype(F32)
             + wall[:, D_MODEL + cb * LANES:D_MODEL + (cb + 1) * LANES] * os2_ref[cb]
             + wall[:, 2 * D_MODEL + cb * LANES:2 * D_MODEL + (cb + 1) * LANES] * os3_ref[cb])
        y_ref[:, sl] = y.astype(BF16)
    out_ref[...] = x_ref[...] + _dot(y_ref[...], w_ref[...])


def _a_out(outs, lses, x, w):
    b, s, _ = x.shape
    tm = 512
    tok = pl.BlockSpec((None, tm, D_MODEL), lambda bi, i: (bi, i, 0))

    def grouped(arr):
        dil, width = arr.shape[1], arr.shape[3]
        return pl.BlockSpec((None, dil, tm // dil, width), lambda bi, i: (bi, 0, i, 0))

    col = np.arange(A_N_GROUPS * D_MODEL)
    row = np.arange(LANES)
    spread = jnp.asarray((row[:, None] // (2 * A_HEADS) == col[None, :] // D_MODEL)
                         & (row[:, None] % A_HEADS == (col[None, :] % D_MODEL) // A_HEAD_DIM)
                         & (row[:, None] < 2 * A_HEADS * A_N_GROUPS), dtype=BF16)
    return pl.pallas_call(
        _a_out_body,
        grid=(b, s // tm),
        in_specs=[grouped(a) for a in outs] + [grouped(a) for a in lses]
                 + [tok, _resident(w.shape), _resident(spread.shape)],
        out_specs=tok,
        out_shape=jax.ShapeDtypeStruct(x.shape, F32),
        scratch_shapes=[pltpu.VMEM((D_MODEL // LANES, tm, LANES), F32), pltpu.VMEM((D_MODEL // LANES, tm, LANES), F32),
                        pltpu.VMEM((tm, LANES), F32), pltpu.VMEM((tm, LANES), F32),
                        pltpu.VMEM((tm, D_MODEL), BF16)],
        compiler_params=_params("parallel", "parallel"),
        name="a_out",
    )(*outs, *lses, x, w, spread)


def _rope_tables(seq_len):
    inv = ROPE_THETA ** (-jnp.arange(0, A_HEAD_DIM, 2, dtype=F32) / A_HEAD_DIM)
    ang = jnp.arange(seq_len, dtype=F32)[:, None] * inv[None, :]
    cos, sin = jnp.cos(ang), jnp.sin(ang)
    reps = LANES // A_HEAD_DIM
    return (jnp.concatenate([cos, cos] * reps, axis=1),
            jnp.concatenate([-sin, sin] * reps, axis=1))


def _mixer_a(x, g, w_in, w_out):
    b, s, _ = x.shape
    cos_t, sin_t = _rope_tables(s)
    outs, lses = [], []
    for qkv in _a_in(x, g, w_in, cos_t, sin_t):
        o, l = _attn_group(qkv)
        outs.append(o)
        lses.append(l)
    return _a_out(outs, lses, x, w_out)


HALO = 8


def _hy_in_body(xp_ref, x_ref, xn_ref, g_ref, w_ref, cw_ref, cb_ref, x0_ref, vv_ref, u_ref, *, n_tiles):
    tm = x_ref.shape[0]
    i = pl.program_id(1)
    xe = jnp.concatenate([xp_ref[...], x_ref[...], xn_ref[...]], axis=0)
    xn = _rms(xe, g_ref[...]).astype(BF16)
    row = lax.broadcasted_iota(jnp.int32, (tm + 2 * HALO, 1), 0)
    inside = ((row >= HALO) | (i > 0)) & ((row < tm + HALO) | (i < n_tiles - 1))
    parts = []
    for c in range(3):
        lo = c * D_MODEL
        u_ref[...] = jnp.where(inside, _dot(xn, w_ref[:, lo:lo + D_MODEL]), 0.0)
        conv = (u_ref[pl.ds(HALO - 1, tm), :] * cw_ref[0:1, lo:lo + D_MODEL]
                + u_ref[pl.ds(HALO, tm), :] * cw_ref[1:2, lo:lo + D_MODEL]
                + u_ref[pl.ds(HALO + 1, tm), :] * cw_ref[2:3, lo:lo + D_MODEL]
                + cb_ref[:, lo:lo + D_MODEL])
        parts.append(conv)
    x0_ref[...] = parts[0].astype(BF16)
    vv_ref[...] = (parts[2] * parts[1]).astype(BF16)


def _hy_in(x, g, w, conv_w, conv_b):
    b, s, _ = x.shape
    tm = 512
    n_tiles = s // tm
    per = tm // HALO
    n_halo = s // HALO
    tok = pl.BlockSpec((None, tm, D_MODEL), lambda bi, i: (bi, i, 0))
    return pl.pallas_call(
        functools.partial(_hy_in_body, n_tiles=n_tiles),
        grid=(b, n_tiles),
        in_specs=[
            pl.BlockSpec((None, HALO, D_MODEL), lambda bi, i: (bi, jnp.maximum(i * per - 1, 0), 0)),
            tok,
            pl.BlockSpec((None, HALO, D_MODEL), lambda bi, i: (bi, jnp.minimum((i + 1) * per, n_halo - 1), 0)),
            pl.BlockSpec((1, D_MODEL), lambda bi, i: (0, 0)),
            _resident(w.shape),
            pl.BlockSpec(conv_w.shape, lambda bi, i: (0, 0)),
            pl.BlockSpec(conv_b.shape, lambda bi, i: (0, 0)),
        ],
        out_specs=[tok, tok],
        out_shape=[jax.ShapeDtypeStruct(x.shape, BF16), jax.ShapeDtypeStruct(x.shape, BF16)],
        scratch_shapes=[pltpu.VMEM((tm + 2 * HALO, D_MODEL), F32)],
        compiler_params=_params("parallel", "parallel"),
        name="hy_in",
    )(x, x, x, g, w, conv_w, conv_b)


def _hdot(a, b):
    return jnp.dot(a, b, precision=lax.Precision.HIGHEST, preferred_element_type=F32)


def _hy_filter_body(z_ref, t_ref, a_ref, b_ref, w1_ref, b1_ref, w2_ref, b2_ref, w3_ref, b3_ref,
                    wo_ref, fr_ref, dl_ref, h_ref, sum_ref):
    fr = fr_ref[...]
    hid = jnp.sin(fr * (_hdot(z_ref[...], w1_ref[...]) + b1_ref[...]))
    hid = jnp.sin(fr * (_hdot(hid, w2_ref[...]) + b2_ref[...]))
    hid = jnp.sin(fr * (_hdot(hid, w3_ref[...]) + b3_ref[...]))
    decay = jnp.exp(-t_ref[...] * dl_ref[...])
    hid = hid.astype(BF16)
    h_fwd = _dot(hid, wo_ref[:, :D_MODEL]) * decay
    h_bwd = _dot(hid, wo_ref[:, D_MODEL:]) * decay
    h = a_ref[...] * h_fwd + b_ref[...] * h_bwd
    h_ref[...] = h

    @pl.when(pl.program_id(0) == 0)
    def _():
        sum_ref[...] = jnp.zeros_like(sum_ref)

    sum_ref[...] += jnp.sum(jnp.abs(h), axis=0, keepdims=True)


def _hy_filter(seq_len, f_w1, f_b1, f_w2, f_b2, f_w3, f_b3, f_wout, f_freq):
    n = 2 * seq_len
    src = np.concatenate([np.arange(seq_len), [0], np.arange(seq_len - 1, 0, -1)])
    lag = jnp.asarray(src, dtype=F32)[:, None]
    t2 = lag / (seq_len - 1)
    w = 2.0 * math.pi * lag / seq_len
    f = jnp.linspace(1e-4, HY_BANDS - 1, HY_BANDS, dtype=F32)[None, :]
    z2 = jnp.concatenate([t2, jnp.cos(f * w), -jnp.sin(f * w),
                          jnp.zeros((n, HY_HID_PAD - HY_EMB), F32)], axis=-1)
    pos = np.arange(n)
    use_fwd = (pos < seq_len).astype(np.float32)[:, None]
    use_bwd = ((pos == 0) | (pos > seq_len)).astype(np.float32)[:, None]
    max_decay = math.log(HY_DECAY_TARGET) / HY_DECAY_STRONG_PCT
    min_decay = math.log(HY_DECAY_TARGET) / HY_DECAY_WEAK_PCT
    deltas = jnp.abs(jnp.linspace(min_decay, max_decay, D_MODEL, dtype=F32))[None, :]

    def pad2(m, rows):
        return jnp.pad(m, ((0, rows - m.shape[0]), (0, HY_HID_PAD - m.shape[1])))

    def padv(v):
        return jnp.pad(v, (0, HY_HID_PAD - v.shape[0]))[None, :]

    wo = jnp.pad(f_wout, ((0, HY_HID_PAD - f_wout.shape[0]), (0, 0))).astype(BF16)
    tr = 512
    rowblk = lambda width: pl.BlockSpec((tr, width), lambda i: (i, 0))
    full = lambda shape: pl.BlockSpec(shape, lambda i: (0, 0))
    sq = (HY_HID_PAD, HY_HID_PAD)
    vec = (1, HY_HID_PAD)
    return pl.pallas_call(
        _hy_filter_body,
        grid=(n // tr,),
        in_specs=[rowblk(HY_HID_PAD), rowblk(1), rowblk(1), rowblk(1),
                  full(sq), full(vec), full(sq), full(vec), full(sq), full(vec),
                  full(wo.shape), full(vec), full((1, D_MODEL))],
        out_specs=[rowblk(D_MODEL), full((1, D_MODEL))],
        out_shape=[jax.ShapeDtypeStruct((n, D_MODEL), F32), jax.ShapeDtypeStruct((1, D_MODEL), F32)],
        compiler_params=_params("arbitrary"),
        name="hy_filter",
    )(z2, t2, jnp.asarray(use_fwd), jnp.asarray(use_bwd),
      pad2(f_w1, HY_HID_PAD), padv(f_b1), pad2(f_w2, HY_HID_PAD), padv(f_b2),
      pad2(f_w3, HY_HID_PAD), padv(f_b3), wo, padv(f_freq), deltas)


def _fft_split(n):
    n1 = 1 << ((n.bit_length() - 1 + 1) // 2)
    return n1, n // n1


def _fft_tables(n1, n2):
    n = n1 * n2
    h = n1 // 2
    idx = np.arange(n1)
    ang = -2.0 * np.pi * ((idx[:, None] * idx[None, :]) % n1) / n1
    fr, fi = np.cos(ang), np.sin(ang)
    m1_data = np.block([[fr[:, :h], -fi[:, :h]], [fi[:, :h], fr[:, :h]]])
    m1_filt = np.concatenate([fr, fi], axis=0)
    ifr, ifi = fr.T[:h] / n, -fi.T[:h] / n
    m3 = np.block([[ifr, -ifi], [ifi, ifr]])
    k1 = jnp.arange(n1, dtype=jnp.int32)[:, None, None]
    k2 = jnp.arange(n2, dtype=jnp.int32)[None, :, None]
    i2 = jnp.arange(n2, dtype=jnp.int32)[None, None, :]
    phase = (i2 * k1 + n1 * i2 * k2) % n
    ga = (-2.0 * math.pi / n) * phase.astype(F32)
    gr, gi = jnp.cos(ga), jnp.sin(ga)
    g_fwd = jnp.concatenate([jnp.concatenate([gr, -gi], axis=2), jnp.concatenate([gi, gr], axis=2)], axis=1)
    grt, git = jnp.swapaxes(gr, 1, 2), jnp.swapaxes(gi, 1, 2)
    g_inv = jnp.concatenate([jnp.concatenate([grt, git], axis=2), jnp.concatenate([-git, grt], axis=2)], axis=1)
    as_bf = lambda m: jnp.asarray(m, dtype=F32).astype(BF16)
    return as_bf(m1_data), as_bf(m1_filt), as_bf(m3), g_fwd.astype(BF16), g_inv.astype(BF16)


FFT_SUB = 16
FFT_TC = 512


def _dft_outer_body(m_ref, z_ref, o_ref, zs_ref, rs_ref):
    q, sub, tc = z_ref.shape
    r = m_ref.shape[0]
    nlb = tc // LANES
    for cb in range(nlb):
        zs_ref[cb] = z_ref[:, :, cb * LANES:(cb + 1) * LANES].astype(F32).reshape(q * sub, LANES)
    for j in range(sub):
        z = jnp.concatenate([zs_ref[cb, pl.ds(j, q, stride=sub), :] for cb in range(nlb)], axis=1)
        res = _dot(m_ref[...], z.astype(BF16))
        for cb in range(nlb):
            rs_ref[cb, pl.ds(j, r, stride=sub), :] = res[:, cb * LANES:(cb + 1) * LANES]
    for cb in range(nlb):
        o_ref[:, :, :, cb * LANES:(cb + 1) * LANES] = rs_ref[cb].reshape(2, r // 2, sub, LANES).astype(BF16)


def _dft_outer(mat, z5, name):
    p, q, nh, sub, c = z5.shape
    r = mat.shape[0]
    tc = FFT_TC
    return pl.pallas_call(
        _dft_outer_body,
        grid=(p, nh, c // tc),
        in_specs=[pl.BlockSpec(mat.shape, lambda pi, h, ci: (0, 0)),
                  pl.BlockSpec((None, q, None, sub, tc), lambda pi, h, ci: (pi, 0, h, 0, ci))],
        out_specs=pl.BlockSpec((None, None, 2, r // 2, sub, tc), lambda pi, h, ci: (pi, h, 0, 0, 0, ci)),
        out_shape=jax.ShapeDtypeStruct((p, nh, 2, r // 2, sub, c), BF16),
        scratch_shapes=[pltpu.VMEM((tc // LANES, q * sub, LANES), F32),
                        pltpu.VMEM((tc // LANES, r * sub, LANES), F32)],
        compiler_params=_params("parallel", "parallel", "parallel"),
        name=name,
    )(mat, z5)


def _idft_outer_body(m_ref, b_ref, o_ref, bs_ref, ys_ref):
    _, n1, sub, tc = b_ref.shape
    q = m_ref.shape[0]
    nlb = tc // LANES
    for cb in range(nlb):
        bs_ref[cb] = b_ref[:, :, :, cb * LANES:(cb + 1) * LANES].astype(F32).reshape(2 * n1 * sub, LANES)
    for j in range(sub):
        bj = jnp.concatenate([bs_ref[cb, pl.ds(j, 2 * n1, stride=sub), :] for cb in range(nlb)], axis=1)
        res = _dot(m_ref[...], bj.astype(BF16))
        for cb in range(nlb):
            ys_ref[cb, pl.ds(j, q, stride=sub), :] = res[:, cb * LANES:(cb + 1) * LANES]
    for cb in range(nlb):
        o_ref[:, :, cb * LANES:(cb + 1) * LANES] = ys_ref[cb].reshape(q, sub, LANES)


def _idft_outer(mat, b6):
    p, nh, _, n1, sub, c = b6.shape
    q = mat.shape[0]
    tc = FFT_TC
    return pl.pallas_call(
        _idft_outer_body,
        grid=(p, nh, c // tc),
        in_specs=[pl.BlockSpec(mat.shape, lambda pi, h, ci: (0, 0)),
                  pl.BlockSpec((None, None, 2, n1, sub, tc), lambda pi, h, ci: (pi, h, 0, 0, 0, ci))],
        out_specs=pl.BlockSpec((None, q, None, sub, tc), lambda pi, h, ci: (pi, 0, h, 0, ci)),
        out_shape=jax.ShapeDtypeStruct((p, q, nh, sub, c), F32),
        scratch_shapes=[pltpu.VMEM((tc // LANES, 2 * n1 * sub, LANES), F32),
                        pltpu.VMEM((tc // LANES, q * sub, LANES), F32)],
        compiler_params=_params("parallel", "parallel", "parallel"),
        name="hy_idft",
    )(mat, b6)


FFT_KB = 4


def _stack_re_im(a_ref, kk):
    nh, _, _, sub, ct = a_ref.shape
    return jnp.concatenate([a_ref[:, 0, kk].reshape(nh * sub, ct), a_ref[:, 1, kk].reshape(nh * sub, ct)], axis=0)


def _spec_filter_body(g_ref, a_ref, sc_ref, h_ref):
    n2 = g_ref.shape[1] // 2
    for kk in range(g_ref.shape[0]):
        spec = _dot(g_ref[kk], _stack_re_im(a_ref, kk)) * sc_ref[...]
        h_ref[kk] = spec.reshape(2, n2, spec.shape[1])


def _spec_filter(g_fwd, a6, scale):
    _, nh, _, n1, sub, c = a6.shape
    n2 = nh * sub
    return pl.pallas_call(
        _spec_filter_body,
        grid=(n1 // FFT_KB,),
        in_specs=[pl.BlockSpec((FFT_KB, 2 * n2, 2 * n2), lambda k: (k, 0, 0)),
                  pl.BlockSpec((None, nh, 2, FFT_KB, sub, c), lambda k: (0, 0, 0, k, 0, 0)),
                  pl.BlockSpec((1, c), lambda k: (0, 0))],
        out_specs=pl.BlockSpec((FFT_KB, 2, n2, c), lambda k: (k, 0, 0, 0)),
        out_shape=jax.ShapeDtypeStruct((n1, 2, n2, c), F32),
        compiler_params=_params("parallel"),
        name="hy_spec_filter",
    )(g_fwd, a6, scale)


def _spec_body(gf_ref, gi_ref, h_ref, a_ref, o_ref):
    nh, _, kb, sub, ct = a_ref.shape
    n2 = nh * sub
    for kk in range(kb):
        spec = _dot(gf_ref[kk], _stack_re_im(a_ref, kk))
        xr, xi = spec[:n2], spec[n2:]
        hr, hi = h_ref[kk, 0], h_ref[kk, 1]
        y = jnp.concatenate([xr * hr - xi * hi, xr * hi + xi * hr], axis=0).astype(BF16)
        back = _dot(gi_ref[kk], y)
        o_ref[:, 0, kk] = back[:n2].reshape(nh, sub, ct).astype(BF16)
        o_ref[:, 1, kk] = back[n2:].reshape(nh, sub, ct).astype(BF16)


def _spec(g_fwd, g_inv, hspec, a6):
    p, nh, _, n1, sub, c = a6.shape
    n2 = nh * sub
    blk = pl.BlockSpec((None, nh, 2, FFT_KB, sub, c), lambda k, pi: (pi, 0, 0, k, 0, 0))
    mat = pl.BlockSpec((FFT_KB, 2 * n2, 2 * n2), lambda k, pi: (k, 0, 0))
    return pl.pallas_call(
        _spec_body,
        grid=(n1 // FFT_KB, p),
        in_specs=[mat, mat, pl.BlockSpec((FFT_KB, 2, n2, c), lambda k, pi: (k, 0, 0, 0)), blk],
        out_specs=blk,
        out_shape=jax.ShapeDtypeStruct(a6.shape, BF16),
        compiler_params=_params("parallel", "parallel"),
        name="hy_spec",
    )(g_fwd, g_inv, hspec, a6)


def _long_conv(vv, h_raw, h_norm):
    b, l, c = vv.shape
    n1, n2 = _fft_split(2 * l)
    nh = n2 // FFT_SUB
    m1_data, m1_filt, m3, g_fwd, g_inv = _fft_tables(n1, n2)
    a_h = _dft_outer(m1_filt, h_raw.reshape(1, n1, nh, FFT_SUB, c), "hy_dft1_filter")
    hspec = _spec_filter(g_fwd, a_h, 1.0 / h_norm)
    a = _dft_outer(m1_data, vv.reshape(b // 2, n1, nh, FFT_SUB, c), "hy_dft1")
    y = _idft_outer(m3, _spec(g_fwd, g_inv, hspec, a))
    return y.reshape(b, l, c)


def _hy_out_body(cv_ref, vv_ref, x0_ref, bd_ref, x_ref, w_ref, o_ref):
    y = ((cv_ref[...] + bd_ref[...] * vv_ref[...].astype(F32)) * x0_ref[...].astype(F32)).astype(BF16)
    o_ref[...] = x_ref[...] + _dot(y, w_ref[...])


def _hy_out(conv, vv, x0, bias_d, x2, w):
    t = x2.shape[0]
    tm = 512
    tok = pl.BlockSpec((tm, D_MODEL), lambda i: (i, 0))
    return pl.pallas_call(
        _hy_out_body,
        grid=(t // tm,),
        in_specs=[tok, tok, tok, pl.BlockSpec((1, D_MODEL), lambda i: (0, 0)), tok, _resident(w.shape)],
        out_specs=tok,
        out_shape=jax.ShapeDtypeStruct((t, D_MODEL), F32),
        compiler_params=_params("parallel"),
        name="hy_out",
    )(conv.reshape(t, D_MODEL), vv.reshape(t, D_MODEL), x0.reshape(t, D_MODEL), bias_d, x2, w)


def _mixer_b(x, g, w_in, conv_w, conv_b, filt, bias_d, w_out):
    b, s, _ = x.shape
    x0, vv = _hy_in(x, g, w_in, conv_w, conv_b)
    h_raw, h_norm = _hy_filter(s, *filt)
    conv = _long_conv(vv, h_raw, h_norm)
    return _hy_out(conv, vv, x0, bias_d, x.reshape(b * s, D_MODEL), w_out).reshape(x.shape)


def _gelu(z):
    return 0.5 * z * (1.0 + lax.erf(z * (2.0 ** -0.5)))


def _sgu_body(x_ref, g_ref, win_ref, lng_ref, lnb_ref, ws_ref, bs_ref, wout_ref, o_ref, y_ref):
    tm = x_ref.shape[0]
    x = x_ref[...]
    xn = _rms(x, g_ref[...]).astype(BF16)
    zu = _gelu(_dot(xn, win_ref[:, :D_MODEL]))
    zv = _gelu(_dot(xn, win_ref[:, D_MODEL:]))
    zc = zv - jnp.mean(zv, axis=-1, keepdims=True)
    zv = zc * lax.rsqrt(jnp.mean(zc * zc, axis=-1, keepdims=True) + EPS) * lng_ref[...] + lnb_ref[...]
    zvb = zv.astype(BF16)
    for c in range(tm // C_CHUNK):
        rows = slice(c * C_CHUNK, (c + 1) * C_CHUNK)
        for h in range(C_GROUPS):
            cols = slice(h * LANES, (h + 1) * LANES)
            sv = _dot(ws_ref[h], zvb[rows, cols]) + bs_ref[:, h:h + 1]
            y_ref[rows, cols] = (zu[rows, cols] * sv).astype(BF16)
    o_ref[...] = x + _dot(y_ref[...], wout_ref[...])


def _mixer_c(x, g, w_in, ln_g, ln_b, w_s, b_s_t, w_out):
    b, s, _ = x.shape
    t = b * s
    tm = 512
    tok = pl.BlockSpec((tm, D_MODEL), lambda i: (i, 0))
    row = pl.BlockSpec((1, D_MODEL), lambda i: (0, 0))
    return pl.pallas_call(
        _sgu_body,
        grid=(t // tm,),
        in_specs=[tok, row, _resident(w_in.shape), row, row, _resident(w_s.shape),
                  pl.BlockSpec(b_s_t.shape, lambda i: (0, 0)), _resident(w_out.shape)],
        out_specs=tok,
        out_shape=jax.ShapeDtypeStruct((t, D_MODEL), F32),
        scratch_shapes=[pltpu.VMEM((tm, D_MODEL), BF16)],
        compiler_params=_params("parallel"),
        name="sgu",
    )(x.reshape(t, D_MODEL), g, w_in, ln_g, ln_b, w_s, b_s_t, w_out).reshape(x.shape)


def _trunk(x, kvs, boff, w):
    b, s, _ = x.shape
    t = b * s
    for i in range(DEPTH):
        kind, j = i % N_MIXERS, i // N_MIXERS
        g = w["g_mix"][i][None, :]
        if kind == 0:
            x = _mixer_a(x, g, w["a_w_in"][j], w["a_w_out"][j])
        elif kind == 1:
            filt = tuple(w[k][j] for k in ("b_f_w1", "b_f_b1", "b_f_w2", "b_f_b2", "b_f_w3", "b_f_b3",
                                          "b_f_wout", "b_f_freq"))
            x = _mixer_b(x, g, w["b_w_in"][j], w["b_conv_w"][j], w["b_conv_b"][j][None, :], filt,
                         w["b_bias_d"][j][None, :], w["b_w_out"][j])
        else:
            x = _mixer_c(x, g, w["c_w_in"][j], w["c_ln_g"][j][None, :], w["c_ln_b"][j][None, :],
                         w["c_w_s"][j], w["c_b_s"][j].T, w["c_w_out"][j])
        x = _xattn(x, w["g_cross"][i][None, :], w["x_w_q"][i], kvs[i], boff, w["x_w_o"][i])
        g_final = w["g_final"][None, :] if i == DEPTH - 1 else None
        x = _ffn(x.reshape(t, D_MODEL), w["g_ffn"][i][None, :], w["f_w_gu"][i], w["f_w_down"][i],
                 g_final).reshape(b, s, D_MODEL)
    return x


_BF16_WEIGHTS = ("a_w_in", "a_w_out", "b_w_in", "b_w_out", "c_w_in", "c_w_s", "c_w_out",
                 "x_w_q", "x_w_kv", "x_w_o", "f_w_gu", "f_w_down")


def kernel(x_prompt, x_sample, mem_prompt, mem_sample, g_mix, g_cross, g_ffn, g_final, a_w_in, a_w_out, b_w_in, b_conv_w, b_conv_b, b_f_w1, b_f_b1, b_f_w2, b_f_b2, b_f_w3, b_f_b3, b_f_wout, b_f_freq, b_bias_d, b_w_out, c_w_in, c_ln_g, c_ln_b, c_w_s, c_b_s, c_w_out, x_w_q, x_w_kv, x_w_o, f_w_gu, f_w_down):
    w = dict(g_mix=g_mix, g_cross=g_cross, g_ffn=g_ffn, g_final=g_final,
             a_w_in=a_w_in, a_w_out=a_w_out, b_w_in=b_w_in, b_conv_w=b_conv_w, b_conv_b=b_conv_b,
             b_f_w1=b_f_w1, b_f_b1=b_f_b1, b_f_w2=b_f_w2, b_f_b2=b_f_b2, b_f_w3=b_f_w3,
             b_f_b3=b_f_b3, b_f_wout=b_f_wout, b_f_freq=b_f_freq, b_bias_d=b_bias_d,
             b_w_out=b_w_out, c_w_in=c_w_in, c_ln_g=c_ln_g, c_ln_b=c_ln_b, c_w_s=c_w_s,
             c_b_s=c_b_s, c_w_out=c_w_out, x_w_q=x_w_q, x_w_kv=x_w_kv, x_w_o=x_w_o,
             f_w_gu=f_w_gu, f_w_down=f_w_down)
    for name in _BF16_WEIGHTS:
        w[name] = w[name].astype(BF16)
    nb_prompt = mem_prompt.shape[0]
    mem = jnp.concatenate([mem_prompt, mem_sample], axis=0)
    mem2 = mem.reshape(mem.shape[0] * MEM_LEN, D_MODEL)
    kvs = [_kv_proj(mem2, w["x_w_kv"][i]).reshape(mem.shape[0], MEM_LEN, 2 * D_MODEL) for i in range(DEPTH)]
    y_prompt = _trunk(x_prompt, kvs, 0, w)
    y_sample = _trunk(x_sample, kvs, nb_prompt, w)
    return (y_prompt, y_sample)
```

```python
import functools
import math

import numpy as np
import jax
import jax.numpy as jnp
from jax import lax
from jax.experimental import pallas as pl
from jax.experimental.pallas import tpu as pltpu

F32 = jnp.float32
BF16 = jnp.bfloat16

D_MODEL = 1024
DEPTH = 4
N_MIXERS = 3

A_GROUPS = ((128, 1), (512, 4), (2048, 16))
A_N_GROUPS = len(A_GROUPS)
A_HEADS = 16
A_HEAD_DIM = D_MODEL // A_HEADS
A_HALF = 64
A_SUBTILE = 128
A_Q_SCALE = A_HEAD_DIM ** -0.5 * math.log2(math.e)
ROPE_THETA = 10000.0

HY_EMB = 33
HY_BANDS = (HY_EMB - 1) // 2
HY_HID_PAD = 128
HY_DECAY_TARGET = 1e-2
HY_DECAY_STRONG_PCT = 0.3
HY_DECAY_WEAK_PCT = 1.5

C_CHUNK = 128
C_GROUPS = 8

MEM_LEN = 256
X_HEADS = 4
X_HEAD_DIM = D_MODEL // X_HEADS

D_FF = -(-8 * D_MODEL // (3 * 256)) * 256
FF_CHUNK = 256

EPS = 1e-6
NEG_INF = -1e30

LANES = 128
VMEM_LIMIT = 56 * 1024 * 1024


def _params(*sem):
    return pltpu.CompilerParams(dimension_semantics=sem, vmem_limit_bytes=VMEM_LIMIT)


def _resident(shape):
    nd = len(shape)
    return pl.BlockSpec(shape, lambda *_: (0,) * nd, pipeline_mode=pl.Buffered(1))


def _dot(a, b):
    return jnp.dot(a, b, preferred_element_type=F32)


def _dot_nt(a, b):
    return lax.dot_general(a, b, (((1,), (1,)), ((), ())), preferred_element_type=F32)


def _rms(x, g):
    return x * lax.rsqrt(jnp.mean(x * x, axis=-1, keepdims=True) + EPS) * g


def _ffn_body(*refs, final):
    if final:
        x_ref, g_ref, wgu_ref, wd_ref, gf_ref, o_ref, act_ref = refs
    else:
        x_ref, g_ref, wgu_ref, wd_ref, o_ref, act_ref = refs
    x = x_ref[...]
    xn = _rms(x, g_ref[...]).astype(BF16)
    for c in range(D_FF // FF_CHUNK):
        lo = c * FF_CHUNK
        gate = _dot(xn, wgu_ref[:, lo:lo + FF_CHUNK])
        up = _dot(xn, wgu_ref[:, D_FF + lo:D_FF + lo + FF_CHUNK])
        act_ref[:, lo:lo + FF_CHUNK] = (gate / (1.0 + jnp.exp(-gate)) * up).astype(BF16)
    y = x + _dot(act_ref[...], wd_ref[...])
    if final:
        y = _rms(y, gf_ref[...])
    o_ref[...] = y


def _ffn(x2, g, wgu, wd, g_final=None):
    t = x2.shape[0]
    tm = 512
    final = g_final is not None
    tok = pl.BlockSpec((tm, D_MODEL), lambda i: (i, 0))
    row = pl.BlockSpec((1, D_MODEL), lambda i: (0, 0))
    in_specs = [tok, row, _resident(wgu.shape), _resident(wd.shape)]
    args = [x2, g, wgu, wd]
    if final:
        in_specs.append(row)
        args.append(g_final)
    return pl.pallas_call(
        functools.partial(_ffn_body, final=final),
        grid=(t // tm,),
        in_specs=in_specs,
        out_specs=tok,
        out_shape=jax.ShapeDtypeStruct((t, D_MODEL), F32),
        scratch_shapes=[pltpu.VMEM((tm, D_FF), BF16)],
        compiler_params=_params("parallel"),
        name="ffn_final" if final else "ffn",
    )(*args)


def _kv_body(m_ref, w_ref, o_ref):
    o_ref[...] = _dot(m_ref[...].astype(BF16), w_ref[...]).astype(BF16)


def _kv_proj(mem2, wkv):
    r = mem2.shape[0]
    tm = 256
    return pl.pallas_call(
        _kv_body,
        grid=(r // tm,),
        in_specs=[pl.BlockSpec((tm, D_MODEL), lambda i: (i, 0)), _resident(wkv.shape)],
        out_specs=pl.BlockSpec((tm, 2 * D_MODEL), lambda i: (i, 0)),
        out_shape=jax.ShapeDtypeStruct((r, 2 * D_MODEL), BF16),
        compiler_params=_params("parallel"),
        name="kv_proj",
    )(mem2, wkv)


def _xattn_body(x_ref, g_ref, wq_ref, k_ref, v_ref, wo_ref, o_ref, y_ref):
    x = x_ref[...]
    xn = _rms(x, g_ref[...]).astype(BF16)
    q = (_dot(xn, wq_ref[...]) * (X_HEAD_DIM ** -0.5)).astype(BF16)
    for h in range(X_HEADS):
        sl = slice(h * X_HEAD_DIM, (h + 1) * X_HEAD_DIM)
        s = _dot_nt(q[:, sl], k_ref[:, sl])
        p = jnp.exp(s - jnp.max(s, axis=-1, keepdims=True))
        inv = 1.0 / jnp.sum(p, axis=-1, keepdims=True)
        y_ref[:, sl] = (_dot(p.astype(BF16), v_ref[:, sl]) * inv).astype(BF16)
    o_ref[...] = x + _dot(y_ref[...], wo_ref[...])


def _xattn(x, g, wq, kv, boff, wo):
    b, s, _ = x.shape
    tm = 1024
    tok = pl.BlockSpec((None, tm, D_MODEL), lambda bi, i: (bi, i, 0))
    return pl.pallas_call(
        _xattn_body,
        grid=(b, s // tm),
        in_specs=[
            tok,
            pl.BlockSpec((1, D_MODEL), lambda bi, i: (0, 0)),
            _resident(wq.shape),
            pl.BlockSpec((None, MEM_LEN, D_MODEL), lambda bi, i: (bi + boff, 0, 0)),
            pl.BlockSpec((None, MEM_LEN, D_MODEL), lambda bi, i: (bi + boff, 0, 1)),
            _resident(wo.shape),
        ],
        out_specs=tok,
        out_shape=jax.ShapeDtypeStruct(x.shape, F32),
        scratch_shapes=[pltpu.VMEM((tm, D_MODEL), BF16)],
        compiler_params=_params("parallel", "parallel"),
        name="xattn",
    )(x, g, wq, kv, kv, wo)


def _a_in_body(x_ref, g_ref, w_ref, cos_ref, sin_ref, *rest):
    o_refs, (xs_ref, xp_ref) = rest[:A_N_GROUPS], rest[A_N_GROUPS:]
    tm = x_ref.shape[0]
    xn_f32 = _rms(x_ref[...], g_ref[...])
    reps = D_MODEL // LANES
    for cb in range(reps):
        xs_ref[cb] = xn_f32[:, cb * LANES:(cb + 1) * LANES]
    lane = lax.broadcasted_iota(jnp.int32, (tm, LANES), 1)
    low_half = (lane & (A_HEAD_DIM // 2)) == 0
    for gidx, (o_ref, (_, dil)) in enumerate(zip(o_refs, A_GROUPS)):
        rows = tm // dil
        if dil == 1:
            xn = xn_f32.astype(BF16)
            cos, sin = cos_ref[...], sin_ref[...]
        else:
            for r in range(dil):
                for cb in range(reps):
                    xp_ref[r * rows:(r + 1) * rows, cb * LANES:(cb + 1) * LANES] = (
                        xs_ref[cb, pl.ds(r, rows, stride=dil), :].astype(BF16))
            xn = xp_ref[...]
            cos = jnp.concatenate([cos_ref[pl.ds(r, rows, stride=dil), :] for r in range(dil)], axis=0)
            sin = jnp.concatenate([sin_ref[pl.ds(r, rows, stride=dil), :] for r in range(dil)], axis=0)
        for part in range(3):
            lo = (gidx * 3 + part) * D_MODEL
            y = _dot(xn, w_ref[:, lo:lo + D_MODEL])
            if part == 2:
                o_ref[:, :, 2 * D_MODEL:] = y.reshape(dil, rows, D_MODEL).astype(BF16)
                continue
            c_t, s_t = (cos * A_Q_SCALE, sin * A_Q_SCALE) if part == 0 else (cos, sin)
            for cb in range(reps):
                yb = y[:, cb * LANES:(cb + 1) * LANES]
                partner = jnp.where(low_half, pltpu.roll(yb, LANES - A_HEAD_DIM // 2, 1),
                                    pltpu.roll(yb, A_HEAD_DIM // 2, 1))
                col = part * D_MODEL + cb * LANES
                o_ref[:, :, col:col + LANES] = (yb * c_t + partner * s_t).reshape(dil, rows, LANES).astype(BF16)


def _a_in(x, g, w, cos_t, sin_t):
    b, s, _ = x.shape
    tm = 256
    return pl.pallas_call(
        _a_in_body,
        grid=(b, s // tm),
        in_specs=[
            pl.BlockSpec((None, tm, D_MODEL), lambda bi, i: (bi, i, 0)),
            pl.BlockSpec((1, D_MODEL), lambda bi, i: (0, 0)),
            _resident(w.shape),
            pl.BlockSpec((tm, LANES), lambda bi, i: (i, 0)),
            pl.BlockSpec((tm, LANES), lambda bi, i: (i, 0)),
        ],
        out_specs=[pl.BlockSpec((None, dil, tm // dil, 3 * D_MODEL), lambda bi, i: (bi, 0, i, 0))
                   for _, dil in A_GROUPS],
        out_shape=[jax.ShapeDtypeStruct((b, dil, s // dil, 3 * D_MODEL), BF16) for _, dil in A_GROUPS],
        scratch_shapes=[pltpu.VMEM((D_MODEL // LANES, tm, LANES), F32), pltpu.VMEM((tm, D_MODEL), BF16)],
        compiler_params=_params("parallel", "parallel"),
        name="a_in",
    )(x, g, w, cos_t, sin_t)


def _attn_body(q_ref, kp_ref, k_ref, kn_ref, vp_ref, v_ref, vn_ref, o_ref, lse_ref, *, n_tiles):
    tq = q_ref.shape[0]
    sq = A_SUBTILE
    win = sq + 2 * A_HALF
    n_sub_tiles = tq // sq
    i = pl.program_id(2)
    kw = jnp.concatenate([kp_ref[...], k_ref[...], kn_ref[...]], axis=0)
    vw = jnp.concatenate([vp_ref[...], v_ref[...], vn_ref[...]], axis=0)
    qi = lax.broadcasted_iota(jnp.int32, (sq, win), 0)
    kj = lax.broadcasted_iota(jnp.int32, (sq, win), 1)
    band = (kj >= qi) & (kj <= qi + 2 * A_HALF)
    left = lax.broadcasted_iota(jnp.int32, (sq, LANES), 1) < A_HEAD_DIM
    left_kv = lax.broadcasted_iota(jnp.int32, (win, LANES), 1) < A_HEAD_DIM
    lane = lax.broadcasted_iota(jnp.int32, (sq, LANES), 1)
    for st in range(n_sub_tiles):
        rows = slice(st * sq, (st + 1) * sq)
        mask = band
        if st == 0:
            mask = mask & (kj >= jnp.where(i == 0, A_HALF, 0))
        if st == n_sub_tiles - 1:
            mask = mask & (kj < jnp.where(i == n_tiles - 1, sq + A_HALF, win))
        mask2 = jnp.concatenate([mask, mask], axis=0)
        lse_all = jnp.zeros((sq, LANES), F32)

        def scores(hp):
            sl = slice(hp * LANES, (hp + 1) * LANES)
            q2 = q_ref[rows, sl]
            zero = jnp.zeros_like(q2)
            qq = jnp.concatenate([jnp.where(left, q2, zero), jnp.where(left, zero, q2)], axis=0)
            return _dot_nt(qq, kw[st * sq:st * sq + win, sl])

        s_next = scores(0)
        for hp in range(A_HEADS // 2):
            sl = slice(hp * LANES, (hp + 1) * LANES)
            s = jnp.where(mask2, s_next, NEG_INF)
            if hp + 1 < A_HEADS // 2:
                s_next = scores(hp + 1)
            m = jnp.max(s, axis=-1, keepdims=True)
            p = jnp.exp2(s - m)
            l = jnp.sum(p, axis=-1, keepdims=True)
            r = _dot(p.astype(BF16), vw[st * sq:st * sq + win, sl])
            o_ref[rows, sl] = jnp.where(left, r[:sq], r[sq:]).astype(BF16)
            lse_all = jnp.where(lane == 2 * hp, m[:sq], lse_all)
            lse_all = jnp.where(lane == 2 * hp + 1, m[sq:], lse_all)
            lse_all = jnp.where(lane == A_HEADS + 2 * hp, l[:sq], lse_all)
            lse_all = jnp.where(lane == A_HEADS + 2 * hp + 1, l[sq:], lse_all)
        lse_ref[rows, :] = lse_all


def _attn_group(qkv):
    b, dil, n_sub, _ = qkv.shape
    tq = min(2 * A_SUBTILE, n_sub)
    n_tiles = n_sub // tq
    halo_per_tile = tq // A_HALF
    n_halo = n_sub // A_HALF

    def own(part):
        return pl.BlockSpec((None, None, tq, D_MODEL), lambda bi, r, i: (bi, r, i, part))

    def prev(part):
        return pl.BlockSpec((None, None, A_HALF, D_MODEL),
                            lambda bi, r, i: (bi, r, jnp.maximum(i * halo_per_tile - 1, 0), part))

    def nxt(part):
        return pl.BlockSpec((None, None, A_HALF, D_MODEL),
                            lambda bi, r, i: (bi, r, jnp.minimum((i + 1) * halo_per_tile, n_halo - 1), part))

    return pl.pallas_call(
        functools.partial(_attn_body, n_tiles=n_tiles),
        grid=(b, dil, n_tiles),
        in_specs=[own(0), prev(1), own(1), nxt(1), prev(2), own(2), nxt(2)],
        out_specs=[
            pl.BlockSpec((None, None, tq, D_MODEL), lambda bi, r, i: (bi, r, i, 0)),
            pl.BlockSpec((None, None, tq, LANES), lambda bi, r, i: (bi, r, i, 0)),
        ],
        out_shape=[
            jax.ShapeDtypeStruct((b, dil, n_sub, D_MODEL), BF16),
            jax.ShapeDtypeStruct((b, dil, n_sub, LANES), F32),
        ],
        compiler_params=_params("parallel", "parallel", "parallel"),
        name=f"attn_d{dil}",
    )(qkv, qkv, qkv, qkv, qkv, qkv, qkv)


def _a_out_body(o1_ref, o2_ref, o3_ref, l1_ref, l2_ref, l3_ref, x_ref, w_ref, sp_ref, out_ref,
                os2_ref, os3_ref, ls2_ref, ls3_ref, y_ref):
    tm = x_ref.shape[0]
    for o_ref, l_ref, os_ref, ls_ref in ((o2_ref, l2_ref, os2_ref, ls2_ref), (o3_ref, l3_ref, os3_ref, ls3_ref)):
        dil = o_ref.shape[0]
        rows = tm // dil
        for r in range(dil):
            for cb in range(D_MODEL // LANES):
                os_ref[cb, pl.ds(r, rows, stride=dil), :] = o_ref[r, :, cb * LANES:(cb + 1) * LANES].astype(F32)
            ls_ref[pl.ds(r, rows, stride=dil), :] = l_ref[r]
    s1, s2, s3 = l1_ref[0], ls2_ref[...], ls3_ref[...]
    m = jnp.maximum(jnp.maximum(s1, s2), s3)
    e1, e2, e3 = jnp.exp2(s1 - m), jnp.exp2(s2 - m), jnp.exp2(s3 - m)
    den = sum(e * pltpu.roll(st, LANES - A_HEADS, 1) for e, st in ((e1, s1), (e2, s2), (e3, s3)))
    inv = 1.0 / den
    valid = lax.broadcasted_iota(jnp.int32, (tm, LANES), 1) < A_HEADS
    packed = jnp.zeros((tm, LANES), F32)
    for gidx, e in enumerate((e1, e2, e3)):
        wgt = jnp.where(valid, e * inv, 0.0)
        hi = wgt.astype(BF16).astype(F32)
        for term, val in enumerate((hi, wgt - hi)):
            shift = 2 * A_HEADS * gidx + A_HEADS * term
            packed = packed + (pltpu.roll(val, shift, 1) if shift else val)
    wall = _dot(packed.astype(BF16), sp_ref[...])
    for cb in range(D_MODEL // LANES):
        sl = slice(cb * LANES, (cb + 1) * LANES)
        y = (wall[:, sl] * o1_ref[0, :, sl].astype(F32)
             + wall[:, D_MODEL + cb * LANES:D_MODEL + (cb + 1) * LANES] * os2_ref[cb]
             + wall[:, 2 * D_MODEL + cb * LANES:2 * D_MODEL + (cb + 1) * LANES] * os3_ref[cb])
        y_ref[:, sl] = y.astype(BF16)
    out_ref[...] = x_ref[...] + _dot(y_ref[...], w_ref[...])


def _a_out(outs, lses, x, w):
    b, s, _ = x.shape
    tm = 512
    tok = pl.BlockSpec((None, tm, D_MODEL), lambda bi, i: (bi, i, 0))

    def grouped(arr):
        dil, width = arr.shape[1], arr.shape[3]
        return pl.BlockSpec((None, dil, tm // dil, width), lambda bi, i: (bi, 0, i, 0))

    col = np.arange(A_N_GROUPS * D_MODEL)
    row = np.arange(LANES)
    spread = jnp.asarray((row[:, None] // (2 * A_HEADS) == col[None, :] // D_MODEL)
                         & (row[:, None] % A_HEADS == (col[None, :] % D_MODEL) // A_HEAD_DIM)
                         & (row[:, None] < 2 * A_HEADS * A_N_GROUPS), dtype=BF16)
    return pl.pallas_call(
        _a_out_body,
        grid=(b, s // tm),
        in_specs=[grouped(a) for a in outs] + [grouped(a) for a in lses]
                 + [tok, _resident(w.shape), _resident(spread.shape)],
        out_specs=tok,
        out_shape=jax.ShapeDtypeStruct(x.shape, F32),
        scratch_shapes=[pltpu.VMEM((D_MODEL // LANES, tm, LANES), F32), pltpu.VMEM((D_MODEL // LANES, tm, LANES), F32),
                        pltpu.VMEM((tm, LANES), F32), pltpu.VMEM((tm, LANES), F32),
                        pltpu.VMEM((tm, D_MODEL), BF16)],
        compiler_params=_params("parallel", "parallel"),
        name="a_out",
    )(*outs, *lses, x, w, spread)


def _rope_tables(seq_len):
    inv = ROPE_THETA ** (-jnp.arange(0, A_HEAD_DIM, 2, dtype=F32) / A_HEAD_DIM)
    ang = jnp.arange(seq_len, dtype=F32)[:, None] * inv[None, :]
    cos, sin = jnp.cos(ang), jnp.sin(ang)
    reps = LANES // A_HEAD_DIM
    return (jnp.concatenate([cos, cos] * reps, axis=1),
            jnp.concatenate([-sin, sin] * reps, axis=1))


def _mixer_a(x, g, w_in, w_out):
    b, s, _ = x.shape
    cos_t, sin_t = _rope_tables(s)
    outs, lses = [], []
    for qkv in _a_in(x, g, w_in, cos_t, sin_t):
        o, l = _attn_group(qkv)
        outs.append(o)
        lses.append(l)
    return _a_out(outs, lses, x, w_out)


HALO = 8


def _hy_in_body(xp_ref, x_ref, xn_ref, g_ref, w_ref, cw_ref, cb_ref, x0_ref, vv_ref, u_ref, *, n_tiles):
    tm = x_ref.shape[0]
    i = pl.program_id(1)
    xe = jnp.concatenate([xp_ref[...], x_ref[...], xn_ref[...]], axis=0)
    xn = _rms(xe, g_ref[...]).astype(BF16)
    row = lax.broadcasted_iota(jnp.int32, (tm + 2 * HALO, 1), 0)
    inside = ((row >= HALO) | (i > 0)) & ((row < tm + HALO) | (i < n_tiles - 1))
    parts = []
    for c in range(3):
        lo = c * D_MODEL
        u_ref[...] = jnp.where(inside, _dot(xn, w_ref[:, lo:lo + D_MODEL]), 0.0)
        conv = (u_ref[pl.ds(HALO - 1, tm), :] * cw_ref[0:1, lo:lo + D_MODEL]
                + u_ref[pl.ds(HALO, tm), :] * cw_ref[1:2, lo:lo + D_MODEL]
                + u_ref[pl.ds(HALO + 1, tm), :] * cw_ref[2:3, lo:lo + D_MODEL]
                + cb_ref[:, lo:lo + D_MODEL])
        parts.append(conv)
    x0_ref[...] = parts[0].astype(BF16)
    vv_ref[...] = (parts[2] * parts[1]).astype(BF16)


def _hy_in(x, g, w, conv_w, conv_b):
    b, s, _ = x.shape
    tm = 512
    n_tiles = s // tm
    per = tm // HALO
    n_halo = s // HALO
    tok = pl.BlockSpec((None, tm, D_MODEL), lambda bi, i: (bi, i, 0))
    return pl.pallas_call(
        functools.partial(_hy_in_body, n_tiles=n_tiles),
        grid=(b, n_tiles),
        in_specs=[
            pl.BlockSpec((None, HALO, D_MODEL), lambda bi, i: (bi, jnp.maximum(i * per - 1, 0), 0)),
            tok,
            pl.BlockSpec((None, HALO, D_MODEL), lambda bi, i: (bi, jnp.minimum((i + 1) * per, n_halo - 1), 0)),
            pl.BlockSpec((1, D_MODEL), lambda bi, i: (0, 0)),
            _resident(w.shape),
            pl.BlockSpec(conv_w.shape, lambda bi, i: (0, 0)),
            pl.BlockSpec(conv_b.shape, lambda bi, i: (0, 0)),
        ],
        out_specs=[tok, tok],
        out_shape=[jax.ShapeDtypeStruct(x.shape, BF16), jax.ShapeDtypeStruct(x.shape, BF16)],
        scratch_shapes=[pltpu.VMEM((tm + 2 * HALO, D_MODEL), F32)],
        compiler_params=_params("parallel", "parallel"),
        name="hy_in",
    )(x, x, x, g, w, conv_w, conv_b)


def _hdot(a, b):
    return jnp.dot(a, b, precision=lax.Precision.HIGHEST, preferred_element_type=F32)


def _hy_filter_body(z_ref, t_ref, a_ref, b_ref, w1_ref, b1_ref, w2_ref, b2_ref, w3_ref, b3_ref,
                    wo_ref, fr_ref, dl_ref, h_ref, sum_ref):
    fr = fr_ref[...]
    hid = jnp.sin(fr * (_hdot(z_ref[...], w1_ref[...]) + b1_ref[...]))
    hid = jnp.sin(fr * (_hdot(hid, w2_ref[...]) + b2_ref[...]))
    hid = jnp.sin(fr * (_hdot(hid, w3_ref[...]) + b3_ref[...]))
    decay = jnp.exp(-t_ref[...] * dl_ref[...])
    hid = hid.astype(BF16)
    h_fwd = _dot(hid, wo_ref[:, :D_MODEL]) * decay
    h_bwd = _dot(hid, wo_ref[:, D_MODEL:]) * decay
    h = a_ref[...] * h_fwd + b_ref[...] * h_bwd
    h_ref[...] = h

    @pl.when(pl.program_id(0) == 0)
    def _():
        sum_ref[...] = jnp.zeros_like(sum_ref)

    sum_ref[...] += jnp.sum(jnp.abs(h), axis=0, keepdims=True)


def _hy_filter(seq_len, f_w1, f_b1, f_w2, f_b2, f_w3, f_b3, f_wout, f_freq):
    n = 2 * seq_len
    src = np.concatenate([np.arange(seq_len), [0], np.arange(seq_len - 1, 0, -1)])
    lag = jnp.asarray(src, dtype=F32)[:, None]
    t2 = lag / (seq_len - 1)
    w = 2.0 * math.pi * lag / seq_len
    f = jnp.linspace(1e-4, HY_BANDS - 1, HY_BANDS, dtype=F32)[None, :]
    z2 = jnp.concatenate([t2, jnp.cos(f * w), -jnp.sin(f * w),
                          jnp.zeros((n, HY_HID_PAD - HY_EMB), F32)], axis=-1)
    pos = np.arange(n)
    use_fwd = (pos < seq_len).astype(np.float32)[:, None]
    use_bwd = ((pos == 0) | (pos > seq_len)).astype(np.float32)[:, None]
    max_decay = math.log(HY_DECAY_TARGET) / HY_DECAY_STRONG_PCT
    min_decay = math.log(HY_DECAY_TARGET) / HY_DECAY_WEAK_PCT
    deltas = jnp.abs(jnp.linspace(min_decay, max_decay, D_MODEL, dtype=F32))[None, :]

    def pad2(m, rows):
        return jnp.pad(m, ((0, rows - m.shape[0]), (0, HY_HID_PAD - m.shape[1])))

    def padv(v):
        return jnp.pad(v, (0, HY_HID_PAD - v.shape[0]))[None, :]

    wo = jnp.pad(f_wout, ((0, HY_HID_PAD - f_wout.shape[0]), (0, 0))).astype(BF16)
    tr = 512
    rowblk = lambda width: pl.BlockSpec((tr, width), lambda i: (i, 0))
    full = lambda shape: pl.BlockSpec(shape, lambda i: (0, 0))
    sq = (HY_HID_PAD, HY_HID_PAD)
    vec = (1, HY_HID_PAD)
    return pl.pallas_call(
        _hy_filter_body,
        grid=(n // tr,),
        in_specs=[rowblk(HY_HID_PAD), rowblk(1), rowblk(1), rowblk(1),
                  full(sq), full(vec), full(sq), full(vec), full(sq), full(vec),
                  full(wo.shape), full(vec), full((1, D_MODEL))],
        out_specs=[rowblk(D_MODEL), full((1, D_MODEL))],
        out_shape=[jax.ShapeDtypeStruct((n, D_MODEL), F32), jax.ShapeDtypeStruct((1, D_MODEL), F32)],
        compiler_params=_params("arbitrary"),
        name="hy_filter",
    )(z2, t2, jnp.asarray(use_fwd), jnp.asarray(use_bwd),
      pad2(f_w1, HY_HID_PAD), padv(f_b1), pad2(f_w2, HY_HID_PAD), padv(f_b2),
      pad2(f_w3, HY_HID_PAD), padv(f_b3), wo, padv(f_freq), deltas)


def _fft_split(n):
    n1 = 1 << ((n.bit_length() - 1 + 1) // 2)
    return n1, n // n1


def _fft_tables(n1, n2):
    n = n1 * n2
    h = n1 // 2
    idx = np.arange(n1)
    ang = -2.0 * np.pi * ((idx[:, None] * idx[None, :]) % n1) / n1
    fr, fi = np.cos(ang), np.sin(ang)
    m1_data = np.block([[fr[:, :h], -fi[:, :h]], [fi[:, :h], fr[:, :h]]])
    m1_filt = np.concatenate([fr, fi], axis=0)
    ifr, ifi = fr.T[:h] / n, -fi.T[:h] / n
    m3 = np.block([[ifr, -ifi], [ifi, ifr]])
    k1 = jnp.arange(n1, dtype=jnp.int32)[:, None, None]
    k2 = jnp.arange(n2, dtype=jnp.int32)[None, :, None]
    i2 = jnp.arange(n2, dtype=jnp.int32)[None, None, :]
    phase = (i2 * k1 + n1 * i2 * k2) % n
    ga = (-2.0 * math.pi / n) * phase.astype(F32)
    gr, gi = jnp.cos(ga), jnp.sin(ga)
    g_fwd = jnp.concatenate([jnp.concatenate([gr, -gi], axis=2), jnp.concatenate([gi, gr], axis=2)], axis=1)
    grt, git = jnp.swapaxes(gr, 1, 2), jnp.swapaxes(gi, 1, 2)
    g_inv = jnp.concatenate([jnp.concatenate([grt, git], axis=2), jnp.concatenate([-git, grt], axis=2)], axis=1)
    as_bf = lambda m: jnp.asarray(m, dtype=F32).astype(BF16)
    return as_bf(m1_data), as_bf(m1_filt), as_bf(m3), g_fwd.astype(BF16), g_inv.astype(BF16)


FFT_SUB = 16
FFT_TC = 512


def _dft_outer_body(m_ref, pm_ref, z_ref, o_ref, zs_ref, rb_ref):
    q, sub, tc = z_ref.shape
    r = m_ref.shape[0]
    nlb = tc // LANES
    for cb in range(nlb):
        zs_ref[cb] = z_ref[:, :, cb * LANES:(cb + 1) * LANES].astype(F32).reshape(q * sub, LANES)
    for j in range(sub):
        z = jnp.concatenate([zs_ref[cb, pl.ds(j, q, stride=sub), :] for cb in range(nlb)], axis=1)
        rb_ref[j] = _dot(m_ref[...], z.astype(BF16)).astype(BF16)
    for g in range(r // sub):
        part, i0 = divmod(g * sub, r // 2)
        grouped = rb_ref[:, g * sub:(g + 1) * sub, :].reshape(sub * sub, tc)
        o_ref[part, i0:i0 + sub] = _dot(pm_ref[...], grouped).reshape(sub, sub, tc).astype(BF16)


def _dft_outer(mat, perm, z5, name):
    p, q, nh, sub, c = z5.shape
    r = mat.shape[0]
    tc = FFT_TC
    return pl.pallas_call(
        _dft_outer_body,
        grid=(p, nh, c // tc),
        in_specs=[pl.BlockSpec(mat.shape, lambda pi, h, ci: (0, 0)),
                  pl.BlockSpec(perm.shape, lambda pi, h, ci: (0, 0)),
                  pl.BlockSpec((None, q, None, sub, tc), lambda pi, h, ci: (pi, 0, h, 0, ci))],
        out_specs=pl.BlockSpec((None, None, 2, r // 2, sub, tc), lambda pi, h, ci: (pi, h, 0, 0, 0, ci)),
        out_shape=jax.ShapeDtypeStruct((p, nh, 2, r // 2, sub, c), BF16),
        scratch_shapes=[pltpu.VMEM((tc // LANES, q * sub, LANES), F32),
                        pltpu.VMEM((sub, r, tc), BF16)],
        compiler_params=_params("parallel", "parallel", "parallel"),
        name=name,
    )(mat, perm, z5)


def _idft_outer_body(m_ref, pm_ref, b_ref, o_ref, tb_ref, ys_ref):
    _, n1, sub, tc = b_ref.shape
    q = m_ref.shape[0]
    nlb = tc // LANES
    for g in range(2 * n1 // sub):
        part, i0 = divmod(g * sub, n1)
        grouped = b_ref[part, i0:i0 + sub].reshape(sub * sub, tc)
        tb_ref[:, g * sub:(g + 1) * sub, :] = _dot(pm_ref[...], grouped).reshape(sub, sub, tc).astype(BF16)
    for j in range(sub):
        res = _dot(m_ref[...], tb_ref[j])
        for cb in range(nlb):
            ys_ref[cb, pl.ds(j, q, stride=sub), :] = res[:, cb * LANES:(cb + 1) * LANES]
    for cb in range(nlb):
        o_ref[:, :, cb * LANES:(cb + 1) * LANES] = ys_ref[cb].reshape(q, sub, LANES)


def _idft_outer(mat, perm, b6):
    p, nh, _, n1, sub, c = b6.shape
    q = mat.shape[0]
    tc = FFT_TC
    return pl.pallas_call(
        _idft_outer_body,
        grid=(p, nh, c // tc),
        in_specs=[pl.BlockSpec(mat.shape, lambda pi, h, ci: (0, 0)),
                  pl.BlockSpec(perm.shape, lambda pi, h, ci: (0, 0)),
                  pl.BlockSpec((None, None, 2, n1, sub, tc), lambda pi, h, ci: (pi, h, 0, 0, 0, ci))],
        out_specs=pl.BlockSpec((None, q, None, sub, tc), lambda pi, h, ci: (pi, 0, h, 0, ci)),
        out_shape=jax.ShapeDtypeStruct((p, q, nh, sub, c), F32),
        scratch_shapes=[pltpu.VMEM((sub, 2 * n1, tc), BF16),
                        pltpu.VMEM((tc // LANES, q * sub, LANES), F32)],
        compiler_params=_params("parallel", "parallel", "parallel"),
        name="hy_idft",
    )(mat, perm, b6)


FFT_KB = 4


def _stack_re_im(a_ref, kk):
    nh, _, _, sub, ct = a_ref.shape
    return jnp.concatenate([a_ref[:, 0, kk].reshape(nh * sub, ct), a_ref[:, 1, kk].reshape(nh * sub, ct)], axis=0)


def _spec_filter_body(g_ref, a_ref, sc_ref, h_ref):
    n2 = g_ref.shape[1] // 2
    for kk in range(g_ref.shape[0]):
        spec = _dot(g_ref[kk], _stack_re_im(a_ref, kk)) * sc_ref[...]
        h_ref[kk] = spec.reshape(2, n2, spec.shape[1])


def _spec_filter(g_fwd, a6, scale):
    _, nh, _, n1, sub, c = a6.shape
    n2 = nh * sub
    return pl.pallas_call(
        _spec_filter_body,
        grid=(n1 // FFT_KB,),
        in_specs=[pl.BlockSpec((FFT_KB, 2 * n2, 2 * n2), lambda k: (k, 0, 0)),
                  pl.BlockSpec((None, nh, 2, FFT_KB, sub, c), lambda k: (0, 0, 0, k, 0, 0)),
                  pl.BlockSpec((1, c), lambda k: (0, 0))],
        out_specs=pl.BlockSpec((FFT_KB, 2, n2, c), lambda k: (k, 0, 0, 0)),
        out_shape=jax.ShapeDtypeStruct((n1, 2, n2, c), F32),
        compiler_params=_params("parallel"),
        name="hy_spec_filter",
    )(g_fwd, a6, scale)


def _spec_body(gf_ref, gi_ref, h_ref, a_ref, o_ref):
    nh, _, kb, sub, ct = a_ref.shape
    n2 = nh * sub
    for kk in range(kb):
        spec = _dot(gf_ref[kk], _stack_re_im(a_ref, kk))
        xr, xi = spec[:n2], spec[n2:]
        hr, hi = h_ref[kk, 0], h_ref[kk, 1]
        y = jnp.concatenate([xr * hr - xi * hi, xr * hi + xi * hr], axis=0).astype(BF16)
        back = _dot(gi_ref[kk], y)
        o_ref[:, 0, kk] = back[:n2].reshape(nh, sub, ct).astype(BF16)
        o_ref[:, 1, kk] = back[n2:].reshape(nh, sub, ct).astype(BF16)


def _spec(g_fwd, g_inv, hspec, a6):
    p, nh, _, n1, sub, c = a6.shape
    n2 = nh * sub
    blk = pl.BlockSpec((None, nh, 2, FFT_KB, sub, c), lambda k, pi: (pi, 0, 0, k, 0, 0))
    mat = pl.BlockSpec((FFT_KB, 2 * n2, 2 * n2), lambda k, pi: (k, 0, 0))
    return pl.pallas_call(
        _spec_body,
        grid=(n1 // FFT_KB, p),
        in_specs=[mat, mat, pl.BlockSpec((FFT_KB, 2, n2, c), lambda k, pi: (k, 0, 0, 0)), blk],
        out_specs=blk,
        out_shape=jax.ShapeDtypeStruct(a6.shape, BF16),
        compiler_params=_params("parallel", "parallel"),
        name="hy_spec",
    )(g_fwd, g_inv, hspec, a6)


def _long_conv(vv, h_raw, h_norm):
    b, l, c = vv.shape
    n1, n2 = _fft_split(2 * l)
    nh = n2 // FFT_SUB
    m1_data, m1_filt, m3, g_fwd, g_inv = _fft_tables(n1, n2)
    idx = np.arange(FFT_SUB * FFT_SUB)
    swapped = (idx % FFT_SUB) * FFT_SUB + idx // FFT_SUB
    perm = jnp.asarray(idx[None, :] == swapped[:, None], dtype=BF16)
    a_h = _dft_outer(m1_filt, perm, h_raw.reshape(1, n1, nh, FFT_SUB, c), "hy_dft1_filter")
    hspec = _spec_filter(g_fwd, a_h, 1.0 / h_norm)
    a = _dft_outer(m1_data, perm, vv.reshape(b // 2, n1, nh, FFT_SUB, c), "hy_dft1")
    y = _idft_outer(m3, perm, _spec(g_fwd, g_inv, hspec, a))
    return y.reshape(b, l, c)


def _hy_out_body(cv_ref, vv_ref, x0_ref, bd_ref, x_ref, w_ref, o_ref):
    y = ((cv_ref[...] + bd_ref[...] * vv_ref[...].astype(F32)) * x0_ref[...].astype(F32)).astype(BF16)
    o_ref[...] = x_ref[...] + _dot(y, w_ref[...])


def _hy_out(conv, vv, x0, bias_d, x2, w):
    t = x2.shape[0]
    tm = 512
    tok = pl.BlockSpec((tm, D_MODEL), lambda i: (i, 0))
    return pl.pallas_call(
        _hy_out_body,
        grid=(t // tm,),
        in_specs=[tok, tok, tok, pl.BlockSpec((1, D_MODEL), lambda i: (0, 0)), tok, _resident(w.shape)],
        out_specs=tok,
        out_shape=jax.ShapeDtypeStruct((t, D_MODEL), F32),
        compiler_params=_params("parallel"),
        name="hy_out",
    )(conv.reshape(t, D_MODEL), vv.reshape(t, D_MODEL), x0.reshape(t, D_MODEL), bias_d, x2, w)


def _mixer_b(x, g, w_in, conv_w, conv_b, filt, bias_d, w_out):
    b, s, _ = x.shape
    x0, vv = _hy_in(x, g, w_in, conv_w, conv_b)
    h_raw, h_norm = _hy_filter(s, *filt)
    conv = _long_conv(vv, h_raw, h_norm)
    return _hy_out(conv, vv, x0, bias_d, x.reshape(b * s, D_MODEL), w_out).reshape(x.shape)


def _gelu(z):
    return 0.5 * z * (1.0 + lax.erf(z * (2.0 ** -0.5)))


def _sgu_body(x_ref, g_ref, win_ref, lng_ref, lnb_ref, ws_ref, bs_ref, wout_ref, o_ref, y_ref):
    tm = x_ref.shape[0]
    x = x_ref[...]
    xn = _rms(x, g_ref[...]).astype(BF16)
    zu = _gelu(_dot(xn, win_ref[:, :D_MODEL]))
    zv = _gelu(_dot(xn, win_ref[:, D_MODEL:]))
    zc = zv - jnp.mean(zv, axis=-1, keepdims=True)
    zv = zc * lax.rsqrt(jnp.mean(zc * zc, axis=-1, keepdims=True) + EPS) * lng_ref[...] + lnb_ref[...]
    zvb = zv.astype(BF16)
    for c in range(tm // C_CHUNK):
        rows = slice(c * C_CHUNK, (c + 1) * C_CHUNK)
        for h in range(C_GROUPS):
            cols = slice(h * LANES, (h + 1) * LANES)
            sv = _dot(ws_ref[h], zvb[rows, cols]) + bs_ref[:, h:h + 1]
            y_ref[rows, cols] = (zu[rows, cols] * sv).astype(BF16)
    o_ref[...] = x + _dot(y_ref[...], wout_ref[...])


def _mixer_c(x, g, w_in, ln_g, ln_b, w_s, b_s_t, w_out):
    b, s, _ = x.shape
    t = b * s
    tm = 512
    tok = pl.BlockSpec((tm, D_MODEL), lambda i: (i, 0))
    row = pl.BlockSpec((1, D_MODEL), lambda i: (0, 0))
    return pl.pallas_call(
        _sgu_body,
        grid=(t // tm,),
        in_specs=[tok, row, _resident(w_in.shape), row, row, _resident(w_s.shape),
                  pl.BlockSpec(b_s_t.shape, lambda i: (0, 0)), _resident(w_out.shape)],
        out_specs=tok,
        out_shape=jax.ShapeDtypeStruct((t, D_MODEL), F32),
        scratch_shapes=[pltpu.VMEM((tm, D_MODEL), BF16)],
        compiler_params=_params("parallel"),
        name="sgu",
    )(x.reshape(t, D_MODEL), g, w_in, ln_g, ln_b, w_s, b_s_t, w_out).reshape(x.shape)


def _trunk(x, kvs, boff, w):
    b, s, _ = x.shape
    t = b * s
    for i in range(DEPTH):
        kind, j = i % N_MIXERS, i // N_MIXERS
        g = w["g_mix"][i][None, :]
        if kind == 0:
            x = _mixer_a(x, g, w["a_w_in"][j], w["a_w_out"][j])
        elif kind == 1:
            filt = tuple(w[k][j] for k in ("b_f_w1", "b_f_b1", "b_f_w2", "b_f_b2", "b_f_w3", "b_f_b3",
                                          "b_f_wout", "b_f_freq"))
            x = _mixer_b(x, g, w["b_w_in"][j], w["b_conv_w"][j], w["b_conv_b"][j][None, :], filt,
                         w["b_bias_d"][j][None, :], w["b_w_out"][j])
        else:
            x = _mixer_c(x, g, w["c_w_in"][j], w["c_ln_g"][j][None, :], w["c_ln_b"][j][None, :],
                         w["c_w_s"][j], w["c_b_s"][j].T, w["c_w_out"][j])
        x = _xattn(x, w["g_cross"][i][None, :], w["x_w_q"][i], kvs[i], boff, w["x_w_o"][i])
        g_final = w["g_final"][None, :] if i == DEPTH - 1 else None
        x = _ffn(x.reshape(t, D_MODEL), w["g_ffn"][i][None, :], w["f_w_gu"][i], w["f_w_down"][i],
                 g_final).reshape(b, s, D_MODEL)
    return x


_BF16_WEIGHTS = ("a_w_in", "a_w_out", "b_w_in", "b_w_out", "c_w_in", "c_w_s", "c_w_out",
                 "x_w_q", "x_w_kv", "x_w_o", "f_w_gu", "f_w_down")


def kernel(x_prompt, x_sample, mem_prompt, mem_sample, g_mix, g_cross, g_ffn, g_final, a_w_in, a_w_out, b_w_in, b_conv_w, b_conv_b, b_f_w1, b_f_b1, b_f_w2, b_f_b2, b_f_w3, b_f_b3, b_f_wout, b_f_freq, b_bias_d, b_w_out, c_w_in, c_ln_g, c_ln_b, c_w_s, c_b_s, c_w_out, x_w_q, x_w_kv, x_w_o, f_w_gu, f_w_down):
    w = dict(g_mix=g_mix, g_cross=g_cross, g_ffn=g_ffn, g_final=g_final,
             a_w_in=a_w_in, a_w_out=a_w_out, b_w_in=b_w_in, b_conv_w=b_conv_w, b_conv_b=b_conv_b,
             b_f_w1=b_f_w1, b_f_b1=b_f_b1, b_f_w2=b_f_w2, b_f_b2=b_f_b2, b_f_w3=b_f_w3,
             b_f_b3=b_f_b3, b_f_wout=b_f_wout, b_f_freq=b_f_freq, b_bias_d=b_bias_d,
             b_w_out=b_w_out, c_w_in=c_w_in, c_ln_g=c_ln_g, c_ln_b=c_ln_b, c_w_s=c_w_s,
             c_b_s=c_b_s, c_w_out=c_w_out, x_w_q=x_w_q, x_w_kv=x_w_kv, x_w_o=x_w_o,
             f_w_gu=f_w_gu, f_w_down=f_w_down)
    for name in _BF16_WEIGHTS:
        w[name] = w[name].astype(BF16)
    nb_prompt = mem_prompt.shape[0]
    mem = jnp.concatenate([mem_prompt, mem_sample], axis=0)
    mem2 = mem.reshape(mem.shape[0] * MEM_LEN, D_MODEL)
    kvs = [_kv_proj(mem2, w["x_w_kv"][i]).reshape(mem.shape[0], MEM_LEN, 2 * D_MODEL) for i in range(DEPTH)]
    y_prompt = _trunk(x_prompt, kvs, 0, w)
    y_sample = _trunk(x_sample, kvs, nb_prompt, w)
    return (y_prompt, y_sample)
```

```python
import functools
import math

import numpy as np
import jax
import jax.numpy as jnp
from jax import lax
from jax.experimental import pallas as pl
from jax.experimental.pallas import tpu as pltpu

F32 = jnp.float32
BF16 = jnp.bfloat16

D_MODEL = 1024
DEPTH = 4
N_MIXERS = 3

A_GROUPS = ((128, 1), (512, 4), (2048, 16))
A_N_GROUPS = len(A_GROUPS)
A_HEADS = 16
A_HEAD_DIM = D_MODEL // A_HEADS
A_HALF = 64
A_SUBTILE = 128
A_Q_SCALE = A_HEAD_DIM ** -0.5 * math.log2(math.e)
ROPE_THETA = 10000.0

HY_EMB = 33
HY_BANDS = (HY_EMB - 1) // 2
HY_HID_PAD = 128
HY_DECAY_TARGET = 1e-2
HY_DECAY_STRONG_PCT = 0.3
HY_DECAY_WEAK_PCT = 1.5

C_CHUNK = 128
C_GROUPS = 8

MEM_LEN = 256
X_HEADS = 4
X_HEAD_DIM = D_MODEL // X_HEADS

D_FF = -(-8 * D_MODEL // (3 * 256)) * 256
FF_CHUNK = 256

EPS = 1e-6
NEG_INF = -1e30

LANES = 128
VMEM_LIMIT = 56 * 1024 * 1024


def _params(*sem):
    return pltpu.CompilerParams(dimension_semantics=sem, vmem_limit_bytes=VMEM_LIMIT)


def _resident(shape):
    nd = len(shape)
    return pl.BlockSpec(shape, lambda *_: (0,) * nd, pipeline_mode=pl.Buffered(1))


def _dot(a, b):
    return jnp.dot(a, b, preferred_element_type=F32)


def _dot_nt(a, b):
    return lax.dot_general(a, b, (((1,), (1,)), ((), ())), preferred_element_type=F32)


def _rms(x, g):
    return x * lax.rsqrt(jnp.mean(x * x, axis=-1, keepdims=True) + EPS) * g


def _ffn_body(*refs, final):
    if final:
        x_ref, g_ref, wgu_ref, wd_ref, gf_ref, o_ref, act_ref = refs
    else:
        x_ref, g_ref, wgu_ref, wd_ref, o_ref, act_ref = refs
    x = x_ref[...]
    xn = _rms(x, g_ref[...]).astype(BF16)
    for c in range(D_FF // FF_CHUNK):
        lo = c * FF_CHUNK
        gate = _dot(xn, wgu_ref[:, lo:lo + FF_CHUNK])
        up = _dot(xn, wgu_ref[:, D_FF + lo:D_FF + lo + FF_CHUNK])
        act_ref[:, lo:lo + FF_CHUNK] = (gate / (1.0 + jnp.exp(-gate)) * up).astype(BF16)
    y = x + _dot(act_ref[...], wd_ref[...])
    if final:
        y = _rms(y, gf_ref[...])
    o_ref[...] = y


def _ffn(x2, g, wgu, wd, g_final=None):
    t = x2.shape[0]
    tm = 512
    final = g_final is not None
    tok = pl.BlockSpec((tm, D_MODEL), lambda i: (i, 0))
    row = pl.BlockSpec((1, D_MODEL), lambda i: (0, 0))
    in_specs = [tok, row, _resident(wgu.shape), _resident(wd.shape)]
    args = [x2, g, wgu, wd]
    if final:
        in_specs.append(row)
        args.append(g_final)
    return pl.pallas_call(
        functools.partial(_ffn_body, final=final),
        grid=(t // tm,),
        in_specs=in_specs,
        out_specs=tok,
        out_shape=jax.ShapeDtypeStruct((t, D_MODEL), F32),
        scratch_shapes=[pltpu.VMEM((tm, D_FF), BF16)],
        compiler_params=_params("parallel"),
        name="ffn_final" if final else "ffn",
    )(*args)


def _kv_body(m_ref, w_ref, o_ref):
    o_ref[...] = _dot(m_ref[...].astype(BF16), w_ref[...]).astype(BF16)


def _kv_proj(mem2, wkv):
    r = mem2.shape[0]
    tm = 256
    return pl.pallas_call(
        _kv_body,
        grid=(r // tm,),
        in_specs=[pl.BlockSpec((tm, D_MODEL), lambda i: (i, 0)), _resident(wkv.shape)],
        out_specs=pl.BlockSpec((tm, 2 * D_MODEL), lambda i: (i, 0)),
        out_shape=jax.ShapeDtypeStruct((r, 2 * D_MODEL), BF16),
        compiler_params=_params("parallel"),
        name="kv_proj",
    )(mem2, wkv)


def _xattn_stage(x, g_ref, wq_ref, k_ref, v_ref, wo_ref, y_ref):
    xn = _rms(x, g_ref[...]).astype(BF16)
    q = (_dot(xn, wq_ref[...]) * (X_HEAD_DIM ** -0.5)).astype(BF16)
    for h in range(X_HEADS):
        sl = slice(h * X_HEAD_DIM, (h + 1) * X_HEAD_DIM)
        s = _dot_nt(q[:, sl], k_ref[:, sl])
        p = jnp.exp(s - jnp.max(s, axis=-1, keepdims=True))
        inv = 1.0 / jnp.sum(p, axis=-1, keepdims=True)
        y_ref[:, sl] = (_dot(p.astype(BF16), v_ref[:, sl]) * inv).astype(BF16)
    return x + _dot(y_ref[...], wo_ref[...])


def _xattn_body(*refs, mixer_out, n_mixer_in, n_mixer_scratch):
    n_in = n_mixer_in + 6
    mixer_in, (x_ref, g_ref, wq_ref, k_ref, v_ref, wo_ref) = refs[:n_mixer_in], refs[n_mixer_in:n_in]
    o_ref = refs[n_in]
    mixer_scratch, y_ref = refs[n_in + 1:n_in + 1 + n_mixer_scratch], refs[n_in + 1 + n_mixer_scratch]
    x = x_ref[...] if mixer_out is None else mixer_out(*mixer_in, x_ref, *mixer_scratch)
    o_ref[...] = _xattn_stage(x, g_ref, wq_ref, k_ref, v_ref, wo_ref, y_ref)


def _xattn(x, g, wq, kv, boff, wo, mixer=None):
    b, s, _ = x.shape
    if mixer is None:
        stage, m_args, m_specs, m_scratch, tm, name = None, [], [], [], 1024, "xattn"
    else:
        stage, m_args, specs_fn, scratch_fn, tm, name = mixer
        m_specs, m_scratch = specs_fn(tm), scratch_fn(tm)
    tok = pl.BlockSpec((None, tm, D_MODEL), lambda bi, i: (bi, i, 0))
    return pl.pallas_call(
        functools.partial(_xattn_body, mixer_out=stage, n_mixer_in=len(m_args), n_mixer_scratch=len(m_scratch)),
        grid=(b, s // tm),
        in_specs=m_specs + [
            tok,
            pl.BlockSpec((1, D_MODEL), lambda bi, i: (0, 0)),
            _resident(wq.shape),
            pl.BlockSpec((None, MEM_LEN, D_MODEL), lambda bi, i: (bi + boff, 0, 0)),
            pl.BlockSpec((None, MEM_LEN, D_MODEL), lambda bi, i: (bi + boff, 0, 1)),
            _resident(wo.shape),
        ],
        out_specs=tok,
        out_shape=jax.ShapeDtypeStruct(x.shape, F32),
        scratch_shapes=m_scratch + [pltpu.VMEM((tm, D_MODEL), BF16)],
        compiler_params=_params("parallel", "parallel"),
        name=name,
    )(*m_args, x, g, wq, kv, kv, wo)


def _a_in_body(x_ref, g_ref, w_ref, cos_ref, sin_ref, *rest):
    o_refs, (xs_ref, xp_ref) = rest[:A_N_GROUPS], rest[A_N_GROUPS:]
    tm = x_ref.shape[0]
    xn_f32 = _rms(x_ref[...], g_ref[...])
    reps = D_MODEL // LANES
    for cb in range(reps):
        xs_ref[cb] = xn_f32[:, cb * LANES:(cb + 1) * LANES]
    lane = lax.broadcasted_iota(jnp.int32, (tm, LANES), 1)
    low_half = (lane & (A_HEAD_DIM // 2)) == 0
    for gidx, (o_ref, (_, dil)) in enumerate(zip(o_refs, A_GROUPS)):
        rows = tm // dil
        if dil == 1:
            xn = xn_f32.astype(BF16)
            cos, sin = cos_ref[...], sin_ref[...]
        else:
            for r in range(dil):
                for cb in range(reps):
                    xp_ref[r * rows:(r + 1) * rows, cb * LANES:(cb + 1) * LANES] = (
                        xs_ref[cb, pl.ds(r, rows, stride=dil), :].astype(BF16))
            xn = xp_ref[...]
            cos = jnp.concatenate([cos_ref[pl.ds(r, rows, stride=dil), :] for r in range(dil)], axis=0)
            sin = jnp.concatenate([sin_ref[pl.ds(r, rows, stride=dil), :] for r in range(dil)], axis=0)
        for part in range(3):
            lo = (gidx * 3 + part) * D_MODEL
            y = _dot(xn, w_ref[:, lo:lo + D_MODEL])
            if part == 2:
                o_ref[:, :, 2 * D_MODEL:] = y.reshape(dil, rows, D_MODEL).astype(BF16)
                continue
            c_t, s_t = (cos * A_Q_SCALE, sin * A_Q_SCALE) if part == 0 else (cos, sin)
            for cb in range(reps):
                yb = y[:, cb * LANES:(cb + 1) * LANES]
                partner = jnp.where(low_half, pltpu.roll(yb, LANES - A_HEAD_DIM // 2, 1),
                                    pltpu.roll(yb, A_HEAD_DIM // 2, 1))
                col = part * D_MODEL + cb * LANES
                o_ref[:, :, col:col + LANES] = (yb * c_t + partner * s_t).reshape(dil, rows, LANES).astype(BF16)


def _a_in(x, g, w, cos_t, sin_t):
    b, s, _ = x.shape
    tm = 512
    return pl.pallas_call(
        _a_in_body,
        grid=(b, s // tm),
        in_specs=[
            pl.BlockSpec((None, tm, D_MODEL), lambda bi, i: (bi, i, 0)),
            pl.BlockSpec((1, D_MODEL), lambda bi, i: (0, 0)),
            _resident(w.shape),
            pl.BlockSpec((tm, LANES), lambda bi, i: (i, 0)),
            pl.BlockSpec((tm, LANES), lambda bi, i: (i, 0)),
        ],
        out_specs=[pl.BlockSpec((None, dil, tm // dil, 3 * D_MODEL), lambda bi, i: (bi, 0, i, 0))
                   for _, dil in A_GROUPS],
        out_shape=[jax.ShapeDtypeStruct((b, dil, s // dil, 3 * D_MODEL), BF16) for _, dil in A_GROUPS],
        scratch_shapes=[pltpu.VMEM((D_MODEL // LANES, tm, LANES), F32), pltpu.VMEM((tm, D_MODEL), BF16)],
        compiler_params=_params("parallel", "parallel"),
        name="a_in",
    )(x, g, w, cos_t, sin_t)


def _attn_body(q_ref, kp_ref, k_ref, kn_ref, vp_ref, v_ref, vn_ref, o_ref, lse_ref, *, n_tiles):
    tq = q_ref.shape[0]
    sq = A_SUBTILE
    win = sq + 2 * A_HALF
    n_sub_tiles = tq // sq
    i = pl.program_id(2)
    kw = jnp.concatenate([kp_ref[...], k_ref[...], kn_ref[...]], axis=0)
    vw = jnp.concatenate([vp_ref[...], v_ref[...], vn_ref[...]], axis=0)
    qi = lax.broadcasted_iota(jnp.int32, (sq, win), 0)
    kj = lax.broadcasted_iota(jnp.int32, (sq, win), 1)
    band = (kj >= qi) & (kj <= qi + 2 * A_HALF)
    left = lax.broadcasted_iota(jnp.int32, (sq, LANES), 1) < A_HEAD_DIM
    lane = lax.broadcasted_iota(jnp.int32, (sq, LANES), 1)
    for st in range(n_sub_tiles):
        rows = slice(st * sq, (st + 1) * sq)
        mask = band
        if st == 0:
            mask = mask & (kj >= jnp.where(i == 0, A_HALF, 0))
        if st == n_sub_tiles - 1:
            mask = mask & (kj < jnp.where(i == n_tiles - 1, sq + A_HALF, win))
        mask2 = jnp.concatenate([mask, mask], axis=0)
        lse_ref[rows, :] = jnp.zeros((sq, LANES), F32)

        def scores(hp):
            sl = slice(hp * LANES, (hp + 1) * LANES)
            q2 = q_ref[rows, sl]
            zero = jnp.zeros_like(q2)
            qq = jnp.concatenate([jnp.where(left, q2, zero), jnp.where(left, zero, q2)], axis=0)
            return _dot_nt(qq, kw[st * sq:st * sq + win, sl])

        s_next = scores(0)
        for hp in range(A_HEADS // 2):
            sl = slice(hp * LANES, (hp + 1) * LANES)
            s = jnp.where(mask2, s_next, NEG_INF)
            if hp + 1 < A_HEADS // 2:
                s_next = scores(hp + 1)
            m = jnp.max(s, axis=-1, keepdims=True)
            p = jnp.exp2(s - m)
            l = jnp.sum(p, axis=-1, keepdims=True)
            r = _dot(p.astype(BF16), vw[st * sq:st * sq + win, sl])
            o_ref[rows, sl] = jnp.where(left, r[:sq], r[sq:]).astype(BF16)
            for off, col in ((0, m), (A_HEADS, l)):
                lse_ref[rows, off + 2 * hp:off + 2 * hp + 1] = col[:sq]
                lse_ref[rows, off + 2 * hp + 1:off + 2 * hp + 2] = col[sq:]


def _attn_group(qkv):
    b, dil, n_sub, _ = qkv.shape
    tq = min(2 * A_SUBTILE, n_sub)
    n_tiles = n_sub // tq
    halo_per_tile = tq // A_HALF
    n_halo = n_sub // A_HALF

    def own(part):
        return pl.BlockSpec((None, None, tq, D_MODEL), lambda bi, r, i: (bi, r, i, part))

    def prev(part):
        return pl.BlockSpec((None, None, A_HALF, D_MODEL),
                            lambda bi, r, i: (bi, r, jnp.maximum(i * halo_per_tile - 1, 0), part))

    def nxt(part):
        return pl.BlockSpec((None, None, A_HALF, D_MODEL),
                            lambda bi, r, i: (bi, r, jnp.minimum((i + 1) * halo_per_tile, n_halo - 1), part))

    return pl.pallas_call(
        functools.partial(_attn_body, n_tiles=n_tiles),
        grid=(b, dil, n_tiles),
        in_specs=[own(0), prev(1), own(1), nxt(1), prev(2), own(2), nxt(2)],
        out_specs=[
            pl.BlockSpec((None, None, tq, D_MODEL), lambda bi, r, i: (bi, r, i, 0)),
            pl.BlockSpec((None, None, tq, LANES), lambda bi, r, i: (bi, r, i, 0)),
        ],
        out_shape=[
            jax.ShapeDtypeStruct((b, dil, n_sub, D_MODEL), BF16),
            jax.ShapeDtypeStruct((b, dil, n_sub, LANES), F32),
        ],
        compiler_params=_params("parallel", "parallel", "parallel"),
        name=f"attn_d{dil}",
    )(qkv, qkv, qkv, qkv, qkv, qkv, qkv)


def _a_out_stage(o1_ref, o2_ref, o3_ref, l1_ref, l2_ref, l3_ref, w_ref, sp_ref, x_ref,
                 os2_ref, os3_ref, ls2_ref, ls3_ref, y_ref):
    tm = x_ref.shape[0]
    for o_ref, l_ref, os_ref, ls_ref in ((o2_ref, l2_ref, os2_ref, ls2_ref), (o3_ref, l3_ref, os3_ref, ls3_ref)):
        dil = o_ref.shape[0]
        rows = tm // dil
        for r in range(dil):
            for cb in range(D_MODEL // LANES):
                os_ref[cb, pl.ds(r, rows, stride=dil), :] = o_ref[r, :, cb * LANES:(cb + 1) * LANES].astype(F32)
            ls_ref[pl.ds(r, rows, stride=dil), :] = l_ref[r]
    s1, s2, s3 = l1_ref[0], ls2_ref[...], ls3_ref[...]
    m = jnp.maximum(jnp.maximum(s1, s2), s3)
    e1, e2, e3 = jnp.exp2(s1 - m), jnp.exp2(s2 - m), jnp.exp2(s3 - m)
    den = sum(e * pltpu.roll(st, LANES - A_HEADS, 1) for e, st in ((e1, s1), (e2, s2), (e3, s3)))
    inv = 1.0 / den
    valid = lax.broadcasted_iota(jnp.int32, (tm, LANES), 1) < A_HEADS
    packed = jnp.zeros((tm, LANES), F32)
    for gidx, e in enumerate((e1, e2, e3)):
        wgt = jnp.where(valid, e * inv, 0.0)
        hi = wgt.astype(BF16).astype(F32)
        for term, val in enumerate((hi, wgt - hi)):
            shift = 2 * A_HEADS * gidx + A_HEADS * term
            packed = packed + (pltpu.roll(val, shift, 1) if shift else val)
    wall = _dot(packed.astype(BF16), sp_ref[...])
    for cb in range(D_MODEL // LANES):
        sl = slice(cb * LANES, (cb + 1) * LANES)
        y = (wall[:, sl] * o1_ref[0, :, sl].astype(F32)
             + wall[:, D_MODEL + cb * LANES:D_MODEL + (cb + 1) * LANES] * os2_ref[cb]
             + wall[:, 2 * D_MODEL + cb * LANES:2 * D_MODEL + (cb + 1) * LANES] * os3_ref[cb])
        y_ref[:, sl] = y.astype(BF16)
    return x_ref[...] + _dot(y_ref[...], w_ref[...])


def _a_out_mixer(outs, lses, w):
    col = np.arange(A_N_GROUPS * D_MODEL)
    row = np.arange(LANES)
    spread = jnp.asarray((row[:, None] // (2 * A_HEADS) == col[None, :] // D_MODEL)
                         & (row[:, None] % A_HEADS == (col[None, :] % D_MODEL) // A_HEAD_DIM)
                         & (row[:, None] < 2 * A_HEADS * A_N_GROUPS), dtype=BF16)

    def specs(tm):
        def grouped(arr):
            dil, width = arr.shape[1], arr.shape[3]
            return pl.BlockSpec((None, dil, tm // dil, width), lambda bi, i: (bi, 0, i, 0))

        return [grouped(a) for a in outs] + [grouped(a) for a in lses] + [_resident(w.shape), _resident(spread.shape)]

    def scratch(tm):
        return [pltpu.VMEM((D_MODEL // LANES, tm, LANES), F32), pltpu.VMEM((D_MODEL // LANES, tm, LANES), F32),
                pltpu.VMEM((tm, LANES), F32), pltpu.VMEM((tm, LANES), F32), pltpu.VMEM((tm, D_MODEL), BF16)]

    return (_a_out_stage, [*outs, *lses, w, spread], specs, scratch, 512, "a_out_xattn")


def _rope_tables(seq_len):
    inv = ROPE_THETA ** (-jnp.arange(0, A_HEAD_DIM, 2, dtype=F32) / A_HEAD_DIM)
    ang = jnp.arange(seq_len, dtype=F32)[:, None] * inv[None, :]
    cos, sin = jnp.cos(ang), jnp.sin(ang)
    reps = LANES // A_HEAD_DIM
    return (jnp.concatenate([cos, cos] * reps, axis=1),
            jnp.concatenate([-sin, sin] * reps, axis=1))


def _mixer_a(x, g, w_in, w_out):
    b, s, _ = x.shape
    cos_t, sin_t = _rope_tables(s)
    outs, lses = [], []
    for qkv in _a_in(x, g, w_in, cos_t, sin_t):
        o, l = _attn_group(qkv)
        outs.append(o)
        lses.append(l)
    return _a_out_mixer(outs, lses, w_out)


HALO = 8


def _hy_in_body(xp_ref, x_ref, xn_ref, g_ref, w_ref, cw_ref, cb_ref, x0_ref, vv_ref, u_ref, *, n_tiles):
    tm = x_ref.shape[0]
    i = pl.program_id(1)
    xe = jnp.concatenate([xp_ref[...], x_ref[...], xn_ref[...]], axis=0)
    xn = _rms(xe, g_ref[...]).astype(BF16)
    row = lax.broadcasted_iota(jnp.int32, (tm + 2 * HALO, 1), 0)
    inside = ((row >= HALO) | (i > 0)) & ((row < tm + HALO) | (i < n_tiles - 1))
    parts = []
    for c in range(3):
        lo = c * D_MODEL
        u_ref[...] = jnp.where(inside, _dot(xn, w_ref[:, lo:lo + D_MODEL]), 0.0)
        conv = (u_ref[pl.ds(HALO - 1, tm), :] * cw_ref[0:1, lo:lo + D_MODEL]
                + u_ref[pl.ds(HALO, tm), :] * cw_ref[1:2, lo:lo + D_MODEL]
                + u_ref[pl.ds(HALO + 1, tm), :] * cw_ref[2:3, lo:lo + D_MODEL]
                + cb_ref[:, lo:lo + D_MODEL])
        parts.append(conv)
    x0_ref[...] = parts[0].astype(BF16)
    vv_ref[...] = (parts[2] * parts[1]).astype(BF16)


def _hy_in(x, g, w, conv_w, conv_b):
    b, s, _ = x.shape
    tm = 512
    n_tiles = s // tm
    per = tm // HALO
    n_halo = s // HALO
    tok = pl.BlockSpec((None, tm, D_MODEL), lambda bi, i: (bi, i, 0))
    return pl.pallas_call(
        functools.partial(_hy_in_body, n_tiles=n_tiles),
        grid=(b, n_tiles),
        in_specs=[
            pl.BlockSpec((None, HALO, D_MODEL), lambda bi, i: (bi, jnp.maximum(i * per - 1, 0), 0)),
            tok,
            pl.BlockSpec((None, HALO, D_MODEL), lambda bi, i: (bi, jnp.minimum((i + 1) * per, n_halo - 1), 0)),
            pl.BlockSpec((1, D_MODEL), lambda bi, i: (0, 0)),
            _resident(w.shape),
            pl.BlockSpec(conv_w.shape, lambda bi, i: (0, 0)),
            pl.BlockSpec(conv_b.shape, lambda bi, i: (0, 0)),
        ],
        out_specs=[tok, tok],
        out_shape=[jax.ShapeDtypeStruct(x.shape, BF16), jax.ShapeDtypeStruct(x.shape, BF16)],
        scratch_shapes=[pltpu.VMEM((tm + 2 * HALO, D_MODEL), F32)],
        compiler_params=_params("parallel", "parallel"),
        name="hy_in",
    )(x, x, x, g, w, conv_w, conv_b)


def _hdot(a, b):
    return jnp.dot(a, b, precision=lax.Precision.HIGHEST, preferred_element_type=F32)


def _hy_filter_body(z_ref, t_ref, a_ref, b_ref, w1_ref, b1_ref, w2_ref, b2_ref, w3_ref, b3_ref,
                    wo_ref, fr_ref, dl_ref, h_ref, sum_ref):
    fr = fr_ref[...]
    hid = jnp.sin(fr * (_hdot(z_ref[...], w1_ref[...]) + b1_ref[...]))
    hid = jnp.sin(fr * (_hdot(hid, w2_ref[...]) + b2_ref[...]))
    hid = jnp.sin(fr * (_hdot(hid, w3_ref[...]) + b3_ref[...]))
    decay = jnp.exp(-t_ref[...] * dl_ref[...])
    hid = hid.astype(BF16)
    h_fwd = _dot(hid, wo_ref[:, :D_MODEL]) * decay
    h_bwd = _dot(hid, wo_ref[:, D_MODEL:]) * decay
    h = a_ref[...] * h_fwd + b_ref[...] * h_bwd
    h_ref[...] = h

    @pl.when(pl.program_id(0) == 0)
    def _():
        sum_ref[...] = jnp.zeros_like(sum_ref)

    sum_ref[...] += jnp.sum(jnp.abs(h), axis=0, keepdims=True)


def _hy_filter(seq_len, f_w1, f_b1, f_w2, f_b2, f_w3, f_b3, f_wout, f_freq):
    n = 2 * seq_len
    src = np.concatenate([np.arange(seq_len), [0], np.arange(seq_len - 1, 0, -1)])
    lag = jnp.asarray(src, dtype=F32)[:, None]
    t2 = lag / (seq_len - 1)
    w = 2.0 * math.pi * lag / seq_len
    f = jnp.linspace(1e-4, HY_BANDS - 1, HY_BANDS, dtype=F32)[None, :]
    z2 = jnp.concatenate([t2, jnp.cos(f * w), -jnp.sin(f * w),
                          jnp.zeros((n, HY_HID_PAD - HY_EMB), F32)], axis=-1)
    pos = np.arange(n)
    use_fwd = (pos < seq_len).astype(np.float32)[:, None]
    use_bwd = ((pos == 0) | (pos > seq_len)).astype(np.float32)[:, None]
    max_decay = math.log(HY_DECAY_TARGET) / HY_DECAY_STRONG_PCT
    min_decay = math.log(HY_DECAY_TARGET) / HY_DECAY_WEAK_PCT
    deltas = jnp.abs(jnp.linspace(min_decay, max_decay, D_MODEL, dtype=F32))[None, :]

    def pad2(m, rows):
        return jnp.pad(m, ((0, rows - m.shape[0]), (0, HY_HID_PAD - m.shape[1])))

    def padv(v):
        return jnp.pad(v, (0, HY_HID_PAD - v.shape[0]))[None, :]

    wo = jnp.pad(f_wout, ((0, HY_HID_PAD - f_wout.shape[0]), (0, 0))).astype(BF16)
    tr = 512
    rowblk = lambda width: pl.BlockSpec((tr, width), lambda i: (i, 0))
    full = lambda shape: pl.BlockSpec(shape, lambda i: (0, 0))
    sq = (HY_HID_PAD, HY_HID_PAD)
    vec = (1, HY_HID_PAD)
    return pl.pallas_call(
        _hy_filter_body,
        grid=(n // tr,),
        in_specs=[rowblk(HY_HID_PAD), rowblk(1), rowblk(1), rowblk(1),
                  full(sq), full(vec), full(sq), full(vec), full(sq), full(vec),
                  full(wo.shape), full(vec), full((1, D_MODEL))],
        out_specs=[rowblk(D_MODEL), full((1, D_MODEL))],
        out_shape=[jax.ShapeDtypeStruct((n, D_MODEL), F32), jax.ShapeDtypeStruct((1, D_MODEL), F32)],
        compiler_params=_params("arbitrary"),
        name="hy_filter",
    )(z2, t2, jnp.asarray(use_fwd), jnp.asarray(use_bwd),
      pad2(f_w1, HY_HID_PAD), padv(f_b1), pad2(f_w2, HY_HID_PAD), padv(f_b2),
      pad2(f_w3, HY_HID_PAD), padv(f_b3), wo, padv(f_freq), deltas)


def _fft_split(n):
    n1 = 1 << ((n.bit_length() - 1 + 1) // 2)
    return n1, n // n1


def _fft_tables(n1, n2):
    n = n1 * n2
    h = n1 // 2
    idx = np.arange(n1)
    ang = -2.0 * np.pi * ((idx[:, None] * idx[None, :]) % n1) / n1
    fr, fi = np.cos(ang), np.sin(ang)
    m1_data = np.block([[fr[:, :h], -fi[:, :h]], [fi[:, :h], fr[:, :h]]])
    m1_filt = np.concatenate([fr, fi], axis=0)
    ifr, ifi = fr.T[:h] / n, -fi.T[:h] / n
    m3 = np.block([[ifr, -ifi], [ifi, ifr]])
    k1 = jnp.arange(n1, dtype=jnp.int32)[:, None, None]
    k2 = jnp.arange(n2, dtype=jnp.int32)[None, :, None]
    i2 = jnp.arange(n2, dtype=jnp.int32)[None, None, :]
    phase = (i2 * k1 + n1 * i2 * k2) % n
    ga = (-2.0 * math.pi / n) * phase.astype(F32)
    gr, gi = jnp.cos(ga), jnp.sin(ga)
    g_fwd = jnp.concatenate([jnp.concatenate([gr, -gi], axis=2), jnp.concatenate([gi, gr], axis=2)], axis=1)
    grt, git = jnp.swapaxes(gr, 1, 2), jnp.swapaxes(gi, 1, 2)
    g_inv = jnp.concatenate([jnp.concatenate([grt, git], axis=2), jnp.concatenate([-git, grt], axis=2)], axis=1)
    as_bf = lambda m: jnp.asarray(m, dtype=F32).astype(BF16)
    return as_bf(m1_data), as_bf(m1_filt), as_bf(m3), g_fwd.astype(BF16), g_inv.astype(BF16)


FFT_SUB = 16
FFT_TC = 512


def _dft_outer_body(m_ref, pm_ref, z_ref, o_ref, zs_ref, rb_ref):
    q, sub, tc = z_ref.shape
    r = m_ref.shape[0]
    nlb = tc // LANES
    for cb in range(nlb):
        zs_ref[cb] = z_ref[:, :, cb * LANES:(cb + 1) * LANES].astype(F32).reshape(q * sub, LANES)
    for j in range(sub):
        z = jnp.concatenate([zs_ref[cb, pl.ds(j, q, stride=sub), :] for cb in range(nlb)], axis=1)
        rb_ref[j] = _dot(m_ref[...], z.astype(BF16)).astype(BF16)
    for g in range(r // sub):
        part, i0 = divmod(g * sub, r // 2)
        grouped = rb_ref[:, g * sub:(g + 1) * sub, :].reshape(sub * sub, tc)
        o_ref[part, i0:i0 + sub] = _dot(pm_ref[...], grouped).reshape(sub, sub, tc).astype(BF16)


def _dft_outer(mat, perm, z5, name):
    p, q, nh, sub, c = z5.shape
    r = mat.shape[0]
    tc = FFT_TC
    return pl.pallas_call(
        _dft_outer_body,
        grid=(p, nh, c // tc),
        in_specs=[pl.BlockSpec(mat.shape, lambda pi, h, ci: (0, 0)),
                  pl.BlockSpec(perm.shape, lambda pi, h, ci: (0, 0)),
                  pl.BlockSpec((None, q, None, sub, tc), lambda pi, h, ci: (pi, 0, h, 0, ci))],
        out_specs=pl.BlockSpec((None, None, 2, r // 2, sub, tc), lambda pi, h, ci: (pi, h, 0, 0, 0, ci)),
        out_shape=jax.ShapeDtypeStruct((p, nh, 2, r // 2, sub, c), BF16),
        scratch_shapes=[pltpu.VMEM((tc // LANES, q * sub, LANES), F32),
                        pltpu.VMEM((sub, r, tc), BF16)],
        compiler_params=_params("parallel", "parallel", "parallel"),
        name=name,
    )(mat, perm, z5)


def _idft_outer_body(m_ref, pm_ref, b_ref, o_ref, tb_ref, ys_ref):
    _, n1, sub, tc = b_ref.shape
    q = m_ref.shape[0]
    nlb = tc // LANES
    for g in range(2 * n1 // sub):
        part, i0 = divmod(g * sub, n1)
        grouped = b_ref[part, i0:i0 + sub].reshape(sub * sub, tc)
        tb_ref[:, g * sub:(g + 1) * sub, :] = _dot(pm_ref[...], grouped).reshape(sub, sub, tc).astype(BF16)
    for j in range(sub):
        res = _dot(m_ref[...], tb_ref[j])
        for cb in range(nlb):
            ys_ref[cb, pl.ds(j, q, stride=sub), :] = res[:, cb * LANES:(cb + 1) * LANES]
    for cb in range(nlb):
        o_ref[:, :, cb * LANES:(cb + 1) * LANES] = ys_ref[cb].reshape(q, sub, LANES)


def _idft_outer(mat, perm, b6):
    p, nh, _, n1, sub, c = b6.shape
    q = mat.shape[0]
    tc = FFT_TC
    return pl.pallas_call(
        _idft_outer_body,
        grid=(p, nh, c // tc),
        in_specs=[pl.BlockSpec(mat.shape, lambda pi, h, ci: (0, 0)),
                  pl.BlockSpec(perm.shape, lambda pi, h, ci: (0, 0)),
                  pl.BlockSpec((None, None, 2, n1, sub, tc), lambda pi, h, ci: (pi, h, 0, 0, 0, ci))],
        out_specs=pl.BlockSpec((None, q, None, sub, tc), lambda pi, h, ci: (pi, 0, h, 0, ci)),
        out_shape=jax.ShapeDtypeStruct((p, q, nh, sub, c), F32),
        scratch_shapes=[pltpu.VMEM((sub, 2 * n1, tc), BF16),
                        pltpu.VMEM((tc // LANES, q * sub, LANES), F32)],
        compiler_params=_params("parallel", "parallel", "parallel"),
        name="hy_idft",
    )(mat, perm, b6)


FFT_KB = 4


def _stack_re_im(a_ref, kk):
    nh, _, _, sub, ct = a_ref.shape
    return jnp.concatenate([a_ref[:, 0, kk].reshape(nh * sub, ct), a_ref[:, 1, kk].reshape(nh * sub, ct)], axis=0)


def _spec_filter_body(g_ref, a_ref, sc_ref, h_ref):
    n2 = g_ref.shape[1] // 2
    for kk in range(g_ref.shape[0]):
        spec = _dot(g_ref[kk], _stack_re_im(a_ref, kk)) * sc_ref[...]
        h_ref[kk] = spec.reshape(2, n2, spec.shape[1])


def _spec_filter(g_fwd, a6, scale):
    _, nh, _, n1, sub, c = a6.shape
    n2 = nh * sub
    return pl.pallas_call(
        _spec_filter_body,
        grid=(n1 // FFT_KB,),
        in_specs=[pl.BlockSpec((FFT_KB, 2 * n2, 2 * n2), lambda k: (k, 0, 0)),
                  pl.BlockSpec((None, nh, 2, FFT_KB, sub, c), lambda k: (0, 0, 0, k, 0, 0)),
                  pl.BlockSpec((1, c), lambda k: (0, 0))],
        out_specs=pl.BlockSpec((FFT_KB, 2, n2, c), lambda k: (k, 0, 0, 0)),
        out_shape=jax.ShapeDtypeStruct((n1, 2, n2, c), F32),
        compiler_params=_params("parallel"),
        name="hy_spec_filter",
    )(g_fwd, a6, scale)


def _spec_body(gf_ref, gi_ref, h_ref, a_ref, o_ref):
    nh, _, kb, sub, ct = a_ref.shape
    n2 = nh * sub
    for kk in range(kb):
        spec = _dot(gf_ref[kk], _stack_re_im(a_ref, kk))
        xr, xi = spec[:n2], spec[n2:]
        hr, hi = h_ref[kk, 0], h_ref[kk, 1]
        y = jnp.concatenate([xr * hr - xi * hi, xr * hi + xi * hr], axis=0).astype(BF16)
        back = _dot(gi_ref[kk], y)
        o_ref[:, 0, kk] = back[:n2].reshape(nh, sub, ct).astype(BF16)
        o_ref[:, 1, kk] = back[n2:].reshape(nh, sub, ct).astype(BF16)


def _spec(g_fwd, g_inv, hspec, a6):
    p, nh, _, n1, sub, c = a6.shape
    n2 = nh * sub
    blk = pl.BlockSpec((None, nh, 2, FFT_KB, sub, c), lambda k, pi: (pi, 0, 0, k, 0, 0))
    mat = pl.BlockSpec((FFT_KB, 2 * n2, 2 * n2), lambda k, pi: (k, 0, 0))
    return pl.pallas_call(
        _spec_body,
        grid=(n1 // FFT_KB, p),
        in_specs=[mat, mat, pl.BlockSpec((FFT_KB, 2, n2, c), lambda k, pi: (k, 0, 0, 0)), blk],
        out_specs=blk,
        out_shape=jax.ShapeDtypeStruct(a6.shape, BF16),
        compiler_params=_params("parallel", "parallel"),
        name="hy_spec",
    )(g_fwd, g_inv, hspec, a6)


def _long_conv(vv, h_raw, h_norm):
    b, l, c = vv.shape
    n1, n2 = _fft_split(2 * l)
    nh = n2 // FFT_SUB
    m1_data, m1_filt, m3, g_fwd, g_inv = _fft_tables(n1, n2)
    idx = np.arange(FFT_SUB * FFT_SUB)
    swapped = (idx % FFT_SUB) * FFT_SUB + idx // FFT_SUB
    perm = jnp.asarray(idx[None, :] == swapped[:, None], dtype=BF16)
    a_h = _dft_outer(m1_filt, perm, h_raw.reshape(1, n1, nh, FFT_SUB, c), "hy_dft1_filter")
    hspec = _spec_filter(g_fwd, a_h, 1.0 / h_norm)
    a = _dft_outer(m1_data, perm, vv.reshape(b // 2, n1, nh, FFT_SUB, c), "hy_dft1")
    y = _idft_outer(m3, perm, _spec(g_fwd, g_inv, hspec, a))
    return y.reshape(b, l, c)


def _hy_out_stage(cv_ref, vv_ref, x0_ref, bd_ref, w_ref, x_ref):
    y = ((cv_ref[...] + bd_ref[...] * vv_ref[...].astype(F32)) * x0_ref[...].astype(F32)).astype(BF16)
    return x_ref[...] + _dot(y, w_ref[...])


def _mixer_b(x, g, w_in, conv_w, conv_b, filt, bias_d, w_out):
    _, s, _ = x.shape
    x0, vv = _hy_in(x, g, w_in, conv_w, conv_b)
    h_raw, h_norm = _hy_filter(s, *filt)
    conv = _long_conv(vv, h_raw, h_norm)

    def specs(tm):
        tok = pl.BlockSpec((None, tm, D_MODEL), lambda bi, i: (bi, i, 0))
        return [tok, tok, tok, pl.BlockSpec((1, D_MODEL), lambda bi, i: (0, 0)), _resident(w_out.shape)]

    return (_hy_out_stage, [conv, vv, x0, bias_d, w_out], specs, lambda tm: [], 512, "hy_out_xattn")


def _gelu(z):
    return 0.5 * z * (1.0 + lax.erf(z * (2.0 ** -0.5)))


def _sgu_body(x_ref, g_ref, win_ref, lng_ref, lnb_ref, ws_ref, bs_ref, wout_ref, o_ref, y_ref):
    tm = x_ref.shape[0]
    x = x_ref[...]
    xn = _rms(x, g_ref[...]).astype(BF16)
    zu = _gelu(_dot(xn, win_ref[:, :D_MODEL]))
    zv = _gelu(_dot(xn, win_ref[:, D_MODEL:]))
    zc = zv - jnp.mean(zv, axis=-1, keepdims=True)
    zv = zc * lax.rsqrt(jnp.mean(zc * zc, axis=-1, keepdims=True) + EPS) * lng_ref[...] + lnb_ref[...]
    zvb = zv.astype(BF16)
    for c in range(tm // C_CHUNK):
        rows = slice(c * C_CHUNK, (c + 1) * C_CHUNK)
        for h in range(C_GROUPS):
            cols = slice(h * LANES, (h + 1) * LANES)
            sv = _dot(ws_ref[h], zvb[rows, cols]) + bs_ref[:, h:h + 1]
            y_ref[rows, cols] = (zu[rows, cols] * sv).astype(BF16)
    o_ref[...] = x + _dot(y_ref[...], wout_ref[...])


def _mixer_c(x, g, w_in, ln_g, ln_b, w_s, b_s_t, w_out):
    b, s, _ = x.shape
    t = b * s
    tm = 512
    tok = pl.BlockSpec((tm, D_MODEL), lambda i: (i, 0))
    row = pl.BlockSpec((1, D_MODEL), lambda i: (0, 0))
    return pl.pallas_call(
        _sgu_body,
        grid=(t // tm,),
        in_specs=[tok, row, _resident(w_in.shape), row, row, _resident(w_s.shape),
                  pl.BlockSpec(b_s_t.shape, lambda i: (0, 0)), _resident(w_out.shape)],
        out_specs=tok,
        out_shape=jax.ShapeDtypeStruct((t, D_MODEL), F32),
        scratch_shapes=[pltpu.VMEM((tm, D_MODEL), BF16)],
        compiler_params=_params("parallel"),
        name="sgu",
    )(x.reshape(t, D_MODEL), g, w_in, ln_g, ln_b, w_s, b_s_t, w_out).reshape(x.shape)


def _trunk(x, kvs, boff, w):
    b, s, _ = x.shape
    t = b * s
    for i in range(DEPTH):
        kind, j = i % N_MIXERS, i // N_MIXERS
        g = w["g_mix"][i][None, :]
        mixer_out = None
        if kind == 0:
            mixer_out = _mixer_a(x, g, w["a_w_in"][j], w["a_w_out"][j])
        elif kind == 1:
            filt = tuple(w[k][j] for k in ("b_f_w1", "b_f_b1", "b_f_w2", "b_f_b2", "b_f_w3", "b_f_b3",
                                          "b_f_wout", "b_f_freq"))
            mixer_out = _mixer_b(x, g, w["b_w_in"][j], w["b_conv_w"][j], w["b_conv_b"][j][None, :], filt,
                                 w["b_bias_d"][j][None, :], w["b_w_out"][j])
        else:
            x = _mixer_c(x, g, w["c_w_in"][j], w["c_ln_g"][j][None, :], w["c_ln_b"][j][None, :],
                         w["c_w_s"][j], w["c_b_s"][j].T, w["c_w_out"][j])
        x = _xattn(x, w["g_cross"][i][None, :], w["x_w_q"][i], kvs[i], boff, w["x_w_o"][i], mixer_out)
        g_final = w["g_final"][None, :] if i == DEPTH - 1 else None
        x = _ffn(x.reshape(t, D_MODEL), w["g_ffn"][i][None, :], w["f_w_gu"][i], w["f_w_down"][i],
                 g_final).reshape(b, s, D_MODEL)
    return x


_BF16_WEIGHTS = ("a_w_in", "a_w_out", "b_w_in", "b_w_out", "c_w_in", "c_w_s", "c_w_out",
                 "x_w_q", "x_w_kv", "x_w_o", "f_w_gu", "f_w_down")


def kernel(x_prompt, x_sample, mem_prompt, mem_sample, g_mix, g_cross, g_ffn, g_final, a_w_in, a_w_out, b_w_in, b_conv_w, b_conv_b, b_f_w1, b_f_b1, b_f_w2, b_f_b2, b_f_w3, b_f_b3, b_f_wout, b_f_freq, b_bias_d, b_w_out, c_w_in, c_ln_g, c_ln_b, c_w_s, c_b_s, c_w_out, x_w_q, x_w_kv, x_w_o, f_w_gu, f_w_down):
    w = dict(g_mix=g_mix, g_cross=g_cross, g_ffn=g_ffn, g_final=g_final,
             a_w_in=a_w_in, a_w_out=a_w_out, b_w_in=b_w_in, b_conv_w=b_conv_w, b_conv_b=b_conv_b,
             b_f_w1=b_f_w1, b_f_b1=b_f_b1, b_f_w2=b_f_w2, b_f_b2=b_f_b2, b_f_w3=b_f_w3,
             b_f_b3=b_f_b3, b_f_wout=b_f_wout, b_f_freq=b_f_freq, b_bias_d=b_bias_d,
             b_w_out=b_w_out, c_w_in=c_w_in, c_ln_g=c_ln_g, c_ln_b=c_ln_b, c_w_s=c_w_s,
             c_b_s=c_b_s, c_w_out=c_w_out, x_w_q=x_w_q, x_w_kv=x_w_kv, x_w_o=x_w_o,
             f_w_gu=f_w_gu, f_w_down=f_w_down)
    for name in _BF16_WEIGHTS:
        w[name] = w[name].astype(BF16)
    nb_prompt = mem_prompt.shape[0]
    mem = jnp.concatenate([mem_prompt, mem_sample], axis=0)
    mem2 = mem.reshape(mem.shape[0] * MEM_LEN, D_MODEL)
    kvs = [_kv_proj(mem2, w["x_w_kv"][i]).reshape(mem.shape[0], MEM_LEN, 2 * D_MODEL) for i in range(DEPTH)]
    y_prompt = _trunk(x_prompt, kvs, 0, w)
    y_sample = _trunk(x_sample, kvs, nb_prompt, w)
    return (y_prompt, y_sample)
```

```python
import functools
import math

import numpy as np
import jax
import jax.numpy as jnp
from jax import lax
from jax.experimental import pallas as pl
from jax.experimental.pallas import tpu as pltpu

F32 = jnp.float32
BF16 = jnp.bfloat16

D_MODEL = 1024
DEPTH = 4
N_MIXERS = 3

A_GROUPS = ((128, 1), (512, 4), (2048, 16))
A_N_GROUPS = len(A_GROUPS)
A_HEADS = 16
A_HEAD_DIM = D_MODEL // A_HEADS
A_HALF = 64
A_SUBTILE = 128
A_Q_SCALE = A_HEAD_DIM ** -0.5 * math.log2(math.e)
ROPE_THETA = 10000.0

HY_EMB = 33
HY_BANDS = (HY_EMB - 1) // 2
HY_HID_PAD = 128
HY_DECAY_TARGET = 1e-2
HY_DECAY_STRONG_PCT = 0.3
HY_DECAY_WEAK_PCT = 1.5

C_CHUNK = 128
C_GROUPS = 8

MEM_LEN = 256
X_HEADS = 4
X_HEAD_DIM = D_MODEL // X_HEADS

D_FF = -(-8 * D_MODEL // (3 * 256)) * 256
FF_CHUNK = 256

EPS = 1e-6
NEG_INF = -1e30

LANES = 128
VMEM_LIMIT = 56 * 1024 * 1024


def _params(*sem):
    return pltpu.CompilerParams(dimension_semantics=sem, vmem_limit_bytes=VMEM_LIMIT)


def _resident(shape):
    nd = len(shape)
    return pl.BlockSpec(shape, lambda *_: (0,) * nd, pipeline_mode=pl.Buffered(1))


def _dot(a, b):
    return jnp.dot(a, b, preferred_element_type=F32)


def _dot_nt(a, b):
    return lax.dot_general(a, b, (((1,), (1,)), ((), ())), preferred_element_type=F32)


def _rms(x, g):
    return x * lax.rsqrt(jnp.mean(x * x, axis=-1, keepdims=True) + EPS) * g


def _ffn_body(*refs, final):
    if final:
        x_ref, g_ref, wgu_ref, wd_ref, gf_ref, o_ref, act_ref = refs
    else:
        x_ref, g_ref, wgu_ref, wd_ref, o_ref, act_ref = refs
    x = x_ref[...]
    xn = _rms(x, g_ref[...]).astype(BF16)
    for c in range(D_FF // FF_CHUNK):
        lo = c * FF_CHUNK
        gate = _dot(xn, wgu_ref[:, lo:lo + FF_CHUNK])
        up = _dot(xn, wgu_ref[:, D_FF + lo:D_FF + lo + FF_CHUNK])
        act_ref[:, lo:lo + FF_CHUNK] = (gate / (1.0 + jnp.exp(-gate)) * up).astype(BF16)
    y = x + _dot(act_ref[...], wd_ref[...])
    if final:
        y = _rms(y, gf_ref[...])
    o_ref[...] = y


def _ffn(x2, g, wgu, wd, g_final=None):
    t = x2.shape[0]
    tm = 512
    final = g_final is not None
    tok = pl.BlockSpec((tm, D_MODEL), lambda i: (i, 0))
    row = pl.BlockSpec((1, D_MODEL), lambda i: (0, 0))
    in_specs = [tok, row, _resident(wgu.shape), _resident(wd.shape)]
    args = [x2, g, wgu, wd]
    if final:
        in_specs.append(row)
        args.append(g_final)
    return pl.pallas_call(
        functools.partial(_ffn_body, final=final),
        grid=(t // tm,),
        in_specs=in_specs,
        out_specs=tok,
        out_shape=jax.ShapeDtypeStruct((t, D_MODEL), F32),
        scratch_shapes=[pltpu.VMEM((tm, D_FF), BF16)],
        compiler_params=_params("parallel"),
        name="ffn_final" if final else "ffn",
    )(*args)


def _kv_body(m_ref, w_ref, o_ref):
    o_ref[...] = _dot(m_ref[...].astype(BF16), w_ref[...]).astype(BF16)


def _kv_proj(mem2, wkv):
    r = mem2.shape[0]
    tm = 256
    return pl.pallas_call(
        _kv_body,
        grid=(r // tm,),
        in_specs=[pl.BlockSpec((tm, D_MODEL), lambda i: (i, 0)), _resident(wkv.shape)],
        out_specs=pl.BlockSpec((tm, 2 * D_MODEL), lambda i: (i, 0)),
        out_shape=jax.ShapeDtypeStruct((r, 2 * D_MODEL), BF16),
        compiler_params=_params("parallel"),
        name="kv_proj",
    )(mem2, wkv)


def _xattn_stage(x, g_ref, wq_ref, k_ref, v_ref, wo_ref, y_ref):
    xn = _rms(x, g_ref[...]).astype(BF16)
    q = (_dot(xn, wq_ref[...]) * (X_HEAD_DIM ** -0.5)).astype(BF16)
    for h in range(X_HEADS):
        sl = slice(h * X_HEAD_DIM, (h + 1) * X_HEAD_DIM)
        s = _dot_nt(q[:, sl], k_ref[:, sl])
        p = jnp.exp(s - jnp.max(s, axis=-1, keepdims=True))
        inv = 1.0 / jnp.sum(p, axis=-1, keepdims=True)
        y_ref[:, sl] = (_dot(p.astype(BF16), v_ref[:, sl]) * inv).astype(BF16)
    return x + _dot(y_ref[...], wo_ref[...])


def _xattn_body(*refs, mixer_out, n_mixer_in, n_mixer_scratch):
    n_in = n_mixer_in + 6
    mixer_in, (x_ref, g_ref, wq_ref, k_ref, v_ref, wo_ref) = refs[:n_mixer_in], refs[n_mixer_in:n_in]
    o_ref = refs[n_in]
    mixer_scratch, y_ref = refs[n_in + 1:n_in + 1 + n_mixer_scratch], refs[n_in + 1 + n_mixer_scratch]
    x = x_ref[...] if mixer_out is None else mixer_out(*mixer_in, x_ref, *mixer_scratch)
    o_ref[...] = _xattn_stage(x, g_ref, wq_ref, k_ref, v_ref, wo_ref, y_ref)


def _xattn(x, g, wq, kv, boff, wo, mixer=None):
    b, s, _ = x.shape
    if mixer is None:
        stage, m_args, m_specs, m_scratch, tm, name = None, [], [], [], 1024, "xattn"
    else:
        stage, m_args, specs_fn, scratch_fn, tm, name = mixer
        m_specs, m_scratch = specs_fn(tm), scratch_fn(tm)
    tok = pl.BlockSpec((None, tm, D_MODEL), lambda bi, i: (bi, i, 0))
    return pl.pallas_call(
        functools.partial(_xattn_body, mixer_out=stage, n_mixer_in=len(m_args), n_mixer_scratch=len(m_scratch)),
        grid=(b, s // tm),
        in_specs=m_specs + [
            tok,
            pl.BlockSpec((1, D_MODEL), lambda bi, i: (0, 0)),
            _resident(wq.shape),
            pl.BlockSpec((None, MEM_LEN, D_MODEL), lambda bi, i: (bi + boff, 0, 0)),
            pl.BlockSpec((None, MEM_LEN, D_MODEL), lambda bi, i: (bi + boff, 0, 1)),
            _resident(wo.shape),
        ],
        out_specs=tok,
        out_shape=jax.ShapeDtypeStruct(x.shape, F32),
        scratch_shapes=m_scratch + [pltpu.VMEM((tm, D_MODEL), BF16)],
        compiler_params=_params("parallel", "parallel"),
        name=name,
    )(*m_args, x, g, wq, kv, kv, wo)


def _a_in_body(x_ref, g_ref, w_ref, cos_ref, sin_ref, *rest):
    o_refs, (xs_ref, xp_ref) = rest[:A_N_GROUPS], rest[A_N_GROUPS:]
    tm = x_ref.shape[0]
    xn_f32 = _rms(x_ref[...], g_ref[...])
    reps = D_MODEL // LANES
    for cb in range(reps):
        xs_ref[cb] = xn_f32[:, cb * LANES:(cb + 1) * LANES]
    lane = lax.broadcasted_iota(jnp.int32, (tm, LANES), 1)
    low_half = (lane & (A_HEAD_DIM // 2)) == 0
    for gidx, (o_ref, (_, dil)) in enumerate(zip(o_refs, A_GROUPS)):
        rows = tm // dil
        if dil == 1:
            xn = xn_f32.astype(BF16)
            cos, sin = cos_ref[...], sin_ref[...]
        else:
            for r in range(dil):
                for cb in range(reps):
                    xp_ref[r * rows:(r + 1) * rows, cb * LANES:(cb + 1) * LANES] = (
                        xs_ref[cb, pl.ds(r, rows, stride=dil), :].astype(BF16))
            xn = xp_ref[...]
            cos = jnp.concatenate([cos_ref[pl.ds(r, rows, stride=dil), :] for r in range(dil)], axis=0)
            sin = jnp.concatenate([sin_ref[pl.ds(r, rows, stride=dil), :] for r in range(dil)], axis=0)
        for part in range(3):
            lo = (gidx * 3 + part) * D_MODEL
            y = _dot(xn, w_ref[:, lo:lo + D_MODEL])
            if part == 2:
                o_ref[:, :, 2 * D_MODEL:] = y.reshape(dil, rows, D_MODEL).astype(BF16)
                continue
            c_t, s_t = (cos * A_Q_SCALE, sin * A_Q_SCALE) if part == 0 else (cos, sin)
            for cb in range(reps):
                yb = y[:, cb * LANES:(cb + 1) * LANES]
                partner = jnp.where(low_half, pltpu.roll(yb, LANES - A_HEAD_DIM // 2, 1),
                                    pltpu.roll(yb, A_HEAD_DIM // 2, 1))
                col = part * D_MODEL + cb * LANES
                o_ref[:, :, col:col + LANES] = (yb * c_t + partner * s_t).reshape(dil, rows, LANES).astype(BF16)


def _a_in(x, g, w, cos_t, sin_t):
    b, s, _ = x.shape
    tm = 512
    return pl.pallas_call(
        _a_in_body,
        grid=(b, s // tm),
        in_specs=[
            pl.BlockSpec((None, tm, D_MODEL), lambda bi, i: (bi, i, 0)),
            pl.BlockSpec((1, D_MODEL), lambda bi, i: (0, 0)),
            _resident(w.shape),
            pl.BlockSpec((tm, LANES), lambda bi, i: (i, 0)),
            pl.BlockSpec((tm, LANES), lambda bi, i: (i, 0)),
        ],
        out_specs=[pl.BlockSpec((None, dil, tm // dil, 3 * D_MODEL), lambda bi, i: (bi, 0, i, 0))
                   for _, dil in A_GROUPS],
        out_shape=[jax.ShapeDtypeStruct((b, dil, s // dil, 3 * D_MODEL), BF16) for _, dil in A_GROUPS],
        scratch_shapes=[pltpu.VMEM((D_MODEL // LANES, tm, LANES), F32), pltpu.VMEM((tm, D_MODEL), BF16)],
        compiler_params=_params("parallel", "parallel"),
        name="a_in",
    )(x, g, w, cos_t, sin_t)


def _attn_body(q_ref, kp_ref, k_ref, kn_ref, vp_ref, v_ref, vn_ref, o_ref, lse_ref, *, n_tiles):
    tq = q_ref.shape[0]
    sq = A_SUBTILE
    win = sq + 2 * A_HALF
    n_sub_tiles = tq // sq
    i = pl.program_id(2)
    kw = jnp.concatenate([kp_ref[...], k_ref[...], kn_ref[...]], axis=0)
    vw = jnp.concatenate([vp_ref[...], v_ref[...], vn_ref[...]], axis=0)
    qi = lax.broadcasted_iota(jnp.int32, (sq, win), 0)
    kj = lax.broadcasted_iota(jnp.int32, (sq, win), 1)
    band = (kj >= qi) & (kj <= qi + 2 * A_HALF)
    left = lax.broadcasted_iota(jnp.int32, (sq, LANES), 1) < A_HEAD_DIM
    for st in range(n_sub_tiles):
        rows = slice(st * sq, (st + 1) * sq)
        mask = band
        if st == 0:
            mask = mask & (kj >= jnp.where(i == 0, A_HALF, 0))
        if st == n_sub_tiles - 1:
            mask = mask & (kj < jnp.where(i == n_tiles - 1, sq + A_HALF, win))
        mask2 = jnp.concatenate([mask, mask], axis=0)
        lse_ref[rows, :] = jnp.zeros((sq, LANES), F32)

        def scores(hp):
            sl = slice(hp * LANES, (hp + 1) * LANES)
            q2 = q_ref[rows, sl]
            zero = jnp.zeros_like(q2)
            qq = jnp.concatenate([jnp.where(left, q2, zero), jnp.where(left, zero, q2)], axis=0)
            return _dot_nt(qq, kw[st * sq:st * sq + win, sl])

        s_next = scores(0)
        for hp in range(A_HEADS // 2):
            sl = slice(hp * LANES, (hp + 1) * LANES)
            s = jnp.where(mask2, s_next, NEG_INF)
            if hp + 1 < A_HEADS // 2:
                s_next = scores(hp + 1)
            m = jnp.max(s, axis=-1, keepdims=True)
            p = jnp.exp2(s - m)
            l = jnp.sum(p, axis=-1, keepdims=True)
            r = _dot(p.astype(BF16), vw[st * sq:st * sq + win, sl])
            o_ref[rows, sl] = jnp.where(left, r[:sq], r[sq:]).astype(BF16)
            for off, col in ((0, m), (A_HEADS, l)):
                lse_ref[rows, off + 2 * hp:off + 2 * hp + 1] = col[:sq]
                lse_ref[rows, off + 2 * hp + 1:off + 2 * hp + 2] = col[sq:]


def _attn_group(qkv):
    b, dil, n_sub, _ = qkv.shape
    tq = min(2 * A_SUBTILE, n_sub)
    n_tiles = n_sub // tq
    halo_per_tile = tq // A_HALF
    n_halo = n_sub // A_HALF

    def own(part):
        return pl.BlockSpec((None, None, tq, D_MODEL), lambda bi, r, i: (bi, r, i, part))

    def prev(part):
        return pl.BlockSpec((None, None, A_HALF, D_MODEL),
                            lambda bi, r, i: (bi, r, jnp.maximum(i * halo_per_tile - 1, 0), part))

    def nxt(part):
        return pl.BlockSpec((None, None, A_HALF, D_MODEL),
                            lambda bi, r, i: (bi, r, jnp.minimum((i + 1) * halo_per_tile, n_halo - 1), part))

    return pl.pallas_call(
        functools.partial(_attn_body, n_tiles=n_tiles),
        grid=(b, dil, n_tiles),
        in_specs=[own(0), prev(1), own(1), nxt(1), prev(2), own(2), nxt(2)],
        out_specs=[
            pl.BlockSpec((None, None, tq, D_MODEL), lambda bi, r, i: (bi, r, i, 0)),
            pl.BlockSpec((None, None, tq, LANES), lambda bi, r, i: (bi, r, i, 0)),
        ],
        out_shape=[
            jax.ShapeDtypeStruct((b, dil, n_sub, D_MODEL), BF16),
            jax.ShapeDtypeStruct((b, dil, n_sub, LANES), F32),
        ],
        compiler_params=_params("parallel", "parallel", "parallel"),
        name=f"attn_d{dil}",
    )(qkv, qkv, qkv, qkv, qkv, qkv, qkv)


def _a_out_stage(o1_ref, o2_ref, o3_ref, l1_ref, l2_ref, l3_ref, w_ref, sp_ref, x_ref,
                 os2_ref, os3_ref, ls2_ref, ls3_ref, y_ref):
    tm = x_ref.shape[0]
    for o_ref, l_ref, os_ref, ls_ref in ((o2_ref, l2_ref, os2_ref, ls2_ref), (o3_ref, l3_ref, os3_ref, ls3_ref)):
        dil = o_ref.shape[0]
        rows = tm // dil
        for r in range(dil):
            for cb in range(D_MODEL // LANES):
                os_ref[cb, pl.ds(r, rows, stride=dil), :] = o_ref[r, :, cb * LANES:(cb + 1) * LANES].astype(F32)
            ls_ref[pl.ds(r, rows, stride=dil), :] = l_ref[r]
    s1, s2, s3 = l1_ref[0], ls2_ref[...], ls3_ref[...]
    m = jnp.maximum(jnp.maximum(s1, s2), s3)
    e1, e2, e3 = jnp.exp2(s1 - m), jnp.exp2(s2 - m), jnp.exp2(s3 - m)
    den = sum(e * pltpu.roll(st, LANES - A_HEADS, 1) for e, st in ((e1, s1), (e2, s2), (e3, s3)))
    inv = 1.0 / den
    valid = lax.broadcasted_iota(jnp.int32, (tm, LANES), 1) < A_HEADS
    packed = jnp.zeros((tm, LANES), F32)
    for gidx, e in enumerate((e1, e2, e3)):
        wgt = jnp.where(valid, e * inv, 0.0)
        hi = wgt.astype(BF16).astype(F32)
        for term, val in enumerate((hi, wgt - hi)):
            shift = 2 * A_HEADS * gidx + A_HEADS * term
            packed = packed + (pltpu.roll(val, shift, 1) if shift else val)
    wall = _dot(packed.astype(BF16), sp_ref[...])
    for cb in range(D_MODEL // LANES):
        sl = slice(cb * LANES, (cb + 1) * LANES)
        y = (wall[:, sl] * o1_ref[0, :, sl].astype(F32)
             + wall[:, D_MODEL + cb * LANES:D_MODEL + (cb + 1) * LANES] * os2_ref[cb]
             + wall[:, 2 * D_MODEL + cb * LANES:2 * D_MODEL + (cb + 1) * LANES] * os3_ref[cb])
        y_ref[:, sl] = y.astype(BF16)
    return x_ref[...] + _dot(y_ref[...], w_ref[...])


def _a_out_mixer(outs, lses, w):
    col = np.arange(A_N_GROUPS * D_MODEL)
    row = np.arange(LANES)
    spread = jnp.asarray((row[:, None] // (2 * A_HEADS) == col[None, :] // D_MODEL)
                         & (row[:, None] % A_HEADS == (col[None, :] % D_MODEL) // A_HEAD_DIM)
                         & (row[:, None] < 2 * A_HEADS * A_N_GROUPS), dtype=BF16)

    def specs(tm):
        def grouped(arr):
            dil, width = arr.shape[1], arr.shape[3]
            return pl.BlockSpec((None, dil, tm // dil, width), lambda bi, i: (bi, 0, i, 0))

        return [grouped(a) for a in outs] + [grouped(a) for a in lses] + [_resident(w.shape), _resident(spread.shape)]

    def scratch(tm):
        return [pltpu.VMEM((D_MODEL // LANES, tm, LANES), F32), pltpu.VMEM((D_MODEL // LANES, tm, LANES), F32),
                pltpu.VMEM((tm, LANES), F32), pltpu.VMEM((tm, LANES), F32), pltpu.VMEM((tm, D_MODEL), BF16)]

    return (_a_out_stage, [*outs, *lses, w, spread], specs, scratch, 512, "a_out_xattn")


def _rope_tables(seq_len):
    inv = ROPE_THETA ** (-jnp.arange(0, A_HEAD_DIM, 2, dtype=F32) / A_HEAD_DIM)
    ang = jnp.arange(seq_len, dtype=F32)[:, None] * inv[None, :]
    cos, sin = jnp.cos(ang), jnp.sin(ang)
    reps = LANES // A_HEAD_DIM
    return (jnp.concatenate([cos, cos] * reps, axis=1),
            jnp.concatenate([-sin, sin] * reps, axis=1))


def _mixer_a(x, g, w_in, w_out):
    b, s, _ = x.shape
    cos_t, sin_t = _rope_tables(s)
    outs, lses = [], []
    for qkv in _a_in(x, g, w_in, cos_t, sin_t):
        o, l = _attn_group(qkv)
        outs.append(o)
        lses.append(l)
    return _a_out_mixer(outs, lses, w_out)


HALO = 8
HY_COL_CHUNK = 256


def _hy_in_body(xp_ref, x_ref, xn_ref, g_ref, w_ref, cw_ref, cb_ref, x0_ref, vv_ref, u_ref, *, n_tiles):
    tm = x_ref.shape[0]
    i = pl.program_id(1)
    xe = jnp.concatenate([xp_ref[...], x_ref[...], xn_ref[...]], axis=0)
    xn = _rms(xe, g_ref[...]).astype(BF16)
    row = lax.broadcasted_iota(jnp.int32, (tm + 2 * HALO, 1), 0)
    inside = ((row >= HALO) | (i > 0)) & ((row < tm + HALO) | (i < n_tiles - 1))
    cw = HY_COL_CHUNK
    for c in range(D_MODEL // cw):
        parts = []
        for k in range(3):
            cols = slice(k * D_MODEL + c * cw, k * D_MODEL + (c + 1) * cw)
            u_ref[k] = jnp.where(inside, _dot(xn, w_ref[:, cols]), 0.0)
            parts.append(u_ref[k, pl.ds(HALO - 1, tm), :] * cw_ref[0:1, cols]
                         + u_ref[k, pl.ds(HALO, tm), :] * cw_ref[1:2, cols]
                         + u_ref[k, pl.ds(HALO + 1, tm), :] * cw_ref[2:3, cols]
                         + cb_ref[:, cols])
        x0_ref[:, c * cw:(c + 1) * cw] = parts[0].astype(BF16)
        vv_ref[:, c * cw:(c + 1) * cw] = (parts[2] * parts[1]).astype(BF16)


def _hy_in(x, g, w, conv_w, conv_b):
    b, s, _ = x.shape
    tm = 512
    n_tiles = s // tm
    per = tm // HALO
    n_halo = s // HALO
    tok = pl.BlockSpec((None, tm, D_MODEL), lambda bi, i: (bi, i, 0))
    return pl.pallas_call(
        functools.partial(_hy_in_body, n_tiles=n_tiles),
        grid=(b, n_tiles),
        in_specs=[
            pl.BlockSpec((None, HALO, D_MODEL), lambda bi, i: (bi, jnp.maximum(i * per - 1, 0), 0)),
            tok,
            pl.BlockSpec((None, HALO, D_MODEL), lambda bi, i: (bi, jnp.minimum((i + 1) * per, n_halo - 1), 0)),
            pl.BlockSpec((1, D_MODEL), lambda bi, i: (0, 0)),
            _resident(w.shape),
            pl.BlockSpec(conv_w.shape, lambda bi, i: (0, 0)),
            pl.BlockSpec(conv_b.shape, lambda bi, i: (0, 0)),
        ],
        out_specs=[tok, tok],
        out_shape=[jax.ShapeDtypeStruct(x.shape, BF16), jax.ShapeDtypeStruct(x.shape, BF16)],
        scratch_shapes=[pltpu.VMEM((3, tm + 2 * HALO, HY_COL_CHUNK), F32)],
        compiler_params=_params("parallel", "parallel"),
        name="hy_in",
    )(x, x, x, g, w, conv_w, conv_b)


def _hdot(a, b):
    return jnp.dot(a, b, precision=lax.Precision.HIGHEST, preferred_element_type=F32)


def _hy_filter_body(z_ref, t_ref, a_ref, b_ref, w1_ref, b1_ref, w2_ref, b2_ref, w3_ref, b3_ref,
                    wo_ref, fr_ref, dl_ref, h_ref, sum_ref):
    fr = fr_ref[...]
    hid = jnp.sin(fr * (_hdot(z_ref[...], w1_ref[...]) + b1_ref[...]))
    hid = jnp.sin(fr * (_hdot(hid, w2_ref[...]) + b2_ref[...]))
    hid = jnp.sin(fr * (_hdot(hid, w3_ref[...]) + b3_ref[...]))
    decay = jnp.exp(-t_ref[...] * dl_ref[...])
    hid = hid.astype(BF16)
    h_fwd = _dot(hid, wo_ref[:, :D_MODEL]) * decay
    h_bwd = _dot(hid, wo_ref[:, D_MODEL:]) * decay
    h = a_ref[...] * h_fwd + b_ref[...] * h_bwd
    h_ref[...] = h

    @pl.when(pl.program_id(0) == 0)
    def _():
        sum_ref[...] = jnp.zeros_like(sum_ref)

    sum_ref[...] += jnp.sum(jnp.abs(h), axis=0, keepdims=True)


def _hy_filter(seq_len, f_w1, f_b1, f_w2, f_b2, f_w3, f_b3, f_wout, f_freq):
    n = 2 * seq_len
    src = np.concatenate([np.arange(seq_len), [0], np.arange(seq_len - 1, 0, -1)])
    lag = jnp.asarray(src, dtype=F32)[:, None]
    t2 = lag / (seq_len - 1)
    w = 2.0 * math.pi * lag / seq_len
    f = jnp.linspace(1e-4, HY_BANDS - 1, HY_BANDS, dtype=F32)[None, :]
    z2 = jnp.concatenate([t2, jnp.cos(f * w), -jnp.sin(f * w),
                          jnp.zeros((n, HY_HID_PAD - HY_EMB), F32)], axis=-1)
    pos = np.arange(n)
    use_fwd = (pos < seq_len).astype(np.float32)[:, None]
    use_bwd = ((pos == 0) | (pos > seq_len)).astype(np.float32)[:, None]
    max_decay = math.log(HY_DECAY_TARGET) / HY_DECAY_STRONG_PCT
    min_decay = math.log(HY_DECAY_TARGET) / HY_DECAY_WEAK_PCT
    deltas = jnp.abs(jnp.linspace(min_decay, max_decay, D_MODEL, dtype=F32))[None, :]

    def pad2(m, rows):
        return jnp.pad(m, ((0, rows - m.shape[0]), (0, HY_HID_PAD - m.shape[1])))

    def padv(v):
        return jnp.pad(v, (0, HY_HID_PAD - v.shape[0]))[None, :]

    wo = jnp.pad(f_wout, ((0, HY_HID_PAD - f_wout.shape[0]), (0, 0))).astype(BF16)
    tr = 512
    rowblk = lambda width: pl.BlockSpec((tr, width), lambda i: (i, 0))
    full = lambda shape: pl.BlockSpec(shape, lambda i: (0, 0))
    sq = (HY_HID_PAD, HY_HID_PAD)
    vec = (1, HY_HID_PAD)
    return pl.pallas_call(
        _hy_filter_body,
        grid=(n // tr,),
        in_specs=[rowblk(HY_HID_PAD), rowblk(1), rowblk(1), rowblk(1),
                  full(sq), full(vec), full(sq), full(vec), full(sq), full(vec),
                  full(wo.shape), full(vec), full((1, D_MODEL))],
        out_specs=[rowblk(D_MODEL), full((1, D_MODEL))],
        out_shape=[jax.ShapeDtypeStruct((n, D_MODEL), F32), jax.ShapeDtypeStruct((1, D_MODEL), F32)],
        compiler_params=_params("arbitrary"),
        name="hy_filter",
    )(z2, t2, jnp.asarray(use_fwd), jnp.asarray(use_bwd),
      pad2(f_w1, HY_HID_PAD), padv(f_b1), pad2(f_w2, HY_HID_PAD), padv(f_b2),
      pad2(f_w3, HY_HID_PAD), padv(f_b3), wo, padv(f_freq), deltas)


def _fft_split(n):
    n1 = 1 << ((n.bit_length() - 1 + 1) // 2)
    return n1, n // n1


def _fft_tables(n1, n2):
    n = n1 * n2
    h = n1 // 2
    idx = np.arange(n1)
    ang = -2.0 * np.pi * ((idx[:, None] * idx[None, :]) % n1) / n1
    fr, fi = np.cos(ang), np.sin(ang)
    m1_data = np.block([[fr[:, :h], -fi[:, :h]], [fi[:, :h], fr[:, :h]]])
    m1_filt = np.concatenate([fr, fi], axis=0)
    ifr, ifi = fr.T[:h] / n, -fi.T[:h] / n
    m3 = np.block([[ifr, -ifi], [ifi, ifr]])
    k1 = jnp.arange(n1, dtype=jnp.int32)[:, None, None]
    k2 = jnp.arange(n2, dtype=jnp.int32)[None, :, None]
    i2 = jnp.arange(n2, dtype=jnp.int32)[None, None, :]
    phase = (i2 * k1 + n1 * i2 * k2) % n
    ga = (-2.0 * math.pi / n) * phase.astype(F32)
    gr, gi = jnp.cos(ga), jnp.sin(ga)
    g_fwd = jnp.concatenate([jnp.concatenate([gr, -gi], axis=2), jnp.concatenate([gi, gr], axis=2)], axis=1)
    grt, git = jnp.swapaxes(gr, 1, 2), jnp.swapaxes(gi, 1, 2)
    g_inv = jnp.concatenate([jnp.concatenate([grt, git], axis=2), jnp.concatenate([-git, grt], axis=2)], axis=1)
    as_bf = lambda m: jnp.asarray(m, dtype=F32).astype(BF16)
    return as_bf(m1_data), as_bf(m1_filt), as_bf(m3), g_fwd.astype(BF16), g_inv.astype(BF16)


FFT_SUB = 16
FFT_TC = 512


def _dft_outer_body(m_ref, pm_ref, z_ref, o_ref, zs_ref, rb_ref):
    q, sub, tc = z_ref.shape
    r = m_ref.shape[0]
    nlb = tc // LANES
    for cb in range(nlb):
        zs_ref[cb] = z_ref[:, :, cb * LANES:(cb + 1) * LANES].astype(F32).reshape(q * sub, LANES)
    hw = tc // 2
    n_groups = r // sub

    def transform(hf, j):
        z = jnp.concatenate([zs_ref[cb, pl.ds(j, q, stride=sub), :]
                             for cb in range(hf * nlb // 2, (hf + 1) * nlb // 2)], axis=1)
        rb_ref[hf, j] = _dot(m_ref[...], z.astype(BF16)).astype(BF16)

    def regroup(hf, g):
        part, i0 = divmod(g * sub, r // 2)
        grouped = rb_ref[hf, :, g * sub:(g + 1) * sub, :].reshape(sub * sub, hw)
        o_ref[part, i0:i0 + sub, :, hf * hw:(hf + 1) * hw] = (
            _dot(pm_ref[...], grouped).reshape(sub, sub, hw).astype(BF16))

    for j in range(sub):
        transform(0, j)
    for step in range(max(sub, n_groups)):
        if step < sub:
            transform(1, step)
        if step < n_groups:
            regroup(0, step)
    for g in range(n_groups):
        regroup(1, g)


def _dft_outer(mat, perm, z5, name):
    p, q, nh, sub, c = z5.shape
    r = mat.shape[0]
    tc = FFT_TC
    return pl.pallas_call(
        _dft_outer_body,
        grid=(p, nh, c // tc),
        in_specs=[pl.BlockSpec(mat.shape, lambda pi, h, ci: (0, 0)),
                  pl.BlockSpec(perm.shape, lambda pi, h, ci: (0, 0)),
                  pl.BlockSpec((None, q, None, sub, tc), lambda pi, h, ci: (pi, 0, h, 0, ci))],
        out_specs=pl.BlockSpec((None, None, 2, r // 2, sub, tc), lambda pi, h, ci: (pi, h, 0, 0, 0, ci)),
        out_shape=jax.ShapeDtypeStruct((p, nh, 2, r // 2, sub, c), BF16),
        scratch_shapes=[pltpu.VMEM((tc // LANES, q * sub, LANES), F32),
                        pltpu.VMEM((2, sub, r, tc // 2), BF16)],
        compiler_params=_params("parallel", "parallel", "parallel"),
        name=name,
    )(mat, perm, z5)


def _idft_outer_body(m_ref, pm_ref, b_ref, o_ref, tb_ref, ys_ref):
    _, n1, sub, tc = b_ref.shape
    q = m_ref.shape[0]
    nlb = tc // LANES
    hw = tc // 2
    n_groups = 2 * n1 // sub

    def regroup(hf, g):
        part, i0 = divmod(g * sub, n1)
        grouped = b_ref[part, i0:i0 + sub, :, hf * hw:(hf + 1) * hw].reshape(sub * sub, hw)
        tb_ref[hf, :, g * sub:(g + 1) * sub, :] = _dot(pm_ref[...], grouped).reshape(sub, sub, hw).astype(BF16)

    def transform(hf, j):
        res = _dot(m_ref[...], tb_ref[hf, j])
        for c in range(nlb // 2):
            ys_ref[hf * nlb // 2 + c, pl.ds(j, q, stride=sub), :] = res[:, c * LANES:(c + 1) * LANES]

    for g in range(n_groups):
        regroup(0, g)
    for step in range(max(sub, n_groups)):
        if step < n_groups:
            regroup(1, step)
        if step < sub:
            transform(0, step)
    for j in range(sub):
        transform(1, j)
    for cb in range(nlb):
        o_ref[:, :, cb * LANES:(cb + 1) * LANES] = ys_ref[cb].reshape(q, sub, LANES)


def _idft_outer(mat, perm, b6):
    p, nh, _, n1, sub, c = b6.shape
    q = mat.shape[0]
    tc = FFT_TC
    return pl.pallas_call(
        _idft_outer_body,
        grid=(p, nh, c // tc),
        in_specs=[pl.BlockSpec(mat.shape, lambda pi, h, ci: (0, 0)),
                  pl.BlockSpec(perm.shape, lambda pi, h, ci: (0, 0)),
                  pl.BlockSpec((None, None, 2, n1, sub, tc), lambda pi, h, ci: (pi, h, 0, 0, 0, ci))],
        out_specs=pl.BlockSpec((None, q, None, sub, tc), lambda pi, h, ci: (pi, 0, h, 0, ci)),
        out_shape=jax.ShapeDtypeStruct((p, q, nh, sub, c), F32),
        scratch_shapes=[pltpu.VMEM((2, sub, 2 * n1, tc // 2), BF16),
                        pltpu.VMEM((tc // LANES, q * sub, LANES), F32)],
        compiler_params=_params("parallel", "parallel", "parallel"),
        name="hy_idft",
    )(mat, perm, b6)


FFT_KB = 4


def _stack_re_im(a_ref, kk):
    nh, _, _, sub, ct = a_ref.shape
    return jnp.concatenate([a_ref[:, 0, kk].reshape(nh * sub, ct), a_ref[:, 1, kk].reshape(nh * sub, ct)], axis=0)


def _spec_filter_body(g_ref, a_ref, sc_ref, h_ref):
    n2 = g_ref.shape[1] // 2
    for kk in range(g_ref.shape[0]):
        spec = _dot(g_ref[kk], _stack_re_im(a_ref, kk)) * sc_ref[...]
        h_ref[kk] = spec.reshape(2, n2, spec.shape[1])


def _spec_filter(g_fwd, a6, scale):
    _, nh, _, n1, sub, c = a6.shape
    n2 = nh * sub
    return pl.pallas_call(
        _spec_filter_body,
        grid=(n1 // FFT_KB,),
        in_specs=[pl.BlockSpec((FFT_KB, 2 * n2, 2 * n2), lambda k: (k, 0, 0)),
                  pl.BlockSpec((None, nh, 2, FFT_KB, sub, c), lambda k: (0, 0, 0, k, 0, 0)),
                  pl.BlockSpec((1, c), lambda k: (0, 0))],
        out_specs=pl.BlockSpec((FFT_KB, 2, n2, c), lambda k: (k, 0, 0, 0)),
        out_shape=jax.ShapeDtypeStruct((n1, 2, n2, c), F32),
        compiler_params=_params("parallel"),
        name="hy_spec_filter",
    )(g_fwd, a6, scale)


def _spec_body(gf_ref, gi_ref, h_ref, a_ref, o_ref):
    nh, _, kb, sub, ct = a_ref.shape
    n2 = nh * sub
    def forward(kk):
        return _dot(gf_ref[kk], _stack_re_im(a_ref, kk))

    spec_next = forward(0)
    for kk in range(kb):
        spec = spec_next
        if kk + 1 < kb:
            spec_next = forward(kk + 1)
        xr, xi = spec[:n2], spec[n2:]
        hr, hi = h_ref[kk, 0], h_ref[kk, 1]
        y = jnp.concatenate([xr * hr - xi * hi, xr * hi + xi * hr], axis=0).astype(BF16)
        back = _dot(gi_ref[kk], y)
        o_ref[:, 0, kk] = back[:n2].reshape(nh, sub, ct).astype(BF16)
        o_ref[:, 1, kk] = back[n2:].reshape(nh, sub, ct).astype(BF16)


def _spec(g_fwd, g_inv, hspec, a6):
    p, nh, _, n1, sub, c = a6.shape
    n2 = nh * sub
    blk = pl.BlockSpec((None, nh, 2, FFT_KB, sub, c), lambda k, pi: (pi, 0, 0, k, 0, 0))
    mat = pl.BlockSpec((FFT_KB, 2 * n2, 2 * n2), lambda k, pi: (k, 0, 0))
    return pl.pallas_call(
        _spec_body,
        grid=(n1 // FFT_KB, p),
        in_specs=[mat, mat, pl.BlockSpec((FFT_KB, 2, n2, c), lambda k, pi: (k, 0, 0, 0)), blk],
        out_specs=blk,
        out_shape=jax.ShapeDtypeStruct(a6.shape, BF16),
        compiler_params=_params("parallel", "parallel"),
        name="hy_spec",
    )(g_fwd, g_inv, hspec, a6)


def _long_conv(vv, h_raw, h_norm):
    b, l, c = vv.shape
    n1, n2 = _fft_split(2 * l)
    nh = n2 // FFT_SUB
    m1_data, m1_filt, m3, g_fwd, g_inv = _fft_tables(n1, n2)
    idx = np.arange(FFT_SUB * FFT_SUB)
    swapped = (idx % FFT_SUB) * FFT_SUB + idx // FFT_SUB
    perm = jnp.asarray(idx[None, :] == swapped[:, None], dtype=BF16)
    a_h = _dft_outer(m1_filt, perm, h_raw.reshape(1, n1, nh, FFT_SUB, c), "hy_dft1_filter")
    hspec = _spec_filter(g_fwd, a_h, 1.0 / h_norm)
    a = _dft_outer(m1_data, perm, vv.reshape(b // 2, n1, nh, FFT_SUB, c), "hy_dft1")
    y = _idft_outer(m3, perm, _spec(g_fwd, g_inv, hspec, a))
    return y.reshape(b, l, c)


def _hy_out_stage(cv_ref, vv_ref, x0_ref, bd_ref, w_ref, x_ref):
    y = ((cv_ref[...] + bd_ref[...] * vv_ref[...].astype(F32)) * x0_ref[...].astype(F32)).astype(BF16)
    return x_ref[...] + _dot(y, w_ref[...])


def _mixer_b(x, g, w_in, conv_w, conv_b, filt, bias_d, w_out):
    _, s, _ = x.shape
    x0, vv = _hy_in(x, g, w_in, conv_w, conv_b)
    h_raw, h_norm = _hy_filter(s, *filt)
    conv = _long_conv(vv, h_raw, h_norm)

    def specs(tm):
        tok = pl.BlockSpec((None, tm, D_MODEL), lambda bi, i: (bi, i, 0))
        return [tok, tok, tok, pl.BlockSpec((1, D_MODEL), lambda bi, i: (0, 0)), _resident(w_out.shape)]

    return (_hy_out_stage, [conv, vv, x0, bias_d, w_out], specs, lambda tm: [], 512, "hy_out_xattn")


def _gelu(z):
    return 0.5 * z * (1.0 + lax.erf(z * (2.0 ** -0.5)))


def _sgu_body(x_ref, g_ref, win_ref, lng_ref, lnb_ref, ws_ref, bs_ref, wout_ref, o_ref, y_ref, z_ref):
    tm = x_ref.shape[0]
    x = x_ref[...]
    xn = _rms(x, g_ref[...]).astype(BF16)
    for c in range(2 * D_MODEL // FF_CHUNK):
        cols = slice(c * FF_CHUNK, (c + 1) * FF_CHUNK)
        z_ref[:, cols] = _gelu(_dot(xn, win_ref[:, cols]))
    zv = z_ref[:, D_MODEL:]
    zc = zv - jnp.mean(zv, axis=-1, keepdims=True)
    zv = zc * lax.rsqrt(jnp.mean(zc * zc, axis=-1, keepdims=True) + EPS) * lng_ref[...] + lnb_ref[...]
    zvb = zv.astype(BF16)
    n_chunks = tm // C_CHUNK
    for h in range(C_GROUPS):
        cols = slice(h * LANES, (h + 1) * LANES)
        v_all = jnp.concatenate([zvb[c * C_CHUNK:(c + 1) * C_CHUNK, cols] for c in range(n_chunks)], axis=1)
        sv = _dot(ws_ref[h], v_all) + bs_ref[:, h:h + 1]
        for c in range(n_chunks):
            rows = slice(c * C_CHUNK, (c + 1) * C_CHUNK)
            y_ref[rows, cols] = (z_ref[rows, cols] * sv[:, c * LANES:(c + 1) * LANES]).astype(BF16)
    o_ref[...] = x + _dot(y_ref[...], wout_ref[...])


def _mixer_c(x, g, w_in, ln_g, ln_b, w_s, b_s_t, w_out):
    b, s, _ = x.shape
    t = b * s
    tm = 512
    tok = pl.BlockSpec((tm, D_MODEL), lambda i: (i, 0))
    row = pl.BlockSpec((1, D_MODEL), lambda i: (0, 0))
    return pl.pallas_call(
        _sgu_body,
        grid=(t // tm,),
        in_specs=[tok, row, _resident(w_in.shape), row, row, _resident(w_s.shape),
                  pl.BlockSpec(b_s_t.shape, lambda i: (0, 0)), _resident(w_out.shape)],
        out_specs=tok,
        out_shape=jax.ShapeDtypeStruct((t, D_MODEL), F32),
        scratch_shapes=[pltpu.VMEM((tm, D_MODEL), BF16), pltpu.VMEM((tm, 2 * D_MODEL), F32)],
        compiler_params=_params("parallel"),
        name="sgu",
    )(x.reshape(t, D_MODEL), g, w_in, ln_g, ln_b, w_s, b_s_t, w_out).reshape(x.shape)


def _trunk(x, kvs, boff, w):
    b, s, _ = x.shape
    t = b * s
    for i in range(DEPTH):
        kind, j = i % N_MIXERS, i // N_MIXERS
        g = w["g_mix"][i][None, :]
        mixer_out = None
        if kind == 0:
            mixer_out = _mixer_a(x, g, w["a_w_in"][j], w["a_w_out"][j])
        elif kind == 1:
            filt = tuple(w[k][j] for k in ("b_f_w1", "b_f_b1", "b_f_w2", "b_f_b2", "b_f_w3", "b_f_b3",
                                          "b_f_wout", "b_f_freq"))
            mixer_out = _mixer_b(x, g, w["b_w_in"][j], w["b_conv_w"][j], w["b_conv_b"][j][None, :], filt,
                                 w["b_bias_d"][j][None, :], w["b_w_out"][j])
        else:
            x = _mixer_c(x, g, w["c_w_in"][j], w["c_ln_g"][j][None, :], w["c_ln_b"][j][None, :],
                         w["c_w_s"][j], w["c_b_s"][j].T, w["c_w_out"][j])
        x = _xattn(x, w["g_cross"][i][None, :], w["x_w_q"][i], kvs[i], boff, w["x_w_o"][i], mixer_out)
        g_final = w["g_final"][None, :] if i == DEPTH - 1 else None
        x = _ffn(x.reshape(t, D_MODEL), w["g_ffn"][i][None, :], w["f_w_gu"][i], w["f_w_down"][i],
                 g_final).reshape(b, s, D_MODEL)
    return x


_BF16_WEIGHTS = ("a_w_in", "a_w_out", "b_w_in", "b_w_out", "c_w_in", "c_w_s", "c_w_out",
                 "x_w_q", "x_w_kv", "x_w_o", "f_w_gu", "f_w_down")


def kernel(x_prompt, x_sample, mem_prompt, mem_sample, g_mix, g_cross, g_ffn, g_final, a_w_in, a_w_out, b_w_in, b_conv_w, b_conv_b, b_f_w1, b_f_b1, b_f_w2, b_f_b2, b_f_w3, b_f_b3, b_f_wout, b_f_freq, b_bias_d, b_w_out, c_w_in, c_ln_g, c_ln_b, c_w_s, c_b_s, c_w_out, x_w_q, x_w_kv, x_w_o, f_w_gu, f_w_down):
    w = dict(g_mix=g_mix, g_cross=g_cross, g_ffn=g_ffn, g_final=g_final,
             a_w_in=a_w_in, a_w_out=a_w_out, b_w_in=b_w_in, b_conv_w=b_conv_w, b_conv_b=b_conv_b,
             b_f_w1=b_f_w1, b_f_b1=b_f_b1, b_f_w2=b_f_w2, b_f_b2=b_f_b2, b_f_w3=b_f_w3,
             b_f_b3=b_f_b3, b_f_wout=b_f_wout, b_f_freq=b_f_freq, b_bias_d=b_bias_d,
             b_w_out=b_w_out, c_w_in=c_w_in, c_ln_g=c_ln_g, c_ln_b=c_ln_b, c_w_s=c_w_s,
             c_b_s=c_b_s, c_w_out=c_w_out, x_w_q=x_w_q, x_w_kv=x_w_kv, x_w_o=x_w_o,
             f_w_gu=f_w_gu, f_w_down=f_w_down)
    for name in _BF16_WEIGHTS:
        w[name] = w[name].astype(BF16)
    nb_prompt = mem_prompt.shape[0]
    mem = jnp.concatenate([mem_prompt, mem_sample], axis=0)
    mem2 = mem.reshape(mem.shape[0] * MEM_LEN, D_MODEL)
    kvs = [_kv_proj(mem2, w["x_w_kv"][i]).reshape(mem.shape[0], MEM_LEN, 2 * D_MODEL) for i in range(DEPTH)]
    y_prompt = _trunk(x_prompt, kvs, 0, w)
    y_sample = _trunk(x_sample, kvs, nb_prompt, w)
    return (y_prompt, y_sample)
```

```python
import functools
import math

import numpy as np
import jax
import jax.numpy as jnp
from jax import lax
from jax.experimental import pallas as pl
from jax.experimental.pallas import tpu as pltpu

F32 = jnp.float32
BF16 = jnp.bfloat16

D_MODEL = 1024
DEPTH = 4
N_MIXERS = 3

A_GROUPS = ((128, 1), (512, 4), (2048, 16))
A_N_GROUPS = len(A_GROUPS)
A_HEADS = 16
A_HEAD_DIM = D_MODEL // A_HEADS
A_HALF = 64
A_SUBTILE = 128
A_QUERIES_PER_STEP = 512
A_Q_SCALE = A_HEAD_DIM ** -0.5 * math.log2(math.e)
ROPE_THETA = 10000.0

HY_EMB = 33
HY_BANDS = (HY_EMB - 1) // 2
HY_HID_PAD = 128
HY_DECAY_TARGET = 1e-2
HY_DECAY_STRONG_PCT = 0.3
HY_DECAY_WEAK_PCT = 1.5

C_CHUNK = 128
C_GROUPS = 8

MEM_LEN = 256
X_HEADS = 4
X_HEAD_DIM = D_MODEL // X_HEADS

D_FF = -(-8 * D_MODEL // (3 * 256)) * 256
FF_CHUNK = 256

EPS = 1e-6
NEG_INF = -1e30

LANES = 128
VMEM_LIMIT = 56 * 1024 * 1024


def _params(*sem):
    return pltpu.CompilerParams(dimension_semantics=sem, vmem_limit_bytes=VMEM_LIMIT)


def _resident(shape):
    nd = len(shape)
    return pl.BlockSpec(shape, lambda *_: (0,) * nd, pipeline_mode=pl.Buffered(1))


def _dot(a, b):
    return jnp.dot(a, b, preferred_element_type=F32)


def _dot_nt(a, b):
    return lax.dot_general(a, b, (((1,), (1,)), ((), ())), preferred_element_type=F32)


def _rms(x, g):
    return x * lax.rsqrt(jnp.mean(x * x, axis=-1, keepdims=True) + EPS) * g


def _ffn_body(*refs, final):
    if final:
        x_ref, g_ref, wgu_ref, wd_ref, gf_ref, o_ref, act_ref = refs
    else:
        x_ref, g_ref, wgu_ref, wd_ref, o_ref, act_ref = refs
    x = x_ref[...]
    xn = _rms(x, g_ref[...]).astype(BF16)
    for c in range(D_FF // FF_CHUNK):
        lo = c * FF_CHUNK
        gate = _dot(xn, wgu_ref[:, lo:lo + FF_CHUNK])
        up = _dot(xn, wgu_ref[:, D_FF + lo:D_FF + lo + FF_CHUNK])
        act_ref[:, lo:lo + FF_CHUNK] = (gate / (1.0 + jnp.exp(-gate)) * up).astype(BF16)
    y = x + _dot(act_ref[...], wd_ref[...])
    if final:
        y = _rms(y, gf_ref[...])
    o_ref[...] = y


def _ffn(x2, g, wgu, wd, g_final=None):
    t = x2.shape[0]
    tm = 512
    final = g_final is not None
    tok = pl.BlockSpec((tm, D_MODEL), lambda i: (i, 0))
    row = pl.BlockSpec((1, D_MODEL), lambda i: (0, 0))
    in_specs = [tok, row, _resident(wgu.shape), _resident(wd.shape)]
    args = [x2, g, wgu, wd]
    if final:
        in_specs.append(row)
        args.append(g_final)
    return pl.pallas_call(
        functools.partial(_ffn_body, final=final),
        grid=(t // tm,),
        in_specs=in_specs,
        out_specs=tok,
        out_shape=jax.ShapeDtypeStruct((t, D_MODEL), F32),
        scratch_shapes=[pltpu.VMEM((tm, D_FF), BF16)],
        compiler_params=_params("parallel"),
        name="ffn_final" if final else "ffn",
    )(*args)


def _kv_body(m_ref, w_ref, o_ref):
    o_ref[...] = _dot(m_ref[...].astype(BF16), w_ref[...]).astype(BF16)


def _kv_proj(mem2, wkv):
    r = mem2.shape[0]
    tm = 256
    return pl.pallas_call(
        _kv_body,
        grid=(r // tm,),
        in_specs=[pl.BlockSpec((tm, D_MODEL), lambda i: (i, 0)), _resident(wkv.shape)],
        out_specs=pl.BlockSpec((tm, 2 * D_MODEL), lambda i: (i, 0)),
        out_shape=jax.ShapeDtypeStruct((r, 2 * D_MODEL), BF16),
        compiler_params=_params("parallel"),
        name="kv_proj",
    )(mem2, wkv)


def _xattn_stage(x, g_ref, wq_ref, k_ref, v_ref, wo_ref, y_ref):
    xn = _rms(x, g_ref[...]).astype(BF16)
    q = (_dot(xn, wq_ref[...]) * (X_HEAD_DIM ** -0.5)).astype(BF16)
    for h in range(X_HEADS):
        sl = slice(h * X_HEAD_DIM, (h + 1) * X_HEAD_DIM)
        s = _dot_nt(q[:, sl], k_ref[:, sl])
        p = jnp.exp(s - jnp.max(s, axis=-1, keepdims=True))
        inv = 1.0 / jnp.sum(p, axis=-1, keepdims=True)
        y_ref[:, sl] = (_dot(p.astype(BF16), v_ref[:, sl]) * inv).astype(BF16)
    return x + _dot(y_ref[...], wo_ref[...])


def _xattn_body(*refs, mixer_out, n_mixer_in, n_mixer_scratch):
    n_in = n_mixer_in + 6
    mixer_in, (x_ref, g_ref, wq_ref, k_ref, v_ref, wo_ref) = refs[:n_mixer_in], refs[n_mixer_in:n_in]
    o_ref = refs[n_in]
    mixer_scratch, y_ref = refs[n_in + 1:n_in + 1 + n_mixer_scratch], refs[n_in + 1 + n_mixer_scratch]
    x = x_ref[...] if mixer_out is None else mixer_out(*mixer_in, x_ref, *mixer_scratch)
    o_ref[...] = _xattn_stage(x, g_ref, wq_ref, k_ref, v_ref, wo_ref, y_ref)


def _xattn(x, g, wq, kv, boff, wo, mixer=None):
    b, s, _ = x.shape
    if mixer is None:
        stage, m_args, m_specs, m_scratch, tm, name = None, [], [], [], 1024, "xattn"
    else:
        stage, m_args, specs_fn, scratch_fn, tm, name = mixer
        m_specs, m_scratch = specs_fn(tm), scratch_fn(tm)
    tok = pl.BlockSpec((None, tm, D_MODEL), lambda bi, i: (bi, i, 0))
    return pl.pallas_call(
        functools.partial(_xattn_body, mixer_out=stage, n_mixer_in=len(m_args), n_mixer_scratch=len(m_scratch)),
        grid=(b, s // tm),
        in_specs=m_specs + [
            tok,
            pl.BlockSpec((1, D_MODEL), lambda bi, i: (0, 0)),
            _resident(wq.shape),
            pl.BlockSpec((None, MEM_LEN, D_MODEL), lambda bi, i: (bi + boff, 0, 0)),
            pl.BlockSpec((None, MEM_LEN, D_MODEL), lambda bi, i: (bi + boff, 0, 1)),
            _resident(wo.shape),
        ],
        out_specs=tok,
        out_shape=jax.ShapeDtypeStruct(x.shape, F32),
        scratch_shapes=m_scratch + [pltpu.VMEM((tm, D_MODEL), BF16)],
        compiler_params=_params("parallel", "parallel"),
        name=name,
    )(*m_args, x, g, wq, kv, kv, wo)


def _a_in_body(x_ref, g_ref, w_ref, cos_ref, sin_ref, *rest):
    o_refs, (xs_ref, xp_ref) = rest[:A_N_GROUPS], rest[A_N_GROUPS:]
    tm = x_ref.shape[0]
    xn_f32 = _rms(x_ref[...], g_ref[...])
    reps = D_MODEL // LANES
    for cb in range(reps):
        xs_ref[cb] = xn_f32[:, cb * LANES:(cb + 1) * LANES]
    lane = lax.broadcasted_iota(jnp.int32, (tm, LANES), 1)
    low_half = (lane & (A_HEAD_DIM // 2)) == 0
    for gidx, (o_ref, (_, dil)) in enumerate(zip(o_refs, A_GROUPS)):
        rows = tm // dil
        if dil == 1:
            xn = xn_f32.astype(BF16)
            cos, sin = cos_ref[...], sin_ref[...]
        else:
            for r in range(dil):
                for cb in range(reps):
                    xp_ref[r * rows:(r + 1) * rows, cb * LANES:(cb + 1) * LANES] = (
                        xs_ref[cb, pl.ds(r, rows, stride=dil), :].astype(BF16))
            xn = xp_ref[...]
            cos = jnp.concatenate([cos_ref[pl.ds(r, rows, stride=dil), :] for r in range(dil)], axis=0)
            sin = jnp.concatenate([sin_ref[pl.ds(r, rows, stride=dil), :] for r in range(dil)], axis=0)
        for part in range(3):
            lo = (gidx * 3 + part) * D_MODEL
            y = _dot(xn, w_ref[:, lo:lo + D_MODEL])
            if part == 2:
                o_ref[:, :, 2 * D_MODEL:] = y.reshape(dil, rows, D_MODEL).astype(BF16)
                continue
            c_t, s_t = (cos * A_Q_SCALE, sin * A_Q_SCALE) if part == 0 else (cos, sin)
            for cb in range(reps):
                yb = y[:, cb * LANES:(cb + 1) * LANES]
                partner = jnp.where(low_half, pltpu.roll(yb, LANES - A_HEAD_DIM // 2, 1),
                                    pltpu.roll(yb, A_HEAD_DIM // 2, 1))
                col = part * D_MODEL + cb * LANES
                o_ref[:, :, col:col + LANES] = (yb * c_t + partner * s_t).reshape(dil, rows, LANES).astype(BF16)


def _a_in(x, g, w, cos_t, sin_t):
    b, s, _ = x.shape
    tm = 512
    return pl.pallas_call(
        _a_in_body,
        grid=(b, s // tm),
        in_specs=[
            pl.BlockSpec((None, tm, D_MODEL), lambda bi, i: (bi, i, 0)),
            pl.BlockSpec((1, D_MODEL), lambda bi, i: (0, 0)),
            _resident(w.shape),
            pl.BlockSpec((tm, LANES), lambda bi, i: (i, 0)),
            pl.BlockSpec((tm, LANES), lambda bi, i: (i, 0)),
        ],
        out_specs=[pl.BlockSpec((None, dil, tm // dil, 3 * D_MODEL), lambda bi, i: (bi, 0, i, 0))
                   for _, dil in A_GROUPS],
        out_shape=[jax.ShapeDtypeStruct((b, dil, s // dil, 3 * D_MODEL), BF16) for _, dil in A_GROUPS],
        scratch_shapes=[pltpu.VMEM((D_MODEL // LANES, tm, LANES), F32), pltpu.VMEM((tm, D_MODEL), BF16)],
        compiler_params=_params("parallel", "parallel"),
        name="a_in",
    )(x, g, w, cos_t, sin_t)


def _attn_body(q_ref, kp_ref, k_ref, kn_ref, vp_ref, v_ref, vn_ref, o_ref, lse_ref, *, n_tiles):
    n_res, tq = q_ref.shape[0], q_ref.shape[1]
    sq = A_SUBTILE
    win = sq + 2 * A_HALF
    n_sub_tiles = tq // sq
    i = pl.program_id(2)
    qi = lax.broadcasted_iota(jnp.int32, (sq, win), 0)
    kj = lax.broadcasted_iota(jnp.int32, (sq, win), 1)
    band = (kj >= qi) & (kj <= qi + 2 * A_HALF)
    left = lax.broadcasted_iota(jnp.int32, (sq, LANES), 1) < A_HEAD_DIM
    for res, st in [(a, c) for a in range(n_res) for c in range(n_sub_tiles)]:
        if st == 0:
            kw = jnp.concatenate([kp_ref[res], k_ref[res], kn_ref[res]], axis=0)
            vw = jnp.concatenate([vp_ref[res], v_ref[res], vn_ref[res]], axis=0)
        q_res, o_res, lse_res = q_ref.at[res], o_ref.at[res], lse_ref.at[res]
        rows = slice(st * sq, (st + 1) * sq)
        mask = band
        if st == 0:
            mask = mask & (kj >= jnp.where(i == 0, A_HALF, 0))
        if st == n_sub_tiles - 1:
            mask = mask & (kj < jnp.where(i == n_tiles - 1, sq + A_HALF, win))
        mask2 = jnp.concatenate([mask, mask], axis=0)
        lse_res[rows, :] = jnp.zeros((sq, LANES), F32)

        def scores(hp, q_res=q_res, rows=rows, kw=kw, st=st):
            sl = slice(hp * LANES, (hp + 1) * LANES)
            q2 = q_res[rows, sl]
            zero = jnp.zeros_like(q2)
            qq = jnp.concatenate([jnp.where(left, q2, zero), jnp.where(left, zero, q2)], axis=0)
            return _dot_nt(qq, kw[st * sq:st * sq + win, sl])

        s_next = scores(0)
        for hp in range(A_HEADS // 2):
            sl = slice(hp * LANES, (hp + 1) * LANES)
            s = jnp.where(mask2, s_next, NEG_INF)
            if hp + 1 < A_HEADS // 2:
                s_next = scores(hp + 1)
            m = jnp.max(s, axis=-1, keepdims=True)
            p = jnp.exp2(s - m)
            l = jnp.sum(p, axis=-1, keepdims=True)
            r = _dot(p.astype(BF16), vw[st * sq:st * sq + win, sl])
            o_res[rows, sl] = jnp.where(left, r[:sq], r[sq:]).astype(BF16)
            for off, col in ((0, m), (A_HEADS, l)):
                lse_res[rows, off + 2 * hp:off + 2 * hp + 1] = col[:sq]
                lse_res[rows, off + 2 * hp + 1:off + 2 * hp + 2] = col[sq:]


def _attn_group(qkv):
    b, dil, n_sub, _ = qkv.shape
    tq = min(A_QUERIES_PER_STEP, n_sub)
    n_res = min(dil, A_QUERIES_PER_STEP // tq)
    n_tiles = n_sub // tq
    halo_per_tile = tq // A_HALF
    n_halo = n_sub // A_HALF

    def own(part):
        return pl.BlockSpec((None, n_res, tq, D_MODEL), lambda bi, r, i: (bi, r, i, part))

    def prev(part):
        return pl.BlockSpec((None, n_res, A_HALF, D_MODEL),
                            lambda bi, r, i: (bi, r, jnp.maximum(i * halo_per_tile - 1, 0), part))

    def nxt(part):
        return pl.BlockSpec((None, n_res, A_HALF, D_MODEL),
                            lambda bi, r, i: (bi, r, jnp.minimum((i + 1) * halo_per_tile, n_halo - 1), part))

    return pl.pallas_call(
        functools.partial(_attn_body, n_tiles=n_tiles),
        grid=(b, dil // n_res, n_tiles),
        in_specs=[own(0), prev(1), own(1), nxt(1), prev(2), own(2), nxt(2)],
        out_specs=[
            pl.BlockSpec((None, n_res, tq, D_MODEL), lambda bi, r, i: (bi, r, i, 0)),
            pl.BlockSpec((None, n_res, tq, LANES), lambda bi, r, i: (bi, r, i, 0)),
        ],
        out_shape=[
            jax.ShapeDtypeStruct((b, dil, n_sub, D_MODEL), BF16),
            jax.ShapeDtypeStruct((b, dil, n_sub, LANES), F32),
        ],
        compiler_params=_params("parallel", "parallel", "parallel"),
        name=f"attn_d{dil}",
    )(qkv, qkv, qkv, qkv, qkv, qkv, qkv)


def _a_out_stage(o1_ref, o2_ref, o3_ref, l1_ref, l2_ref, l3_ref, w_ref, sp_ref, x_ref,
                 os2_ref, os3_ref, ls2_ref, ls3_ref, y_ref):
    tm = x_ref.shape[0]
    for o_ref, l_ref, os_ref, ls_ref in ((o2_ref, l2_ref, os2_ref, ls2_ref), (o3_ref, l3_ref, os3_ref, ls3_ref)):
        dil = o_ref.shape[0]
        rows = tm // dil
        for r in range(dil):
            for cb in range(D_MODEL // LANES):
                os_ref[cb, pl.ds(r, rows, stride=dil), :] = o_ref[r, :, cb * LANES:(cb + 1) * LANES].astype(F32)
            ls_ref[pl.ds(r, rows, stride=dil), :] = l_ref[r]
    s1, s2, s3 = l1_ref[0], ls2_ref[...], ls3_ref[...]
    m = jnp.maximum(jnp.maximum(s1, s2), s3)
    e1, e2, e3 = jnp.exp2(s1 - m), jnp.exp2(s2 - m), jnp.exp2(s3 - m)
    den = sum(e * pltpu.roll(st, LANES - A_HEADS, 1) for e, st in ((e1, s1), (e2, s2), (e3, s3)))
    inv = 1.0 / den
    valid = lax.broadcasted_iota(jnp.int32, (tm, LANES), 1) < A_HEADS
    packed = jnp.zeros((tm, LANES), F32)
    for gidx, e in enumerate((e1, e2, e3)):
        wgt = jnp.where(valid, e * inv, 0.0)
        hi = wgt.astype(BF16).astype(F32)
        for term, val in enumerate((hi, wgt - hi)):
            shift = 2 * A_HEADS * gidx + A_HEADS * term
            packed = packed + (pltpu.roll(val, shift, 1) if shift else val)
    wall = _dot(packed.astype(BF16), sp_ref[...])
    for cb in range(D_MODEL // LANES):
        sl = slice(cb * LANES, (cb + 1) * LANES)
        y = (wall[:, sl] * o1_ref[0, :, sl].astype(F32)
             + wall[:, D_MODEL + cb * LANES:D_MODEL + (cb + 1) * LANES] * os2_ref[cb]
             + wall[:, 2 * D_MODEL + cb * LANES:2 * D_MODEL + (cb + 1) * LANES] * os3_ref[cb])
        y_ref[:, sl] = y.astype(BF16)
    return x_ref[...] + _dot(y_ref[...], w_ref[...])


def _a_out_mixer(outs, lses, w):
    col = np.arange(A_N_GROUPS * D_MODEL)
    row = np.arange(LANES)
    spread = jnp.asarray((row[:, None] // (2 * A_HEADS) == col[None, :] // D_MODEL)
                         & (row[:, None] % A_HEADS == (col[None, :] % D_MODEL) // A_HEAD_DIM)
                         & (row[:, None] < 2 * A_HEADS * A_N_GROUPS), dtype=BF16)

    def specs(tm):
        def grouped(arr):
            dil, width = arr.shape[1], arr.shape[3]
            return pl.BlockSpec((None, dil, tm // dil, width), lambda bi, i: (bi, 0, i, 0))

        return [grouped(a) for a in outs] + [grouped(a) for a in lses] + [_resident(w.shape), _resident(spread.shape)]

    def scratch(tm):
        return [pltpu.VMEM((D_MODEL // LANES, tm, LANES), F32), pltpu.VMEM((D_MODEL // LANES, tm, LANES), F32),
                pltpu.VMEM((tm, LANES), F32), pltpu.VMEM((tm, LANES), F32), pltpu.VMEM((tm, D_MODEL), BF16)]

    return (_a_out_stage, [*outs, *lses, w, spread], specs, scratch, 512, "a_out_xattn")


def _rope_tables(seq_len):
    inv = ROPE_THETA ** (-jnp.arange(0, A_HEAD_DIM, 2, dtype=F32) / A_HEAD_DIM)
    ang = jnp.arange(seq_len, dtype=F32)[:, None] * inv[None, :]
    cos, sin = jnp.cos(ang), jnp.sin(ang)
    reps = LANES // A_HEAD_DIM
    return (jnp.concatenate([cos, cos] * reps, axis=1),
            jnp.concatenate([-sin, sin] * reps, axis=1))


def _mixer_a(x, g, w_in, w_out):
    b, s, _ = x.shape
    cos_t, sin_t = _rope_tables(s)
    outs, lses = [], []
    for qkv in _a_in(x, g, w_in, cos_t, sin_t):
        o, l = _attn_group(qkv)
        outs.append(o)
        lses.append(l)
    return _a_out_mixer(outs, lses, w_out)


HALO = 8
HY_COL_CHUNK = 256


def _hy_in_body(xp_ref, x_ref, xn_ref, g_ref, w_ref, cw_ref, cb_ref, x0_ref, vv_ref, u_ref, *, n_tiles):
    tm = x_ref.shape[0]
    i = pl.program_id(1)
    xe = jnp.concatenate([xp_ref[...], x_ref[...], xn_ref[...]], axis=0)
    xn = _rms(xe, g_ref[...]).astype(BF16)
    row = lax.broadcasted_iota(jnp.int32, (tm + 2 * HALO, 1), 0)
    inside = ((row >= HALO) | (i > 0)) & ((row < tm + HALO) | (i < n_tiles - 1))
    cw = HY_COL_CHUNK
    for c in range(D_MODEL // cw):
        parts = []
        for k in range(3):
            cols = slice(k * D_MODEL + c * cw, k * D_MODEL + (c + 1) * cw)
            u_ref[k] = jnp.where(inside, _dot(xn, w_ref[:, cols]), 0.0)
            parts.append(u_ref[k, pl.ds(HALO - 1, tm), :] * cw_ref[0:1, cols]
                         + u_ref[k, pl.ds(HALO, tm), :] * cw_ref[1:2, cols]
                         + u_ref[k, pl.ds(HALO + 1, tm), :] * cw_ref[2:3, cols]
                         + cb_ref[:, cols])
        x0_ref[:, c * cw:(c + 1) * cw] = parts[0].astype(BF16)
        vv_ref[:, c * cw:(c + 1) * cw] = (parts[2] * parts[1]).astype(BF16)


def _hy_in(x, g, w, conv_w, conv_b):
    b, s, _ = x.shape
    tm = 512
    n_tiles = s // tm
    per = tm // HALO
    n_halo = s // HALO
    tok = pl.BlockSpec((None, tm, D_MODEL), lambda bi, i: (bi, i, 0))
    return pl.pallas_call(
        functools.partial(_hy_in_body, n_tiles=n_tiles),
        grid=(b, n_tiles),
        in_specs=[
            pl.BlockSpec((None, HALO, D_MODEL), lambda bi, i: (bi, jnp.maximum(i * per - 1, 0), 0)),
            tok,
            pl.BlockSpec((None, HALO, D_MODEL), lambda bi, i: (bi, jnp.minimum((i + 1) * per, n_halo - 1), 0)),
            pl.BlockSpec((1, D_MODEL), lambda bi, i: (0, 0)),
            _resident(w.shape),
            pl.BlockSpec(conv_w.shape, lambda bi, i: (0, 0)),
            pl.BlockSpec(conv_b.shape, lambda bi, i: (0, 0)),
        ],
        out_specs=[tok, tok],
        out_shape=[jax.ShapeDtypeStruct(x.shape, BF16), jax.ShapeDtypeStruct(x.shape, BF16)],
        scratch_shapes=[pltpu.VMEM((3, tm + 2 * HALO, HY_COL_CHUNK), F32)],
        compiler_params=_params("parallel", "parallel"),
        name="hy_in",
    )(x, x, x, g, w, conv_w, conv_b)


def _hdot(a, b):
    return jnp.dot(a, b, precision=lax.Precision.HIGHEST, preferred_element_type=F32)


def _hy_filter_body(z_ref, t_ref, a_ref, b_ref, w1_ref, b1_ref, w2_ref, b2_ref, w3_ref, b3_ref,
                    wo_ref, fr_ref, dl_ref, h_ref, sum_ref):
    fr = fr_ref[...]
    hid = jnp.sin(fr * (_hdot(z_ref[...], w1_ref[...]) + b1_ref[...]))
    hid = jnp.sin(fr * (_hdot(hid, w2_ref[...]) + b2_ref[...]))
    hid = jnp.sin(fr * (_hdot(hid, w3_ref[...]) + b3_ref[...]))
    decay = jnp.exp(-t_ref[...] * dl_ref[...])
    hid = hid.astype(BF16)
    h_fwd = _dot(hid, wo_ref[:, :D_MODEL]) * decay
    h_bwd = _dot(hid, wo_ref[:, D_MODEL:]) * decay
    h = a_ref[...] * h_fwd + b_ref[...] * h_bwd
    h_ref[...] = h

    @pl.when(pl.program_id(0) == 0)
    def _():
        sum_ref[...] = jnp.zeros_like(sum_ref)

    sum_ref[...] += jnp.sum(jnp.abs(h), axis=0, keepdims=True)


def _hy_filter(seq_len, f_w1, f_b1, f_w2, f_b2, f_w3, f_b3, f_wout, f_freq):
    n = 2 * seq_len
    lag = jnp.arange(seq_len, dtype=F32)[:, None]
    t = lag / (seq_len - 1)
    w = 2.0 * math.pi * lag / seq_len
    f = jnp.linspace(1e-4, HY_BANDS - 1, HY_BANDS, dtype=F32)[None, :]
    z = jnp.concatenate([t, jnp.cos(f * w), -jnp.sin(f * w),
                         jnp.zeros((seq_len, HY_HID_PAD - HY_EMB), F32)], axis=-1)

    def two_sided(a):
        return jnp.concatenate([a, a[:1], jnp.flip(a[1:], axis=0)], axis=0)

    z2, t2 = two_sided(z), two_sided(t)
    pos = np.arange(n)
    use_fwd = (pos < seq_len).astype(np.float32)[:, None]
    use_bwd = ((pos == 0) | (pos > seq_len)).astype(np.float32)[:, None]
    max_decay = math.log(HY_DECAY_TARGET) / HY_DECAY_STRONG_PCT
    min_decay = math.log(HY_DECAY_TARGET) / HY_DECAY_WEAK_PCT
    deltas = jnp.abs(jnp.linspace(min_decay, max_decay, D_MODEL, dtype=F32))[None, :]

    def pad2(m, rows):
        return jnp.pad(m, ((0, rows - m.shape[0]), (0, HY_HID_PAD - m.shape[1])))

    def padv(v):
        return jnp.pad(v, (0, HY_HID_PAD - v.shape[0]))[None, :]

    wo = jnp.pad(f_wout, ((0, HY_HID_PAD - f_wout.shape[0]), (0, 0))).astype(BF16)
    tr = 512
    rowblk = lambda width: pl.BlockSpec((tr, width), lambda i: (i, 0))
    full = lambda shape: pl.BlockSpec(shape, lambda i: (0, 0))
    sq = (HY_HID_PAD, HY_HID_PAD)
    vec = (1, HY_HID_PAD)
    return pl.pallas_call(
        _hy_filter_body,
        grid=(n // tr,),
        in_specs=[rowblk(HY_HID_PAD), rowblk(1), rowblk(1), rowblk(1),
                  full(sq), full(vec), full(sq), full(vec), full(sq), full(vec),
                  full(wo.shape), full(vec), full((1, D_MODEL))],
        out_specs=[rowblk(D_MODEL), full((1, D_MODEL))],
        out_shape=[jax.ShapeDtypeStruct((n, D_MODEL), F32), jax.ShapeDtypeStruct((1, D_MODEL), F32)],
        compiler_params=_params("arbitrary"),
        name="hy_filter",
    )(z2, t2, jnp.asarray(use_fwd), jnp.asarray(use_bwd),
      pad2(f_w1, HY_HID_PAD), padv(f_b1), pad2(f_w2, HY_HID_PAD), padv(f_b2),
      pad2(f_w3, HY_HID_PAD), padv(f_b3), wo, padv(f_freq), deltas)


def _fft_split(n):
    n1 = 1 << ((n.bit_length() - 1 + 1) // 2)
    return n1, n // n1


def _fft_tables(n1, n2):
    n = n1 * n2
    h = n1 // 2
    idx = np.arange(n1)
    ang = -2.0 * np.pi * ((idx[:, None] * idx[None, :]) % n1) / n1
    fr, fi = np.cos(ang), np.sin(ang)
    m1_data = np.block([[fr[:, :h], -fi[:, :h]], [fi[:, :h], fr[:, :h]]])
    m1_filt = np.concatenate([fr, fi], axis=0)
    ifr, ifi = fr.T[:h] / n, -fi.T[:h] / n
    m3 = np.block([[ifr, -ifi], [ifi, ifr]])
    k1 = jnp.arange(n1, dtype=jnp.int32)[:, None, None]
    k2 = jnp.arange(n2, dtype=jnp.int32)[None, :, None]
    i2 = jnp.arange(n2, dtype=jnp.int32)[None, None, :]
    phase = (i2 * k1 + n1 * i2 * k2) % n
    ga = (-2.0 * math.pi / n) * phase.astype(F32)
    gr, gi = jnp.cos(ga), jnp.sin(ga)
    g_fwd = jnp.concatenate([jnp.concatenate([gr, -gi], axis=2), jnp.concatenate([gi, gr], axis=2)], axis=1)
    grt, git = jnp.swapaxes(gr, 1, 2), jnp.swapaxes(gi, 1, 2)
    g_inv = jnp.concatenate([jnp.concatenate([grt, git], axis=2), jnp.concatenate([-git, grt], axis=2)], axis=1)
    as_bf = lambda m: jnp.asarray(m, dtype=F32).astype(BF16)
    return as_bf(m1_data), as_bf(m1_filt), as_bf(m3), g_fwd.astype(BF16), g_inv.astype(BF16)


FFT_SUB = 16
FFT_TC = 512


def _dft_outer_body(m_ref, pm_ref, z_ref, o_ref, zs_ref, rb_ref):
    q, sub, tc = z_ref.shape
    r = m_ref.shape[0]
    nlb = tc // LANES
    for cb in range(nlb):
        zs_ref[cb] = z_ref[:, :, cb * LANES:(cb + 1) * LANES].astype(F32).reshape(q * sub, LANES)
    hw = tc // 2
    n_groups = r // sub

    def transform(hf, j):
        z = jnp.concatenate([zs_ref[cb, pl.ds(j, q, stride=sub), :]
                             for cb in range(hf * nlb // 2, (hf + 1) * nlb // 2)], axis=1)
        rb_ref[hf, j] = _dot(m_ref[...], z.astype(BF16)).astype(BF16)

    def regroup(hf, g):
        part, i0 = divmod(g * sub, r // 2)
        grouped = rb_ref[hf, :, g * sub:(g + 1) * sub, :].reshape(sub * sub, hw)
        o_ref[part, i0:i0 + sub, :, hf * hw:(hf + 1) * hw] = (
            _dot(pm_ref[...], grouped).reshape(sub, sub, hw).astype(BF16))

    for j in range(sub):
        transform(0, j)
    for step in range(max(sub, n_groups)):
        if step < sub:
            transform(1, step)
        if step < n_groups:
            regroup(0, step)
    for g in range(n_groups):
        regroup(1, g)


def _dft_outer(mat, perm, z5, name):
    p, q, nh, sub, c = z5.shape
    r = mat.shape[0]
    tc = FFT_TC
    return pl.pallas_call(
        _dft_outer_body,
        grid=(p, nh, c // tc),
        in_specs=[pl.BlockSpec(mat.shape, lambda pi, h, ci: (0, 0)),
                  pl.BlockSpec(perm.shape, lambda pi, h, ci: (0, 0)),
                  pl.BlockSpec((None, q, None, sub, tc), lambda pi, h, ci: (pi, 0, h, 0, ci))],
        out_specs=pl.BlockSpec((None, None, 2, r // 2, sub, tc), lambda pi, h, ci: (pi, h, 0, 0, 0, ci)),
        out_shape=jax.ShapeDtypeStruct((p, nh, 2, r // 2, sub, c), BF16),
        scratch_shapes=[pltpu.VMEM((tc // LANES, q * sub, LANES), F32),
                        pltpu.VMEM((2, sub, r, tc // 2), BF16)],
        compiler_params=_params("parallel", "parallel", "parallel"),
        name=name,
    )(mat, perm, z5)


def _idft_outer_body(m_ref, pm_ref, b_ref, o_ref, tb_ref, ys_ref):
    _, n1, sub, tc = b_ref.shape
    q = m_ref.shape[0]
    nlb = tc // LANES
    hw = tc // 2
    n_groups = 2 * n1 // sub

    def regroup(hf, g):
        part, i0 = divmod(g * sub, n1)
        grouped = b_ref[part, i0:i0 + sub, :, hf * hw:(hf + 1) * hw].reshape(sub * sub, hw)
        tb_ref[hf, :, g * sub:(g + 1) * sub, :] = _dot(pm_ref[...], grouped).reshape(sub, sub, hw).astype(BF16)

    def transform(hf, j):
        res = _dot(m_ref[...], tb_ref[hf, j])
        for c in range(nlb // 2):
            ys_ref[hf * nlb // 2 + c, pl.ds(j, q, stride=sub), :] = res[:, c * LANES:(c + 1) * LANES]

    for g in range(n_groups):
        regroup(0, g)
    for step in range(max(sub, n_groups)):
        if step < n_groups:
            regroup(1, step)
        if step < sub:
            transform(0, step)
    for j in range(sub):
        transform(1, j)
    for cb in range(nlb):
        o_ref[:, :, cb * LANES:(cb + 1) * LANES] = ys_ref[cb].reshape(q, sub, LANES)


def _idft_outer(mat, perm, b6):
    p, nh, _, n1, sub, c = b6.shape
    q = mat.shape[0]
    tc = FFT_TC
    return pl.pallas_call(
        _idft_outer_body,
        grid=(p, nh, c // tc),
        in_specs=[pl.BlockSpec(mat.shape, lambda pi, h, ci: (0, 0)),
                  pl.BlockSpec(perm.shape, lambda pi, h, ci: (0, 0)),
                  pl.BlockSpec((None, None, 2, n1, sub, tc), lambda pi, h, ci: (pi, h, 0, 0, 0, ci))],
        out_specs=pl.BlockSpec((None, q, None, sub, tc), lambda pi, h, ci: (pi, 0, h, 0, ci)),
        out_shape=jax.ShapeDtypeStruct((p, q, nh, sub, c), F32),
        scratch_shapes=[pltpu.VMEM((2, sub, 2 * n1, tc // 2), BF16),
                        pltpu.VMEM((tc // LANES, q * sub, LANES), F32)],
        compiler_params=_params("parallel", "parallel", "parallel"),
        name="hy_idft",
    )(mat, perm, b6)


FFT_KB = 4


def _stack_re_im(a_ref, kk):
    nh, _, _, sub, ct = a_ref.shape
    return jnp.concatenate([a_ref[:, 0, kk].reshape(nh * sub, ct), a_ref[:, 1, kk].reshape(nh * sub, ct)], axis=0)


def _spec_filter_body(g_ref, a_ref, sc_ref, h_ref):
    n2 = g_ref.shape[1] // 2
    for kk in range(g_ref.shape[0]):
        spec = _dot(g_ref[kk], _stack_re_im(a_ref, kk)) * sc_ref[...]
        h_ref[kk] = spec.reshape(2, n2, spec.shape[1])


def _spec_filter(g_fwd, a6, scale):
    _, nh, _, n1, sub, c = a6.shape
    n2 = nh * sub
    return pl.pallas_call(
        _spec_filter_body,
        grid=(n1 // FFT_KB,),
        in_specs=[pl.BlockSpec((FFT_KB, 2 * n2, 2 * n2), lambda k: (k, 0, 0)),
                  pl.BlockSpec((None, nh, 2, FFT_KB, sub, c), lambda k: (0, 0, 0, k, 0, 0)),
                  pl.BlockSpec((1, c), lambda k: (0, 0))],
        out_specs=pl.BlockSpec((FFT_KB, 2, n2, c), lambda k: (k, 0, 0, 0)),
        out_shape=jax.ShapeDtypeStruct((n1, 2, n2, c), F32),
        compiler_params=_params("parallel"),
        name="hy_spec_filter",
    )(g_fwd, a6, scale)


def _spec_body(gf_ref, gi_ref, h_ref, a_ref, o_ref):
    nh, _, kb, sub, ct = a_ref.shape
    n2 = nh * sub
    def forward(kk):
        return _dot(gf_ref[kk], _stack_re_im(a_ref, kk))

    spec_next = forward(0)
    for kk in range(kb):
        spec = spec_next
        if kk + 1 < kb:
            spec_next = forward(kk + 1)
        xr, xi = spec[:n2], spec[n2:]
        hr, hi = h_ref[kk, 0], h_ref[kk, 1]
        y = jnp.concatenate([xr * hr - xi * hi, xr * hi + xi * hr], axis=0).astype(BF16)
        back = _dot(gi_ref[kk], y)
        o_ref[:, 0, kk] = back[:n2].reshape(nh, sub, ct).astype(BF16)
        o_ref[:, 1, kk] = back[n2:].reshape(nh, sub, ct).astype(BF16)


def _spec(g_fwd, g_inv, hspec, a6):
    p, nh, _, n1, sub, c = a6.shape
    n2 = nh * sub
    blk = pl.BlockSpec((None, nh, 2, FFT_KB, sub, c), lambda k, pi: (pi, 0, 0, k, 0, 0))
    mat = pl.BlockSpec((FFT_KB, 2 * n2, 2 * n2), lambda k, pi: (k, 0, 0))
    return pl.pallas_call(
        _spec_body,
        grid=(n1 // FFT_KB, p),
        in_specs=[mat, mat, pl.BlockSpec((FFT_KB, 2, n2, c), lambda k, pi: (k, 0, 0, 0)), blk],
        out_specs=blk,
        out_shape=jax.ShapeDtypeStruct(a6.shape, BF16),
        compiler_params=_params("parallel", "parallel"),
        name="hy_spec",
    )(g_fwd, g_inv, hspec, a6)


def _long_conv(vv, h_raw, h_norm):
    b, l, c = vv.shape
    n1, n2 = _fft_split(2 * l)
    nh = n2 // FFT_SUB
    m1_data, m1_filt, m3, g_fwd, g_inv = _fft_tables(n1, n2)
    idx = np.arange(FFT_SUB * FFT_SUB)
    swapped = (idx % FFT_SUB) * FFT_SUB + idx // FFT_SUB
    perm = jnp.asarray(idx[None, :] == swapped[:, None], dtype=BF16)
    a_h = _dft_outer(m1_filt, perm, h_raw.reshape(1, n1, nh, FFT_SUB, c), "hy_dft1_filter")
    hspec = _spec_filter(g_fwd, a_h, 1.0 / h_norm)
    a = _dft_outer(m1_data, perm, vv.reshape(b // 2, n1, nh, FFT_SUB, c), "hy_dft1")
    y = _idft_outer(m3, perm, _spec(g_fwd, g_inv, hspec, a))
    return y.reshape(b, l, c)


def _hy_out_stage(cv_ref, vv_ref, x0_ref, bd_ref, w_ref, x_ref):
    y = ((cv_ref[...] + bd_ref[...] * vv_ref[...].astype(F32)) * x0_ref[...].astype(F32)).astype(BF16)
    return x_ref[...] + _dot(y, w_ref[...])


def _mixer_b(x, g, w_in, conv_w, conv_b, filt, bias_d, w_out):
    _, s, _ = x.shape
    x0, vv = _hy_in(x, g, w_in, conv_w, conv_b)
    h_raw, h_norm = _hy_filter(s, *filt)
    conv = _long_conv(vv, h_raw, h_norm)

    def specs(tm):
        tok = pl.BlockSpec((None, tm, D_MODEL), lambda bi, i: (bi, i, 0))
        return [tok, tok, tok, pl.BlockSpec((1, D_MODEL), lambda bi, i: (0, 0)), _resident(w_out.shape)]

    return (_hy_out_stage, [conv, vv, x0, bias_d, w_out], specs, lambda tm: [], 512, "hy_out_xattn")


def _gelu(z):
    return 0.5 * z * (1.0 + lax.erf(z * (2.0 ** -0.5)))


def _sgu_body(x_ref, g_ref, win_ref, lng_ref, lnb_ref, ws_ref, bs_ref, wout_ref, o_ref, y_ref, z_ref):
    tm = x_ref.shape[0]
    x = x_ref[...]
    xn = _rms(x, g_ref[...]).astype(BF16)
    for c in range(2 * D_MODEL // FF_CHUNK):
        cols = slice(c * FF_CHUNK, (c + 1) * FF_CHUNK)
        z_ref[:, cols] = _gelu(_dot(xn, win_ref[:, cols]))
    zv = z_ref[:, D_MODEL:]
    zc = zv - jnp.mean(zv, axis=-1, keepdims=True)
    zv = zc * lax.rsqrt(jnp.mean(zc * zc, axis=-1, keepdims=True) + EPS) * lng_ref[...] + lnb_ref[...]
    zvb = zv.astype(BF16)
    n_chunks = tm // C_CHUNK
    for h in range(C_GROUPS):
        cols = slice(h * LANES, (h + 1) * LANES)
        v_all = jnp.concatenate([zvb[c * C_CHUNK:(c + 1) * C_CHUNK, cols] for c in range(n_chunks)], axis=1)
        sv = _dot(ws_ref[h], v_all) + bs_ref[:, h:h + 1]
        for c in range(n_chunks):
            rows = slice(c * C_CHUNK, (c + 1) * C_CHUNK)
            y_ref[rows, cols] = (z_ref[rows, cols] * sv[:, c * LANES:(c + 1) * LANES]).astype(BF16)
    o_ref[...] = x + _dot(y_ref[...], wout_ref[...])


def _mixer_c(x, g, w_in, ln_g, ln_b, w_s, b_s_t, w_out):
    b, s, _ = x.shape
    t = b * s
    tm = 512
    tok = pl.BlockSpec((tm, D_MODEL), lambda i: (i, 0))
    row = pl.BlockSpec((1, D_MODEL), lambda i: (0, 0))
    return pl.pallas_call(
        _sgu_body,
        grid=(t // tm,),
        in_specs=[tok, row, _resident(w_in.shape), row, row, _resident(w_s.shape),
                  pl.BlockSpec(b_s_t.shape, lambda i: (0, 0)), _resident(w_out.shape)],
        out_specs=tok,
        out_shape=jax.ShapeDtypeStruct((t, D_MODEL), F32),
        scratch_shapes=[pltpu.VMEM((tm, D_MODEL), BF16), pltpu.VMEM((tm, 2 * D_MODEL), F32)],
        compiler_params=_params("parallel"),
        name="sgu",
    )(x.reshape(t, D_MODEL), g, w_in, ln_g, ln_b, w_s, b_s_t, w_out).reshape(x.shape)


def _trunk(x, kvs, boff, w):
    b, s, _ = x.shape
    t = b * s
    for i in range(DEPTH):
        kind, j = i % N_MIXERS, i // N_MIXERS
        g = w["g_mix"][i][None, :]
        mixer_out = None
        if kind == 0:
            mixer_out = _mixer_a(x, g, w["a_w_in"][j], w["a_w_out"][j])
        elif kind == 1:
            filt = tuple(w[k][j] for k in ("b_f_w1", "b_f_b1", "b_f_w2", "b_f_b2", "b_f_w3", "b_f_b3",
                                          "b_f_wout", "b_f_freq"))
            mixer_out = _mixer_b(x, g, w["b_w_in"][j], w["b_conv_w"][j], w["b_conv_b"][j][None, :], filt,
                                 w["b_bias_d"][j][None, :], w["b_w_out"][j])
        else:
            x = _mixer_c(x, g, w["c_w_in"][j], w["c_ln_g"][j][None, :], w["c_ln_b"][j][None, :],
                         w["c_w_s"][j], w["c_b_s"][j].T, w["c_w_out"][j])
        x = _xattn(x, w["g_cross"][i][None, :], w["x_w_q"][i], kvs[i], boff, w["x_w_o"][i], mixer_out)
        g_final = w["g_final"][None, :] if i == DEPTH - 1 else None
        x = _ffn(x.reshape(t, D_MODEL), w["g_ffn"][i][None, :], w["f_w_gu"][i], w["f_w_down"][i],
                 g_final).reshape(b, s, D_MODEL)
    return x


_BF16_WEIGHTS = ("a_w_in", "a_w_out", "b_w_in", "b_w_out", "c_w_in", "c_w_s", "c_w_out",
                 "x_w_q", "x_w_kv", "x_w_o", "f_w_gu", "f_w_down")


def kernel(x_prompt, x_sample, mem_prompt, mem_sample, g_mix, g_cross, g_ffn, g_final, a_w_in, a_w_out, b_w_in, b_conv_w, b_conv_b, b_f_w1, b_f_b1, b_f_w2, b_f_b2, b_f_w3, b_f_b3, b_f_wout, b_f_freq, b_bias_d, b_w_out, c_w_in, c_ln_g, c_ln_b, c_w_s, c_b_s, c_w_out, x_w_q, x_w_kv, x_w_o, f_w_gu, f_w_down):
    w = dict(g_mix=g_mix, g_cross=g_cross, g_ffn=g_ffn, g_final=g_final,
             a_w_in=a_w_in, a_w_out=a_w_out, b_w_in=b_w_in, b_conv_w=b_conv_w, b_conv_b=b_conv_b,
             b_f_w1=b_f_w1, b_f_b1=b_f_b1, b_f_w2=b_f_w2, b_f_b2=b_f_b2, b_f_w3=b_f_w3,
             b_f_b3=b_f_b3, b_f_wout=b_f_wout, b_f_freq=b_f_freq, b_bias_d=b_bias_d,
             b_w_out=b_w_out, c_w_in=c_w_in, c_ln_g=c_ln_g, c_ln_b=c_ln_b, c_w_s=c_w_s,
             c_b_s=c_b_s, c_w_out=c_w_out, x_w_q=x_w_q, x_w_kv=x_w_kv, x_w_o=x_w_o,
             f_w_gu=f_w_gu, f_w_down=f_w_down)
    for name in _BF16_WEIGHTS:
        w[name] = w[name].astype(BF16)
    nb_prompt = mem_prompt.shape[0]
    mem = jnp.concatenate([mem_prompt, mem_sample], axis=0)
    mem2 = mem.reshape(mem.shape[0] * MEM_LEN, D_MODEL)
    kvs = [_kv_proj(mem2, w["x_w_kv"][i]).reshape(mem.shape[0], MEM_LEN, 2 * D_MODEL) for i in range(DEPTH)]
    y_prompt = _trunk(x_prompt, kvs, 0, w)
    y_sample = _trunk(x_sample, kvs, nb_prompt, w)
    return (y_prompt, y_sample)
```

```python
import functools
import math

import numpy as np
import jax
import jax.numpy as jnp
from jax import lax
from jax.experimental import pallas as pl
from jax.experimental.pallas import tpu as pltpu

F32 = jnp.float32
BF16 = jnp.bfloat16

D_MODEL = 1024
DEPTH = 4
N_MIXERS = 3

A_GROUPS = ((128, 1), (512, 4), (2048, 16))
A_N_GROUPS = len(A_GROUPS)
A_HEADS = 16
A_HEAD_DIM = D_MODEL // A_HEADS
A_HALF = 64
A_SUBTILE = 128
A_QUERIES_PER_STEP = 512
A_Q_SCALE = A_HEAD_DIM ** -0.5 * math.log2(math.e)
ROPE_THETA = 10000.0

HY_EMB = 33
HY_BANDS = (HY_EMB - 1) // 2
HY_HID_PAD = 128
HY_DECAY_TARGET = 1e-2
HY_DECAY_STRONG_PCT = 0.3
HY_DECAY_WEAK_PCT = 1.5

C_CHUNK = 128
C_GROUPS = 8

MEM_LEN = 256
X_HEADS = 4
X_HEAD_DIM = D_MODEL // X_HEADS

D_FF = -(-8 * D_MODEL // (3 * 256)) * 256
FF_CHUNK = 256

EPS = 1e-6
NEG_INF = -1e30

LANES = 128
VMEM_LIMIT = 56 * 1024 * 1024


def _params(*sem):
    return pltpu.CompilerParams(dimension_semantics=sem, vmem_limit_bytes=VMEM_LIMIT)


def _resident(shape):
    nd = len(shape)
    return pl.BlockSpec(shape, lambda *_: (0,) * nd, pipeline_mode=pl.Buffered(1))


def _dot(a, b):
    return jnp.dot(a, b, preferred_element_type=F32)


def _dot_nt(a, b):
    return lax.dot_general(a, b, (((1,), (1,)), ((), ())), preferred_element_type=F32)


def _rms(x, g):
    return x * lax.rsqrt(jnp.mean(x * x, axis=-1, keepdims=True) + EPS) * g


def _ffn_body(*refs, final):
    if final:
        x_ref, g_ref, wgu_ref, wd_ref, gf_ref, o_ref, act_ref = refs
    else:
        x_ref, g_ref, wgu_ref, wd_ref, o_ref, act_ref = refs
    x = x_ref[...]
    xn = _rms(x, g_ref[...]).astype(BF16)
    for c in range(D_FF // FF_CHUNK):
        lo = c * FF_CHUNK
        gate = _dot(xn, wgu_ref[:, lo:lo + FF_CHUNK])
        up = _dot(xn, wgu_ref[:, D_FF + lo:D_FF + lo + FF_CHUNK])
        act_ref[:, lo:lo + FF_CHUNK] = (gate / (1.0 + jnp.exp(-gate)) * up).astype(BF16)
    y = x + _dot(act_ref[...], wd_ref[...])
    if final:
        y = _rms(y, gf_ref[...])
    o_ref[...] = y


def _ffn(x2, g, wgu, wd, g_final=None):
    t = x2.shape[0]
    tm = 1024
    final = g_final is not None
    tok = pl.BlockSpec((tm, D_MODEL), lambda i: (i, 0))
    row = pl.BlockSpec((1, D_MODEL), lambda i: (0, 0))
    in_specs = [tok, row, _resident(wgu.shape), _resident(wd.shape)]
    args = [x2, g, wgu, wd]
    if final:
        in_specs.append(row)
        args.append(g_final)
    return pl.pallas_call(
        functools.partial(_ffn_body, final=final),
        grid=(t // tm,),
        in_specs=in_specs,
        out_specs=tok,
        out_shape=jax.ShapeDtypeStruct((t, D_MODEL), F32),
        scratch_shapes=[pltpu.VMEM((tm, D_FF), BF16)],
        compiler_params=_params("parallel"),
        name="ffn_final" if final else "ffn",
    )(*args)


def _kv_body(m_ref, w_ref, o_ref):
    o_ref[...] = _dot(m_ref[...].astype(BF16), w_ref[...]).astype(BF16)


def _kv_proj(mem2, wkv):
    r = mem2.shape[0]
    tm = 256
    return pl.pallas_call(
        _kv_body,
        grid=(r // tm,),
        in_specs=[pl.BlockSpec((tm, D_MODEL), lambda i: (i, 0)), _resident(wkv.shape)],
        out_specs=pl.BlockSpec((tm, 2 * D_MODEL), lambda i: (i, 0)),
        out_shape=jax.ShapeDtypeStruct((r, 2 * D_MODEL), BF16),
        compiler_params=_params("parallel"),
        name="kv_proj",
    )(mem2, wkv)


def _xattn_stage(x, g_ref, wq_ref, k_ref, v_ref, wo_ref, y_ref):
    xn = _rms(x, g_ref[...]).astype(BF16)
    q = (_dot(xn, wq_ref[...]) * (X_HEAD_DIM ** -0.5)).astype(BF16)
    for h in range(X_HEADS):
        sl = slice(h * X_HEAD_DIM, (h + 1) * X_HEAD_DIM)
        s = _dot_nt(q[:, sl], k_ref[:, sl])
        p = jnp.exp(s - jnp.max(s, axis=-1, keepdims=True))
        inv = 1.0 / jnp.sum(p, axis=-1, keepdims=True)
        y_ref[:, sl] = (_dot(p.astype(BF16), v_ref[:, sl]) * inv).astype(BF16)
    return x + _dot(y_ref[...], wo_ref[...])


def _xattn_body(*refs, mixer_out, n_mixer_in, n_mixer_scratch):
    n_in = n_mixer_in + 6
    mixer_in, (x_ref, g_ref, wq_ref, k_ref, v_ref, wo_ref) = refs[:n_mixer_in], refs[n_mixer_in:n_in]
    o_ref = refs[n_in]
    mixer_scratch, y_ref = refs[n_in + 1:n_in + 1 + n_mixer_scratch], refs[n_in + 1 + n_mixer_scratch]
    x = x_ref[...] if mixer_out is None else mixer_out(*mixer_in, x_ref, *mixer_scratch)
    o_ref[...] = _xattn_stage(x, g_ref, wq_ref, k_ref, v_ref, wo_ref, y_ref)


def _xattn(x, g, wq, kv, boff, wo, mixer=None):
    b, s, _ = x.shape
    if mixer is None:
        stage, m_args, m_specs, m_scratch, tm, name = None, [], [], [], 1024, "xattn"
    else:
        stage, m_args, specs_fn, scratch_fn, tm, name = mixer
        m_specs, m_scratch = specs_fn(tm), scratch_fn(tm)
    tok = pl.BlockSpec((None, tm, D_MODEL), lambda bi, i: (bi, i, 0))
    return pl.pallas_call(
        functools.partial(_xattn_body, mixer_out=stage, n_mixer_in=len(m_args), n_mixer_scratch=len(m_scratch)),
        grid=(b, s // tm),
        in_specs=m_specs + [
            tok,
            pl.BlockSpec((1, D_MODEL), lambda bi, i: (0, 0)),
            _resident(wq.shape),
            pl.BlockSpec((None, MEM_LEN, D_MODEL), lambda bi, i: (bi + boff, 0, 0)),
            pl.BlockSpec((None, MEM_LEN, D_MODEL), lambda bi, i: (bi + boff, 0, 1)),
            _resident(wo.shape),
        ],
        out_specs=tok,
        out_shape=jax.ShapeDtypeStruct(x.shape, F32),
        scratch_shapes=m_scratch + [pltpu.VMEM((tm, D_MODEL), BF16)],
        compiler_params=_params("parallel", "parallel"),
        name=name,
    )(*m_args, x, g, wq, kv, kv, wo)


def _a_in_body(x_ref, g_ref, w_ref, cos_ref, sin_ref, *rest):
    o_refs, (xs_ref, xp_ref) = rest[:A_N_GROUPS], rest[A_N_GROUPS:]
    tm = x_ref.shape[0]
    xn_f32 = _rms(x_ref[...], g_ref[...])
    reps = D_MODEL // LANES
    for cb in range(reps):
        xs_ref[cb] = xn_f32[:, cb * LANES:(cb + 1) * LANES]
    lane = lax.broadcasted_iota(jnp.int32, (tm, LANES), 1)
    low_half = (lane & (A_HEAD_DIM // 2)) == 0
    for gidx, (o_ref, (_, dil)) in enumerate(zip(o_refs, A_GROUPS)):
        rows = tm // dil
        if dil == 1:
            xn = xn_f32.astype(BF16)
            cos, sin = cos_ref[...], sin_ref[...]
        else:
            for r in range(dil):
                for cb in range(reps):
                    xp_ref[r * rows:(r + 1) * rows, cb * LANES:(cb + 1) * LANES] = (
                        xs_ref[cb, pl.ds(r, rows, stride=dil), :].astype(BF16))
            xn = xp_ref[...]
            cos = jnp.concatenate([cos_ref[pl.ds(r, rows, stride=dil), :] for r in range(dil)], axis=0)
            sin = jnp.concatenate([sin_ref[pl.ds(r, rows, stride=dil), :] for r in range(dil)], axis=0)
        for part in range(3):
            lo = (gidx * 3 + part) * D_MODEL
            y = _dot(xn, w_ref[:, lo:lo + D_MODEL])
            if part == 2:
                o_ref[:, :, 2 * D_MODEL:] = y.reshape(dil, rows, D_MODEL).astype(BF16)
                continue
            c_t, s_t = (cos * A_Q_SCALE, sin * A_Q_SCALE) if part == 0 else (cos, sin)
            for cb in range(reps):
                yb = y[:, cb * LANES:(cb + 1) * LANES]
                partner = jnp.where(low_half, pltpu.roll(yb, LANES - A_HEAD_DIM // 2, 1),
                                    pltpu.roll(yb, A_HEAD_DIM // 2, 1))
                col = part * D_MODEL + cb * LANES
                o_ref[:, :, col:col + LANES] = (yb * c_t + partner * s_t).reshape(dil, rows, LANES).astype(BF16)


def _a_in(x, g, w, cos_t, sin_t):
    b, s, _ = x.shape
    tm = 512
    return pl.pallas_call(
        _a_in_body,
        grid=(b, s // tm),
        in_specs=[
            pl.BlockSpec((None, tm, D_MODEL), lambda bi, i: (bi, i, 0)),
            pl.BlockSpec((1, D_MODEL), lambda bi, i: (0, 0)),
            _resident(w.shape),
            pl.BlockSpec((tm, LANES), lambda bi, i: (i, 0)),
            pl.BlockSpec((tm, LANES), lambda bi, i: (i, 0)),
        ],
        out_specs=[pl.BlockSpec((None, dil, tm // dil, 3 * D_MODEL), lambda bi, i: (bi, 0, i, 0))
                   for _, dil in A_GROUPS],
        out_shape=[jax.ShapeDtypeStruct((b, dil, s // dil, 3 * D_MODEL), BF16) for _, dil in A_GROUPS],
        scratch_shapes=[pltpu.VMEM((D_MODEL // LANES, tm, LANES), F32), pltpu.VMEM((tm, D_MODEL), BF16)],
        compiler_params=_params("parallel", "parallel"),
        name="a_in",
    )(x, g, w, cos_t, sin_t)


def _attn_body(q_ref, kp_ref, k_ref, kn_ref, vp_ref, v_ref, vn_ref, o_ref, lse_ref, *, n_tiles):
    n_res, tq = q_ref.shape[0], q_ref.shape[1]
    sq = A_SUBTILE
    win = sq + 2 * A_HALF
    n_sub_tiles = tq // sq
    i = pl.program_id(2)
    qi = lax.broadcasted_iota(jnp.int32, (sq, win), 0)
    kj = lax.broadcasted_iota(jnp.int32, (sq, win), 1)
    band = (kj >= qi) & (kj <= qi + 2 * A_HALF)
    left = lax.broadcasted_iota(jnp.int32, (sq, LANES), 1) < A_HEAD_DIM
    for res, st in [(a, c) for a in range(n_res) for c in range(n_sub_tiles)]:
        if st == 0:
            kw = jnp.concatenate([kp_ref[res], k_ref[res], kn_ref[res]], axis=0)
            vw = jnp.concatenate([vp_ref[res], v_ref[res], vn_ref[res]], axis=0)
        q_res, o_res, lse_res = q_ref.at[res], o_ref.at[res], lse_ref.at[res]
        rows = slice(st * sq, (st + 1) * sq)
        mask = band
        if st == 0:
            mask = mask & (kj >= jnp.where(i == 0, A_HALF, 0))
        if st == n_sub_tiles - 1:
            mask = mask & (kj < jnp.where(i == n_tiles - 1, sq + A_HALF, win))
        mask2 = jnp.concatenate([mask, mask], axis=0)
        lse_res[rows, :] = jnp.zeros((sq, LANES), F32)

        def scores(hp, q_res=q_res, rows=rows, kw=kw, st=st):
            sl = slice(hp * LANES, (hp + 1) * LANES)
            q2 = q_res[rows, sl]
            zero = jnp.zeros_like(q2)
            qq = jnp.concatenate([jnp.where(left, q2, zero), jnp.where(left, zero, q2)], axis=0)
            return _dot_nt(qq, kw[st * sq:st * sq + win, sl])

        s_next = scores(0)
        for hp in range(A_HEADS // 2):
            sl = slice(hp * LANES, (hp + 1) * LANES)
            s = jnp.where(mask2, s_next, NEG_INF)
            if hp + 1 < A_HEADS // 2:
                s_next = scores(hp + 1)
            m = jnp.max(s, axis=-1, keepdims=True)
            p = jnp.exp2(s - m)
            l = jnp.sum(p, axis=-1, keepdims=True)
            r = _dot(p.astype(BF16), vw[st * sq:st * sq + win, sl])
            o_res[rows, sl] = jnp.where(left, r[:sq], r[sq:]).astype(BF16)
            for off, col in ((0, m), (A_HEADS, l)):
                lse_res[rows, off + 2 * hp:off + 2 * hp + 1] = col[:sq]
                lse_res[rows, off + 2 * hp + 1:off + 2 * hp + 2] = col[sq:]


def _attn_group(qkv):
    b, dil, n_sub, _ = qkv.shape
    tq = min(A_QUERIES_PER_STEP, n_sub)
    n_res = min(dil, A_QUERIES_PER_STEP // tq)
    n_tiles = n_sub // tq
    halo_per_tile = tq // A_HALF
    n_halo = n_sub // A_HALF

    def own(part):
        return pl.BlockSpec((None, n_res, tq, D_MODEL), lambda bi, r, i: (bi, r, i, part))

    def prev(part):
        return pl.BlockSpec((None, n_res, A_HALF, D_MODEL),
                            lambda bi, r, i: (bi, r, jnp.maximum(i * halo_per_tile - 1, 0), part))

    def nxt(part):
        return pl.BlockSpec((None, n_res, A_HALF, D_MODEL),
                            lambda bi, r, i: (bi, r, jnp.minimum((i + 1) * halo_per_tile, n_halo - 1), part))

    return pl.pallas_call(
        functools.partial(_attn_body, n_tiles=n_tiles),
        grid=(b, dil // n_res, n_tiles),
        in_specs=[own(0), prev(1), own(1), nxt(1), prev(2), own(2), nxt(2)],
        out_specs=[
            pl.BlockSpec((None, n_res, tq, D_MODEL), lambda bi, r, i: (bi, r, i, 0)),
            pl.BlockSpec((None, n_res, tq, LANES), lambda bi, r, i: (bi, r, i, 0)),
        ],
        out_shape=[
            jax.ShapeDtypeStruct((b, dil, n_sub, D_MODEL), BF16),
            jax.ShapeDtypeStruct((b, dil, n_sub, LANES), F32),
        ],
        compiler_params=_params("parallel", "parallel", "parallel"),
        name=f"attn_d{dil}",
    )(qkv, qkv, qkv, qkv, qkv, qkv, qkv)


def _a_out_stage(o1_ref, o2_ref, o3_ref, l1_ref, l2_ref, l3_ref, w_ref, sp_ref, x_ref,
                 os2_ref, os3_ref, ls2_ref, ls3_ref, y_ref):
    tm = x_ref.shape[0]
    for o_ref, l_ref, os_ref, ls_ref in ((o2_ref, l2_ref, os2_ref, ls2_ref), (o3_ref, l3_ref, os3_ref, ls3_ref)):
        dil = o_ref.shape[0]
        rows = tm // dil
        for r in range(dil):
            for cb in range(D_MODEL // LANES):
                os_ref[cb, pl.ds(r, rows, stride=dil), :] = o_ref[r, :, cb * LANES:(cb + 1) * LANES].astype(F32)
            ls_ref[pl.ds(r, rows, stride=dil), :] = l_ref[r]
    s1, s2, s3 = l1_ref[0], ls2_ref[...], ls3_ref[...]
    m = jnp.maximum(jnp.maximum(s1, s2), s3)
    e1, e2, e3 = jnp.exp2(s1 - m), jnp.exp2(s2 - m), jnp.exp2(s3 - m)
    den = sum(e * pltpu.roll(st, LANES - A_HEADS, 1) for e, st in ((e1, s1), (e2, s2), (e3, s3)))
    inv = 1.0 / den
    valid = lax.broadcasted_iota(jnp.int32, (tm, LANES), 1) < A_HEADS
    packed = jnp.zeros((tm, LANES), F32)
    for gidx, e in enumerate((e1, e2, e3)):
        wgt = jnp.where(valid, e * inv, 0.0)
        hi = wgt.astype(BF16).astype(F32)
        for term, val in enumerate((hi, wgt - hi)):
            shift = 2 * A_HEADS * gidx + A_HEADS * term
            packed = packed + (pltpu.roll(val, shift, 1) if shift else val)
    wall = _dot(packed.astype(BF16), sp_ref[...])
    for cb in range(D_MODEL // LANES):
        sl = slice(cb * LANES, (cb + 1) * LANES)
        y = (wall[:, sl] * o1_ref[0, :, sl].astype(F32)
             + wall[:, D_MODEL + cb * LANES:D_MODEL + (cb + 1) * LANES] * os2_ref[cb]
             + wall[:, 2 * D_MODEL + cb * LANES:2 * D_MODEL + (cb + 1) * LANES] * os3_ref[cb])
        y_ref[:, sl] = y.astype(BF16)
    return x_ref[...] + _dot(y_ref[...], w_ref[...])


def _a_out_mixer(outs, lses, w):
    col = np.arange(A_N_GROUPS * D_MODEL)
    row = np.arange(LANES)
    spread = jnp.asarray((row[:, None] // (2 * A_HEADS) == col[None, :] // D_MODEL)
                         & (row[:, None] % A_HEADS == (col[None, :] % D_MODEL) // A_HEAD_DIM)
                         & (row[:, None] < 2 * A_HEADS * A_N_GROUPS), dtype=BF16)

    def specs(tm):
        def grouped(arr):
            dil, width = arr.shape[1], arr.shape[3]
            return pl.BlockSpec((None, dil, tm // dil, width), lambda bi, i: (bi, 0, i, 0))

        return [grouped(a) for a in outs] + [grouped(a) for a in lses] + [_resident(w.shape), _resident(spread.shape)]

    def scratch(tm):
        return [pltpu.VMEM((D_MODEL // LANES, tm, LANES), F32), pltpu.VMEM((D_MODEL // LANES, tm, LANES), F32),
                pltpu.VMEM((tm, LANES), F32), pltpu.VMEM((tm, LANES), F32), pltpu.VMEM((tm, D_MODEL), BF16)]

    return (_a_out_stage, [*outs, *lses, w, spread], specs, scratch, 512, "a_out_xattn")


def _rope_tables(seq_len):
    inv = ROPE_THETA ** (-jnp.arange(0, A_HEAD_DIM, 2, dtype=F32) / A_HEAD_DIM)
    ang = jnp.arange(seq_len, dtype=F32)[:, None] * inv[None, :]
    cos, sin = jnp.cos(ang), jnp.sin(ang)
    reps = LANES // A_HEAD_DIM
    return (jnp.concatenate([cos, cos] * reps, axis=1),
            jnp.concatenate([-sin, sin] * reps, axis=1))


def _mixer_a(x, g, w_in, w_out):
    b, s, _ = x.shape
    cos_t, sin_t = _rope_tables(s)
    outs, lses = [], []
    for qkv in _a_in(x, g, w_in, cos_t, sin_t):
        o, l = _attn_group(qkv)
        outs.append(o)
        lses.append(l)
    return _a_out_mixer(outs, lses, w_out)


HALO = 8
HY_COL_CHUNK = 256


def _hy_in_body(xp_ref, x_ref, xn_ref, g_ref, w_ref, cw_ref, cb_ref, x0_ref, vv_ref, u_ref, *, n_tiles):
    tm = x_ref.shape[0]
    i = pl.program_id(1)
    xe = jnp.concatenate([xp_ref[...], x_ref[...], xn_ref[...]], axis=0)
    xn = _rms(xe, g_ref[...]).astype(BF16)
    row = lax.broadcasted_iota(jnp.int32, (tm + 2 * HALO, 1), 0)
    inside = ((row >= HALO) | (i > 0)) & ((row < tm + HALO) | (i < n_tiles - 1))
    cw = HY_COL_CHUNK
    for c in range(D_MODEL // cw):
        parts = []
        for k in range(3):
            cols = slice(k * D_MODEL + c * cw, k * D_MODEL + (c + 1) * cw)
            u_ref[k] = jnp.where(inside, _dot(xn, w_ref[:, cols]), 0.0)
            parts.append(u_ref[k, pl.ds(HALO - 1, tm), :] * cw_ref[0:1, cols]
                         + u_ref[k, pl.ds(HALO, tm), :] * cw_ref[1:2, cols]
                         + u_ref[k, pl.ds(HALO + 1, tm), :] * cw_ref[2:3, cols]
                         + cb_ref[:, cols])
        x0_ref[:, c * cw:(c + 1) * cw] = parts[0].astype(BF16)
        vv_ref[:, c * cw:(c + 1) * cw] = (parts[2] * parts[1]).astype(BF16)


def _hy_in(x, g, w, conv_w, conv_b):
    b, s, _ = x.shape
    tm = 512
    n_tiles = s // tm
    per = tm // HALO
    n_halo = s // HALO
    tok = pl.BlockSpec((None, tm, D_MODEL), lambda bi, i: (bi, i, 0))
    return pl.pallas_call(
        functools.partial(_hy_in_body, n_tiles=n_tiles),
        grid=(b, n_tiles),
        in_specs=[
            pl.BlockSpec((None, HALO, D_MODEL), lambda bi, i: (bi, jnp.maximum(i * per - 1, 0), 0)),
            tok,
            pl.BlockSpec((None, HALO, D_MODEL), lambda bi, i: (bi, jnp.minimum((i + 1) * per, n_halo - 1), 0)),
            pl.BlockSpec((1, D_MODEL), lambda bi, i: (0, 0)),
            _resident(w.shape),
            pl.BlockSpec(conv_w.shape, lambda bi, i: (0, 0)),
            pl.BlockSpec(conv_b.shape, lambda bi, i: (0, 0)),
        ],
        out_specs=[tok, tok],
        out_shape=[jax.ShapeDtypeStruct(x.shape, BF16), jax.ShapeDtypeStruct(x.shape, BF16)],
        scratch_shapes=[pltpu.VMEM((3, tm + 2 * HALO, HY_COL_CHUNK), F32)],
        compiler_params=_params("parallel", "parallel"),
        name="hy_in",
    )(x, x, x, g, w, conv_w, conv_b)


def _hdot(a, b):
    return jnp.dot(a, b, precision=lax.Precision.HIGHEST, preferred_element_type=F32)


def _hy_filter_body(z_ref, t_ref, a_ref, b_ref, w1_ref, b1_ref, w2_ref, b2_ref, w3_ref, b3_ref,
                    wo_ref, fr_ref, dl_ref, h_ref, sum_ref):
    fr = fr_ref[...]
    hid = jnp.sin(fr * (_hdot(z_ref[...], w1_ref[...]) + b1_ref[...]))
    hid = jnp.sin(fr * (_hdot(hid, w2_ref[...]) + b2_ref[...]))
    hid = jnp.sin(fr * (_hdot(hid, w3_ref[...]) + b3_ref[...]))
    decay = jnp.exp(-t_ref[...] * dl_ref[...])
    hid = hid.astype(BF16)
    h_fwd = _dot(hid, wo_ref[:, :D_MODEL]) * decay
    h_bwd = _dot(hid, wo_ref[:, D_MODEL:]) * decay
    h = a_ref[...] * h_fwd + b_ref[...] * h_bwd
    h_ref[...] = h

    @pl.when(pl.program_id(0) == 0)
    def _():
        sum_ref[...] = jnp.zeros_like(sum_ref)

    sum_ref[...] += jnp.sum(jnp.abs(h), axis=0, keepdims=True)


def _hy_filter(seq_len, f_w1, f_b1, f_w2, f_b2, f_w3, f_b3, f_wout, f_freq):
    n = 2 * seq_len
    src = np.concatenate([np.arange(seq_len), [0], np.arange(seq_len - 1, 0, -1)])
    lag = jnp.asarray(src, dtype=F32)[:, None]
    t2 = lag / (seq_len - 1)
    w = 2.0 * math.pi * lag / seq_len
    f = jnp.linspace(1e-4, HY_BANDS - 1, HY_BANDS, dtype=F32)[None, :]
    z2 = jnp.concatenate([t2, jnp.cos(f * w), -jnp.sin(f * w),
                          jnp.zeros((n, HY_HID_PAD - HY_EMB), F32)], axis=-1)
    pos = np.arange(n)
    use_fwd = (pos < seq_len).astype(np.float32)[:, None]
    use_bwd = ((pos == 0) | (pos > seq_len)).astype(np.float32)[:, None]
    max_decay = math.log(HY_DECAY_TARGET) / HY_DECAY_STRONG_PCT
    min_decay = math.log(HY_DECAY_TARGET) / HY_DECAY_WEAK_PCT
    deltas = jnp.abs(jnp.linspace(min_decay, max_decay, D_MODEL, dtype=F32))[None, :]

    def pad2(m, rows):
        return jnp.pad(m, ((0, rows - m.shape[0]), (0, HY_HID_PAD - m.shape[1])))

    def padv(v):
        return jnp.pad(v, (0, HY_HID_PAD - v.shape[0]))[None, :]

    wo = jnp.pad(f_wout, ((0, HY_HID_PAD - f_wout.shape[0]), (0, 0))).astype(BF16)
    tr = 512
    rowblk = lambda width: pl.BlockSpec((tr, width), lambda i: (i, 0))
    full = lambda shape: pl.BlockSpec(shape, lambda i: (0, 0))
    sq = (HY_HID_PAD, HY_HID_PAD)
    vec = (1, HY_HID_PAD)
    return pl.pallas_call(
        _hy_filter_body,
        grid=(n // tr,),
        in_specs=[rowblk(HY_HID_PAD), rowblk(1), rowblk(1), rowblk(1),
                  full(sq), full(vec), full(sq), full(vec), full(sq), full(vec),
                  full(wo.shape), full(vec), full((1, D_MODEL))],
        out_specs=[rowblk(D_MODEL), full((1, D_MODEL))],
        out_shape=[jax.ShapeDtypeStruct((n, D_MODEL), F32), jax.ShapeDtypeStruct((1, D_MODEL), F32)],
        compiler_params=_params("arbitrary"),
        name="hy_filter",
    )(z2, t2, jnp.asarray(use_fwd), jnp.asarray(use_bwd),
      pad2(f_w1, HY_HID_PAD), padv(f_b1), pad2(f_w2, HY_HID_PAD), padv(f_b2),
      pad2(f_w3, HY_HID_PAD), padv(f_b3), wo, padv(f_freq), deltas)


def _fft_split(n):
    n1 = 1 << ((n.bit_length() - 1 + 1) // 2)
    return n1, n // n1


def _fft_tables(n1, n2):
    n = n1 * n2
    h = n1 // 2
    idx = np.arange(n1)
    ang = -2.0 * np.pi * ((idx[:, None] * idx[None, :]) % n1) / n1
    fr, fi = np.cos(ang), np.sin(ang)
    m1_data = np.block([[fr[:, :h], -fi[:, :h]], [fi[:, :h], fr[:, :h]]])
    m1_filt = np.concatenate([fr, fi], axis=0)
    ifr, ifi = fr.T[:h] / n, -fi.T[:h] / n
    m3 = np.block([[ifr, -ifi], [ifi, ifr]])
    k1 = jnp.arange(n1, dtype=jnp.int32)[:, None, None]
    k2 = jnp.arange(n2, dtype=jnp.int32)[None, :, None]
    i2 = jnp.arange(n2, dtype=jnp.int32)[None, None, :]
    phase = (i2 * k1 + n1 * i2 * k2) % n
    ga = (-2.0 * math.pi / n) * phase.astype(F32)
    gr, gi = jnp.cos(ga), jnp.sin(ga)
    g_fwd = jnp.concatenate([jnp.concatenate([gr, -gi], axis=2), jnp.concatenate([gi, gr], axis=2)], axis=1)
    grt, git = jnp.swapaxes(gr, 1, 2), jnp.swapaxes(gi, 1, 2)
    g_inv = jnp.concatenate([jnp.concatenate([grt, git], axis=2), jnp.concatenate([-git, grt], axis=2)], axis=1)
    as_bf = lambda m: jnp.asarray(m, dtype=F32).astype(BF16)
    return as_bf(m1_data), as_bf(m1_filt), as_bf(m3), g_fwd.astype(BF16), g_inv.astype(BF16)


FFT_SUB = 16
FFT_TC = 512


def _dft_outer_body(m_ref, pm_ref, z_ref, o_ref, zs_ref, rb_ref):
    q, sub, tc = z_ref.shape
    r = m_ref.shape[0]
    nlb = tc // LANES
    for cb in range(nlb):
        zs_ref[cb] = z_ref[:, :, cb * LANES:(cb + 1) * LANES].astype(F32).reshape(q * sub, LANES)
    hw = tc // 2
    n_groups = r // sub

    def transform(hf, j):
        z = jnp.concatenate([zs_ref[cb, pl.ds(j, q, stride=sub), :]
                             for cb in range(hf * nlb // 2, (hf + 1) * nlb // 2)], axis=1)
        rb_ref[hf, j] = _dot(m_ref[...], z.astype(BF16)).astype(BF16)

    def regroup(hf, g):
        part, i0 = divmod(g * sub, r // 2)
        grouped = rb_ref[hf, :, g * sub:(g + 1) * sub, :].reshape(sub * sub, hw)
        o_ref[part, i0:i0 + sub, :, hf * hw:(hf + 1) * hw] = (
            _dot(pm_ref[...], grouped).reshape(sub, sub, hw).astype(BF16))

    for j in range(sub):
        transform(0, j)
    for step in range(max(sub, n_groups)):
        if step < sub:
            transform(1, step)
        if step < n_groups:
            regroup(0, step)
    for g in range(n_groups):
        regroup(1, g)


def _dft_outer(mat, perm, z5, name):
    p, q, nh, sub, c = z5.shape
    r = mat.shape[0]
    tc = FFT_TC
    return pl.pallas_call(
        _dft_outer_body,
        grid=(p, nh, c // tc),
        in_specs=[pl.BlockSpec(mat.shape, lambda pi, h, ci: (0, 0)),
                  pl.BlockSpec(perm.shape, lambda pi, h, ci: (0, 0)),
                  pl.BlockSpec((None, q, None, sub, tc), lambda pi, h, ci: (pi, 0, h, 0, ci))],
        out_specs=pl.BlockSpec((None, None, 2, r // 2, sub, tc), lambda pi, h, ci: (pi, h, 0, 0, 0, ci)),
        out_shape=jax.ShapeDtypeStruct((p, nh, 2, r // 2, sub, c), BF16),
        scratch_shapes=[pltpu.VMEM((tc // LANES, q * sub, LANES), F32),
                        pltpu.VMEM((2, sub, r, tc // 2), BF16)],
        compiler_params=_params("parallel", "parallel", "parallel"),
        name=name,
    )(mat, perm, z5)


def _idft_outer_body(m_ref, pm_ref, b_ref, o_ref, tb_ref, ys_ref):
    _, n1, sub, tc = b_ref.shape
    q = m_ref.shape[0]
    nlb = tc // LANES
    hw = tc // 2
    n_groups = 2 * n1 // sub

    def regroup(hf, g):
        part, i0 = divmod(g * sub, n1)
        grouped = b_ref[part, i0:i0 + sub, :, hf * hw:(hf + 1) * hw].reshape(sub * sub, hw)
        tb_ref[hf, :, g * sub:(g + 1) * sub, :] = _dot(pm_ref[...], grouped).reshape(sub, sub, hw).astype(BF16)

    def transform(hf, j):
        res = _dot(m_ref[...], tb_ref[hf, j])
        for c in range(nlb // 2):
            ys_ref[hf * nlb // 2 + c, pl.ds(j, q, stride=sub), :] = res[:, c * LANES:(c + 1) * LANES]

    for g in range(n_groups):
        regroup(0, g)
    for step in range(max(sub, n_groups)):
        if step < n_groups:
            regroup(1, step)
        if step < sub:
            transform(0, step)
    for j in range(sub):
        transform(1, j)
    for cb in range(nlb):
        o_ref[:, :, cb * LANES:(cb + 1) * LANES] = ys_ref[cb].reshape(q, sub, LANES)


def _idft_outer(mat, perm, b6):
    p, nh, _, n1, sub, c = b6.shape
    q = mat.shape[0]
    tc = FFT_TC
    return pl.pallas_call(
        _idft_outer_body,
        grid=(p, nh, c // tc),
        in_specs=[pl.BlockSpec(mat.shape, lambda pi, h, ci: (0, 0)),
                  pl.BlockSpec(perm.shape, lambda pi, h, ci: (0, 0)),
                  pl.BlockSpec((None, None, 2, n1, sub, tc), lambda pi, h, ci: (pi, h, 0, 0, 0, ci))],
        out_specs=pl.BlockSpec((None, q, None, sub, tc), lambda pi, h, ci: (pi, 0, h, 0, ci)),
        out_shape=jax.ShapeDtypeStruct((p, q, nh, sub, c), F32),
        scratch_shapes=[pltpu.VMEM((2, sub, 2 * n1, tc // 2), BF16),
                        pltpu.VMEM((tc // LANES, q * sub, LANES), F32)],
        compiler_params=_params("parallel", "parallel", "parallel"),
        name="hy_idft",
    )(mat, perm, b6)


FFT_KB = 4


def _stack_re_im(a_ref, kk):
    nh, _, _, sub, ct = a_ref.shape
    return jnp.concatenate([a_ref[:, 0, kk].reshape(nh * sub, ct), a_ref[:, 1, kk].reshape(nh * sub, ct)], axis=0)


def _spec_filter_body(g_ref, a_ref, sc_ref, h_ref):
    n2 = g_ref.shape[1] // 2
    for kk in range(g_ref.shape[0]):
        spec = _dot(g_ref[kk], _stack_re_im(a_ref, kk)) * sc_ref[...]
        h_ref[kk] = spec.reshape(2, n2, spec.shape[1])


def _spec_filter(g_fwd, a6, scale):
    _, nh, _, n1, sub, c = a6.shape
    n2 = nh * sub
    return pl.pallas_call(
        _spec_filter_body,
        grid=(n1 // FFT_KB,),
        in_specs=[pl.BlockSpec((FFT_KB, 2 * n2, 2 * n2), lambda k: (k, 0, 0)),
                  pl.BlockSpec((None, nh, 2, FFT_KB, sub, c), lambda k: (0, 0, 0, k, 0, 0)),
                  pl.BlockSpec((1, c), lambda k: (0, 0))],
        out_specs=pl.BlockSpec((FFT_KB, 2, n2, c), lambda k: (k, 0, 0, 0)),
        out_shape=jax.ShapeDtypeStruct((n1, 2, n2, c), F32),
        compiler_params=_params("parallel"),
        name="hy_spec_filter",
    )(g_fwd, a6, scale)


def _spec_body(gf_ref, gi_ref, h_ref, a_ref, o_ref):
    nh, _, kb, sub, ct = a_ref.shape
    n2 = nh * sub
    def forward(kk):
        return _dot(gf_ref[kk], _stack_re_im(a_ref, kk))

    spec_next = forward(0)
    for kk in range(kb):
        spec = spec_next
        if kk + 1 < kb:
            spec_next = forward(kk + 1)
        xr, xi = spec[:n2], spec[n2:]
        hr, hi = h_ref[kk, 0], h_ref[kk, 1]
        y = jnp.concatenate([xr * hr - xi * hi, xr * hi + xi * hr], axis=0).astype(BF16)
        back = _dot(gi_ref[kk], y)
        o_ref[:, 0, kk] = back[:n2].reshape(nh, sub, ct).astype(BF16)
        o_ref[:, 1, kk] = back[n2:].reshape(nh, sub, ct).astype(BF16)


def _spec(g_fwd, g_inv, hspec, a6):
    p, nh, _, n1, sub, c = a6.shape
    n2 = nh * sub
    blk = pl.BlockSpec((None, nh, 2, FFT_KB, sub, c), lambda k, pi: (pi, 0, 0, k, 0, 0))
    mat = pl.BlockSpec((FFT_KB, 2 * n2, 2 * n2), lambda k, pi: (k, 0, 0))
    return pl.pallas_call(
        _spec_body,
        grid=(n1 // FFT_KB, p),
        in_specs=[mat, mat, pl.BlockSpec((FFT_KB, 2, n2, c), lambda k, pi: (k, 0, 0, 0)), blk],
        out_specs=blk,
        out_shape=jax.ShapeDtypeStruct(a6.shape, BF16),
        compiler_params=_params("parallel", "parallel"),
        name="hy_spec",
    )(g_fwd, g_inv, hspec, a6)


def _long_conv(vv, h_raw, h_norm):
    b, l, c = vv.shape
    n1, n2 = _fft_split(2 * l)
    nh = n2 // FFT_SUB
    m1_data, m1_filt, m3, g_fwd, g_inv = _fft_tables(n1, n2)
    idx = np.arange(FFT_SUB * FFT_SUB)
    swapped = (idx % FFT_SUB) * FFT_SUB + idx // FFT_SUB
    perm = jnp.asarray(idx[None, :] == swapped[:, None], dtype=BF16)
    a_h = _dft_outer(m1_filt, perm, h_raw.reshape(1, n1, nh, FFT_SUB, c), "hy_dft1_filter")
    hspec = _spec_filter(g_fwd, a_h, 1.0 / h_norm)
    a = _dft_outer(m1_data, perm, vv.reshape(b // 2, n1, nh, FFT_SUB, c), "hy_dft1")
    y = _idft_outer(m3, perm, _spec(g_fwd, g_inv, hspec, a))
    return y.reshape(b, l, c)


def _hy_out_stage(cv_ref, vv_ref, x0_ref, bd_ref, w_ref, x_ref):
    y = ((cv_ref[...] + bd_ref[...] * vv_ref[...].astype(F32)) * x0_ref[...].astype(F32)).astype(BF16)
    return x_ref[...] + _dot(y, w_ref[...])


def _mixer_b(x, g, w_in, conv_w, conv_b, filt, bias_d, w_out):
    _, s, _ = x.shape
    x0, vv = _hy_in(x, g, w_in, conv_w, conv_b)
    h_raw, h_norm = _hy_filter(s, *filt)
    conv = _long_conv(vv, h_raw, h_norm)

    def specs(tm):
        tok = pl.BlockSpec((None, tm, D_MODEL), lambda bi, i: (bi, i, 0))
        return [tok, tok, tok, pl.BlockSpec((1, D_MODEL), lambda bi, i: (0, 0)), _resident(w_out.shape)]

    return (_hy_out_stage, [conv, vv, x0, bias_d, w_out], specs, lambda tm: [], 512, "hy_out_xattn")


def _gelu(z):
    return 0.5 * z * (1.0 + lax.erf(z * (2.0 ** -0.5)))


def _sgu_body(x_ref, g_ref, win_ref, lng_ref, lnb_ref, ws_ref, bs_ref, wout_ref, o_ref, y_ref, z_ref):
    tm = x_ref.shape[0]
    x = x_ref[...]
    xn = _rms(x, g_ref[...]).astype(BF16)
    for c in range(2 * D_MODEL // FF_CHUNK):
        cols = slice(c * FF_CHUNK, (c + 1) * FF_CHUNK)
        z_ref[:, cols] = _gelu(_dot(xn, win_ref[:, cols]))
    zv = z_ref[:, D_MODEL:]
    zc = zv - jnp.mean(zv, axis=-1, keepdims=True)
    zv = zc * lax.rsqrt(jnp.mean(zc * zc, axis=-1, keepdims=True) + EPS) * lng_ref[...] + lnb_ref[...]
    zvb = zv.astype(BF16)
    n_chunks = tm // C_CHUNK
    for h in range(C_GROUPS):
        cols = slice(h * LANES, (h + 1) * LANES)
        v_all = jnp.concatenate([zvb[c * C_CHUNK:(c + 1) * C_CHUNK, cols] for c in range(n_chunks)], axis=1)
        sv = _dot(ws_ref[h], v_all) + bs_ref[:, h:h + 1]
        for c in range(n_chunks):
            rows = slice(c * C_CHUNK, (c + 1) * C_CHUNK)
            y_ref[rows, cols] = (z_ref[rows, cols] * sv[:, c * LANES:(c + 1) * LANES]).astype(BF16)
    o_ref[...] = x + _dot(y_ref[...], wout_ref[...])


def _mixer_c(x, g, w_in, ln_g, ln_b, w_s, b_s_t, w_out):
    b, s, _ = x.shape
    t = b * s
    tm = 512
    tok = pl.BlockSpec((tm, D_MODEL), lambda i: (i, 0))
    row = pl.BlockSpec((1, D_MODEL), lambda i: (0, 0))
    return pl.pallas_call(
        _sgu_body,
        grid=(t // tm,),
        in_specs=[tok, row, _resident(w_in.shape), row, row, _resident(w_s.shape),
                  pl.BlockSpec(b_s_t.shape, lambda i: (0, 0)), _resident(w_out.shape)],
        out_specs=tok,
        out_shape=jax.ShapeDtypeStruct((t, D_MODEL), F32),
        scratch_shapes=[pltpu.VMEM((tm, D_MODEL), BF16), pltpu.VMEM((tm, 2 * D_MODEL), F32)],
        compiler_params=_params("parallel"),
        name="sgu",
    )(x.reshape(t, D_MODEL), g, w_in, ln_g, ln_b, w_s, b_s_t, w_out).reshape(x.shape)


def _trunk(x, kvs, boff, w):
    b, s, _ = x.shape
    t = b * s
    for i in range(DEPTH):
        kind, j = i % N_MIXERS, i // N_MIXERS
        g = w["g_mix"][i][None, :]
        mixer_out = None
        if kind == 0:
            mixer_out = _mixer_a(x, g, w["a_w_in"][j], w["a_w_out"][j])
        elif kind == 1:
            filt = tuple(w[k][j] for k in ("b_f_w1", "b_f_b1", "b_f_w2", "b_f_b2", "b_f_w3", "b_f_b3",
                                          "b_f_wout", "b_f_freq"))
            mixer_out = _mixer_b(x, g, w["b_w_in"][j], w["b_conv_w"][j], w["b_conv_b"][j][None, :], filt,
                                 w["b_bias_d"][j][None, :], w["b_w_out"][j])
        else:
            x = _mixer_c(x, g, w["c_w_in"][j], w["c_ln_g"][j][None, :], w["c_ln_b"][j][None, :],
                         w["c_w_s"][j], w["c_b_s"][j].T, w["c_w_out"][j])
        x = _xattn(x, w["g_cross"][i][None, :], w["x_w_q"][i], kvs[i], boff, w["x_w_o"][i], mixer_out)
        g_final = w["g_final"][None, :] if i == DEPTH - 1 else None
        x = _ffn(x.reshape(t, D_MODEL), w["g_ffn"][i][None, :], w["f_w_gu"][i], w["f_w_down"][i],
                 g_final).reshape(b, s, D_MODEL)
    return x


_BF16_WEIGHTS = ("a_w_in", "a_w_out", "b_w_in", "b_w_out", "c_w_in", "c_w_s", "c_w_out",
                 "x_w_q", "x_w_kv", "x_w_o", "f_w_gu", "f_w_down")


def kernel(x_prompt, x_sample, mem_prompt, mem_sample, g_mix, g_cross, g_ffn, g_final, a_w_in, a_w_out, b_w_in, b_conv_w, b_conv_b, b_f_w1, b_f_b1, b_f_w2, b_f_b2, b_f_w3, b_f_b3, b_f_wout, b_f_freq, b_bias_d, b_w_out, c_w_in, c_ln_g, c_ln_b, c_w_s, c_b_s, c_w_out, x_w_q, x_w_kv, x_w_o, f_w_gu, f_w_down):
    w = dict(g_mix=g_mix, g_cross=g_cross, g_ffn=g_ffn, g_final=g_final,
             a_w_in=a_w_in, a_w_out=a_w_out, b_w_in=b_w_in, b_conv_w=b_conv_w, b_conv_b=b_conv_b,
             b_f_w1=b_f_w1, b_f_b1=b_f_b1, b_f_w2=b_f_w2, b_f_b2=b_f_b2, b_f_w3=b_f_w3,
             b_f_b3=b_f_b3, b_f_wout=b_f_wout, b_f_freq=b_f_freq, b_bias_d=b_bias_d,
             b_w_out=b_w_out, c_w_in=c_w_in, c_ln_g=c_ln_g, c_ln_b=c_ln_b, c_w_s=c_w_s,
             c_b_s=c_b_s, c_w_out=c_w_out, x_w_q=x_w_q, x_w_kv=x_w_kv, x_w_o=x_w_o,
             f_w_gu=f_w_gu, f_w_down=f_w_down)
    for name in _BF16_WEIGHTS:
        w[name] = [w[name][j].astype(BF16) for j in range(w[name].shape[0])]
    nb_prompt = mem_prompt.shape[0]
    mem = jnp.concatenate([mem_prompt, mem_sample], axis=0)
    mem2 = mem.reshape(mem.shape[0] * MEM_LEN, D_MODEL)
    kvs = [_kv_proj(mem2, w["x_w_kv"][i]).reshape(mem.shape[0], MEM_LEN, 2 * D_MODEL) for i in range(DEPTH)]
    y_prompt = _trunk(x_prompt, kvs, 0, w)
    y_sample = _trunk(x_sample, kvs, nb_prompt, w)
    return (y_prompt, y_sample)
```

```python
import functools
import math

import numpy as np
import jax
import jax.numpy as jnp
from jax import lax
from jax.experimental import pallas as pl
from jax.experimental.pallas import tpu as pltpu

F32 = jnp.float32
BF16 = jnp.bfloat16

D_MODEL = 1024
DEPTH = 4
N_MIXERS = 3

A_GROUPS = ((128, 1), (512, 4), (2048, 16))
A_N_GROUPS = len(A_GROUPS)
A_HEADS = 16
A_HEAD_DIM = D_MODEL // A_HEADS
A_HALF = 64
A_SUBTILE = 128
A_QUERIES_PER_STEP = 512
A_Q_SCALE = A_HEAD_DIM ** -0.5 * math.log2(math.e)
ROPE_THETA = 10000.0

HY_EMB = 33
HY_BANDS = (HY_EMB - 1) // 2
HY_HID_PAD = 128
HY_DECAY_TARGET = 1e-2
HY_DECAY_STRONG_PCT = 0.3
HY_DECAY_WEAK_PCT = 1.5

C_CHUNK = 128
C_GROUPS = 8

MEM_LEN = 256
X_HEADS = 4
X_HEAD_DIM = D_MODEL // X_HEADS

D_FF = -(-8 * D_MODEL // (3 * 256)) * 256
FF_CHUNK = 256

EPS = 1e-6
NEG_INF = -1e30

LANES = 128
VMEM_LIMIT = 56 * 1024 * 1024


def _params(*sem):
    return pltpu.CompilerParams(dimension_semantics=sem, vmem_limit_bytes=VMEM_LIMIT)


def _resident(w):
    if isinstance(w, tuple):
        stack, layer = w
        nd = stack.ndim - 1
        return pl.BlockSpec((None,) + stack.shape[1:], lambda *_: (layer,) + (0,) * nd,
                            pipeline_mode=pl.Buffered(1))
    nd = w.ndim
    return pl.BlockSpec(w.shape, lambda *_: (0,) * nd, pipeline_mode=pl.Buffered(1))


def _arr(w):
    return w[0] if isinstance(w, tuple) else w


def _dot(a, b):
    return jnp.dot(a, b, preferred_element_type=F32)


def _dot_nt(a, b):
    return lax.dot_general(a, b, (((1,), (1,)), ((), ())), preferred_element_type=F32)


def _rms(x, g):
    return x * lax.rsqrt(jnp.mean(x * x, axis=-1, keepdims=True) + EPS) * g


def _ffn_body(*refs, final):
    if final:
        x_ref, g_ref, wgu_ref, wd_ref, gf_ref, o_ref, act_ref = refs
    else:
        x_ref, g_ref, wgu_ref, wd_ref, o_ref, act_ref = refs
    x = x_ref[...]
    xn = _rms(x, g_ref[...]).astype(BF16)
    for c in range(D_FF // FF_CHUNK):
        lo = c * FF_CHUNK
        gate = _dot(xn, wgu_ref[:, lo:lo + FF_CHUNK])
        up = _dot(xn, wgu_ref[:, D_FF + lo:D_FF + lo + FF_CHUNK])
        act_ref[:, lo:lo + FF_CHUNK] = (gate / (1.0 + jnp.exp(-gate)) * up).astype(BF16)
    y = x + _dot(act_ref[...], wd_ref[...])
    if final:
        y = _rms(y, gf_ref[...])
    o_ref[...] = y


def _ffn(x2, g, wgu, wd, g_final=None):
    t = x2.shape[0]
    tm = 1024
    final = g_final is not None
    tok = pl.BlockSpec((tm, D_MODEL), lambda i: (i, 0))
    row = pl.BlockSpec((1, D_MODEL), lambda i: (0, 0))
    in_specs = [tok, row, _resident(wgu), _resident(wd)]
    args = [x2, g, _arr(wgu), _arr(wd)]
    if final:
        in_specs.append(row)
        args.append(g_final)
    return pl.pallas_call(
        functools.partial(_ffn_body, final=final),
        grid=(t // tm,),
        in_specs=in_specs,
        out_specs=tok,
        out_shape=jax.ShapeDtypeStruct((t, D_MODEL), F32),
        scratch_shapes=[pltpu.VMEM((tm, D_FF), BF16)],
        compiler_params=_params("parallel"),
        name="ffn_final" if final else "ffn",
    )(*args)


def _kv_body(m_ref, w_ref, o_ref):
    o_ref[...] = _dot(m_ref[...].astype(BF16), w_ref[...]).astype(BF16)


def _kv_proj(mem2, wkv):
    r = mem2.shape[0]
    tm = 256
    return pl.pallas_call(
        _kv_body,
        grid=(r // tm,),
        in_specs=[pl.BlockSpec((tm, D_MODEL), lambda i: (i, 0)), _resident(wkv)],
        out_specs=pl.BlockSpec((tm, 2 * D_MODEL), lambda i: (i, 0)),
        out_shape=jax.ShapeDtypeStruct((r, 2 * D_MODEL), BF16),
        compiler_params=_params("parallel"),
        name="kv_proj",
    )(mem2, _arr(wkv))


def _xattn_stage(x, g_ref, wq_ref, k_ref, v_ref, wo_ref, y_ref):
    xn = _rms(x, g_ref[...]).astype(BF16)
    q = (_dot(xn, wq_ref[...]) * (X_HEAD_DIM ** -0.5)).astype(BF16)
    for h in range(X_HEADS):
        sl = slice(h * X_HEAD_DIM, (h + 1) * X_HEAD_DIM)
        s = _dot_nt(q[:, sl], k_ref[:, sl])
        p = jnp.exp(s - jnp.max(s, axis=-1, keepdims=True))
        inv = 1.0 / jnp.sum(p, axis=-1, keepdims=True)
        y_ref[:, sl] = (_dot(p.astype(BF16), v_ref[:, sl]) * inv).astype(BF16)
    return x + _dot(y_ref[...], wo_ref[...])


def _xattn_body(*refs, mixer_out, n_mixer_in, n_mixer_scratch):
    n_in = n_mixer_in + 6
    mixer_in, (x_ref, g_ref, wq_ref, k_ref, v_ref, wo_ref) = refs[:n_mixer_in], refs[n_mixer_in:n_in]
    o_ref = refs[n_in]
    mixer_scratch, y_ref = refs[n_in + 1:n_in + 1 + n_mixer_scratch], refs[n_in + 1 + n_mixer_scratch]
    x = x_ref[...] if mixer_out is None else mixer_out(*mixer_in, x_ref, *mixer_scratch)
    o_ref[...] = _xattn_stage(x, g_ref, wq_ref, k_ref, v_ref, wo_ref, y_ref)


def _xattn(x, g, wq, kv, boff, wo, mixer=None):
    b, s, _ = x.shape
    if mixer is None:
        stage, m_args, m_specs, m_scratch, tm, name = None, [], [], [], 1024, "xattn"
    else:
        stage, m_args, specs_fn, scratch_fn, tm, name = mixer
        m_specs, m_scratch = specs_fn(tm), scratch_fn(tm)
    tok = pl.BlockSpec((None, tm, D_MODEL), lambda bi, i: (bi, i, 0))
    return pl.pallas_call(
        functools.partial(_xattn_body, mixer_out=stage, n_mixer_in=len(m_args), n_mixer_scratch=len(m_scratch)),
        grid=(b, s // tm),
        in_specs=m_specs + [
            tok,
            pl.BlockSpec((1, D_MODEL), lambda bi, i: (0, 0)),
            _resident(wq),
            pl.BlockSpec((None, MEM_LEN, D_MODEL), lambda bi, i: (bi + boff, 0, 0)),
            pl.BlockSpec((None, MEM_LEN, D_MODEL), lambda bi, i: (bi + boff, 0, 1)),
            _resident(wo),
        ],
        out_specs=tok,
        out_shape=jax.ShapeDtypeStruct(x.shape, F32),
        scratch_shapes=m_scratch + [pltpu.VMEM((tm, D_MODEL), BF16)],
        compiler_params=_params("parallel", "parallel"),
        name=name,
    )(*m_args, x, g, _arr(wq), kv, kv, _arr(wo))


def _a_in_body(x_ref, g_ref, w_ref, cos_ref, sin_ref, *rest):
    o_refs, (xs_ref, xp_ref) = rest[:A_N_GROUPS], rest[A_N_GROUPS:]
    tm = x_ref.shape[0]
    xn_f32 = _rms(x_ref[...], g_ref[...])
    reps = D_MODEL // LANES
    for cb in range(reps):
        xs_ref[cb] = xn_f32[:, cb * LANES:(cb + 1) * LANES]
    lane = lax.broadcasted_iota(jnp.int32, (tm, LANES), 1)
    low_half = (lane & (A_HEAD_DIM // 2)) == 0
    for gidx, (o_ref, (_, dil)) in enumerate(zip(o_refs, A_GROUPS)):
        rows = tm // dil
        if dil == 1:
            xn = xn_f32.astype(BF16)
            cos, sin = cos_ref[...], sin_ref[...]
        else:
            for r in range(dil):
                for cb in range(reps):
                    xp_ref[r * rows:(r + 1) * rows, cb * LANES:(cb + 1) * LANES] = (
                        xs_ref[cb, pl.ds(r, rows, stride=dil), :].astype(BF16))
            xn = xp_ref[...]
            cos = jnp.concatenate([cos_ref[pl.ds(r, rows, stride=dil), :] for r in range(dil)], axis=0)
            sin = jnp.concatenate([sin_ref[pl.ds(r, rows, stride=dil), :] for r in range(dil)], axis=0)
        for part in range(3):
            lo = (gidx * 3 + part) * D_MODEL
            y = _dot(xn, w_ref[:, lo:lo + D_MODEL])
            if part == 2:
                o_ref[:, :, 2 * D_MODEL:] = y.reshape(dil, rows, D_MODEL).astype(BF16)
                continue
            c_t, s_t = (cos * A_Q_SCALE, sin * A_Q_SCALE) if part == 0 else (cos, sin)
            for cb in range(reps):
                yb = y[:, cb * LANES:(cb + 1) * LANES]
                partner = jnp.where(low_half, pltpu.roll(yb, LANES - A_HEAD_DIM // 2, 1),
                                    pltpu.roll(yb, A_HEAD_DIM // 2, 1))
                col = part * D_MODEL + cb * LANES
                o_ref[:, :, col:col + LANES] = (yb * c_t + partner * s_t).reshape(dil, rows, LANES).astype(BF16)


def _a_in(x, g, w, cos_t, sin_t):
    b, s, _ = x.shape
    tm = 512
    return pl.pallas_call(
        _a_in_body,
        grid=(b, s // tm),
        in_specs=[
            pl.BlockSpec((None, tm, D_MODEL), lambda bi, i: (bi, i, 0)),
            pl.BlockSpec((1, D_MODEL), lambda bi, i: (0, 0)),
            _resident(w),
            pl.BlockSpec((tm, LANES), lambda bi, i: (i, 0)),
            pl.BlockSpec((tm, LANES), lambda bi, i: (i, 0)),
        ],
        out_specs=[pl.BlockSpec((None, dil, tm // dil, 3 * D_MODEL), lambda bi, i: (bi, 0, i, 0))
                   for _, dil in A_GROUPS],
        out_shape=[jax.ShapeDtypeStruct((b, dil, s // dil, 3 * D_MODEL), BF16) for _, dil in A_GROUPS],
        scratch_shapes=[pltpu.VMEM((D_MODEL // LANES, tm, LANES), F32), pltpu.VMEM((tm, D_MODEL), BF16)],
        compiler_params=_params("parallel", "parallel"),
        name="a_in",
    )(x, g, _arr(w), cos_t, sin_t)


def _attn_body(q_ref, kp_ref, k_ref, kn_ref, vp_ref, v_ref, vn_ref, o_ref, lse_ref, *, n_tiles):
    n_res, tq = q_ref.shape[0], q_ref.shape[1]
    sq = A_SUBTILE
    win = sq + 2 * A_HALF
    n_sub_tiles = tq // sq
    i = pl.program_id(2)
    qi = lax.broadcasted_iota(jnp.int32, (sq, win), 0)
    kj = lax.broadcasted_iota(jnp.int32, (sq, win), 1)
    band = (kj >= qi) & (kj <= qi + 2 * A_HALF)
    left = lax.broadcasted_iota(jnp.int32, (sq, LANES), 1) < A_HEAD_DIM
    for res, st in [(a, c) for a in range(n_res) for c in range(n_sub_tiles)]:
        if st == 0:
            kw = jnp.concatenate([kp_ref[res], k_ref[res], kn_ref[res]], axis=0)
            vw = jnp.concatenate([vp_ref[res], v_ref[res], vn_ref[res]], axis=0)
        q_res, o_res, lse_res = q_ref.at[res], o_ref.at[res], lse_ref.at[res]
        rows = slice(st * sq, (st + 1) * sq)
        mask = band
        if st == 0:
            mask = mask & (kj >= jnp.where(i == 0, A_HALF, 0))
        if st == n_sub_tiles - 1:
            mask = mask & (kj < jnp.where(i == n_tiles - 1, sq + A_HALF, win))
        mask2 = jnp.concatenate([mask, mask], axis=0)
        lse_res[rows, :] = jnp.zeros((sq, LANES), F32)

        def scores(hp, q_res=q_res, rows=rows, kw=kw, st=st):
            sl = slice(hp * LANES, (hp + 1) * LANES)
            q2 = q_res[rows, sl]
            zero = jnp.zeros_like(q2)
            qq = jnp.concatenate([jnp.where(left, q2, zero), jnp.where(left, zero, q2)], axis=0)
            return _dot_nt(qq, kw[st * sq:st * sq + win, sl])

        s_next = scores(0)
        for hp in range(A_HEADS // 2):
            sl = slice(hp * LANES, (hp + 1) * LANES)
            s = jnp.where(mask2, s_next, NEG_INF)
            if hp + 1 < A_HEADS // 2:
                s_next = scores(hp + 1)
            m = jnp.max(s, axis=-1, keepdims=True)
            p = jnp.exp2(s - m)
            l = jnp.sum(p, axis=-1, keepdims=True)
            r = _dot(p.astype(BF16), vw[st * sq:st * sq + win, sl])
            o_res[rows, sl] = jnp.where(left, r[:sq], r[sq:]).astype(BF16)
            for off, col in ((0, m), (A_HEADS, l)):
                lse_res[rows, off + 2 * hp:off + 2 * hp + 1] = col[:sq]
                lse_res[rows, off + 2 * hp + 1:off + 2 * hp + 2] = col[sq:]


def _attn_group(qkv):
    b, dil, n_sub, _ = qkv.shape
    tq = min(A_QUERIES_PER_STEP, n_sub)
    n_res = min(dil, A_QUERIES_PER_STEP // tq)
    n_tiles = n_sub // tq
    halo_per_tile = tq // A_HALF
    n_halo = n_sub // A_HALF

    def own(part):
        return pl.BlockSpec((None, n_res, tq, D_MODEL), lambda bi, r, i: (bi, r, i, part))

    def prev(part):
        return pl.BlockSpec((None, n_res, A_HALF, D_MODEL),
                            lambda bi, r, i: (bi, r, jnp.maximum(i * halo_per_tile - 1, 0), part))

    def nxt(part):
        return pl.BlockSpec((None, n_res, A_HALF, D_MODEL),
                            lambda bi, r, i: (bi, r, jnp.minimum((i + 1) * halo_per_tile, n_halo - 1), part))

    return pl.pallas_call(
        functools.partial(_attn_body, n_tiles=n_tiles),
        grid=(b, dil // n_res, n_tiles),
        in_specs=[own(0), prev(1), own(1), nxt(1), prev(2), own(2), nxt(2)],
        out_specs=[
            pl.BlockSpec((None, n_res, tq, D_MODEL), lambda bi, r, i: (bi, r, i, 0)),
            pl.BlockSpec((None, n_res, tq, LANES), lambda bi, r, i: (bi, r, i, 0)),
        ],
        out_shape=[
            jax.ShapeDtypeStruct((b, dil, n_sub, D_MODEL), BF16),
            jax.ShapeDtypeStruct((b, dil, n_sub, LANES), F32),
        ],
        compiler_params=_params("parallel", "parallel", "parallel"),
        name=f"attn_d{dil}",
    )(qkv, qkv, qkv, qkv, qkv, qkv, qkv)


def _a_out_stage(o1_ref, o2_ref, o3_ref, l1_ref, l2_ref, l3_ref, w_ref, sp_ref, x_ref,
                 os2_ref, os3_ref, ls2_ref, ls3_ref, y_ref):
    tm = x_ref.shape[0]
    for o_ref, l_ref, os_ref, ls_ref in ((o2_ref, l2_ref, os2_ref, ls2_ref), (o3_ref, l3_ref, os3_ref, ls3_ref)):
        dil = o_ref.shape[0]
        rows = tm // dil
        for r in range(dil):
            for cb in range(D_MODEL // LANES):
                os_ref[cb, pl.ds(r, rows, stride=dil), :] = o_ref[r, :, cb * LANES:(cb + 1) * LANES].astype(F32)
            ls_ref[pl.ds(r, rows, stride=dil), :] = l_ref[r]
    s1, s2, s3 = l1_ref[0], ls2_ref[...], ls3_ref[...]
    m = jnp.maximum(jnp.maximum(s1, s2), s3)
    e1, e2, e3 = jnp.exp2(s1 - m), jnp.exp2(s2 - m), jnp.exp2(s3 - m)
    den = sum(e * pltpu.roll(st, LANES - A_HEADS, 1) for e, st in ((e1, s1), (e2, s2), (e3, s3)))
    inv = 1.0 / den
    valid = lax.broadcasted_iota(jnp.int32, (tm, LANES), 1) < A_HEADS
    packed = jnp.zeros((tm, LANES), F32)
    for gidx, e in enumerate((e1, e2, e3)):
        wgt = jnp.where(valid, e * inv, 0.0)
        hi = wgt.astype(BF16).astype(F32)
        for term, val in enumerate((hi, wgt - hi)):
            shift = 2 * A_HEADS * gidx + A_HEADS * term
            packed = packed + (pltpu.roll(val, shift, 1) if shift else val)
    wall = _dot(packed.astype(BF16), sp_ref[...])
    for cb in range(D_MODEL // LANES):
        sl = slice(cb * LANES, (cb + 1) * LANES)
        y = (wall[:, sl] * o1_ref[0, :, sl].astype(F32)
             + wall[:, D_MODEL + cb * LANES:D_MODEL + (cb + 1) * LANES] * os2_ref[cb]
             + wall[:, 2 * D_MODEL + cb * LANES:2 * D_MODEL + (cb + 1) * LANES] * os3_ref[cb])
        y_ref[:, sl] = y.astype(BF16)
    return x_ref[...] + _dot(y_ref[...], w_ref[...])


def _a_out_mixer(outs, lses, w):
    col = np.arange(A_N_GROUPS * D_MODEL)
    row = np.arange(LANES)
    spread = jnp.asarray((row[:, None] // (2 * A_HEADS) == col[None, :] // D_MODEL)
                         & (row[:, None] % A_HEADS == (col[None, :] % D_MODEL) // A_HEAD_DIM)
                         & (row[:, None] < 2 * A_HEADS * A_N_GROUPS), dtype=BF16)

    def specs(tm):
        def grouped(arr):
            dil, width = arr.shape[1], arr.shape[3]
            return pl.BlockSpec((None, dil, tm // dil, width), lambda bi, i: (bi, 0, i, 0))

        return [grouped(a) for a in outs] + [grouped(a) for a in lses] + [_resident(w), _resident(spread)]

    def scratch(tm):
        return [pltpu.VMEM((D_MODEL // LANES, tm, LANES), F32), pltpu.VMEM((D_MODEL // LANES, tm, LANES), F32),
                pltpu.VMEM((tm, LANES), F32), pltpu.VMEM((tm, LANES), F32), pltpu.VMEM((tm, D_MODEL), BF16)]

    return (_a_out_stage, [*outs, *lses, _arr(w), spread], specs, scratch, 512, "a_out_xattn")


def _rope_tables(seq_len):
    inv = ROPE_THETA ** (-jnp.arange(0, A_HEAD_DIM, 2, dtype=F32) / A_HEAD_DIM)
    ang = jnp.arange(seq_len, dtype=F32)[:, None] * inv[None, :]
    cos, sin = jnp.cos(ang), jnp.sin(ang)
    reps = LANES // A_HEAD_DIM
    return (jnp.concatenate([cos, cos] * reps, axis=1),
            jnp.concatenate([-sin, sin] * reps, axis=1))


def _mixer_a(x, g, w_in, w_out):
    b, s, _ = x.shape
    cos_t, sin_t = _rope_tables(s)
    outs, lses = [], []
    for qkv in _a_in(x, g, w_in, cos_t, sin_t):
        o, l = _attn_group(qkv)
        outs.append(o)
        lses.append(l)
    return _a_out_mixer(outs, lses, w_out)


HALO = 8
HY_COL_CHUNK = 256


def _hy_in_body(xp_ref, x_ref, xn_ref, g_ref, w_ref, cw_ref, cb_ref, x0_ref, vv_ref, u_ref, *, n_tiles):
    tm = x_ref.shape[0]
    i = pl.program_id(1)
    xe = jnp.concatenate([xp_ref[...], x_ref[...], xn_ref[...]], axis=0)
    xn = _rms(xe, g_ref[...]).astype(BF16)
    row = lax.broadcasted_iota(jnp.int32, (tm + 2 * HALO, 1), 0)
    inside = ((row >= HALO) | (i > 0)) & ((row < tm + HALO) | (i < n_tiles - 1))
    cw = HY_COL_CHUNK
    for c in range(D_MODEL // cw):
        parts = []
        for k in range(3):
            cols = slice(k * D_MODEL + c * cw, k * D_MODEL + (c + 1) * cw)
            u_ref[k] = jnp.where(inside, _dot(xn, w_ref[:, cols]), 0.0)
            parts.append(u_ref[k, pl.ds(HALO - 1, tm), :] * cw_ref[0:1, cols]
                         + u_ref[k, pl.ds(HALO, tm), :] * cw_ref[1:2, cols]
                         + u_ref[k, pl.ds(HALO + 1, tm), :] * cw_ref[2:3, cols]
                         + cb_ref[:, cols])
        x0_ref[:, c * cw:(c + 1) * cw] = parts[0].astype(BF16)
        vv_ref[:, c * cw:(c + 1) * cw] = (parts[2] * parts[1]).astype(BF16)


def _hy_in(x, g, w, conv_w, conv_b):
    b, s, _ = x.shape
    tm = 512
    n_tiles = s // tm
    per = tm // HALO
    n_halo = s // HALO
    tok = pl.BlockSpec((None, tm, D_MODEL), lambda bi, i: (bi, i, 0))
    return pl.pallas_call(
        functools.partial(_hy_in_body, n_tiles=n_tiles),
        grid=(b, n_tiles),
        in_specs=[
            pl.BlockSpec((None, HALO, D_MODEL), lambda bi, i: (bi, jnp.maximum(i * per - 1, 0), 0)),
            tok,
            pl.BlockSpec((None, HALO, D_MODEL), lambda bi, i: (bi, jnp.minimum((i + 1) * per, n_halo - 1), 0)),
            pl.BlockSpec((1, D_MODEL), lambda bi, i: (0, 0)),
            _resident(w),
            pl.BlockSpec(conv_w.shape, lambda bi, i: (0, 0)),
            pl.BlockSpec(conv_b.shape, lambda bi, i: (0, 0)),
        ],
        out_specs=[tok, tok],
        out_shape=[jax.ShapeDtypeStruct(x.shape, BF16), jax.ShapeDtypeStruct(x.shape, BF16)],
        scratch_shapes=[pltpu.VMEM((3, tm + 2 * HALO, HY_COL_CHUNK), F32)],
        compiler_params=_params("parallel", "parallel"),
        name="hy_in",
    )(x, x, x, g, _arr(w), conv_w, conv_b)


def _hdot(a, b):
    return jnp.dot(a, b, precision=lax.Precision.HIGHEST, preferred_element_type=F32)


def _hy_filter_body(z_ref, t_ref, a_ref, b_ref, w1_ref, b1_ref, w2_ref, b2_ref, w3_ref, b3_ref,
                    wo_ref, fr_ref, dl_ref, h_ref, sum_ref):
    fr = fr_ref[...]
    hid = jnp.sin(fr * (_hdot(z_ref[...], w1_ref[...]) + b1_ref[...]))
    hid = jnp.sin(fr * (_hdot(hid, w2_ref[...]) + b2_ref[...]))
    hid = jnp.sin(fr * (_hdot(hid, w3_ref[...]) + b3_ref[...]))
    decay = jnp.exp(-t_ref[...] * dl_ref[...])
    hid = hid.astype(BF16)
    h_fwd = _dot(hid, wo_ref[:, :D_MODEL]) * decay
    h_bwd = _dot(hid, wo_ref[:, D_MODEL:]) * decay
    h = a_ref[...] * h_fwd + b_ref[...] * h_bwd
    h_ref[...] = h

    @pl.when(pl.program_id(0) == 0)
    def _():
        sum_ref[...] = jnp.zeros_like(sum_ref)

    sum_ref[...] += jnp.sum(jnp.abs(h), axis=0, keepdims=True)


def _hy_filter(seq_len, f_w1, f_b1, f_w2, f_b2, f_w3, f_b3, f_wout, f_freq):
    n = 2 * seq_len
    src = np.concatenate([np.arange(seq_len), [0], np.arange(seq_len - 1, 0, -1)])
    lag = jnp.asarray(src, dtype=F32)[:, None]
    t2 = lag / (seq_len - 1)
    w = 2.0 * math.pi * lag / seq_len
    f = jnp.linspace(1e-4, HY_BANDS - 1, HY_BANDS, dtype=F32)[None, :]
    z2 = jnp.concatenate([t2, jnp.cos(f * w), -jnp.sin(f * w),
                          jnp.zeros((n, HY_HID_PAD - HY_EMB), F32)], axis=-1)
    pos = np.arange(n)
    use_fwd = (pos < seq_len).astype(np.float32)[:, None]
    use_bwd = ((pos == 0) | (pos > seq_len)).astype(np.float32)[:, None]
    max_decay = math.log(HY_DECAY_TARGET) / HY_DECAY_STRONG_PCT
    min_decay = math.log(HY_DECAY_TARGET) / HY_DECAY_WEAK_PCT
    deltas = jnp.abs(jnp.linspace(min_decay, max_decay, D_MODEL, dtype=F32))[None, :]

    def pad2(m, rows):
        return jnp.pad(m, ((0, rows - m.shape[0]), (0, HY_HID_PAD - m.shape[1])))

    def padv(v):
        return jnp.pad(v, (0, HY_HID_PAD - v.shape[0]))[None, :]

    wo = jnp.pad(f_wout, ((0, HY_HID_PAD - f_wout.shape[0]), (0, 0))).astype(BF16)
    tr = 512
    rowblk = lambda width: pl.BlockSpec((tr, width), lambda i: (i, 0))
    full = lambda shape: pl.BlockSpec(shape, lambda i: (0, 0))
    sq = (HY_HID_PAD, HY_HID_PAD)
    vec = (1, HY_HID_PAD)
    return pl.pallas_call(
        _hy_filter_body,
        grid=(n // tr,),
        in_specs=[rowblk(HY_HID_PAD), rowblk(1), rowblk(1), rowblk(1),
                  full(sq), full(vec), full(sq), full(vec), full(sq), full(vec),
                  full(wo.shape), full(vec), full((1, D_MODEL))],
        out_specs=[rowblk(D_MODEL), full((1, D_MODEL))],
        out_shape=[jax.ShapeDtypeStruct((n, D_MODEL), F32), jax.ShapeDtypeStruct((1, D_MODEL), F32)],
        compiler_params=_params("arbitrary"),
        name="hy_filter",
    )(z2, t2, jnp.asarray(use_fwd), jnp.asarray(use_bwd),
      pad2(f_w1, HY_HID_PAD), padv(f_b1), pad2(f_w2, HY_HID_PAD), padv(f_b2),
      pad2(f_w3, HY_HID_PAD), padv(f_b3), wo, padv(f_freq), deltas)


def _fft_split(n):
    n1 = 1 << ((n.bit_length() - 1 + 1) // 2)
    return n1, n // n1


def _fft_tables(n1, n2):
    n = n1 * n2
    h = n1 // 2
    idx = np.arange(n1)
    ang = -2.0 * np.pi * ((idx[:, None] * idx[None, :]) % n1) / n1
    fr, fi = np.cos(ang), np.sin(ang)
    m1_data = np.block([[fr[:, :h], -fi[:, :h]], [fi[:, :h], fr[:, :h]]])
    m1_filt = np.concatenate([fr, fi], axis=0)
    ifr, ifi = fr.T[:h] / n, -fi.T[:h] / n
    m3 = np.block([[ifr, -ifi], [ifi, ifr]])
    k1 = jnp.arange(n1, dtype=jnp.int32)[:, None, None]
    k2 = jnp.arange(n2, dtype=jnp.int32)[None, :, None]
    i2 = jnp.arange(n2, dtype=jnp.int32)[None, None, :]
    phase = (i2 * k1 + n1 * i2 * k2) % n
    ga = (-2.0 * math.pi / n) * phase.astype(F32)
    gr, gi = jnp.cos(ga), jnp.sin(ga)
    g_fwd = jnp.concatenate([jnp.concatenate([gr, -gi], axis=2), jnp.concatenate([gi, gr], axis=2)], axis=1)
    grt, git = jnp.swapaxes(gr, 1, 2), jnp.swapaxes(gi, 1, 2)
    g_inv = jnp.concatenate([jnp.concatenate([grt, git], axis=2), jnp.concatenate([-git, grt], axis=2)], axis=1)
    as_bf = lambda m: jnp.asarray(m, dtype=F32).astype(BF16)
    return as_bf(m1_data), as_bf(m1_filt), as_bf(m3), g_fwd.astype(BF16), g_inv.astype(BF16)


FFT_SUB = 16
FFT_TC = 512


def _dft_outer_body(m_ref, pm_ref, z_ref, o_ref, zs_ref, rb_ref):
    q, sub, tc = z_ref.shape
    r = m_ref.shape[0]
    nlb = tc // LANES
    for cb in range(nlb):
        zs_ref[cb] = z_ref[:, :, cb * LANES:(cb + 1) * LANES].astype(F32).reshape(q * sub, LANES)
    hw = tc // 2
    n_groups = r // sub

    def transform(hf, j):
        z = jnp.concatenate([zs_ref[cb, pl.ds(j, q, stride=sub), :]
                             for cb in range(hf * nlb // 2, (hf + 1) * nlb // 2)], axis=1)
        rb_ref[hf, j] = _dot(m_ref[...], z.astype(BF16)).astype(BF16)

    def regroup(hf, g):
        part, i0 = divmod(g * sub, r // 2)
        grouped = rb_ref[hf, :, g * sub:(g + 1) * sub, :].reshape(sub * sub, hw)
        o_ref[part, i0:i0 + sub, :, hf * hw:(hf + 1) * hw] = (
            _dot(pm_ref[...], grouped).reshape(sub, sub, hw).astype(BF16))

    for j in range(sub):
        transform(0, j)
    for step in range(max(sub, n_groups)):
        if step < sub:
            transform(1, step)
        if step < n_groups:
            regroup(0, step)
    for g in range(n_groups):
        regroup(1, g)


def _dft_outer(mat, perm, z5, name):
    p, q, nh, sub, c = z5.shape
    r = mat.shape[0]
    tc = FFT_TC
    return pl.pallas_call(
        _dft_outer_body,
        grid=(p, nh, c // tc),
        in_specs=[pl.BlockSpec(mat.shape, lambda pi, h, ci: (0, 0)),
                  pl.BlockSpec(perm.shape, lambda pi, h, ci: (0, 0)),
                  pl.BlockSpec((None, q, None, sub, tc), lambda pi, h, ci: (pi, 0, h, 0, ci))],
        out_specs=pl.BlockSpec((None, None, 2, r // 2, sub, tc), lambda pi, h, ci: (pi, h, 0, 0, 0, ci)),
        out_shape=jax.ShapeDtypeStruct((p, nh, 2, r // 2, sub, c), BF16),
        scratch_shapes=[pltpu.VMEM((tc // LANES, q * sub, LANES), F32),
                        pltpu.VMEM((2, sub, r, tc // 2), BF16)],
        compiler_params=_params("parallel", "parallel", "parallel"),
        name=name,
    )(mat, perm, z5)


def _idft_outer_body(m_ref, pm_ref, b_ref, o_ref, tb_ref, ys_ref):
    _, n1, sub, tc = b_ref.shape
    q = m_ref.shape[0]
    nlb = tc // LANES
    hw = tc // 2
    n_groups = 2 * n1 // sub

    def regroup(hf, g):
        part, i0 = divmod(g * sub, n1)
        grouped = b_ref[part, i0:i0 + sub, :, hf * hw:(hf + 1) * hw].reshape(sub * sub, hw)
        tb_ref[hf, :, g * sub:(g + 1) * sub, :] = _dot(pm_ref[...], grouped).reshape(sub, sub, hw).astype(BF16)

    def transform(hf, j):
        res = _dot(m_ref[...], tb_ref[hf, j])
        for c in range(nlb // 2):
            ys_ref[hf * nlb // 2 + c, pl.ds(j, q, stride=sub), :] = res[:, c * LANES:(c + 1) * LANES]

    for g in range(n_groups):
        regroup(0, g)
    for step in range(max(sub, n_groups)):
        if step < n_groups:
            regroup(1, step)
        if step < sub:
            transform(0, step)
    for j in range(sub):
        transform(1, j)
    for cb in range(nlb):
        o_ref[:, :, cb * LANES:(cb + 1) * LANES] = ys_ref[cb].reshape(q, sub, LANES)


def _idft_outer(mat, perm, b6):
    p, nh, _, n1, sub, c = b6.shape
    q = mat.shape[0]
    tc = FFT_TC
    return pl.pallas_call(
        _idft_outer_body,
        grid=(p, nh, c // tc),
        in_specs=[pl.BlockSpec(mat.shape, lambda pi, h, ci: (0, 0)),
                  pl.BlockSpec(perm.shape, lambda pi, h, ci: (0, 0)),
                  pl.BlockSpec((None, None, 2, n1, sub, tc), lambda pi, h, ci: (pi, h, 0, 0, 0, ci))],
        out_specs=pl.BlockSpec((None, q, None, sub, tc), lambda pi, h, ci: (pi, 0, h, 0, ci)),
        out_shape=jax.ShapeDtypeStruct((p, q, nh, sub, c), F32),
        scratch_shapes=[pltpu.VMEM((2, sub, 2 * n1, tc // 2), BF16),
                        pltpu.VMEM((tc // LANES, q * sub, LANES), F32)],
        compiler_params=_params("parallel", "parallel", "parallel"),
        name="hy_idft",
    )(mat, perm, b6)


FFT_KB = 4


def _stack_re_im(a_ref, kk):
    nh, _, _, sub, ct = a_ref.shape
    return jnp.concatenate([a_ref[:, 0, kk].reshape(nh * sub, ct), a_ref[:, 1, kk].reshape(nh * sub, ct)], axis=0)


def _spec_filter_body(g_ref, a_ref, sc_ref, h_ref):
    n2 = g_ref.shape[1] // 2
    for kk in range(g_ref.shape[0]):
        spec = _dot(g_ref[kk], _stack_re_im(a_ref, kk)) * sc_ref[...]
        h_ref[kk] = spec.reshape(2, n2, spec.shape[1])


def _spec_filter(g_fwd, a6, scale):
    _, nh, _, n1, sub, c = a6.shape
    n2 = nh * sub
    return pl.pallas_call(
        _spec_filter_body,
        grid=(n1 // FFT_KB,),
        in_specs=[pl.BlockSpec((FFT_KB, 2 * n2, 2 * n2), lambda k: (k, 0, 0)),
                  pl.BlockSpec((None, nh, 2, FFT_KB, sub, c), lambda k: (0, 0, 0, k, 0, 0)),
                  pl.BlockSpec((1, c), lambda k: (0, 0))],
        out_specs=pl.BlockSpec((FFT_KB, 2, n2, c), lambda k: (k, 0, 0, 0)),
        out_shape=jax.ShapeDtypeStruct((n1, 2, n2, c), F32),
        compiler_params=_params("parallel"),
        name="hy_spec_filter",
    )(g_fwd, a6, scale)


def _spec_body(gf_ref, gi_ref, h_ref, a_ref, o_ref):
    nh, _, kb, sub, ct = a_ref.shape
    n2 = nh * sub
    def forward(kk):
        return _dot(gf_ref[kk], _stack_re_im(a_ref, kk))

    spec_next = forward(0)
    for kk in range(kb):
        spec = spec_next
        if kk + 1 < kb:
            spec_next = forward(kk + 1)
        xr, xi = spec[:n2], spec[n2:]
        hr, hi = h_ref[kk, 0], h_ref[kk, 1]
        y = jnp.concatenate([xr * hr - xi * hi, xr * hi + xi * hr], axis=0).astype(BF16)
        back = _dot(gi_ref[kk], y)
        o_ref[:, 0, kk] = back[:n2].reshape(nh, sub, ct).astype(BF16)
        o_ref[:, 1, kk] = back[n2:].reshape(nh, sub, ct).astype(BF16)


def _spec(g_fwd, g_inv, hspec, a6):
    p, nh, _, n1, sub, c = a6.shape
    n2 = nh * sub
    blk = pl.BlockSpec((None, nh, 2, FFT_KB, sub, c), lambda k, pi: (pi, 0, 0, k, 0, 0))
    mat = pl.BlockSpec((FFT_KB, 2 * n2, 2 * n2), lambda k, pi: (k, 0, 0))
    return pl.pallas_call(
        _spec_body,
        grid=(n1 // FFT_KB, p),
        in_specs=[mat, mat, pl.BlockSpec((FFT_KB, 2, n2, c), lambda k, pi: (k, 0, 0, 0)), blk],
        out_specs=blk,
        out_shape=jax.ShapeDtypeStruct(a6.shape, BF16),
        compiler_params=_params("parallel", "parallel"),
        name="hy_spec",
    )(g_fwd, g_inv, hspec, a6)


def _long_conv(vv, h_raw, h_norm):
    b, l, c = vv.shape
    n1, n2 = _fft_split(2 * l)
    nh = n2 // FFT_SUB
    m1_data, m1_filt, m3, g_fwd, g_inv = _fft_tables(n1, n2)
    idx = np.arange(FFT_SUB * FFT_SUB)
    swapped = (idx % FFT_SUB) * FFT_SUB + idx // FFT_SUB
    perm = jnp.asarray(idx[None, :] == swapped[:, None], dtype=BF16)
    a_h = _dft_outer(m1_filt, perm, h_raw.reshape(1, n1, nh, FFT_SUB, c), "hy_dft1_filter")
    hspec = _spec_filter(g_fwd, a_h, 1.0 / h_norm)
    a = _dft_outer(m1_data, perm, vv.reshape(b // 2, n1, nh, FFT_SUB, c), "hy_dft1")
    y = _idft_outer(m3, perm, _spec(g_fwd, g_inv, hspec, a))
    return y.reshape(b, l, c)


def _hy_out_stage(cv_ref, vv_ref, x0_ref, bd_ref, w_ref, x_ref):
    y = ((cv_ref[...] + bd_ref[...] * vv_ref[...].astype(F32)) * x0_ref[...].astype(F32)).astype(BF16)
    return x_ref[...] + _dot(y, w_ref[...])


def _mixer_b(x, g, w_in, conv_w, conv_b, filt, bias_d, w_out):
    _, s, _ = x.shape
    x0, vv = _hy_in(x, g, w_in, conv_w, conv_b)
    h_raw, h_norm = _hy_filter(s, *filt)
    conv = _long_conv(vv, h_raw, h_norm)

    def specs(tm):
        tok = pl.BlockSpec((None, tm, D_MODEL), lambda bi, i: (bi, i, 0))
        return [tok, tok, tok, pl.BlockSpec((1, D_MODEL), lambda bi, i: (0, 0)), _resident(w_out)]

    return (_hy_out_stage, [conv, vv, x0, bias_d, _arr(w_out)], specs, lambda tm: [], 512, "hy_out_xattn")


def _gelu(z):
    return 0.5 * z * (1.0 + lax.erf(z * (2.0 ** -0.5)))


def _sgu_body(x_ref, g_ref, win_ref, lng_ref, lnb_ref, ws_ref, bs_ref, wout_ref, o_ref, y_ref, z_ref):
    tm = x_ref.shape[0]
    x = x_ref[...]
    xn = _rms(x, g_ref[...]).astype(BF16)
    for c in range(2 * D_MODEL // FF_CHUNK):
        cols = slice(c * FF_CHUNK, (c + 1) * FF_CHUNK)
        z_ref[:, cols] = _gelu(_dot(xn, win_ref[:, cols]))
    zv = z_ref[:, D_MODEL:]
    zc = zv - jnp.mean(zv, axis=-1, keepdims=True)
    zv = zc * lax.rsqrt(jnp.mean(zc * zc, axis=-1, keepdims=True) + EPS) * lng_ref[...] + lnb_ref[...]
    zvb = zv.astype(BF16)
    n_chunks = tm // C_CHUNK
    for h in range(C_GROUPS):
        cols = slice(h * LANES, (h + 1) * LANES)
        v_all = jnp.concatenate([zvb[c * C_CHUNK:(c + 1) * C_CHUNK, cols] for c in range(n_chunks)], axis=1)
        sv = _dot(ws_ref[h], v_all) + bs_ref[:, h:h + 1]
        for c in range(n_chunks):
            rows = slice(c * C_CHUNK, (c + 1) * C_CHUNK)
            y_ref[rows, cols] = (z_ref[rows, cols] * sv[:, c * LANES:(c + 1) * LANES]).astype(BF16)
    o_ref[...] = x + _dot(y_ref[...], wout_ref[...])


def _mixer_c(x, g, w_in, ln_g, ln_b, w_s, b_s_t, w_out):
    b, s, _ = x.shape
    t = b * s
    tm = 512
    tok = pl.BlockSpec((tm, D_MODEL), lambda i: (i, 0))
    row = pl.BlockSpec((1, D_MODEL), lambda i: (0, 0))
    return pl.pallas_call(
        _sgu_body,
        grid=(t // tm,),
        in_specs=[tok, row, _resident(w_in), row, row, _resident(w_s),
                  pl.BlockSpec(b_s_t.shape, lambda i: (0, 0)), _resident(w_out)],
        out_specs=tok,
        out_shape=jax.ShapeDtypeStruct((t, D_MODEL), F32),
        scratch_shapes=[pltpu.VMEM((tm, D_MODEL), BF16), pltpu.VMEM((tm, 2 * D_MODEL), F32)],
        compiler_params=_params("parallel"),
        name="sgu",
    )(x.reshape(t, D_MODEL), g, _arr(w_in), ln_g, ln_b, _arr(w_s), b_s_t, _arr(w_out)).reshape(x.shape)


def _trunk(x, kvs, boff, w):
    b, s, _ = x.shape
    t = b * s
    for i in range(DEPTH):
        kind, j = i % N_MIXERS, i // N_MIXERS
        g = w["g_mix"][i][None, :]
        mixer_out = None
        if kind == 0:
            mixer_out = _mixer_a(x, g, w["a_w_in"][j], w["a_w_out"][j])
        elif kind == 1:
            filt = tuple(w[k][j] for k in ("b_f_w1", "b_f_b1", "b_f_w2", "b_f_b2", "b_f_w3", "b_f_b3",
                                          "b_f_wout", "b_f_freq"))
            mixer_out = _mixer_b(x, g, w["b_w_in"][j], w["b_conv_w"][j], w["b_conv_b"][j][None, :], filt,
                                 w["b_bias_d"][j][None, :], w["b_w_out"][j])
        else:
            x = _mixer_c(x, g, w["c_w_in"][j], w["c_ln_g"][j][None, :], w["c_ln_b"][j][None, :],
                         w["c_w_s"][j], w["c_b_s"][j].T, w["c_w_out"][j])
        x = _xattn(x, w["g_cross"][i][None, :], w["x_w_q"][i], kvs[i], boff, w["x_w_o"][i], mixer_out)
        g_final = w["g_final"][None, :] if i == DEPTH - 1 else None
        x = _ffn(x.reshape(t, D_MODEL), w["g_ffn"][i][None, :], w["f_w_gu"][i], w["f_w_down"][i],
                 g_final).reshape(b, s, D_MODEL)
    return x


_BF16_WEIGHTS = ("a_w_in", "a_w_out", "b_w_in", "b_w_out", "c_w_in", "c_w_s", "c_w_out",
                 "x_w_q", "x_w_kv", "x_w_o", "f_w_gu", "f_w_down")


def kernel(x_prompt, x_sample, mem_prompt, mem_sample, g_mix, g_cross, g_ffn, g_final, a_w_in, a_w_out, b_w_in, b_conv_w, b_conv_b, b_f_w1, b_f_b1, b_f_w2, b_f_b2, b_f_w3, b_f_b3, b_f_wout, b_f_freq, b_bias_d, b_w_out, c_w_in, c_ln_g, c_ln_b, c_w_s, c_b_s, c_w_out, x_w_q, x_w_kv, x_w_o, f_w_gu, f_w_down):
    w = dict(g_mix=g_mix, g_cross=g_cross, g_ffn=g_ffn, g_final=g_final,
             a_w_in=a_w_in, a_w_out=a_w_out, b_w_in=b_w_in, b_conv_w=b_conv_w, b_conv_b=b_conv_b,
             b_f_w1=b_f_w1, b_f_b1=b_f_b1, b_f_w2=b_f_w2, b_f_b2=b_f_b2, b_f_w3=b_f_w3,
             b_f_b3=b_f_b3, b_f_wout=b_f_wout, b_f_freq=b_f_freq, b_bias_d=b_bias_d,
             b_w_out=b_w_out, c_w_in=c_w_in, c_ln_g=c_ln_g, c_ln_b=c_ln_b, c_w_s=c_w_s,
             c_b_s=c_b_s, c_w_out=c_w_out, x_w_q=x_w_q, x_w_kv=x_w_kv, x_w_o=x_w_o,
             f_w_gu=f_w_gu, f_w_down=f_w_down)
    for name in _BF16_WEIGHTS:
        stack = w[name].astype(BF16)
        w[name] = [(stack, j) for j in range(stack.shape[0])]
    nb_prompt = mem_prompt.shape[0]
    mem = jnp.concatenate([mem_prompt, mem_sample], axis=0)
    mem2 = mem.reshape(mem.shape[0] * MEM_LEN, D_MODEL)
    kvs = [_kv_proj(mem2, w["x_w_kv"][i]).reshape(mem.shape[0], MEM_LEN, 2 * D_MODEL) for i in range(DEPTH)]
    y_prompt = _trunk(x_prompt, kvs, 0, w)
    y_sample = _trunk(x_sample, kvs, nb_prompt, w)
    return (y_prompt, y_sample)
```

```python
import functools
import math

import numpy as np
import jax
import jax.numpy as jnp
from jax import lax
from jax.experimental import pallas as pl
from jax.experimental.pallas import tpu as pltpu

F32 = jnp.float32
BF16 = jnp.bfloat16

D_MODEL = 1024
DEPTH = 4
N_MIXERS = 3

A_GROUPS = ((128, 1), (512, 4), (2048, 16))
A_N_GROUPS = len(A_GROUPS)
A_HEADS = 16
A_HEAD_DIM = D_MODEL // A_HEADS
A_HALF = 64
A_SUBTILE = 128
A_QUERIES_PER_STEP = 1024
A_Q_SCALE = A_HEAD_DIM ** -0.5 * math.log2(math.e)
ROPE_THETA = 10000.0

HY_EMB = 33
HY_BANDS = (HY_EMB - 1) // 2
HY_HID_PAD = 128
HY_DECAY_TARGET = 1e-2
HY_DECAY_STRONG_PCT = 0.3
HY_DECAY_WEAK_PCT = 1.5

C_CHUNK = 128
C_GROUPS = 8

MEM_LEN = 256
X_HEADS = 4
X_HEAD_DIM = D_MODEL // X_HEADS

D_FF = -(-8 * D_MODEL // (3 * 256)) * 256
FF_CHUNK = 256

EPS = 1e-6
NEG_INF = -1e30

LANES = 128
VMEM_LIMIT = 56 * 1024 * 1024

TM_FFN = 1024
TM_XATTN = 1024
TM_MIXER_XATTN = 512
TM_A_IN = 512
TM_HY_IN = 512
TM_SGU = 512
TM_KV = 256
TR_FILTER = 512


def _params(*sem):
    return pltpu.CompilerParams(dimension_semantics=sem, vmem_limit_bytes=VMEM_LIMIT)


def _resident(w):
    if isinstance(w, tuple):
        stack, layer = w
        nd = stack.ndim - 1
        return pl.BlockSpec((None,) + stack.shape[1:], lambda *_: (layer,) + (0,) * nd,
                            pipeline_mode=pl.Buffered(1))
    nd = w.ndim
    return pl.BlockSpec(w.shape, lambda *_: (0,) * nd, pipeline_mode=pl.Buffered(1))


def _arr(w):
    return w[0] if isinstance(w, tuple) else w


def _dot(a, b):
    return jnp.dot(a, b, preferred_element_type=F32)


def _dot_nt(a, b):
    return lax.dot_general(a, b, (((1,), (1,)), ((), ())), preferred_element_type=F32)


def _rms(x, g):
    return x * lax.rsqrt(jnp.mean(x * x, axis=-1, keepdims=True) + EPS) * g


def _ffn_body(*refs, final):
    if final:
        x_ref, g_ref, wgu_ref, wd_ref, gf_ref, o_ref, act_ref = refs
    else:
        x_ref, g_ref, wgu_ref, wd_ref, o_ref, act_ref = refs
    x = x_ref[...]
    xn = _rms(x, g_ref[...]).astype(BF16)
    for c in range(D_FF // FF_CHUNK):
        lo = c * FF_CHUNK
        gate = _dot(xn, wgu_ref[:, lo:lo + FF_CHUNK])
        up = _dot(xn, wgu_ref[:, D_FF + lo:D_FF + lo + FF_CHUNK])
        act_ref[:, lo:lo + FF_CHUNK] = (gate / (1.0 + jnp.exp(-gate)) * up).astype(BF16)
    y = x + _dot(act_ref[...], wd_ref[...])
    if final:
        y = _rms(y, gf_ref[...])
    o_ref[...] = y


def _ffn(x2, g, wgu, wd, g_final=None):
    t = x2.shape[0]
    tm = TM_FFN
    final = g_final is not None
    tok = pl.BlockSpec((tm, D_MODEL), lambda i: (i, 0))
    row = pl.BlockSpec((1, D_MODEL), lambda i: (0, 0))
    in_specs = [tok, row, _resident(wgu), _resident(wd)]
    args = [x2, g, _arr(wgu), _arr(wd)]
    if final:
        in_specs.append(row)
        args.append(g_final)
    return pl.pallas_call(
        functools.partial(_ffn_body, final=final),
        grid=(t // tm,),
        in_specs=in_specs,
        out_specs=tok,
        out_shape=jax.ShapeDtypeStruct((t, D_MODEL), F32),
        scratch_shapes=[pltpu.VMEM((tm, D_FF), BF16)],
        compiler_params=_params("parallel"),
        name="ffn_final" if final else "ffn",
    )(*args)


def _kv_body(m_ref, w_ref, o_ref):
    o_ref[...] = _dot(m_ref[...].astype(BF16), w_ref[...]).astype(BF16)


def _kv_proj(mem2, wkv):
    r = mem2.shape[0]
    tm = TM_KV
    return pl.pallas_call(
        _kv_body,
        grid=(r // tm,),
        in_specs=[pl.BlockSpec((tm, D_MODEL), lambda i: (i, 0)), _resident(wkv)],
        out_specs=pl.BlockSpec((tm, 2 * D_MODEL), lambda i: (i, 0)),
        out_shape=jax.ShapeDtypeStruct((r, 2 * D_MODEL), BF16),
        compiler_params=_params("parallel"),
        name="kv_proj",
    )(mem2, _arr(wkv))


def _xattn_stage(x, g_ref, wq_ref, k_ref, v_ref, wo_ref, y_ref):
    xn = _rms(x, g_ref[...]).astype(BF16)
    q = (_dot(xn, wq_ref[...]) * (X_HEAD_DIM ** -0.5)).astype(BF16)
    for h in range(X_HEADS):
        sl = slice(h * X_HEAD_DIM, (h + 1) * X_HEAD_DIM)
        s = _dot_nt(q[:, sl], k_ref[:, sl])
        p = jnp.exp(s - jnp.max(s, axis=-1, keepdims=True))
        inv = 1.0 / jnp.sum(p, axis=-1, keepdims=True)
        y_ref[:, sl] = (_dot(p.astype(BF16), v_ref[:, sl]) * inv).astype(BF16)
    return x + _dot(y_ref[...], wo_ref[...])


def _xattn_body(*refs, mixer_out, n_mixer_in, n_mixer_scratch):
    n_in = n_mixer_in + 6
    mixer_in, (x_ref, g_ref, wq_ref, k_ref, v_ref, wo_ref) = refs[:n_mixer_in], refs[n_mixer_in:n_in]
    o_ref = refs[n_in]
    mixer_scratch, y_ref = refs[n_in + 1:n_in + 1 + n_mixer_scratch], refs[n_in + 1 + n_mixer_scratch]
    x = x_ref[...] if mixer_out is None else mixer_out(*mixer_in, x_ref, *mixer_scratch)
    o_ref[...] = _xattn_stage(x, g_ref, wq_ref, k_ref, v_ref, wo_ref, y_ref)


def _xattn(x, g, wq, kv, boff, wo, mixer=None):
    b, s, _ = x.shape
    if mixer is None:
        stage, m_args, m_specs, m_scratch, tm, name = None, [], [], [], TM_XATTN, "xattn"
    else:
        stage, m_args, specs_fn, scratch_fn, name = mixer
        tm = TM_MIXER_XATTN
        m_specs, m_scratch = specs_fn(tm), scratch_fn(tm)
    tok = pl.BlockSpec((None, tm, D_MODEL), lambda bi, i: (bi, i, 0))
    return pl.pallas_call(
        functools.partial(_xattn_body, mixer_out=stage, n_mixer_in=len(m_args), n_mixer_scratch=len(m_scratch)),
        grid=(b, s // tm),
        in_specs=m_specs + [
            tok,
            pl.BlockSpec((1, D_MODEL), lambda bi, i: (0, 0)),
            _resident(wq),
            pl.BlockSpec((None, MEM_LEN, D_MODEL), lambda bi, i: (bi + boff, 0, 0)),
            pl.BlockSpec((None, MEM_LEN, D_MODEL), lambda bi, i: (bi + boff, 0, 1)),
            _resident(wo),
        ],
        out_specs=tok,
        out_shape=jax.ShapeDtypeStruct(x.shape, F32),
        scratch_shapes=m_scratch + [pltpu.VMEM((tm, D_MODEL), BF16)],
        compiler_params=_params("parallel", "parallel"),
        name=name,
    )(*m_args, x, g, _arr(wq), kv, kv, _arr(wo))


def _a_in_body(x_ref, g_ref, w_ref, cos_ref, sin_ref, *rest):
    o_refs, (xs_ref, xp_ref) = rest[:A_N_GROUPS], rest[A_N_GROUPS:]
    tm = x_ref.shape[0]
    xn_f32 = _rms(x_ref[...], g_ref[...])
    reps = D_MODEL // LANES
    for cb in range(reps):
        xs_ref[cb] = xn_f32[:, cb * LANES:(cb + 1) * LANES]
    lane = lax.broadcasted_iota(jnp.int32, (tm, LANES), 1)
    low_half = (lane & (A_HEAD_DIM // 2)) == 0
    for gidx, (o_ref, (_, dil)) in enumerate(zip(o_refs, A_GROUPS)):
        rows = tm // dil
        if dil == 1:
            xn = xn_f32.astype(BF16)
            cos, sin = cos_ref[...], sin_ref[...]
        else:
            for r in range(dil):
                for cb in range(reps):
                    xp_ref[r * rows:(r + 1) * rows, cb * LANES:(cb + 1) * LANES] = (
                        xs_ref[cb, pl.ds(r, rows, stride=dil), :].astype(BF16))
            xn = xp_ref[...]
            cos = jnp.concatenate([cos_ref[pl.ds(r, rows, stride=dil), :] for r in range(dil)], axis=0)
            sin = jnp.concatenate([sin_ref[pl.ds(r, rows, stride=dil), :] for r in range(dil)], axis=0)
        for part in range(3):
            lo = (gidx * 3 + part) * D_MODEL
            y = _dot(xn, w_ref[:, lo:lo + D_MODEL])
            if part == 2:
                o_ref[:, :, 2 * D_MODEL:] = y.reshape(dil, rows, D_MODEL).astype(BF16)
                continue
            c_t, s_t = (cos * A_Q_SCALE, sin * A_Q_SCALE) if part == 0 else (cos, sin)
            for cb in range(reps):
                yb = y[:, cb * LANES:(cb + 1) * LANES]
                partner = jnp.where(low_half, pltpu.roll(yb, LANES - A_HEAD_DIM // 2, 1),
                                    pltpu.roll(yb, A_HEAD_DIM // 2, 1))
                col = part * D_MODEL + cb * LANES
                o_ref[:, :, col:col + LANES] = (yb * c_t + partner * s_t).reshape(dil, rows, LANES).astype(BF16)


def _a_in(x, g, w, cos_t, sin_t):
    b, s, _ = x.shape
    tm = TM_A_IN
    return pl.pallas_call(
        _a_in_body,
        grid=(b, s // tm),
        in_specs=[
            pl.BlockSpec((None, tm, D_MODEL), lambda bi, i: (bi, i, 0)),
            pl.BlockSpec((1, D_MODEL), lambda bi, i: (0, 0)),
            _resident(w),
            pl.BlockSpec((tm, LANES), lambda bi, i: (i, 0)),
            pl.BlockSpec((tm, LANES), lambda bi, i: (i, 0)),
        ],
        out_specs=[pl.BlockSpec((None, dil, tm // dil, 3 * D_MODEL), lambda bi, i: (bi, 0, i, 0))
                   for _, dil in A_GROUPS],
        out_shape=[jax.ShapeDtypeStruct((b, dil, s // dil, 3 * D_MODEL), BF16) for _, dil in A_GROUPS],
        scratch_shapes=[pltpu.VMEM((D_MODEL // LANES, tm, LANES), F32), pltpu.VMEM((tm, D_MODEL), BF16)],
        compiler_params=_params("parallel", "parallel"),
        name="a_in",
    )(x, g, _arr(w), cos_t, sin_t)


def _attn_body(q_ref, kp_ref, k_ref, kn_ref, vp_ref, v_ref, vn_ref, o_ref, stat_ref, *, n_tiles):
    n_res, tq = q_ref.shape[0], q_ref.shape[1]
    sq = A_SUBTILE
    win = sq + 2 * A_HALF
    n_sub_tiles = tq // sq
    i = pl.program_id(2)
    qi = lax.broadcasted_iota(jnp.int32, (sq, win), 0)
    kj = lax.broadcasted_iota(jnp.int32, (sq, win), 1)
    band = (kj >= qi) & (kj <= qi + 2 * A_HALF)
    left = lax.broadcasted_iota(jnp.int32, (sq, LANES), 1) < A_HEAD_DIM
    for res, st in [(a, c) for a in range(n_res) for c in range(n_sub_tiles)]:
        if st == 0:
            kw = jnp.concatenate([kp_ref[res], k_ref[res], kn_ref[res]], axis=0)
            vw = jnp.concatenate([vp_ref[res], v_ref[res], vn_ref[res]], axis=0)
        q_res, o_res, stat_res = q_ref.at[res], o_ref.at[res], stat_ref.at[res]
        rows = slice(st * sq, (st + 1) * sq)
        mask = band
        if st == 0:
            mask = mask & (kj >= jnp.where(i == 0, A_HALF, 0))
        if st == n_sub_tiles - 1:
            mask = mask & (kj < jnp.where(i == n_tiles - 1, sq + A_HALF, win))
        mask2 = jnp.concatenate([mask, mask], axis=0)
        stat_res[rows, :] = jnp.zeros((sq, LANES), F32)

        def scores(hp, q_res=q_res, rows=rows, kw=kw, st=st):
            sl = slice(hp * LANES, (hp + 1) * LANES)
            q2 = q_res[rows, sl]
            zero = jnp.zeros_like(q2)
            qq = jnp.concatenate([jnp.where(left, q2, zero), jnp.where(left, zero, q2)], axis=0)
            return _dot_nt(qq, kw[st * sq:st * sq + win, sl])

        s_next = scores(0)
        for hp in range(A_HEADS // 2):
            sl = slice(hp * LANES, (hp + 1) * LANES)
            s = jnp.where(mask2, s_next, NEG_INF)
            if hp + 1 < A_HEADS // 2:
                s_next = scores(hp + 1)
            m = jnp.max(s, axis=-1, keepdims=True)
            p = jnp.exp2(s - m)
            l = jnp.sum(p, axis=-1, keepdims=True)
            r = _dot(p.astype(BF16), vw[st * sq:st * sq + win, sl])
            o_res[rows, sl] = jnp.where(left, r[:sq], r[sq:]).astype(BF16)
            for off, col in ((0, m), (A_HEADS, l)):
                stat_res[rows, off + 2 * hp:off + 2 * hp + 1] = col[:sq]
                stat_res[rows, off + 2 * hp + 1:off + 2 * hp + 2] = col[sq:]


def _attn_group(qkv):
    b, dil, n_sub, _ = qkv.shape
    tq = min(A_QUERIES_PER_STEP, n_sub)
    n_res = min(dil, A_QUERIES_PER_STEP // tq)
    n_tiles = n_sub // tq
    halo_per_tile = tq // A_HALF
    n_halo = n_sub // A_HALF

    def own(part):
        return pl.BlockSpec((None, n_res, tq, D_MODEL), lambda bi, r, i: (bi, r, i, part))

    def prev(part):
        return pl.BlockSpec((None, n_res, A_HALF, D_MODEL),
                            lambda bi, r, i: (bi, r, jnp.maximum(i * halo_per_tile - 1, 0), part))

    def nxt(part):
        return pl.BlockSpec((None, n_res, A_HALF, D_MODEL),
                            lambda bi, r, i: (bi, r, jnp.minimum((i + 1) * halo_per_tile, n_halo - 1), part))

    return pl.pallas_call(
        functools.partial(_attn_body, n_tiles=n_tiles),
        grid=(b, dil // n_res, n_tiles),
        in_specs=[own(0), prev(1), own(1), nxt(1), prev(2), own(2), nxt(2)],
        out_specs=[
            pl.BlockSpec((None, n_res, tq, D_MODEL), lambda bi, r, i: (bi, r, i, 0)),
            pl.BlockSpec((None, n_res, tq, LANES), lambda bi, r, i: (bi, r, i, 0)),
        ],
        out_shape=[
            jax.ShapeDtypeStruct((b, dil, n_sub, D_MODEL), BF16),
            jax.ShapeDtypeStruct((b, dil, n_sub, LANES), F32),
        ],
        compiler_params=_params("parallel", "parallel", "parallel"),
        name=f"attn_d{dil}",
    )(qkv, qkv, qkv, qkv, qkv, qkv, qkv)


def _a_out_stage(o1_ref, o2_ref, o3_ref, l1_ref, l2_ref, l3_ref, w_ref, sp_ref, x_ref,
                 os2_ref, os3_ref, ls2_ref, ls3_ref, y_ref):
    tm = x_ref.shape[0]
    for o_ref, l_ref, os_ref, ls_ref in ((o2_ref, l2_ref, os2_ref, ls2_ref), (o3_ref, l3_ref, os3_ref, ls3_ref)):
        dil = o_ref.shape[0]
        rows = tm // dil
        for r in range(dil):
            for cb in range(D_MODEL // LANES):
                os_ref[cb, pl.ds(r, rows, stride=dil), :] = o_ref[r, :, cb * LANES:(cb + 1) * LANES].astype(F32)
            ls_ref[pl.ds(r, rows, stride=dil), :] = l_ref[r]
    s1, s2, s3 = l1_ref[0], ls2_ref[...], ls3_ref[...]
    m = jnp.maximum(jnp.maximum(s1, s2), s3)
    e1, e2, e3 = jnp.exp2(s1 - m), jnp.exp2(s2 - m), jnp.exp2(s3 - m)
    den = sum(e * pltpu.roll(st, LANES - A_HEADS, 1) for e, st in ((e1, s1), (e2, s2), (e3, s3)))
    inv = 1.0 / den
    valid = lax.broadcasted_iota(jnp.int32, (tm, LANES), 1) < A_HEADS
    packed = jnp.zeros((tm, LANES), F32)
    for gidx, e in enumerate((e1, e2, e3)):
        wgt = jnp.where(valid, e * inv, 0.0)
        hi = wgt.astype(BF16).astype(F32)
        for term, val in enumerate((hi, wgt - hi)):
            shift = 2 * A_HEADS * gidx + A_HEADS * term
            packed = packed + (pltpu.roll(val, shift, 1) if shift else val)
    wall = _dot(packed.astype(BF16), sp_ref[...])
    for cb in range(D_MODEL // LANES):
        sl = slice(cb * LANES, (cb + 1) * LANES)
        y = (wall[:, sl] * o1_ref[0, :, sl].astype(F32)
             + wall[:, D_MODEL + cb * LANES:D_MODEL + (cb + 1) * LANES] * os2_ref[cb]
             + wall[:, 2 * D_MODEL + cb * LANES:2 * D_MODEL + (cb + 1) * LANES] * os3_ref[cb])
        y_ref[:, sl] = y.astype(BF16)
    return x_ref[...] + _dot(y_ref[...], w_ref[...])


def _a_out_mixer(outs, stats, w):
    col = np.arange(A_N_GROUPS * D_MODEL)
    row = np.arange(LANES)
    spread = jnp.asarray((row[:, None] // (2 * A_HEADS) == col[None, :] // D_MODEL)
                         & (row[:, None] % A_HEADS == (col[None, :] % D_MODEL) // A_HEAD_DIM)
                         & (row[:, None] < 2 * A_HEADS * A_N_GROUPS), dtype=BF16)

    def specs(tm):
        def grouped(arr):
            dil, width = arr.shape[1], arr.shape[3]
            return pl.BlockSpec((None, dil, tm // dil, width), lambda bi, i: (bi, 0, i, 0))

        return [grouped(a) for a in outs] + [grouped(a) for a in stats] + [_resident(w), _resident(spread)]

    def scratch(tm):
        return [pltpu.VMEM((D_MODEL // LANES, tm, LANES), F32), pltpu.VMEM((D_MODEL // LANES, tm, LANES), F32),
                pltpu.VMEM((tm, LANES), F32), pltpu.VMEM((tm, LANES), F32), pltpu.VMEM((tm, D_MODEL), BF16)]

    return (_a_out_stage, [*outs, *stats, _arr(w), spread], specs, scratch, "a_out_xattn")


def _rope_tables(seq_len):
    inv = ROPE_THETA ** (-jnp.arange(0, A_HEAD_DIM, 2, dtype=F32) / A_HEAD_DIM)
    ang = jnp.arange(seq_len, dtype=F32)[:, None] * inv[None, :]
    cos, sin = jnp.cos(ang), jnp.sin(ang)
    reps = LANES // A_HEAD_DIM
    return (jnp.concatenate([cos, cos] * reps, axis=1),
            jnp.concatenate([-sin, sin] * reps, axis=1))


def _mixer_a(x, g, w_in, w_out):
    b, s, _ = x.shape
    cos_t, sin_t = _rope_tables(s)
    outs, stats = [], []
    for qkv in _a_in(x, g, w_in, cos_t, sin_t):
        o, l = _attn_group(qkv)
        outs.append(o)
        stats.append(l)
    return _a_out_mixer(outs, stats, w_out)


HALO = 8
HY_COL_CHUNK = 256


def _hy_in_body(xp_ref, x_ref, xn_ref, g_ref, w_ref, cw_ref, cb_ref, x0_ref, vv_ref, u_ref, *, n_tiles):
    tm = x_ref.shape[0]
    i = pl.program_id(1)
    xe = jnp.concatenate([xp_ref[...], x_ref[...], xn_ref[...]], axis=0)
    xn = _rms(xe, g_ref[...]).astype(BF16)
    row = lax.broadcasted_iota(jnp.int32, (tm + 2 * HALO, 1), 0)
    inside = ((row >= HALO) | (i > 0)) & ((row < tm + HALO) | (i < n_tiles - 1))
    cw = HY_COL_CHUNK
    for c in range(D_MODEL // cw):
        parts = []
        for k in range(3):
            cols = slice(k * D_MODEL + c * cw, k * D_MODEL + (c + 1) * cw)
            u_ref[k] = jnp.where(inside, _dot(xn, w_ref[:, cols]), 0.0)
            parts.append(u_ref[k, pl.ds(HALO - 1, tm), :] * cw_ref[0:1, cols]
                         + u_ref[k, pl.ds(HALO, tm), :] * cw_ref[1:2, cols]
                         + u_ref[k, pl.ds(HALO + 1, tm), :] * cw_ref[2:3, cols]
                         + cb_ref[:, cols])
        x0_ref[:, c * cw:(c + 1) * cw] = parts[0].astype(BF16)
        vv_ref[:, c * cw:(c + 1) * cw] = (parts[2] * parts[1]).astype(BF16)


def _hy_in(x, g, w, conv_w, conv_b):
    b, s, _ = x.shape
    tm = TM_HY_IN
    n_tiles = s // tm
    per = tm // HALO
    n_halo = s // HALO
    tok = pl.BlockSpec((None, tm, D_MODEL), lambda bi, i: (bi, i, 0))
    return pl.pallas_call(
        functools.partial(_hy_in_body, n_tiles=n_tiles),
        grid=(b, n_tiles),
        in_specs=[
            pl.BlockSpec((None, HALO, D_MODEL), lambda bi, i: (bi, jnp.maximum(i * per - 1, 0), 0)),
            tok,
            pl.BlockSpec((None, HALO, D_MODEL), lambda bi, i: (bi, jnp.minimum((i + 1) * per, n_halo - 1), 0)),
            pl.BlockSpec((1, D_MODEL), lambda bi, i: (0, 0)),
            _resident(w),
            pl.BlockSpec(conv_w.shape, lambda bi, i: (0, 0)),
            pl.BlockSpec(conv_b.shape, lambda bi, i: (0, 0)),
        ],
        out_specs=[tok, tok],
        out_shape=[jax.ShapeDtypeStruct(x.shape, BF16), jax.ShapeDtypeStruct(x.shape, BF16)],
        scratch_shapes=[pltpu.VMEM((3, tm + 2 * HALO, HY_COL_CHUNK), F32)],
        compiler_params=_params("parallel", "parallel"),
        name="hy_in",
    )(x, x, x, g, _arr(w), conv_w, conv_b)


def _hdot(a, b):
    return jnp.dot(a, b, precision=lax.Precision.HIGHEST, preferred_element_type=F32)


def _hy_filter_body(z_ref, t_ref, a_ref, b_ref, w1_ref, b1_ref, w2_ref, b2_ref, w3_ref, b3_ref,
                    wo_ref, fr_ref, dl_ref, h_ref, sum_ref):
    fr = fr_ref[...]
    hid = jnp.sin(fr * (_hdot(z_ref[...], w1_ref[...]) + b1_ref[...]))
    hid = jnp.sin(fr * (_hdot(hid, w2_ref[...]) + b2_ref[...]))
    hid = jnp.sin(fr * (_hdot(hid, w3_ref[...]) + b3_ref[...]))
    decay = jnp.exp(-t_ref[...] * dl_ref[...])
    hid = hid.astype(BF16)
    h_fwd = _dot(hid, wo_ref[:, :D_MODEL]) * decay
    h_bwd = _dot(hid, wo_ref[:, D_MODEL:]) * decay
    h = a_ref[...] * h_fwd + b_ref[...] * h_bwd
    h_ref[...] = h

    @pl.when(pl.program_id(0) == 0)
    def _():
        sum_ref[...] = jnp.zeros_like(sum_ref)

    sum_ref[...] += jnp.sum(jnp.abs(h), axis=0, keepdims=True)


def _hy_filter(seq_len, f_w1, f_b1, f_w2, f_b2, f_w3, f_b3, f_wout, f_freq):
    n = 2 * seq_len
    src = np.concatenate([np.arange(seq_len), [0], np.arange(seq_len - 1, 0, -1)])
    lag = jnp.asarray(src, dtype=F32)[:, None]
    t2 = lag / (seq_len - 1)
    w = 2.0 * math.pi * lag / seq_len
    f = jnp.linspace(1e-4, HY_BANDS - 1, HY_BANDS, dtype=F32)[None, :]
    z2 = jnp.concatenate([t2, jnp.cos(f * w), -jnp.sin(f * w),
                          jnp.zeros((n, HY_HID_PAD - HY_EMB), F32)], axis=-1)
    pos = np.arange(n)
    use_fwd = (pos < seq_len).astype(np.float32)[:, None]
    use_bwd = ((pos == 0) | (pos > seq_len)).astype(np.float32)[:, None]
    max_decay = math.log(HY_DECAY_TARGET) / HY_DECAY_STRONG_PCT
    min_decay = math.log(HY_DECAY_TARGET) / HY_DECAY_WEAK_PCT
    deltas = jnp.abs(jnp.linspace(min_decay, max_decay, D_MODEL, dtype=F32))[None, :]

    def pad2(m, rows):
        return jnp.pad(m, ((0, rows - m.shape[0]), (0, HY_HID_PAD - m.shape[1])))

    def padv(v):
        return jnp.pad(v, (0, HY_HID_PAD - v.shape[0]))[None, :]

    wo = jnp.pad(f_wout, ((0, HY_HID_PAD - f_wout.shape[0]), (0, 0))).astype(BF16)
    tr = TR_FILTER
    rowblk = lambda width: pl.BlockSpec((tr, width), lambda i: (i, 0))
    full = lambda shape: pl.BlockSpec(shape, lambda i: (0, 0))
    sq = (HY_HID_PAD, HY_HID_PAD)
    vec = (1, HY_HID_PAD)
    return pl.pallas_call(
        _hy_filter_body,
        grid=(n // tr,),
        in_specs=[rowblk(HY_HID_PAD), rowblk(1), rowblk(1), rowblk(1),
                  full(sq), full(vec), full(sq), full(vec), full(sq), full(vec),
                  full(wo.shape), full(vec), full((1, D_MODEL))],
        out_specs=[rowblk(D_MODEL), full((1, D_MODEL))],
        out_shape=[jax.ShapeDtypeStruct((n, D_MODEL), F32), jax.ShapeDtypeStruct((1, D_MODEL), F32)],
        compiler_params=_params("arbitrary"),
        name="hy_filter",
    )(z2, t2, jnp.asarray(use_fwd), jnp.asarray(use_bwd),
      pad2(f_w1, HY_HID_PAD), padv(f_b1), pad2(f_w2, HY_HID_PAD), padv(f_b2),
      pad2(f_w3, HY_HID_PAD), padv(f_b3), wo, padv(f_freq), deltas)


def _fft_split(n):
    n1 = 1 << ((n.bit_length() - 1 + 1) // 2)
    return n1, n // n1


def _fft_tables(n1, n2):
    n = n1 * n2
    h = n1 // 2
    idx = np.arange(n1)
    ang = -2.0 * np.pi * ((idx[:, None] * idx[None, :]) % n1) / n1
    fr, fi = np.cos(ang), np.sin(ang)
    m1_data = np.block([[fr[:, :h], -fi[:, :h]], [fi[:, :h], fr[:, :h]]])
    m1_filt = np.concatenate([fr, fi], axis=0)
    ifr, ifi = fr.T[:h] / n, -fi.T[:h] / n
    m3 = np.block([[ifr, -ifi], [ifi, ifr]])
    k1 = jnp.arange(n1, dtype=jnp.int32)[:, None, None]
    k2 = jnp.arange(n2, dtype=jnp.int32)[None, :, None]
    i2 = jnp.arange(n2, dtype=jnp.int32)[None, None, :]
    phase = (i2 * k1 + n1 * i2 * k2) % n
    ga = (-2.0 * math.pi / n) * phase.astype(F32)
    gr, gi = jnp.cos(ga), jnp.sin(ga)
    g_fwd = jnp.concatenate([jnp.concatenate([gr, -gi], axis=2), jnp.concatenate([gi, gr], axis=2)], axis=1)
    grt, git = jnp.swapaxes(gr, 1, 2), jnp.swapaxes(gi, 1, 2)
    g_inv = jnp.concatenate([jnp.concatenate([grt, git], axis=2), jnp.concatenate([-git, grt], axis=2)], axis=1)
    as_bf = lambda m: jnp.asarray(m, dtype=F32).astype(BF16)
    return as_bf(m1_data), as_bf(m1_filt), as_bf(m3), g_fwd.astype(BF16), g_inv.astype(BF16)


FFT_SUB = 16
FFT_TC = 512


def _dft_outer_body(m_ref, pm_ref, z_ref, o_ref, zs_ref, rb_ref):
    q, sub, tc = z_ref.shape
    r = m_ref.shape[0]
    nlb = tc // LANES
    for cb in range(nlb):
        zs_ref[cb] = z_ref[:, :, cb * LANES:(cb + 1) * LANES].astype(F32).reshape(q * sub, LANES)
    hw = tc // 2
    n_groups = r // sub

    def transform(hf, j):
        z = jnp.concatenate([zs_ref[cb, pl.ds(j, q, stride=sub), :]
                             for cb in range(hf * nlb // 2, (hf + 1) * nlb // 2)], axis=1)
        rb_ref[hf, j] = _dot(m_ref[...], z.astype(BF16)).astype(BF16)

    def regroup(hf, g):
        part, i0 = divmod(g * sub, r // 2)
        grouped = rb_ref[hf, :, g * sub:(g + 1) * sub, :].reshape(sub * sub, hw)
        o_ref[part, i0:i0 + sub, :, hf * hw:(hf + 1) * hw] = (
            _dot(pm_ref[...], grouped).reshape(sub, sub, hw).astype(BF16))

    for j in range(sub):
        transform(0, j)
    for step in range(max(sub, n_groups)):
        if step < sub:
            transform(1, step)
        if step < n_groups:
            regroup(0, step)
    for g in range(n_groups):
        regroup(1, g)


def _dft_outer(mat, perm, z5, name):
    p, q, nh, sub, c = z5.shape
    r = mat.shape[0]
    tc = FFT_TC
    return pl.pallas_call(
        _dft_outer_body,
        grid=(p, nh, c // tc),
        in_specs=[pl.BlockSpec(mat.shape, lambda pi, h, ci: (0, 0)),
                  pl.BlockSpec(perm.shape, lambda pi, h, ci: (0, 0)),
                  pl.BlockSpec((None, q, None, sub, tc), lambda pi, h, ci: (pi, 0, h, 0, ci))],
        out_specs=pl.BlockSpec((None, None, 2, r // 2, sub, tc), lambda pi, h, ci: (pi, h, 0, 0, 0, ci)),
        out_shape=jax.ShapeDtypeStruct((p, nh, 2, r // 2, sub, c), BF16),
        scratch_shapes=[pltpu.VMEM((tc // LANES, q * sub, LANES), F32),
                        pltpu.VMEM((2, sub, r, tc // 2), BF16)],
        compiler_params=_params("parallel", "parallel", "parallel"),
        name=name,
    )(mat, perm, z5)


def _idft_outer_body(m_ref, pm_ref, b_ref, o_ref, tb_ref, ys_ref):
    _, n1, sub, tc = b_ref.shape
    q = m_ref.shape[0]
    nlb = tc // LANES
    hw = tc // 2
    n_groups = 2 * n1 // sub

    def regroup(hf, g):
        part, i0 = divmod(g * sub, n1)
        grouped = b_ref[part, i0:i0 + sub, :, hf * hw:(hf + 1) * hw].reshape(sub * sub, hw)
        tb_ref[hf, :, g * sub:(g + 1) * sub, :] = _dot(pm_ref[...], grouped).reshape(sub, sub, hw).astype(BF16)

    def transform(hf, j):
        res = _dot(m_ref[...], tb_ref[hf, j])
        for c in range(nlb // 2):
            ys_ref[hf * nlb // 2 + c, pl.ds(j, q, stride=sub), :] = res[:, c * LANES:(c + 1) * LANES]

    for g in range(n_groups):
        regroup(0, g)
    for step in range(max(sub, n_groups)):
        if step < n_groups:
            regroup(1, step)
        if step < sub:
            transform(0, step)
    for j in range(sub):
        transform(1, j)
    for cb in range(nlb):
        o_ref[:, :, cb * LANES:(cb + 1) * LANES] = ys_ref[cb].reshape(q, sub, LANES)


def _idft_outer(mat, perm, b6):
    p, nh, _, n1, sub, c = b6.shape
    q = mat.shape[0]
    tc = FFT_TC
    return pl.pallas_call(
        _idft_outer_body,
        grid=(p, nh, c // tc),
        in_specs=[pl.BlockSpec(mat.shape, lambda pi, h, ci: (0, 0)),
                  pl.BlockSpec(perm.shape, lambda pi, h, ci: (0, 0)),
                  pl.BlockSpec((None, None, 2, n1, sub, tc), lambda pi, h, ci: (pi, h, 0, 0, 0, ci))],
        out_specs=pl.BlockSpec((None, q, None, sub, tc), lambda pi, h, ci: (pi, 0, h, 0, ci)),
        out_shape=jax.ShapeDtypeStruct((p, q, nh, sub, c), F32),
        scratch_shapes=[pltpu.VMEM((2, sub, 2 * n1, tc // 2), BF16),
                        pltpu.VMEM((tc // LANES, q * sub, LANES), F32)],
        compiler_params=_params("parallel", "parallel", "parallel"),
        name="hy_idft",
    )(mat, perm, b6)


FFT_KB = 4


def _stack_re_im(a_ref, kk):
    nh, _, _, sub, ct = a_ref.shape
    return jnp.concatenate([a_ref[:, 0, kk].reshape(nh * sub, ct), a_ref[:, 1, kk].reshape(nh * sub, ct)], axis=0)


def _spec_filter_body(g_ref, a_ref, sc_ref, h_ref):
    n2 = g_ref.shape[1] // 2
    for kk in range(g_ref.shape[0]):
        spec = _dot(g_ref[kk], _stack_re_im(a_ref, kk)) * sc_ref[...]
        h_ref[kk] = spec.reshape(2, n2, spec.shape[1])


def _spec_filter(g_fwd, a6, scale):
    _, nh, _, n1, sub, c = a6.shape
    n2 = nh * sub
    return pl.pallas_call(
        _spec_filter_body,
        grid=(n1 // FFT_KB,),
        in_specs=[pl.BlockSpec((FFT_KB, 2 * n2, 2 * n2), lambda k: (k, 0, 0)),
                  pl.BlockSpec((None, nh, 2, FFT_KB, sub, c), lambda k: (0, 0, 0, k, 0, 0)),
                  pl.BlockSpec((1, c), lambda k: (0, 0))],
        out_specs=pl.BlockSpec((FFT_KB, 2, n2, c), lambda k: (k, 0, 0, 0)),
        out_shape=jax.ShapeDtypeStruct((n1, 2, n2, c), F32),
        compiler_params=_params("parallel"),
        name="hy_spec_filter",
    )(g_fwd, a6, scale)


def _spec_body(gf_ref, gi_ref, h_ref, a_ref, o_ref):
    nh, _, kb, sub, ct = a_ref.shape
    n2 = nh * sub

    def forward(kk):
        return _dot(gf_ref[kk], _stack_re_im(a_ref, kk))

    spec_next = forward(0)
    for kk in range(kb):
        spec = spec_next
        if kk + 1 < kb:
            spec_next = forward(kk + 1)
        xr, xi = spec[:n2], spec[n2:]
        hr, hi = h_ref[kk, 0], h_ref[kk, 1]
        y = jnp.concatenate([xr * hr - xi * hi, xr * hi + xi * hr], axis=0).astype(BF16)
        back = _dot(gi_ref[kk], y)
        o_ref[:, 0, kk] = back[:n2].reshape(nh, sub, ct).astype(BF16)
        o_ref[:, 1, kk] = back[n2:].reshape(nh, sub, ct).astype(BF16)


def _spec(g_fwd, g_inv, hspec, a6):
    p, nh, _, n1, sub, c = a6.shape
    n2 = nh * sub
    blk = pl.BlockSpec((None, nh, 2, FFT_KB, sub, c), lambda k, pi: (pi, 0, 0, k, 0, 0))
    mat = pl.BlockSpec((FFT_KB, 2 * n2, 2 * n2), lambda k, pi: (k, 0, 0))
    return pl.pallas_call(
        _spec_body,
        grid=(n1 // FFT_KB, p),
        in_specs=[mat, mat, pl.BlockSpec((FFT_KB, 2, n2, c), lambda k, pi: (k, 0, 0, 0)), blk],
        out_specs=blk,
        out_shape=jax.ShapeDtypeStruct(a6.shape, BF16),
        compiler_params=_params("parallel", "parallel"),
        name="hy_spec",
    )(g_fwd, g_inv, hspec, a6)


def _long_conv(vv, h_raw, h_norm):
    b, l, c = vv.shape
    n1, n2 = _fft_split(2 * l)
    nh = n2 // FFT_SUB
    m1_data, m1_filt, m3, g_fwd, g_inv = _fft_tables(n1, n2)
    idx = np.arange(FFT_SUB * FFT_SUB)
    swapped = (idx % FFT_SUB) * FFT_SUB + idx // FFT_SUB
    perm = jnp.asarray(idx[None, :] == swapped[:, None], dtype=BF16)
    a_h = _dft_outer(m1_filt, perm, h_raw.reshape(1, n1, nh, FFT_SUB, c), "hy_dft1_filter")
    hspec = _spec_filter(g_fwd, a_h, 1.0 / h_norm)
    a = _dft_outer(m1_data, perm, vv.reshape(b // 2, n1, nh, FFT_SUB, c), "hy_dft1")
    y = _idft_outer(m3, perm, _spec(g_fwd, g_inv, hspec, a))
    return y.reshape(b, l, c)


def _hy_out_stage(cv_ref, vv_ref, x0_ref, bd_ref, w_ref, x_ref):
    y = ((cv_ref[...] + bd_ref[...] * vv_ref[...].astype(F32)) * x0_ref[...].astype(F32)).astype(BF16)
    return x_ref[...] + _dot(y, w_ref[...])


def _mixer_b(x, g, w_in, conv_w, conv_b, filt, bias_d, w_out):
    _, s, _ = x.shape
    x0, vv = _hy_in(x, g, w_in, conv_w, conv_b)
    h_raw, h_norm = _hy_filter(s, *filt)
    conv = _long_conv(vv, h_raw, h_norm)

    def specs(tm):
        tok = pl.BlockSpec((None, tm, D_MODEL), lambda bi, i: (bi, i, 0))
        return [tok, tok, tok, pl.BlockSpec((1, D_MODEL), lambda bi, i: (0, 0)), _resident(w_out)]

    return (_hy_out_stage, [conv, vv, x0, bias_d, _arr(w_out)], specs, lambda tm: [], "hy_out_xattn")


def _gelu(z):
    return 0.5 * z * (1.0 + lax.erf(z * (2.0 ** -0.5)))


def _sgu_body(x_ref, g_ref, win_ref, lng_ref, lnb_ref, ws_ref, bs_ref, wout_ref, o_ref, y_ref, z_ref):
    tm = x_ref.shape[0]
    x = x_ref[...]
    xn = _rms(x, g_ref[...]).astype(BF16)
    for c in range(2 * D_MODEL // FF_CHUNK):
        cols = slice(c * FF_CHUNK, (c + 1) * FF_CHUNK)
        z_ref[:, cols] = _gelu(_dot(xn, win_ref[:, cols]))
    zv = z_ref[:, D_MODEL:]
    zc = zv - jnp.mean(zv, axis=-1, keepdims=True)
    zv = zc * lax.rsqrt(jnp.mean(zc * zc, axis=-1, keepdims=True) + EPS) * lng_ref[...] + lnb_ref[...]
    zvb = zv.astype(BF16)
    n_chunks = tm // C_CHUNK
    for h in range(C_GROUPS):
        cols = slice(h * LANES, (h + 1) * LANES)
        v_all = jnp.concatenate([zvb[c * C_CHUNK:(c + 1) * C_CHUNK, cols] for c in range(n_chunks)], axis=1)
        sv = _dot(ws_ref[h], v_all) + bs_ref[:, h:h + 1]
        for c in range(n_chunks):
            rows = slice(c * C_CHUNK, (c + 1) * C_CHUNK)
            y_ref[rows, cols] = (z_ref[rows, cols] * sv[:, c * LANES:(c + 1) * LANES]).astype(BF16)
    o_ref[...] = x + _dot(y_ref[...], wout_ref[...])


def _mixer_c(x, g, w_in, ln_g, ln_b, w_s, b_s_t, w_out):
    b, s, _ = x.shape
    t = b * s
    tm = TM_SGU
    tok = pl.BlockSpec((tm, D_MODEL), lambda i: (i, 0))
    row = pl.BlockSpec((1, D_MODEL), lambda i: (0, 0))
    return pl.pallas_call(
        _sgu_body,
        grid=(t // tm,),
        in_specs=[tok, row, _resident(w_in), row, row, _resident(w_s),
                  pl.BlockSpec(b_s_t.shape, lambda i: (0, 0)), _resident(w_out)],
        out_specs=tok,
        out_shape=jax.ShapeDtypeStruct((t, D_MODEL), F32),
        scratch_shapes=[pltpu.VMEM((tm, D_MODEL), BF16), pltpu.VMEM((tm, 2 * D_MODEL), F32)],
        compiler_params=_params("parallel"),
        name="sgu",
    )(x.reshape(t, D_MODEL), g, _arr(w_in), ln_g, ln_b, _arr(w_s), b_s_t, _arr(w_out)).reshape(x.shape)


def _trunk(x, kvs, boff, w):
    b, s, _ = x.shape
    t = b * s
    for i in range(DEPTH):
        kind, j = i % N_MIXERS, i // N_MIXERS
        g = w["g_mix"][i][None, :]
        mixer_out = None
        if kind == 0:
            mixer_out = _mixer_a(x, g, w["a_w_in"][j], w["a_w_out"][j])
        elif kind == 1:
            filt = tuple(w[k][j] for k in ("b_f_w1", "b_f_b1", "b_f_w2", "b_f_b2", "b_f_w3", "b_f_b3",
                                          "b_f_wout", "b_f_freq"))
            mixer_out = _mixer_b(x, g, w["b_w_in"][j], w["b_conv_w"][j], w["b_conv_b"][j][None, :], filt,
                                 w["b_bias_d"][j][None, :], w["b_w_out"][j])
        else:
            x = _mixer_c(x, g, w["c_w_in"][j], w["c_ln_g"][j][None, :], w["c_ln_b"][j][None, :],
                         w["c_w_s"][j], w["c_b_s"][j].T, w["c_w_out"][j])
        x = _xattn(x, w["g_cross"][i][None, :], w["x_w_q"][i], kvs[i], boff, w["x_w_o"][i], mixer_out)
        g_final = w["g_final"][None, :] if i == DEPTH - 1 else None
        x = _ffn(x.reshape(t, D_MODEL), w["g_ffn"][i][None, :], w["f_w_gu"][i], w["f_w_down"][i],
                 g_final).reshape(b, s, D_MODEL)
    return x


_BF16_WEIGHTS = ("a_w_in", "a_w_out", "b_w_in", "b_w_out", "c_w_in", "c_w_s", "c_w_out",
                 "x_w_q", "x_w_kv", "x_w_o", "f_w_gu", "f_w_down")


def kernel(x_prompt, x_sample, mem_prompt, mem_sample, g_mix, g_cross, g_ffn, g_final, a_w_in, a_w_out, b_w_in, b_conv_w, b_conv_b, b_f_w1, b_f_b1, b_f_w2, b_f_b2, b_f_w3, b_f_b3, b_f_wout, b_f_freq, b_bias_d, b_w_out, c_w_in, c_ln_g, c_ln_b, c_w_s, c_b_s, c_w_out, x_w_q, x_w_kv, x_w_o, f_w_gu, f_w_down):
    w = dict(g_mix=g_mix, g_cross=g_cross, g_ffn=g_ffn, g_final=g_final,
             a_w_in=a_w_in, a_w_out=a_w_out, b_w_in=b_w_in, b_conv_w=b_conv_w, b_conv_b=b_conv_b,
             b_f_w1=b_f_w1, b_f_b1=b_f_b1, b_f_w2=b_f_w2, b_f_b2=b_f_b2, b_f_w3=b_f_w3,
             b_f_b3=b_f_b3, b_f_wout=b_f_wout, b_f_freq=b_f_freq, b_bias_d=b_bias_d,
             b_w_out=b_w_out, c_w_in=c_w_in, c_ln_g=c_ln_g, c_ln_b=c_ln_b, c_w_s=c_w_s,
             c_b_s=c_b_s, c_w_out=c_w_out, x_w_q=x_w_q, x_w_kv=x_w_kv, x_w_o=x_w_o,
             f_w_gu=f_w_gu, f_w_down=f_w_down)
    for name in _BF16_WEIGHTS:
        stack = w[name].astype(BF16)
        w[name] = [(stack, j) for j in range(stack.shape[0])]
    nb_prompt = mem_prompt.shape[0]
    mem = jnp.concatenate([mem_prompt, mem_sample], axis=0)
    mem2 = mem.reshape(mem.shape[0] * MEM_LEN, D_MODEL)
    kvs = [_kv_proj(mem2, w["x_w_kv"][i]).reshape(mem.shape[0], MEM_LEN, 2 * D_MODEL) for i in range(DEPTH)]
    y_prompt = _trunk(x_prompt, kvs, 0, w)
    y_sample = _trunk(x_sample, kvs, nb_prompt, w)
    return (y_prompt, y_sample)
```

```python
import functools
import math

import numpy as np
import jax
import jax.numpy as jnp
from jax import lax
from jax.experimental import pallas as pl
from jax.experimental.pallas import tpu as pltpu

F32 = jnp.float32
BF16 = jnp.bfloat16

D_MODEL = 1024
DEPTH = 4
N_MIXERS = 3

A_GROUPS = ((128, 1), (512, 4), (2048, 16))
A_N_GROUPS = len(A_GROUPS)
A_HEADS = 16
A_HEAD_DIM = D_MODEL // A_HEADS
A_HALF = 64
A_SUBTILE = 128
A_QUERIES_PER_STEP = 1024
A_Q_SCALE = A_HEAD_DIM ** -0.5 * math.log2(math.e)
A_REGROUP = 256
ROPE_THETA = 10000.0

HY_EMB = 33
HY_BANDS = (HY_EMB - 1) // 2
HY_HID_PAD = 128
HY_DECAY_TARGET = 1e-2
HY_DECAY_STRONG_PCT = 0.3
HY_DECAY_WEAK_PCT = 1.5

C_CHUNK = 128
C_GROUPS = 8

MEM_LEN = 256
X_HEADS = 4
X_HEAD_DIM = D_MODEL // X_HEADS

D_FF = -(-8 * D_MODEL // (3 * 256)) * 256
FF_CHUNK = 256

EPS = 1e-6
NEG_INF = -1e30

LANES = 128
VMEM_LIMIT = 56 * 1024 * 1024

TM_FFN = 1024
TM_XATTN = 1024
TM_MIXER_XATTN = 512
TM_A_IN = 512
TM_HY_IN = 512
TM_SGU = 512
TM_KV = 256
TR_FILTER = 512


def _params(*sem):
    return pltpu.CompilerParams(dimension_semantics=sem, vmem_limit_bytes=VMEM_LIMIT)


def _resident(w):
    if isinstance(w, tuple):
        stack, layer = w
        nd = stack.ndim - 1
        return pl.BlockSpec((None,) + stack.shape[1:], lambda *_: (layer,) + (0,) * nd,
                            pipeline_mode=pl.Buffered(1))
    nd = w.ndim
    return pl.BlockSpec(w.shape, lambda *_: (0,) * nd, pipeline_mode=pl.Buffered(1))


def _arr(w):
    return w[0] if isinstance(w, tuple) else w


def _dot(a, b):
    return jnp.dot(a, b, preferred_element_type=F32)


def _dot_nt(a, b):
    return lax.dot_general(a, b, (((1,), (1,)), ((), ())), preferred_element_type=F32)


def _rms(x, g):
    return x * lax.rsqrt(jnp.mean(x * x, axis=-1, keepdims=True) + EPS) * g


def _ffn_body(*refs, final):
    if final:
        x_ref, g_ref, wgu_ref, wd_ref, gf_ref, o_ref, act_ref = refs
    else:
        x_ref, g_ref, wgu_ref, wd_ref, o_ref, act_ref = refs
    x = x_ref[...]
    xn = _rms(x, g_ref[...]).astype(BF16)
    for c in range(D_FF // FF_CHUNK):
        lo = c * FF_CHUNK
        gate = _dot(xn, wgu_ref[:, lo:lo + FF_CHUNK])
        up = _dot(xn, wgu_ref[:, D_FF + lo:D_FF + lo + FF_CHUNK])
        act_ref[:, lo:lo + FF_CHUNK] = (gate / (1.0 + jnp.exp(-gate)) * up).astype(BF16)
    y = x + _dot(act_ref[...], wd_ref[...])
    if final:
        y = _rms(y, gf_ref[...])
    o_ref[...] = y


def _ffn(x2, g, wgu, wd, g_final=None):
    t = x2.shape[0]
    tm = TM_FFN
    final = g_final is not None
    tok = pl.BlockSpec((tm, D_MODEL), lambda i: (i, 0))
    row = pl.BlockSpec((1, D_MODEL), lambda i: (0, 0))
    in_specs = [tok, row, _resident(wgu), _resident(wd)]
    args = [x2, g, _arr(wgu), _arr(wd)]
    if final:
        in_specs.append(row)
        args.append(g_final)
    return pl.pallas_call(
        functools.partial(_ffn_body, final=final),
        grid=(t // tm,),
        in_specs=in_specs,
        out_specs=tok,
        out_shape=jax.ShapeDtypeStruct((t, D_MODEL), F32),
        scratch_shapes=[pltpu.VMEM((tm, D_FF), BF16)],
        compiler_params=_params("parallel"),
        name="ffn_final" if final else "ffn",
    )(*args)


def _kv_body(m_ref, w_ref, o_ref):
    o_ref[...] = _dot(m_ref[...].astype(BF16), w_ref[...]).astype(BF16)


def _kv_proj(mem2, wkv):
    r = mem2.shape[0]
    tm = TM_KV
    return pl.pallas_call(
        _kv_body,
        grid=(r // tm,),
        in_specs=[pl.BlockSpec((tm, D_MODEL), lambda i: (i, 0)), _resident(wkv)],
        out_specs=pl.BlockSpec((tm, 2 * D_MODEL), lambda i: (i, 0)),
        out_shape=jax.ShapeDtypeStruct((r, 2 * D_MODEL), BF16),
        compiler_params=_params("parallel"),
        name="kv_proj",
    )(mem2, _arr(wkv))


def _xattn_stage(x, g_ref, wq_ref, k_ref, v_ref, wo_ref, y_ref):
    xn = _rms(x, g_ref[...]).astype(BF16)
    q = (_dot(xn, wq_ref[...]) * (X_HEAD_DIM ** -0.5)).astype(BF16)
    for h in range(X_HEADS):
        sl = slice(h * X_HEAD_DIM, (h + 1) * X_HEAD_DIM)
        s = _dot_nt(q[:, sl], k_ref[:, sl])
        p = jnp.exp(s - jnp.max(s, axis=-1, keepdims=True))
        inv = 1.0 / jnp.sum(p, axis=-1, keepdims=True)
        y_ref[:, sl] = (_dot(p.astype(BF16), v_ref[:, sl]) * inv).astype(BF16)
    return x + _dot(y_ref[...], wo_ref[...])


def _xattn_body(*refs, mixer_out, n_mixer_in, n_mixer_scratch):
    n_in = n_mixer_in + 6
    mixer_in, (x_ref, g_ref, wq_ref, k_ref, v_ref, wo_ref) = refs[:n_mixer_in], refs[n_mixer_in:n_in]
    o_ref = refs[n_in]
    mixer_scratch, y_ref = refs[n_in + 1:n_in + 1 + n_mixer_scratch], refs[n_in + 1 + n_mixer_scratch]
    x = x_ref[...] if mixer_out is None else mixer_out(*mixer_in, x_ref, *mixer_scratch)
    o_ref[...] = _xattn_stage(x, g_ref, wq_ref, k_ref, v_ref, wo_ref, y_ref)


def _xattn(x, g, wq, kv, boff, wo, mixer=None):
    b, s, _ = x.shape
    if mixer is None:
        stage, m_args, m_specs, m_scratch, tm, name = None, [], [], [], TM_XATTN, "xattn"
    else:
        stage, m_args, specs_fn, scratch_fn, name = mixer
        tm = TM_MIXER_XATTN
        m_specs, m_scratch = specs_fn(tm), scratch_fn(tm)
    tok = pl.BlockSpec((None, tm, D_MODEL), lambda bi, i: (bi, i, 0))
    return pl.pallas_call(
        functools.partial(_xattn_body, mixer_out=stage, n_mixer_in=len(m_args), n_mixer_scratch=len(m_scratch)),
        grid=(b, s // tm),
        in_specs=m_specs + [
            tok,
            pl.BlockSpec((1, D_MODEL), lambda bi, i: (0, 0)),
            _resident(wq),
            pl.BlockSpec((None, MEM_LEN, D_MODEL), lambda bi, i: (bi + boff, 0, 0)),
            pl.BlockSpec((None, MEM_LEN, D_MODEL), lambda bi, i: (bi + boff, 0, 1)),
            _resident(wo),
        ],
        out_specs=tok,
        out_shape=jax.ShapeDtypeStruct(x.shape, F32),
        scratch_shapes=m_scratch + [pltpu.VMEM((tm, D_MODEL), BF16)],
        compiler_params=_params("parallel", "parallel"),
        name=name,
    )(*m_args, x, g, _arr(wq), kv, kv, _arr(wo))


def _a_in_body(x_ref, g_ref, w_ref, cos_ref, sin_ref, *rest):
    o_refs, (xs_ref, xp_ref) = rest[:A_N_GROUPS], rest[A_N_GROUPS:]
    tm = x_ref.shape[0]
    xn_f32 = _rms(x_ref[...], g_ref[...])
    reps = D_MODEL // LANES
    for cb in range(reps):
        xs_ref[cb] = xn_f32[:, cb * LANES:(cb + 1) * LANES]
    lane = lax.broadcasted_iota(jnp.int32, (tm, LANES), 1)
    low_half = (lane & (A_HEAD_DIM // 2)) == 0
    for gidx, (o_ref, (_, dil)) in enumerate(zip(o_refs, A_GROUPS)):
        rows = tm // dil
        if dil == 1:
            xn = xn_f32.astype(BF16)
            cos, sin = cos_ref[...], sin_ref[...]
        else:
            for r in range(dil):
                for cb in range(reps):
                    xp_ref[r * rows:(r + 1) * rows, cb * LANES:(cb + 1) * LANES] = (
                        xs_ref[cb, pl.ds(r, rows, stride=dil), :].astype(BF16))
            xn = xp_ref[...]
            cos = jnp.concatenate([cos_ref[pl.ds(r, rows, stride=dil), :] for r in range(dil)], axis=0)
            sin = jnp.concatenate([sin_ref[pl.ds(r, rows, stride=dil), :] for r in range(dil)], axis=0)
        for part in range(3):
            lo = (gidx * 3 + part) * D_MODEL
            y = _dot(xn, w_ref[:, lo:lo + D_MODEL])
            if part == 2:
                o_ref[:, :, 2 * D_MODEL:] = y.reshape(dil, rows, D_MODEL).astype(BF16)
                continue
            c_t, s_t = (cos * A_Q_SCALE, sin * A_Q_SCALE) if part == 0 else (cos, sin)
            for cb in range(reps):
                yb = y[:, cb * LANES:(cb + 1) * LANES]
                partner = jnp.where(low_half, pltpu.roll(yb, LANES - A_HEAD_DIM // 2, 1),
                                    pltpu.roll(yb, A_HEAD_DIM // 2, 1))
                col = part * D_MODEL + cb * LANES
                o_ref[:, :, col:col + LANES] = (yb * c_t + partner * s_t).reshape(dil, rows, LANES).astype(BF16)


def _a_in(x, g, w, cos_t, sin_t):
    b, s, _ = x.shape
    tm = TM_A_IN
    return pl.pallas_call(
        _a_in_body,
        grid=(b, s // tm),
        in_specs=[
            pl.BlockSpec((None, tm, D_MODEL), lambda bi, i: (bi, i, 0)),
            pl.BlockSpec((1, D_MODEL), lambda bi, i: (0, 0)),
            _resident(w),
            pl.BlockSpec((tm, LANES), lambda bi, i: (i, 0)),
            pl.BlockSpec((tm, LANES), lambda bi, i: (i, 0)),
        ],
        out_specs=[pl.BlockSpec((None, dil, tm // dil, 3 * D_MODEL), lambda bi, i: (bi, 0, i, 0))
                   for _, dil in A_GROUPS],
        out_shape=[jax.ShapeDtypeStruct((b, dil, s // dil, 3 * D_MODEL), BF16) for _, dil in A_GROUPS],
        scratch_shapes=[pltpu.VMEM((D_MODEL // LANES, tm, LANES), F32), pltpu.VMEM((tm, D_MODEL), BF16)],
        compiler_params=_params("parallel", "parallel"),
        name="a_in",
    )(x, g, _arr(w), cos_t, sin_t)


def _attn_body(q_ref, kp_ref, k_ref, kn_ref, vp_ref, v_ref, vn_ref, o_ref, stat_ref, *, n_tiles):
    n_res, tq = q_ref.shape[0], q_ref.shape[1]
    sq = A_SUBTILE
    win = sq + 2 * A_HALF
    n_sub_tiles = tq // sq
    i = pl.program_id(2)
    qi = lax.broadcasted_iota(jnp.int32, (sq, win), 0)
    kj = lax.broadcasted_iota(jnp.int32, (sq, win), 1)
    band = (kj >= qi) & (kj <= qi + 2 * A_HALF)
    left = lax.broadcasted_iota(jnp.int32, (sq, LANES), 1) < A_HEAD_DIM
    for res, st in [(a, c) for a in range(n_res) for c in range(n_sub_tiles)]:
        if st == 0:
            kw = jnp.concatenate([kp_ref[res], k_ref[res], kn_ref[res]], axis=0)
            vw = jnp.concatenate([vp_ref[res], v_ref[res], vn_ref[res]], axis=0)
        q_res, o_res, stat_res = q_ref.at[res], o_ref.at[res], stat_ref.at[res]
        rows = slice(st * sq, (st + 1) * sq)
        mask = band
        if st == 0:
            mask = mask & (kj >= jnp.where(i == 0, A_HALF, 0))
        if st == n_sub_tiles - 1:
            mask = mask & (kj < jnp.where(i == n_tiles - 1, sq + A_HALF, win))
        mask2 = jnp.concatenate([mask, mask], axis=0)
        stat_res[rows, :] = jnp.zeros((sq, LANES), F32)

        def scores(hp, q_res=q_res, rows=rows, kw=kw, st=st):
            sl = slice(hp * LANES, (hp + 1) * LANES)
            q2 = q_res[rows, sl]
            zero = jnp.zeros_like(q2)
            qq = jnp.concatenate([jnp.where(left, q2, zero), jnp.where(left, zero, q2)], axis=0)
            return _dot_nt(qq, kw[st * sq:st * sq + win, sl])

        s_next = scores(0)
        for hp in range(A_HEADS // 2):
            sl = slice(hp * LANES, (hp + 1) * LANES)
            s = jnp.where(mask2, s_next, NEG_INF)
            if hp + 1 < A_HEADS // 2:
                s_next = scores(hp + 1)
            m = jnp.max(s, axis=-1, keepdims=True)
            p = jnp.exp2(s - m)
            l = jnp.sum(p, axis=-1, keepdims=True)
            r = _dot(p.astype(BF16), vw[st * sq:st * sq + win, sl])
            o_res[rows, sl] = jnp.where(left, r[:sq], r[sq:]).astype(BF16)
            for off, col in ((0, m), (A_HEADS, l)):
                stat_res[rows, off + 2 * hp:off + 2 * hp + 1] = col[:sq]
                stat_res[rows, off + 2 * hp + 1:off + 2 * hp + 2] = col[sq:]


def _attn_group(qkv):
    b, dil, n_sub, _ = qkv.shape
    tq = min(A_QUERIES_PER_STEP, n_sub)
    n_res = min(dil, A_QUERIES_PER_STEP // tq)
    n_tiles = n_sub // tq
    halo_per_tile = tq // A_HALF
    n_halo = n_sub // A_HALF

    def own(part):
        return pl.BlockSpec((None, n_res, tq, D_MODEL), lambda bi, r, i: (bi, r, i, part))

    def prev(part):
        return pl.BlockSpec((None, n_res, A_HALF, D_MODEL),
                            lambda bi, r, i: (bi, r, jnp.maximum(i * halo_per_tile - 1, 0), part))

    def nxt(part):
        return pl.BlockSpec((None, n_res, A_HALF, D_MODEL),
                            lambda bi, r, i: (bi, r, jnp.minimum((i + 1) * halo_per_tile, n_halo - 1), part))

    return pl.pallas_call(
        functools.partial(_attn_body, n_tiles=n_tiles),
        grid=(b, dil // n_res, n_tiles),
        in_specs=[own(0), prev(1), own(1), nxt(1), prev(2), own(2), nxt(2)],
        out_specs=[
            pl.BlockSpec((None, n_res, tq, D_MODEL), lambda bi, r, i: (bi, r, i, 0)),
            pl.BlockSpec((None, n_res, tq, LANES), lambda bi, r, i: (bi, r, i, 0)),
        ],
        out_shape=[
            jax.ShapeDtypeStruct((b, dil, n_sub, D_MODEL), BF16),
            jax.ShapeDtypeStruct((b, dil, n_sub, LANES), F32),
        ],
        compiler_params=_params("parallel", "parallel", "parallel"),
        name=f"attn_d{dil}",
    )(qkv, qkv, qkv, qkv, qkv, qkv, qkv)


def _a_out_stage(o1_ref, o2_ref, o3_ref, l1_ref, l2_ref, l3_ref, w_ref, sp_ref, pm2_ref, pm3_ref, x_ref,
                 ls2_ref, ls3_ref, y_ref):
    tm = x_ref.shape[0]
    for l_ref, ls_ref in ((l2_ref, ls2_ref), (l3_ref, ls3_ref)):
        dil = l_ref.shape[0]
        for r in range(dil):
            ls_ref[pl.ds(r, tm // dil, stride=dil), :] = l_ref[r]

    def token_order(o_ref, pm_ref, blk):
        dil = o_ref.shape[0]
        per = A_REGROUP // dil
        grouped = jnp.concatenate([o_ref[r, blk * per:(blk + 1) * per, :] for r in range(dil)], axis=0)
        return _dot(pm_ref[...], grouped)
    s1, s2, s3 = l1_ref[0], ls2_ref[...], ls3_ref[...]
    m = jnp.maximum(jnp.maximum(s1, s2), s3)
    e1, e2, e3 = jnp.exp2(s1 - m), jnp.exp2(s2 - m), jnp.exp2(s3 - m)
    den = sum(e * pltpu.roll(st, LANES - A_HEADS, 1) for e, st in ((e1, s1), (e2, s2), (e3, s3)))
    inv = 1.0 / den
    valid = lax.broadcasted_iota(jnp.int32, (tm, LANES), 1) < A_HEADS
    packed = jnp.zeros((tm, LANES), F32)
    for gidx, e in enumerate((e1, e2, e3)):
        wgt = jnp.where(valid, e * inv, 0.0)
        hi = wgt.astype(BF16).astype(F32)
        for term, val in enumerate((hi, wgt - hi)):
            shift = 2 * A_HEADS * gidx + A_HEADS * term
            packed = packed + (pltpu.roll(val, shift, 1) if shift else val)
    wall = _dot(packed.astype(BF16), sp_ref[...])
    for blk in range(tm // A_REGROUP):
        rows = slice(blk * A_REGROUP, (blk + 1) * A_REGROUP)
        y = (wall[rows, :D_MODEL] * o1_ref[0, rows, :].astype(F32)
             + wall[rows, D_MODEL:2 * D_MODEL] * token_order(o2_ref, pm2_ref, blk)
             + wall[rows, 2 * D_MODEL:] * token_order(o3_ref, pm3_ref, blk))
        y_ref[rows, :] = y.astype(BF16)
    return x_ref[...] + _dot(y_ref[...], w_ref[...])


def _a_out_mixer(outs, stats, w):
    col = np.arange(A_N_GROUPS * D_MODEL)
    row = np.arange(LANES)
    spread = jnp.asarray((row[:, None] // (2 * A_HEADS) == col[None, :] // D_MODEL)
                         & (row[:, None] % A_HEADS == (col[None, :] % D_MODEL) // A_HEAD_DIM)
                         & (row[:, None] < 2 * A_HEADS * A_N_GROUPS), dtype=BF16)

    def regroup_matrix(dil):
        t = np.arange(A_REGROUP)
        src = (t % dil) * (A_REGROUP // dil) + t // dil
        return jnp.asarray(np.arange(A_REGROUP)[None, :] == src[:, None], dtype=BF16)

    perms = [regroup_matrix(dil) for _, dil in A_GROUPS[1:]]

    def specs(tm):
        def grouped(arr):
            dil, width = arr.shape[1], arr.shape[3]
            return pl.BlockSpec((None, dil, tm // dil, width), lambda bi, i: (bi, 0, i, 0))

        return ([grouped(a) for a in outs] + [grouped(a) for a in stats]
                + [_resident(w), _resident(spread)] + [_resident(p) for p in perms])

    def scratch(tm):
        return [pltpu.VMEM((tm, LANES), F32), pltpu.VMEM((tm, LANES), F32), pltpu.VMEM((tm, D_MODEL), BF16)]

    return (_a_out_stage, [*outs, *stats, _arr(w), spread, *perms], specs, scratch, "a_out_xattn")


def _rope_tables(seq_len):
    inv = ROPE_THETA ** (-jnp.arange(0, A_HEAD_DIM, 2, dtype=F32) / A_HEAD_DIM)
    ang = jnp.arange(seq_len, dtype=F32)[:, None] * inv[None, :]
    cos, sin = jnp.cos(ang), jnp.sin(ang)
    reps = LANES // A_HEAD_DIM
    return (jnp.concatenate([cos, cos] * reps, axis=1),
            jnp.concatenate([-sin, sin] * reps, axis=1))


def _mixer_a(x, g, w_in, w_out):
    b, s, _ = x.shape
    cos_t, sin_t = _rope_tables(s)
    outs, stats = [], []
    for qkv in _a_in(x, g, w_in, cos_t, sin_t):
        o, l = _attn_group(qkv)
        outs.append(o)
        stats.append(l)
    return _a_out_mixer(outs, stats, w_out)


HALO = 8
HY_COL_CHUNK = 256


def _hy_in_body(xp_ref, x_ref, xn_ref, g_ref, w_ref, cw_ref, cb_ref, x0_ref, vv_ref, u_ref, *, n_tiles):
    tm = x_ref.shape[0]
    i = pl.program_id(1)
    xe = jnp.concatenate([xp_ref[...], x_ref[...], xn_ref[...]], axis=0)
    xn = _rms(xe, g_ref[...]).astype(BF16)
    row = lax.broadcasted_iota(jnp.int32, (tm + 2 * HALO, 1), 0)
    inside = ((row >= HALO) | (i > 0)) & ((row < tm + HALO) | (i < n_tiles - 1))
    cw = HY_COL_CHUNK
    for c in range(D_MODEL // cw):
        parts = []
        for k in range(3):
            cols = slice(k * D_MODEL + c * cw, k * D_MODEL + (c + 1) * cw)
            u_ref[k] = jnp.where(inside, _dot(xn, w_ref[:, cols]), 0.0)
            parts.append(u_ref[k, pl.ds(HALO - 1, tm), :] * cw_ref[0:1, cols]
                         + u_ref[k, pl.ds(HALO, tm), :] * cw_ref[1:2, cols]
                         + u_ref[k, pl.ds(HALO + 1, tm), :] * cw_ref[2:3, cols]
                         + cb_ref[:, cols])
        x0_ref[:, c * cw:(c + 1) * cw] = parts[0].astype(BF16)
        vv_ref[:, c * cw:(c + 1) * cw] = (parts[2] * parts[1]).astype(BF16)


def _hy_in(x, g, w, conv_w, conv_b):
    b, s, _ = x.shape
    tm = TM_HY_IN
    n_tiles = s // tm
    per = tm // HALO
    n_halo = s // HALO
    tok = pl.BlockSpec((None, tm, D_MODEL), lambda bi, i: (bi, i, 0))
    return pl.pallas_call(
        functools.partial(_hy_in_body, n_tiles=n_tiles),
        grid=(b, n_tiles),
        in_specs=[
            pl.BlockSpec((None, HALO, D_MODEL), lambda bi, i: (bi, jnp.maximum(i * per - 1, 0), 0)),
            tok,
            pl.BlockSpec((None, HALO, D_MODEL), lambda bi, i: (bi, jnp.minimum((i + 1) * per, n_halo - 1), 0)),
            pl.BlockSpec((1, D_MODEL), lambda bi, i: (0, 0)),
            _resident(w),
            pl.BlockSpec(conv_w.shape, lambda bi, i: (0, 0)),
            pl.BlockSpec(conv_b.shape, lambda bi, i: (0, 0)),
        ],
        out_specs=[tok, tok],
        out_shape=[jax.ShapeDtypeStruct(x.shape, BF16), jax.ShapeDtypeStruct(x.shape, BF16)],
        scratch_shapes=[pltpu.VMEM((3, tm + 2 * HALO, HY_COL_CHUNK), F32)],
        compiler_params=_params("parallel", "parallel"),
        name="hy_in",
    )(x, x, x, g, _arr(w), conv_w, conv_b)


def _hdot(a, b):
    return jnp.dot(a, b, precision=lax.Precision.HIGHEST, preferred_element_type=F32)


def _hy_filter_body(z_ref, t_ref, a_ref, b_ref, w1_ref, b1_ref, w2_ref, b2_ref, w3_ref, b3_ref,
                    wo_ref, fr_ref, dl_ref, h_ref, sum_ref):
    fr = fr_ref[...]
    hid = jnp.sin(fr * (_hdot(z_ref[...], w1_ref[...]) + b1_ref[...]))
    hid = jnp.sin(fr * (_hdot(hid, w2_ref[...]) + b2_ref[...]))
    hid = jnp.sin(fr * (_hdot(hid, w3_ref[...]) + b3_ref[...]))
    decay = jnp.exp(-t_ref[...] * dl_ref[...])
    hid = hid.astype(BF16)
    h_fwd = _dot(hid, wo_ref[:, :D_MODEL]) * decay
    h_bwd = _dot(hid, wo_ref[:, D_MODEL:]) * decay
    h = a_ref[...] * h_fwd + b_ref[...] * h_bwd
    h_ref[...] = h

    @pl.when(pl.program_id(0) == 0)
    def _():
        sum_ref[...] = jnp.zeros_like(sum_ref)

    sum_ref[...] += jnp.sum(jnp.abs(h), axis=0, keepdims=True)


def _hy_filter(seq_len, f_w1, f_b1, f_w2, f_b2, f_w3, f_b3, f_wout, f_freq):
    n = 2 * seq_len
    src = np.concatenate([np.arange(seq_len), [0], np.arange(seq_len - 1, 0, -1)])
    lag = jnp.asarray(src, dtype=F32)[:, None]
    t2 = lag / (seq_len - 1)
    w = 2.0 * math.pi * lag / seq_len
    f = jnp.linspace(1e-4, HY_BANDS - 1, HY_BANDS, dtype=F32)[None, :]
    z2 = jnp.concatenate([t2, jnp.cos(f * w), -jnp.sin(f * w),
                          jnp.zeros((n, HY_HID_PAD - HY_EMB), F32)], axis=-1)
    pos = np.arange(n)
    use_fwd = (pos < seq_len).astype(np.float32)[:, None]
    use_bwd = ((pos == 0) | (pos > seq_len)).astype(np.float32)[:, None]
    max_decay = math.log(HY_DECAY_TARGET) / HY_DECAY_STRONG_PCT
    min_decay = math.log(HY_DECAY_TARGET) / HY_DECAY_WEAK_PCT
    deltas = jnp.abs(jnp.linspace(min_decay, max_decay, D_MODEL, dtype=F32))[None, :]

    def pad2(m, rows):
        return jnp.pad(m, ((0, rows - m.shape[0]), (0, HY_HID_PAD - m.shape[1])))

    def padv(v):
        return jnp.pad(v, (0, HY_HID_PAD - v.shape[0]))[None, :]

    wo = jnp.pad(f_wout, ((0, HY_HID_PAD - f_wout.shape[0]), (0, 0))).astype(BF16)
    tr = TR_FILTER
    rowblk = lambda width: pl.BlockSpec((tr, width), lambda i: (i, 0))
    full = lambda shape: pl.BlockSpec(shape, lambda i: (0, 0))
    sq = (HY_HID_PAD, HY_HID_PAD)
    vec = (1, HY_HID_PAD)
    return pl.pallas_call(
        _hy_filter_body,
        grid=(n // tr,),
        in_specs=[rowblk(HY_HID_PAD), rowblk(1), rowblk(1), rowblk(1),
                  full(sq), full(vec), full(sq), full(vec), full(sq), full(vec),
                  full(wo.shape), full(vec), full((1, D_MODEL))],
        out_specs=[rowblk(D_MODEL), full((1, D_MODEL))],
        out_shape=[jax.ShapeDtypeStruct((n, D_MODEL), F32), jax.ShapeDtypeStruct((1, D_MODEL), F32)],
        compiler_params=_params("arbitrary"),
        name="hy_filter",
    )(z2, t2, jnp.asarray(use_fwd), jnp.asarray(use_bwd),
      pad2(f_w1, HY_HID_PAD), padv(f_b1), pad2(f_w2, HY_HID_PAD), padv(f_b2),
      pad2(f_w3, HY_HID_PAD), padv(f_b3), wo, padv(f_freq), deltas)


def _fft_split(n):
    n1 = 1 << ((n.bit_length() - 1 + 1) // 2)
    return n1, n // n1


def _fft_tables(n1, n2):
    n = n1 * n2
    h = n1 // 2
    idx = np.arange(n1)
    ang = -2.0 * np.pi * ((idx[:, None] * idx[None, :]) % n1) / n1
    fr, fi = np.cos(ang), np.sin(ang)
    m1_data = np.block([[fr[:, :h], -fi[:, :h]], [fi[:, :h], fr[:, :h]]])
    m1_filt = np.concatenate([fr, fi], axis=0)
    ifr, ifi = fr.T[:h] / n, -fi.T[:h] / n
    m3 = np.block([[ifr, -ifi], [ifi, ifr]])
    k1 = jnp.arange(n1, dtype=jnp.int32)[:, None, None]
    k2 = jnp.arange(n2, dtype=jnp.int32)[None, :, None]
    i2 = jnp.arange(n2, dtype=jnp.int32)[None, None, :]
    phase = (i2 * k1 + n1 * i2 * k2) % n
    ga = (-2.0 * math.pi / n) * phase.astype(F32)
    gr, gi = jnp.cos(ga), jnp.sin(ga)
    g_fwd = jnp.concatenate([jnp.concatenate([gr, -gi], axis=2), jnp.concatenate([gi, gr], axis=2)], axis=1)
    grt, git = jnp.swapaxes(gr, 1, 2), jnp.swapaxes(gi, 1, 2)
    g_inv = jnp.concatenate([jnp.concatenate([grt, git], axis=2), jnp.concatenate([-git, grt], axis=2)], axis=1)
    as_bf = lambda m: jnp.asarray(m, dtype=F32).astype(BF16)
    return as_bf(m1_data), as_bf(m1_filt), as_bf(m3), g_fwd.astype(BF16), g_inv.astype(BF16)


FFT_SUB = 16
FFT_TC = 512


def _dft_outer_body(m_ref, pm_ref, z_ref, o_ref, zs_ref, rb_ref):
    q, sub, tc = z_ref.shape
    r = m_ref.shape[0]
    nlb = tc // LANES
    for cb in range(nlb):
        zs_ref[cb] = z_ref[:, :, cb * LANES:(cb + 1) * LANES].astype(F32).reshape(q * sub, LANES)
    hw = tc // 2
    n_groups = r // sub

    def transform(hf, j):
        z = jnp.concatenate([zs_ref[cb, pl.ds(j, q, stride=sub), :]
                             for cb in range(hf * nlb // 2, (hf + 1) * nlb // 2)], axis=1)
        rb_ref[hf, j] = _dot(m_ref[...], z.astype(BF16)).astype(BF16)

    def regroup(hf, g):
        part, i0 = divmod(g * sub, r // 2)
        grouped = rb_ref[hf, :, g * sub:(g + 1) * sub, :].reshape(sub * sub, hw)
        o_ref[part, i0:i0 + sub, :, hf * hw:(hf + 1) * hw] = (
            _dot(pm_ref[...], grouped).reshape(sub, sub, hw).astype(BF16))

    for j in range(sub):
        transform(0, j)
    for step in range(max(sub, n_groups)):
        if step < sub:
            transform(1, step)
        if step < n_groups:
            regroup(0, step)
    for g in range(n_groups):
        regroup(1, g)


def _dft_outer(mat, perm, z5, name):
    p, q, nh, sub, c = z5.shape
    r = mat.shape[0]
    tc = FFT_TC
    return pl.pallas_call(
        _dft_outer_body,
        grid=(p, nh, c // tc),
        in_specs=[pl.BlockSpec(mat.shape, lambda pi, h, ci: (0, 0)),
                  pl.BlockSpec(perm.shape, lambda pi, h, ci: (0, 0)),
                  pl.BlockSpec((None, q, None, sub, tc), lambda pi, h, ci: (pi, 0, h, 0, ci))],
        out_specs=pl.BlockSpec((None, None, 2, r // 2, sub, tc), lambda pi, h, ci: (pi, h, 0, 0, 0, ci)),
        out_shape=jax.ShapeDtypeStruct((p, nh, 2, r // 2, sub, c), BF16),
        scratch_shapes=[pltpu.VMEM((tc // LANES, q * sub, LANES), F32),
                        pltpu.VMEM((2, sub, r, tc // 2), BF16)],
        compiler_params=_params("parallel", "parallel", "parallel"),
        name=name,
    )(mat, perm, z5)


def _idft_outer_body(m_ref, pm_ref, b_ref, o_ref, tb_ref, ys_ref):
    _, n1, sub, tc = b_ref.shape
    q = m_ref.shape[0]
    nlb = tc // LANES
    hw = tc // 2
    n_groups = 2 * n1 // sub

    def regroup(hf, g):
        part, i0 = divmod(g * sub, n1)
        grouped = b_ref[part, i0:i0 + sub, :, hf * hw:(hf + 1) * hw].reshape(sub * sub, hw)
        tb_ref[hf, :, g * sub:(g + 1) * sub, :] = _dot(pm_ref[...], grouped).reshape(sub, sub, hw).astype(BF16)

    def transform(hf, j):
        res = _dot(m_ref[...], tb_ref[hf, j])
        for c in range(nlb // 2):
            ys_ref[hf * nlb // 2 + c, pl.ds(j, q, stride=sub), :] = res[:, c * LANES:(c + 1) * LANES]

    for g in range(n_groups):
        regroup(0, g)
    for step in range(max(sub, n_groups)):
        if step < n_groups:
            regroup(1, step)
        if step < sub:
            transform(0, step)
    for j in range(sub):
        transform(1, j)
    for cb in range(nlb):
        o_ref[:, :, cb * LANES:(cb + 1) * LANES] = ys_ref[cb].reshape(q, sub, LANES)


def _idft_outer(mat, perm, b6):
    p, nh, _, n1, sub, c = b6.shape
    q = mat.shape[0]
    tc = FFT_TC
    return pl.pallas_call(
        _idft_outer_body,
        grid=(p, nh, c // tc),
        in_specs=[pl.BlockSpec(mat.shape, lambda pi, h, ci: (0, 0)),
                  pl.BlockSpec(perm.shape, lambda pi, h, ci: (0, 0)),
                  pl.BlockSpec((None, None, 2, n1, sub, tc), lambda pi, h, ci: (pi, h, 0, 0, 0, ci))],
        out_specs=pl.BlockSpec((None, q, None, sub, tc), lambda pi, h, ci: (pi, 0, h, 0, ci)),
        out_shape=jax.ShapeDtypeStruct((p, q, nh, sub, c), F32),
        scratch_shapes=[pltpu.VMEM((2, sub, 2 * n1, tc // 2), BF16),
                        pltpu.VMEM((tc // LANES, q * sub, LANES), F32)],
        compiler_params=_params("parallel", "parallel", "parallel"),
        name="hy_idft",
    )(mat, perm, b6)


FFT_KB = 4


def _stack_re_im(a_ref, kk):
    nh, _, _, sub, ct = a_ref.shape
    return jnp.concatenate([a_ref[:, 0, kk].reshape(nh * sub, ct), a_ref[:, 1, kk].reshape(nh * sub, ct)], axis=0)


def _spec_filter_body(g_ref, a_ref, sc_ref, h_ref):
    n2 = g_ref.shape[1] // 2
    for kk in range(g_ref.shape[0]):
        spec = _dot(g_ref[kk], _stack_re_im(a_ref, kk)) * sc_ref[...]
        h_ref[kk] = spec.reshape(2, n2, spec.shape[1])


def _spec_filter(g_fwd, a6, scale):
    _, nh, _, n1, sub, c = a6.shape
    n2 = nh * sub
    return pl.pallas_call(
        _spec_filter_body,
        grid=(n1 // FFT_KB,),
        in_specs=[pl.BlockSpec((FFT_KB, 2 * n2, 2 * n2), lambda k: (k, 0, 0)),
                  pl.BlockSpec((None, nh, 2, FFT_KB, sub, c), lambda k: (0, 0, 0, k, 0, 0)),
                  pl.BlockSpec((1, c), lambda k: (0, 0))],
        out_specs=pl.BlockSpec((FFT_KB, 2, n2, c), lambda k: (k, 0, 0, 0)),
        out_shape=jax.ShapeDtypeStruct((n1, 2, n2, c), F32),
        compiler_params=_params("parallel"),
        name="hy_spec_filter",
    )(g_fwd, a6, scale)


def _spec_body(gf_ref, gi_ref, h_ref, a_ref, o_ref):
    nh, _, kb, sub, ct = a_ref.shape
    n2 = nh * sub

    def forward(kk):
        return _dot(gf_ref[kk], _stack_re_im(a_ref, kk))

    spec_next = forward(0)
    for kk in range(kb):
        spec = spec_next
        if kk + 1 < kb:
            spec_next = forward(kk + 1)
        xr, xi = spec[:n2], spec[n2:]
        hr, hi = h_ref[kk, 0], h_ref[kk, 1]
        y = jnp.concatenate([xr * hr - xi * hi, xr * hi + xi * hr], axis=0).astype(BF16)
        back = _dot(gi_ref[kk], y)
        o_ref[:, 0, kk] = back[:n2].reshape(nh, sub, ct).astype(BF16)
        o_ref[:, 1, kk] = back[n2:].reshape(nh, sub, ct).astype(BF16)


def _spec(g_fwd, g_inv, hspec, a6):
    p, nh, _, n1, sub, c = a6.shape
    n2 = nh * sub
    blk = pl.BlockSpec((None, nh, 2, FFT_KB, sub, c), lambda k, pi: (pi, 0, 0, k, 0, 0))
    mat = pl.BlockSpec((FFT_KB, 2 * n2, 2 * n2), lambda k, pi: (k, 0, 0))
    return pl.pallas_call(
        _spec_body,
        grid=(n1 // FFT_KB, p),
        in_specs=[mat, mat, pl.BlockSpec((FFT_KB, 2, n2, c), lambda k, pi: (k, 0, 0, 0)), blk],
        out_specs=blk,
        out_shape=jax.ShapeDtypeStruct(a6.shape, BF16),
        compiler_params=_params("parallel", "parallel"),
        name="hy_spec",
    )(g_fwd, g_inv, hspec, a6)


def _long_conv(vv, h_raw, h_norm):
    b, l, c = vv.shape
    n1, n2 = _fft_split(2 * l)
    nh = n2 // FFT_SUB
    m1_data, m1_filt, m3, g_fwd, g_inv = _fft_tables(n1, n2)
    idx = np.arange(FFT_SUB * FFT_SUB)
    swapped = (idx % FFT_SUB) * FFT_SUB + idx // FFT_SUB
    perm = jnp.asarray(idx[None, :] == swapped[:, None], dtype=BF16)
    a_h = _dft_outer(m1_filt, perm, h_raw.reshape(1, n1, nh, FFT_SUB, c), "hy_dft1_filter")
    hspec = _spec_filter(g_fwd, a_h, 1.0 / h_norm)
    a = _dft_outer(m1_data, perm, vv.reshape(b // 2, n1, nh, FFT_SUB, c), "hy_dft1")
    y = _idft_outer(m3, perm, _spec(g_fwd, g_inv, hspec, a))
    return y.reshape(b, l, c)


def _hy_out_stage(cv_ref, vv_ref, x0_ref, bd_ref, w_ref, x_ref):
    y = ((cv_ref[...] + bd_ref[...] * vv_ref[...].astype(F32)) * x0_ref[...].astype(F32)).astype(BF16)
    return x_ref[...] + _dot(y, w_ref[...])


def _mixer_b(x, g, w_in, conv_w, conv_b, filt, bias_d, w_out):
    _, s, _ = x.shape
    x0, vv = _hy_in(x, g, w_in, conv_w, conv_b)
    h_raw, h_norm = _hy_filter(s, *filt)
    conv = _long_conv(vv, h_raw, h_norm)

    def specs(tm):
        tok = pl.BlockSpec((None, tm, D_MODEL), lambda bi, i: (bi, i, 0))
        return [tok, tok, tok, pl.BlockSpec((1, D_MODEL), lambda bi, i: (0, 0)), _resident(w_out)]

    return (_hy_out_stage, [conv, vv, x0, bias_d, _arr(w_out)], specs, lambda tm: [], "hy_out_xattn")


def _gelu(z):
    return 0.5 * z * (1.0 + lax.erf(z * (2.0 ** -0.5)))


def _sgu_body(x_ref, g_ref, win_ref, lng_ref, lnb_ref, ws_ref, bs_ref, wout_ref, o_ref, y_ref, z_ref):
    tm = x_ref.shape[0]
    x = x_ref[...]
    xn = _rms(x, g_ref[...]).astype(BF16)
    for c in range(2 * D_MODEL // FF_CHUNK):
        cols = slice(c * FF_CHUNK, (c + 1) * FF_CHUNK)
        z_ref[:, cols] = _gelu(_dot(xn, win_ref[:, cols]))
    zv = z_ref[:, D_MODEL:]
    zc = zv - jnp.mean(zv, axis=-1, keepdims=True)
    zv = zc * lax.rsqrt(jnp.mean(zc * zc, axis=-1, keepdims=True) + EPS) * lng_ref[...] + lnb_ref[...]
    zvb = zv.astype(BF16)
    n_chunks = tm // C_CHUNK
    for h in range(C_GROUPS):
        cols = slice(h * LANES, (h + 1) * LANES)
        v_all = jnp.concatenate([zvb[c * C_CHUNK:(c + 1) * C_CHUNK, cols] for c in range(n_chunks)], axis=1)
        sv = _dot(ws_ref[h], v_all) + bs_ref[:, h:h + 1]
        for c in range(n_chunks):
            rows = slice(c * C_CHUNK, (c + 1) * C_CHUNK)
            y_ref[rows, cols] = (z_ref[rows, cols] * sv[:, c * LANES:(c + 1) * LANES]).astype(BF16)
    o_ref[...] = x + _dot(y_ref[...], wout_ref[...])


def _mixer_c(x, g, w_in, ln_g, ln_b, w_s, b_s_t, w_out):
    b, s, _ = x.shape
    t = b * s
    tm = TM_SGU
    tok = pl.BlockSpec((tm, D_MODEL), lambda i: (i, 0))
    row = pl.BlockSpec((1, D_MODEL), lambda i: (0, 0))
    return pl.pallas_call(
        _sgu_body,
        grid=(t // tm,),
        in_specs=[tok, row, _resident(w_in), row, row, _resident(w_s),
                  pl.BlockSpec(b_s_t.shape, lambda i: (0, 0)), _resident(w_out)],
        out_specs=tok,
        out_shape=jax.ShapeDtypeStruct((t, D_MODEL), F32),
        scratch_shapes=[pltpu.VMEM((tm, D_MODEL), BF16), pltpu.VMEM((tm, 2 * D_MODEL), F32)],
        compiler_params=_params("parallel"),
        name="sgu",
    )(x.reshape(t, D_MODEL), g, _arr(w_in), ln_g, ln_b, _arr(w_s), b_s_t, _arr(w_out)).reshape(x.shape)


def _trunk(x, kvs, boff, w):
    b, s, _ = x.shape
    t = b * s
    for i in range(DEPTH):
        kind, j = i % N_MIXERS, i // N_MIXERS
        g = w["g_mix"][i][None, :]
        mixer_out = None
        if kind == 0:
            mixer_out = _mixer_a(x, g, w["a_w_in"][j], w["a_w_out"][j])
        elif kind == 1:
            filt = tuple(w[k][j] for k in ("b_f_w1", "b_f_b1", "b_f_w2", "b_f_b2", "b_f_w3", "b_f_b3",
                                          "b_f_wout", "b_f_freq"))
            mixer_out = _mixer_b(x, g, w["b_w_in"][j], w["b_conv_w"][j], w["b_conv_b"][j][None, :], filt,
                                 w["b_bias_d"][j][None, :], w["b_w_out"][j])
        else:
            x = _mixer_c(x, g, w["c_w_in"][j], w["c_ln_g"][j][None, :], w["c_ln_b"][j][None, :],
                         w["c_w_s"][j], w["c_b_s"][j].T, w["c_w_out"][j])
        x = _xattn(x, w["g_cross"][i][None, :], w["x_w_q"][i], kvs[i], boff, w["x_w_o"][i], mixer_out)
        g_final = w["g_final"][None, :] if i == DEPTH - 1 else None
        x = _ffn(x.reshape(t, D_MODEL), w["g_ffn"][i][None, :], w["f_w_gu"][i], w["f_w_down"][i],
                 g_final).reshape(b, s, D_MODEL)
    return x


_BF16_WEIGHTS = ("a_w_in", "a_w_out", "b_w_in", "b_w_out", "c_w_in", "c_w_s", "c_w_out",
                 "x_w_q", "x_w_kv", "x_w_o", "f_w_gu", "f_w_down")


def kernel(x_prompt, x_sample, mem_prompt, mem_sample, g_mix, g_cross, g_ffn, g_final, a_w_in, a_w_out, b_w_in, b_conv_w, b_conv_b, b_f_w1, b_f_b1, b_f_w2, b_f_b2, b_f_w3, b_f_b3, b_f_wout, b_f_freq, b_bias_d, b_w_out, c_w_in, c_ln_g, c_ln_b, c_w_s, c_b_s, c_w_out, x_w_q, x_w_kv, x_w_o, f_w_gu, f_w_down):
    w = dict(g_mix=g_mix, g_cross=g_cross, g_ffn=g_ffn, g_final=g_final,
             a_w_in=a_w_in, a_w_out=a_w_out, b_w_in=b_w_in, b_conv_w=b_conv_w, b_conv_b=b_conv_b,
             b_f_w1=b_f_w1, b_f_b1=b_f_b1, b_f_w2=b_f_w2, b_f_b2=b_f_b2, b_f_w3=b_f_w3,
             b_f_b3=b_f_b3, b_f_wout=b_f_wout, b_f_freq=b_f_freq, b_bias_d=b_bias_d,
             b_w_out=b_w_out, c_w_in=c_w_in, c_ln_g=c_ln_g, c_ln_b=c_ln_b, c_w_s=c_w_s,
             c_b_s=c_b_s, c_w_out=c_w_out, x_w_q=x_w_q, x_w_kv=x_w_kv, x_w_o=x_w_o,
             f_w_gu=f_w_gu, f_w_down=f_w_down)
    for name in _BF16_WEIGHTS:
        stack = w[name].astype(BF16)
        w[name] = [(stack, j) for j in range(stack.shape[0])]
    nb_prompt = mem_prompt.shape[0]
    mem = jnp.concatenate([mem_prompt, mem_sample], axis=0)
    mem2 = mem.reshape(mem.shape[0] * MEM_LEN, D_MODEL)
    kvs = [_kv_proj(mem2, w["x_w_kv"][i]).reshape(mem.shape[0], MEM_LEN, 2 * D_MODEL) for i in range(DEPTH)]
    y_prompt = _trunk(x_prompt, kvs, 0, w)
    y_sample = _trunk(x_sample, kvs, nb_prompt, w)
    return (y_prompt, y_sample)
```

```python
import functools
import math

import numpy as np
import jax
import jax.numpy as jnp
from jax import lax
from jax.experimental import pallas as pl
from jax.experimental.pallas import tpu as pltpu

F32 = jnp.float32
BF16 = jnp.bfloat16

D_MODEL = 1024
DEPTH = 4
N_MIXERS = 3

A_GROUPS = ((128, 1), (512, 4), (2048, 16))
A_N_GROUPS = len(A_GROUPS)
A_HEADS = 16
A_HEAD_DIM = D_MODEL // A_HEADS
A_HALF = 64
A_SUBTILE = 128
A_QUERIES_PER_STEP = 1024
A_Q_SCALE = A_HEAD_DIM ** -0.5 * math.log2(math.e)
A_REGROUP = 256
A_STEP = 4
ROPE_THETA = 10000.0

HY_EMB = 33
HY_BANDS = (HY_EMB - 1) // 2
HY_HID_PAD = 128
HY_DECAY_TARGET = 1e-2
HY_DECAY_STRONG_PCT = 0.3
HY_DECAY_WEAK_PCT = 1.5

C_CHUNK = 128
C_GROUPS = 8

MEM_LEN = 256
X_HEADS = 4
X_HEAD_DIM = D_MODEL // X_HEADS

D_FF = -(-8 * D_MODEL // (3 * 256)) * 256
FF_CHUNK = 256

EPS = 1e-6
NEG_INF = -1e30

LANES = 128
VMEM_LIMIT = 56 * 1024 * 1024

TM_FFN = 1024
TM_XATTN = 1024
TM_MIXER_XATTN = 512
TM_A_IN = 512
TM_HY_IN = 512
TM_SGU = 512
TM_KV = 256
TR_FILTER = 512


def _params(*sem):
    return pltpu.CompilerParams(dimension_semantics=sem, vmem_limit_bytes=VMEM_LIMIT)


def _resident(w):
    if isinstance(w, tuple):
        stack, layer = w
        nd = stack.ndim - 1
        return pl.BlockSpec((None,) + stack.shape[1:], lambda *_: (layer,) + (0,) * nd,
                            pipeline_mode=pl.Buffered(1))
    nd = w.ndim
    return pl.BlockSpec(w.shape, lambda *_: (0,) * nd, pipeline_mode=pl.Buffered(1))


def _arr(w):
    return w[0] if isinstance(w, tuple) else w


def _dot(a, b):
    return jnp.dot(a, b, preferred_element_type=F32)


def _dot_nt(a, b):
    return lax.dot_general(a, b, (((1,), (1,)), ((), ())), preferred_element_type=F32)


def _rms(x, g):
    return x * lax.rsqrt(jnp.mean(x * x, axis=-1, keepdims=True) + EPS) * g


def _ffn_body(*refs, final):
    if final:
        x_ref, g_ref, wgu_ref, wd_ref, gf_ref, o_ref, act_ref = refs
    else:
        x_ref, g_ref, wgu_ref, wd_ref, o_ref, act_ref = refs
    x = x_ref[...]
    xn = _rms(x, g_ref[...]).astype(BF16)
    for c in range(D_FF // FF_CHUNK):
        lo = c * FF_CHUNK
        gate = _dot(xn, wgu_ref[:, lo:lo + FF_CHUNK])
        up = _dot(xn, wgu_ref[:, D_FF + lo:D_FF + lo + FF_CHUNK])
        act_ref[:, lo:lo + FF_CHUNK] = (gate / (1.0 + jnp.exp(-gate)) * up).astype(BF16)
    y = x + _dot(act_ref[...], wd_ref[...])
    if final:
        y = _rms(y, gf_ref[...])
    o_ref[...] = y


def _ffn(x2, g, wgu, wd, g_final=None):
    t = x2.shape[0]
    tm = TM_FFN
    final = g_final is not None
    tok = pl.BlockSpec((tm, D_MODEL), lambda i: (i, 0))
    row = pl.BlockSpec((1, D_MODEL), lambda i: (0, 0))
    in_specs = [tok, row, _resident(wgu), _resident(wd)]
    args = [x2, g, _arr(wgu), _arr(wd)]
    if final:
        in_specs.append(row)
        args.append(g_final)
    return pl.pallas_call(
        functools.partial(_ffn_body, final=final),
        grid=(t // tm,),
        in_specs=in_specs,
        out_specs=tok,
        out_shape=jax.ShapeDtypeStruct((t, D_MODEL), F32),
        scratch_shapes=[pltpu.VMEM((tm, D_FF), BF16)],
        compiler_params=_params("parallel"),
        name="ffn_final" if final else "ffn",
    )(*args)


def _kv_body(m_ref, w_ref, o_ref):
    o_ref[...] = _dot(m_ref[...].astype(BF16), w_ref[...]).astype(BF16)


def _kv_proj(mem2, wkv):
    r = mem2.shape[0]
    tm = TM_KV
    return pl.pallas_call(
        _kv_body,
        grid=(r // tm,),
        in_specs=[pl.BlockSpec((tm, D_MODEL), lambda i: (i, 0)), _resident(wkv)],
        out_specs=pl.BlockSpec((tm, 2 * D_MODEL), lambda i: (i, 0)),
        out_shape=jax.ShapeDtypeStruct((r, 2 * D_MODEL), BF16),
        compiler_params=_params("parallel"),
        name="kv_proj",
    )(mem2, _arr(wkv))


def _xattn_stage(x, g_ref, wq_ref, k_ref, v_ref, wo_ref, y_ref):
    xn = _rms(x, g_ref[...]).astype(BF16)
    q = (_dot(xn, wq_ref[...]) * (X_HEAD_DIM ** -0.5)).astype(BF16)
    for h in range(X_HEADS):
        sl = slice(h * X_HEAD_DIM, (h + 1) * X_HEAD_DIM)
        s = _dot_nt(q[:, sl], k_ref[:, sl])
        p = jnp.exp(s - jnp.max(s, axis=-1, keepdims=True))
        inv = 1.0 / jnp.sum(p, axis=-1, keepdims=True)
        y_ref[:, sl] = (_dot(p.astype(BF16), v_ref[:, sl]) * inv).astype(BF16)
    return x + _dot(y_ref[...], wo_ref[...])


def _xattn_body(*refs, mixer_out, n_mixer_in, n_mixer_scratch):
    n_in = n_mixer_in + 6
    mixer_in, (x_ref, g_ref, wq_ref, k_ref, v_ref, wo_ref) = refs[:n_mixer_in], refs[n_mixer_in:n_in]
    o_ref = refs[n_in]
    mixer_scratch, y_ref = refs[n_in + 1:n_in + 1 + n_mixer_scratch], refs[n_in + 1 + n_mixer_scratch]
    x = x_ref[...] if mixer_out is None else mixer_out(*mixer_in, x_ref, *mixer_scratch)
    o_ref[...] = _xattn_stage(x, g_ref, wq_ref, k_ref, v_ref, wo_ref, y_ref)


def _xattn(x, g, wq, kv, boff, wo, mixer=None):
    b, s, _ = x.shape
    if mixer is None:
        stage, m_args, m_specs, m_scratch, tm, name = None, [], [], [], TM_XATTN, "xattn"
    else:
        stage, m_args, specs_fn, scratch_fn, name = mixer
        tm = TM_MIXER_XATTN
        m_specs, m_scratch = specs_fn(tm), scratch_fn(tm)
    tok = pl.BlockSpec((None, tm, D_MODEL), lambda bi, i: (bi, i, 0))
    return pl.pallas_call(
        functools.partial(_xattn_body, mixer_out=stage, n_mixer_in=len(m_args), n_mixer_scratch=len(m_scratch)),
        grid=(b, s // tm),
        in_specs=m_specs + [
            tok,
            pl.BlockSpec((1, D_MODEL), lambda bi, i: (0, 0)),
            _resident(wq),
            pl.BlockSpec((None, MEM_LEN, D_MODEL), lambda bi, i: (bi + boff, 0, 0)),
            pl.BlockSpec((None, MEM_LEN, D_MODEL), lambda bi, i: (bi + boff, 0, 1)),
            _resident(wo),
        ],
        out_specs=tok,
        out_shape=jax.ShapeDtypeStruct(x.shape, F32),
        scratch_shapes=m_scratch + [pltpu.VMEM((tm, D_MODEL), BF16)],
        compiler_params=_params("parallel", "parallel"),
        name=name,
    )(*m_args, x, g, _arr(wq), kv, kv, _arr(wo))


def _a_in_body(x_ref, g_ref, w_ref, cos_ref, sin_ref, *rest):
    o_refs, (xs_ref, xs2_ref, xp_ref) = rest[:A_N_GROUPS], rest[A_N_GROUPS:]
    tm = x_ref.shape[0]
    xn_f32 = _rms(x_ref[...], g_ref[...])
    reps = D_MODEL // LANES
    for cb in range(reps):
        xs_ref[cb] = xn_f32[:, cb * LANES:(cb + 1) * LANES]
    lane = lax.broadcasted_iota(jnp.int32, (tm, LANES), 1)
    low_half = (lane & (A_HEAD_DIM // 2)) == 0
    for gidx, (o_ref, (_, dil)) in enumerate(zip(o_refs, A_GROUPS)):
        rows = tm // dil
        if dil == 1:
            xn = xn_f32.astype(BF16)
            cos, sin = cos_ref[...], sin_ref[...]
        elif dil == A_STEP:
            for r in range(dil):
                for cb in range(reps):
                    piece = xs_ref[cb, pl.ds(r, rows, stride=dil), :]
                    xs2_ref[cb, r * rows:(r + 1) * rows, :] = piece
                    xp_ref[r * rows:(r + 1) * rows, cb * LANES:(cb + 1) * LANES] = piece.astype(BF16)
            xn = xp_ref[...]
        else:
            for r in range(dil):
                r_lo, r_hi = r % A_STEP, r // A_STEP
                for cb in range(reps):
                    xp_ref[r * rows:(r + 1) * rows, cb * LANES:(cb + 1) * LANES] = (
                        xs2_ref[cb, pl.ds(r_lo * (tm // A_STEP) + r_hi, rows, stride=A_STEP), :].astype(BF16))
            xn = xp_ref[...]
        if dil > 1:
            cos = jnp.concatenate([cos_ref[pl.ds(r, rows, stride=dil), :] for r in range(dil)], axis=0)
            sin = jnp.concatenate([sin_ref[pl.ds(r, rows, stride=dil), :] for r in range(dil)], axis=0)
        for part in range(3):
            lo = (gidx * 3 + part) * D_MODEL
            y = _dot(xn, w_ref[:, lo:lo + D_MODEL])
            if part == 2:
                o_ref[:, :, 2 * D_MODEL:] = y.reshape(dil, rows, D_MODEL).astype(BF16)
                continue
            c_t, s_t = (cos * A_Q_SCALE, sin * A_Q_SCALE) if part == 0 else (cos, sin)
            for cb in range(reps):
                yb = y[:, cb * LANES:(cb + 1) * LANES]
                partner = jnp.where(low_half, pltpu.roll(yb, LANES - A_HEAD_DIM // 2, 1),
                                    pltpu.roll(yb, A_HEAD_DIM // 2, 1))
                col = part * D_MODEL + cb * LANES
                o_ref[:, :, col:col + LANES] = (yb * c_t + partner * s_t).reshape(dil, rows, LANES).astype(BF16)


def _a_in(x, g, w, cos_t, sin_t):
    b, s, _ = x.shape
    tm = TM_A_IN
    return pl.pallas_call(
        _a_in_body,
        grid=(b, s // tm),
        in_specs=[
            pl.BlockSpec((None, tm, D_MODEL), lambda bi, i: (bi, i, 0)),
            pl.BlockSpec((1, D_MODEL), lambda bi, i: (0, 0)),
            _resident(w),
            pl.BlockSpec((tm, LANES), lambda bi, i: (i, 0)),
            pl.BlockSpec((tm, LANES), lambda bi, i: (i, 0)),
        ],
        out_specs=[pl.BlockSpec((None, dil, tm // dil, 3 * D_MODEL), lambda bi, i: (bi, 0, i, 0))
                   for _, dil in A_GROUPS],
        out_shape=[jax.ShapeDtypeStruct((b, dil, s // dil, 3 * D_MODEL), BF16) for _, dil in A_GROUPS],
        scratch_shapes=[pltpu.VMEM((D_MODEL // LANES, tm, LANES), F32), pltpu.VMEM((D_MODEL // LANES, tm, LANES), F32),
                        pltpu.VMEM((tm, D_MODEL), BF16)],
        compiler_params=_params("parallel", "parallel"),
        name="a_in",
    )(x, g, _arr(w), cos_t, sin_t)


def _attn_body(q_ref, kp_ref, k_ref, kn_ref, vp_ref, v_ref, vn_ref, o_ref, stat_ref, *, n_tiles):
    n_res, tq = q_ref.shape[0], q_ref.shape[1]
    sq = A_SUBTILE
    win = sq + 2 * A_HALF
    n_sub_tiles = tq // sq
    i = pl.program_id(2)
    qi = lax.broadcasted_iota(jnp.int32, (sq, win), 0)
    kj = lax.broadcasted_iota(jnp.int32, (sq, win), 1)
    band = (kj >= qi) & (kj <= qi + 2 * A_HALF)
    left = lax.broadcasted_iota(jnp.int32, (sq, LANES), 1) < A_HEAD_DIM
    for res, st in [(a, c) for a in range(n_res) for c in range(n_sub_tiles)]:
        if st == 0:
            kw = jnp.concatenate([kp_ref[res], k_ref[res], kn_ref[res]], axis=0)
            vw = jnp.concatenate([vp_ref[res], v_ref[res], vn_ref[res]], axis=0)
        q_res, o_res, stat_res = q_ref.at[res], o_ref.at[res], stat_ref.at[res]
        rows = slice(st * sq, (st + 1) * sq)
        mask = band
        if st == 0:
            mask = mask & (kj >= jnp.where(i == 0, A_HALF, 0))
        if st == n_sub_tiles - 1:
            mask = mask & (kj < jnp.where(i == n_tiles - 1, sq + A_HALF, win))
        mask2 = jnp.concatenate([mask, mask], axis=0)
        stat_res[rows, :] = jnp.zeros((sq, LANES), F32)

        def scores(hp, q_res=q_res, rows=rows, kw=kw, st=st):
            sl = slice(hp * LANES, (hp + 1) * LANES)
            q2 = q_res[rows, sl]
            zero = jnp.zeros_like(q2)
            qq = jnp.concatenate([jnp.where(left, q2, zero), jnp.where(left, zero, q2)], axis=0)
            return _dot_nt(qq, kw[st * sq:st * sq + win, sl])

        s_next = scores(0)
        for hp in range(A_HEADS // 2):
            sl = slice(hp * LANES, (hp + 1) * LANES)
            s = jnp.where(mask2, s_next, NEG_INF)
            if hp + 1 < A_HEADS // 2:
                s_next = scores(hp + 1)
            m = jnp.max(s, axis=-1, keepdims=True)
            p = jnp.exp2(s - m)
            l = jnp.sum(p, axis=-1, keepdims=True)
            r = _dot(p.astype(BF16), vw[st * sq:st * sq + win, sl])
            o_res[rows, sl] = jnp.where(left, r[:sq], r[sq:]).astype(BF16)
            for off, col in ((0, m), (A_HEADS, l)):
                stat_res[rows, off + 2 * hp:off + 2 * hp + 1] = col[:sq]
                stat_res[rows, off + 2 * hp + 1:off + 2 * hp + 2] = col[sq:]


def _attn_group(qkv):
    b, dil, n_sub, _ = qkv.shape
    tq = min(A_QUERIES_PER_STEP, n_sub)
    n_res = min(dil, A_QUERIES_PER_STEP // tq)
    n_tiles = n_sub // tq
    halo_per_tile = tq // A_HALF
    n_halo = n_sub // A_HALF

    def own(part):
        return pl.BlockSpec((None, n_res, tq, D_MODEL), lambda bi, r, i: (bi, r, i, part))

    def prev(part):
        return pl.BlockSpec((None, n_res, A_HALF, D_MODEL),
                            lambda bi, r, i: (bi, r, jnp.maximum(i * halo_per_tile - 1, 0), part))

    def nxt(part):
        return pl.BlockSpec((None, n_res, A_HALF, D_MODEL),
                            lambda bi, r, i: (bi, r, jnp.minimum((i + 1) * halo_per_tile, n_halo - 1), part))

    return pl.pallas_call(
        functools.partial(_attn_body, n_tiles=n_tiles),
        grid=(b, dil // n_res, n_tiles),
        in_specs=[own(0), prev(1), own(1), nxt(1), prev(2), own(2), nxt(2)],
        out_specs=[
            pl.BlockSpec((None, n_res, tq, D_MODEL), lambda bi, r, i: (bi, r, i, 0)),
            pl.BlockSpec((None, n_res, tq, LANES), lambda bi, r, i: (bi, r, i, 0)),
        ],
        out_shape=[
            jax.ShapeDtypeStruct((b, dil, n_sub, D_MODEL), BF16),
            jax.ShapeDtypeStruct((b, dil, n_sub, LANES), F32),
        ],
        compiler_params=_params("parallel", "parallel", "parallel"),
        name=f"attn_d{dil}",
    )(qkv, qkv, qkv, qkv, qkv, qkv, qkv)


def _a_out_stage(o1_ref, o2_ref, o3_ref, l1_ref, l2_ref, l3_ref, w_ref, sp_ref, pm2_ref, pm3_ref, x_ref,
                 ls2_ref, ls3_ref, y_ref):
    tm = x_ref.shape[0]
    for l_ref, ls_ref in ((l2_ref, ls2_ref), (l3_ref, ls3_ref)):
        dil = l_ref.shape[0]
        for r in range(dil):
            ls_ref[pl.ds(r, tm // dil, stride=dil), :] = l_ref[r]

    def token_order(o_ref, pm_ref, blk):
        dil = o_ref.shape[0]
        per = A_REGROUP // dil
        grouped = jnp.concatenate([o_ref[r, blk * per:(blk + 1) * per, :] for r in range(dil)], axis=0)
        return _dot(pm_ref[...], grouped)
    s1, s2, s3 = l1_ref[0], ls2_ref[...], ls3_ref[...]
    m = jnp.maximum(jnp.maximum(s1, s2), s3)
    e1, e2, e3 = jnp.exp2(s1 - m), jnp.exp2(s2 - m), jnp.exp2(s3 - m)
    den = sum(e * pltpu.roll(st, LANES - A_HEADS, 1) for e, st in ((e1, s1), (e2, s2), (e3, s3)))
    inv = 1.0 / den
    valid = lax.broadcasted_iota(jnp.int32, (tm, LANES), 1) < A_HEADS
    packed = jnp.zeros((tm, LANES), F32)
    for gidx, e in enumerate((e1, e2, e3)):
        wgt = jnp.where(valid, e * inv, 0.0)
        hi = wgt.astype(BF16).astype(F32)
        for term, val in enumerate((hi, wgt - hi)):
            shift = 2 * A_HEADS * gidx + A_HEADS * term
            packed = packed + (pltpu.roll(val, shift, 1) if shift else val)
    wall = _dot(packed.astype(BF16), sp_ref[...])
    for blk in range(tm // A_REGROUP):
        rows = slice(blk * A_REGROUP, (blk + 1) * A_REGROUP)
        y = (wall[rows, :D_MODEL] * o1_ref[0, rows, :].astype(F32)
             + wall[rows, D_MODEL:2 * D_MODEL] * token_order(o2_ref, pm2_ref, blk)
             + wall[rows, 2 * D_MODEL:] * token_order(o3_ref, pm3_ref, blk))
        y_ref[rows, :] = y.astype(BF16)
    return x_ref[...] + _dot(y_ref[...], w_ref[...])


def _a_out_mixer(outs, stats, w):
    col = np.arange(A_N_GROUPS * D_MODEL)
    row = np.arange(LANES)
    spread = jnp.asarray((row[:, None] // (2 * A_HEADS) == col[None, :] // D_MODEL)
                         & (row[:, None] % A_HEADS == (col[None, :] % D_MODEL) // A_HEAD_DIM)
                         & (row[:, None] < 2 * A_HEADS * A_N_GROUPS), dtype=BF16)

    def regroup_matrix(dil):
        t = np.arange(A_REGROUP)
        src = (t % dil) * (A_REGROUP // dil) + t // dil
        return jnp.asarray(np.arange(A_REGROUP)[None, :] == src[:, None], dtype=BF16)

    perms = [regroup_matrix(dil) for _, dil in A_GROUPS[1:]]

    def specs(tm):
        def grouped(arr):
            dil, width = arr.shape[1], arr.shape[3]
            return pl.BlockSpec((None, dil, tm // dil, width), lambda bi, i: (bi, 0, i, 0))

        return ([grouped(a) for a in outs] + [grouped(a) for a in stats]
                + [_resident(w), _resident(spread)] + [_resident(p) for p in perms])

    def scratch(tm):
        return [pltpu.VMEM((tm, LANES), F32), pltpu.VMEM((tm, LANES), F32), pltpu.VMEM((tm, D_MODEL), BF16)]

    return (_a_out_stage, [*outs, *stats, _arr(w), spread, *perms], specs, scratch, "a_out_xattn")


def _rope_tables(seq_len):
    inv = ROPE_THETA ** (-jnp.arange(0, A_HEAD_DIM, 2, dtype=F32) / A_HEAD_DIM)
    ang = jnp.arange(seq_len, dtype=F32)[:, None] * inv[None, :]
    cos, sin = jnp.cos(ang), jnp.sin(ang)
    reps = LANES // A_HEAD_DIM
    return (jnp.concatenate([cos, cos] * reps, axis=1),
            jnp.concatenate([-sin, sin] * reps, axis=1))


def _mixer_a(x, g, w_in, w_out):
    b, s, _ = x.shape
    cos_t, sin_t = _rope_tables(s)
    outs, stats = [], []
    for qkv in _a_in(x, g, w_in, cos_t, sin_t):
        o, l = _attn_group(qkv)
        outs.append(o)
        stats.append(l)
    return _a_out_mixer(outs, stats, w_out)


HALO = 8
HY_COL_CHUNK = 256


def _hy_in_body(xp_ref, x_ref, xn_ref, g_ref, w_ref, cw_ref, cb_ref, x0_ref, vv_ref, u_ref, *, n_tiles):
    tm = x_ref.shape[0]
    i = pl.program_id(1)
    xe = jnp.concatenate([xp_ref[...], x_ref[...], xn_ref[...]], axis=0)
    xn = _rms(xe, g_ref[...]).astype(BF16)
    row = lax.broadcasted_iota(jnp.int32, (tm + 2 * HALO, 1), 0)
    inside = ((row >= HALO) | (i > 0)) & ((row < tm + HALO) | (i < n_tiles - 1))
    cw = HY_COL_CHUNK
    for c in range(D_MODEL // cw):
        parts = []
        for k in range(3):
            cols = slice(k * D_MODEL + c * cw, k * D_MODEL + (c + 1) * cw)
            u_ref[k] = jnp.where(inside, _dot(xn, w_ref[:, cols]), 0.0)
            parts.append(u_ref[k, pl.ds(HALO - 1, tm), :] * cw_ref[0:1, cols]
                         + u_ref[k, pl.ds(HALO, tm), :] * cw_ref[1:2, cols]
                         + u_ref[k, pl.ds(HALO + 1, tm), :] * cw_ref[2:3, cols]
                         + cb_ref[:, cols])
        x0_ref[:, c * cw:(c + 1) * cw] = parts[0].astype(BF16)
        vv_ref[:, c * cw:(c + 1) * cw] = (parts[2] * parts[1]).astype(BF16)


def _hy_in(x, g, w, conv_w, conv_b):
    b, s, _ = x.shape
    tm = TM_HY_IN
    n_tiles = s // tm
    per = tm // HALO
    n_halo = s // HALO
    tok = pl.BlockSpec((None, tm, D_MODEL), lambda bi, i: (bi, i, 0))
    return pl.pallas_call(
        functools.partial(_hy_in_body, n_tiles=n_tiles),
        grid=(b, n_tiles),
        in_specs=[
            pl.BlockSpec((None, HALO, D_MODEL), lambda bi, i: (bi, jnp.maximum(i * per - 1, 0), 0)),
            tok,
            pl.BlockSpec((None, HALO, D_MODEL), lambda bi, i: (bi, jnp.minimum((i + 1) * per, n_halo - 1), 0)),
            pl.BlockSpec((1, D_MODEL), lambda bi, i: (0, 0)),
            _resident(w),
            pl.BlockSpec(conv_w.shape, lambda bi, i: (0, 0)),
            pl.BlockSpec(conv_b.shape, lambda bi, i: (0, 0)),
        ],
        out_specs=[tok, tok],
        out_shape=[jax.ShapeDtypeStruct(x.shape, BF16), jax.ShapeDtypeStruct(x.shape, BF16)],
        scratch_shapes=[pltpu.VMEM((3, tm + 2 * HALO, HY_COL_CHUNK), F32)],
        compiler_params=_params("parallel", "parallel"),
        name="hy_in",
    )(x, x, x, g, _arr(w), conv_w, conv_b)


def _hdot(a, b):
    return jnp.dot(a, b, precision=lax.Precision.HIGHEST, preferred_element_type=F32)


def _hy_filter_body(z_ref, t_ref, a_ref, b_ref, w1_ref, b1_ref, w2_ref, b2_ref, w3_ref, b3_ref,
                    wo_ref, fr_ref, dl_ref, h_ref, sum_ref):
    fr = fr_ref[...]
    hid = jnp.sin(fr * (_hdot(z_ref[...], w1_ref[...]) + b1_ref[...]))
    hid = jnp.sin(fr * (_hdot(hid, w2_ref[...]) + b2_ref[...]))
    hid = jnp.sin(fr * (_hdot(hid, w3_ref[...]) + b3_ref[...]))
    decay = jnp.exp(-t_ref[...] * dl_ref[...])
    hid = hid.astype(BF16)
    h_fwd = _dot(hid, wo_ref[:, :D_MODEL]) * decay
    h_bwd = _dot(hid, wo_ref[:, D_MODEL:]) * decay
    h = a_ref[...] * h_fwd + b_ref[...] * h_bwd
    h_ref[...] = h

    @pl.when(pl.program_id(0) == 0)
    def _():
        sum_ref[...] = jnp.zeros_like(sum_ref)

    sum_ref[...] += jnp.sum(jnp.abs(h), axis=0, keepdims=True)


def _hy_filter(seq_len, f_w1, f_b1, f_w2, f_b2, f_w3, f_b3, f_wout, f_freq):
    n = 2 * seq_len
    src = np.concatenate([np.arange(seq_len), [0], np.arange(seq_len - 1, 0, -1)])
    lag = jnp.asarray(src, dtype=F32)[:, None]
    t2 = lag / (seq_len - 1)
    w = 2.0 * math.pi * lag / seq_len
    f = jnp.linspace(1e-4, HY_BANDS - 1, HY_BANDS, dtype=F32)[None, :]
    z2 = jnp.concatenate([t2, jnp.cos(f * w), -jnp.sin(f * w),
                          jnp.zeros((n, HY_HID_PAD - HY_EMB), F32)], axis=-1)
    pos = np.arange(n)
    use_fwd = (pos < seq_len).astype(np.float32)[:, None]
    use_bwd = ((pos == 0) | (pos > seq_len)).astype(np.float32)[:, None]
    max_decay = math.log(HY_DECAY_TARGET) / HY_DECAY_STRONG_PCT
    min_decay = math.log(HY_DECAY_TARGET) / HY_DECAY_WEAK_PCT
    deltas = jnp.abs(jnp.linspace(min_decay, max_decay, D_MODEL, dtype=F32))[None, :]

    def pad2(m, rows):
        return jnp.pad(m, ((0, rows - m.shape[0]), (0, HY_HID_PAD - m.shape[1])))

    def padv(v):
        return jnp.pad(v, (0, HY_HID_PAD - v.shape[0]))[None, :]

    wo = jnp.pad(f_wout, ((0, HY_HID_PAD - f_wout.shape[0]), (0, 0))).astype(BF16)
    tr = TR_FILTER
    rowblk = lambda width: pl.BlockSpec((tr, width), lambda i: (i, 0))
    full = lambda shape: pl.BlockSpec(shape, lambda i: (0, 0))
    sq = (HY_HID_PAD, HY_HID_PAD)
    vec = (1, HY_HID_PAD)
    return pl.pallas_call(
        _hy_filter_body,
        grid=(n // tr,),
        in_specs=[rowblk(HY_HID_PAD), rowblk(1), rowblk(1), rowblk(1),
                  full(sq), full(vec), full(sq), full(vec), full(sq), full(vec),
                  full(wo.shape), full(vec), full((1, D_MODEL))],
        out_specs=[rowblk(D_MODEL), full((1, D_MODEL))],
        out_shape=[jax.ShapeDtypeStruct((n, D_MODEL), F32), jax.ShapeDtypeStruct((1, D_MODEL), F32)],
        compiler_params=_params("arbitrary"),
        name="hy_filter",
    )(z2, t2, jnp.asarray(use_fwd), jnp.asarray(use_bwd),
      pad2(f_w1, HY_HID_PAD), padv(f_b1), pad2(f_w2, HY_HID_PAD), padv(f_b2),
      pad2(f_w3, HY_HID_PAD), padv(f_b3), wo, padv(f_freq), deltas)


def _fft_split(n):
    n1 = 1 << ((n.bit_length() - 1 + 1) // 2)
    return n1, n // n1


def _fft_tables(n1, n2):
    n = n1 * n2
    h = n1 // 2
    idx = np.arange(n1)
    ang = -2.0 * np.pi * ((idx[:, None] * idx[None, :]) % n1) / n1
    fr, fi = np.cos(ang), np.sin(ang)
    m1_data = np.block([[fr[:, :h], -fi[:, :h]], [fi[:, :h], fr[:, :h]]])
    m1_filt = np.concatenate([fr, fi], axis=0)
    ifr, ifi = fr.T[:h] / n, -fi.T[:h] / n
    m3 = np.block([[ifr, -ifi], [ifi, ifr]])
    k1 = jnp.arange(n1, dtype=jnp.int32)[:, None, None]
    k2 = jnp.arange(n2, dtype=jnp.int32)[None, :, None]
    i2 = jnp.arange(n2, dtype=jnp.int32)[None, None, :]
    phase = (i2 * k1 + n1 * i2 * k2) % n
    ga = (-2.0 * math.pi / n) * phase.astype(F32)
    gr, gi = jnp.cos(ga), jnp.sin(ga)
    g_fwd = jnp.concatenate([jnp.concatenate([gr, -gi], axis=2), jnp.concatenate([gi, gr], axis=2)], axis=1)
    grt, git = jnp.swapaxes(gr, 1, 2), jnp.swapaxes(gi, 1, 2)
    g_inv = jnp.concatenate([jnp.concatenate([grt, git], axis=2), jnp.concatenate([-git, grt], axis=2)], axis=1)
    as_bf = lambda m: jnp.asarray(m, dtype=F32).astype(BF16)
    return as_bf(m1_data), as_bf(m1_filt), as_bf(m3), g_fwd.astype(BF16), g_inv.astype(BF16)


FFT_SUB = 16
FFT_TC = 512


def _dft_outer_body(m_ref, pm_ref, z_ref, o_ref, zs_ref, rb_ref):
    q, sub, tc = z_ref.shape
    r = m_ref.shape[0]
    nlb = tc // LANES
    for cb in range(nlb):
        zs_ref[cb] = z_ref[:, :, cb * LANES:(cb + 1) * LANES].astype(F32).reshape(q * sub, LANES)
    hw = tc // 2
    n_groups = r // sub

    def transform(hf, j):
        z = jnp.concatenate([zs_ref[cb, pl.ds(j, q, stride=sub), :]
                             for cb in range(hf * nlb // 2, (hf + 1) * nlb // 2)], axis=1)
        rb_ref[hf, j] = _dot(m_ref[...], z.astype(BF16)).astype(BF16)

    def regroup(hf, g):
        part, i0 = divmod(g * sub, r // 2)
        grouped = rb_ref[hf, :, g * sub:(g + 1) * sub, :].reshape(sub * sub, hw)
        o_ref[part, i0:i0 + sub, :, hf * hw:(hf + 1) * hw] = (
            _dot(pm_ref[...], grouped).reshape(sub, sub, hw).astype(BF16))

    for j in range(sub):
        transform(0, j)
    for step in range(max(sub, n_groups)):
        if step < sub:
            transform(1, step)
        if step < n_groups:
            regroup(0, step)
    for g in range(n_groups):
        regroup(1, g)


def _dft_outer(mat, perm, z5, name):
    p, q, nh, sub, c = z5.shape
    r = mat.shape[0]
    tc = FFT_TC
    return pl.pallas_call(
        _dft_outer_body,
        grid=(p, nh, c // tc),
        in_specs=[pl.BlockSpec(mat.shape, lambda pi, h, ci: (0, 0)),
                  pl.BlockSpec(perm.shape, lambda pi, h, ci: (0, 0)),
                  pl.BlockSpec((None, q, None, sub, tc), lambda pi, h, ci: (pi, 0, h, 0, ci))],
        out_specs=pl.BlockSpec((None, None, 2, r // 2, sub, tc), lambda pi, h, ci: (pi, h, 0, 0, 0, ci)),
        out_shape=jax.ShapeDtypeStruct((p, nh, 2, r // 2, sub, c), BF16),
        scratch_shapes=[pltpu.VMEM((tc // LANES, q * sub, LANES), F32),
                        pltpu.VMEM((2, sub, r, tc // 2), BF16)],
        compiler_params=_params("parallel", "parallel", "parallel"),
        name=name,
    )(mat, perm, z5)


def _idft_outer_body(m_ref, pm_ref, b_ref, o_ref, tb_ref, ys_ref):
    _, n1, sub, tc = b_ref.shape
    q = m_ref.shape[0]
    nlb = tc // LANES
    hw = tc // 2
    n_groups = 2 * n1 // sub

    def regroup(hf, g):
        part, i0 = divmod(g * sub, n1)
        grouped = b_ref[part, i0:i0 + sub, :, hf * hw:(hf + 1) * hw].reshape(sub * sub, hw)
        tb_ref[hf, :, g * sub:(g + 1) * sub, :] = _dot(pm_ref[...], grouped).reshape(sub, sub, hw).astype(BF16)

    def transform(hf, j):
        res = _dot(m_ref[...], tb_ref[hf, j])
        for c in range(nlb // 2):
            ys_ref[hf * nlb // 2 + c, pl.ds(j, q, stride=sub), :] = res[:, c * LANES:(c + 1) * LANES]

    for g in range(n_groups):
        regroup(0, g)
    for step in range(max(sub, n_groups)):
        if step < n_groups:
            regroup(1, step)
        if step < sub:
            transform(0, step)
    for j in range(sub):
        transform(1, j)
    for cb in range(nlb):
        o_ref[:, :, cb * LANES:(cb + 1) * LANES] = ys_ref[cb].reshape(q, sub, LANES)


def _idft_outer(mat, perm, b6):
    p, nh, _, n1, sub, c = b6.shape
    q = mat.shape[0]
    tc = FFT_TC
    return pl.pallas_call(
        _idft_outer_body,
        grid=(p, nh, c // tc),
        in_specs=[pl.BlockSpec(mat.shape, lambda pi, h, ci: (0, 0)),
                  pl.BlockSpec(perm.shape, lambda pi, h, ci: (0, 0)),
                  pl.BlockSpec((None, None, 2, n1, sub, tc), lambda pi, h, ci: (pi, h, 0, 0, 0, ci))],
        out_specs=pl.BlockSpec((None, q, None, sub, tc), lambda pi, h, ci: (pi, 0, h, 0, ci)),
        out_shape=jax.ShapeDtypeStruct((p, q, nh, sub, c), F32),
        scratch_shapes=[pltpu.VMEM((2, sub, 2 * n1, tc // 2), BF16),
                        pltpu.VMEM((tc // LANES, q * sub, LANES), F32)],
        compiler_params=_params("parallel", "parallel", "parallel"),
        name="hy_idft",
    )(mat, perm, b6)


FFT_KB = 4


def _stack_re_im(a_ref, kk):
    nh, _, _, sub, ct = a_ref.shape
    return jnp.concatenate([a_ref[:, 0, kk].reshape(nh * sub, ct), a_ref[:, 1, kk].reshape(nh * sub, ct)], axis=0)


def _spec_filter_body(g_ref, a_ref, sc_ref, h_ref):
    n2 = g_ref.shape[1] // 2
    for kk in range(g_ref.shape[0]):
        spec = _dot(g_ref[kk], _stack_re_im(a_ref, kk)) * sc_ref[...]
        h_ref[kk] = spec.reshape(2, n2, spec.shape[1])


def _spec_filter(g_fwd, a6, scale):
    _, nh, _, n1, sub, c = a6.shape
    n2 = nh * sub
    return pl.pallas_call(
        _spec_filter_body,
        grid=(n1 // FFT_KB,),
        in_specs=[pl.BlockSpec((FFT_KB, 2 * n2, 2 * n2), lambda k: (k, 0, 0)),
                  pl.BlockSpec((None, nh, 2, FFT_KB, sub, c), lambda k: (0, 0, 0, k, 0, 0)),
                  pl.BlockSpec((1, c), lambda k: (0, 0))],
        out_specs=pl.BlockSpec((FFT_KB, 2, n2, c), lambda k: (k, 0, 0, 0)),
        out_shape=jax.ShapeDtypeStruct((n1, 2, n2, c), F32),
        compiler_params=_params("parallel"),
        name="hy_spec_filter",
    )(g_fwd, a6, scale)


def _spec_body(gf_ref, gi_ref, h_ref, a_ref, o_ref):
    nh, _, kb, sub, ct = a_ref.shape
    n2 = nh * sub

    def forward(kk):
        return _dot(gf_ref[kk], _stack_re_im(a_ref, kk))

    spec_next = forward(0)
    for kk in range(kb):
        spec = spec_next
        if kk + 1 < kb:
            spec_next = forward(kk + 1)
        xr, xi = spec[:n2], spec[n2:]
        hr, hi = h_ref[kk, 0], h_ref[kk, 1]
        y = jnp.concatenate([xr * hr - xi * hi, xr * hi + xi * hr], axis=0).astype(BF16)
        back = _dot(gi_ref[kk], y)
        o_ref[:, 0, kk] = back[:n2].reshape(nh, sub, ct).astype(BF16)
        o_ref[:, 1, kk] = back[n2:].reshape(nh, sub, ct).astype(BF16)


def _spec(g_fwd, g_inv, hspec, a6):
    p, nh, _, n1, sub, c = a6.shape
    n2 = nh * sub
    blk = pl.BlockSpec((None, nh, 2, FFT_KB, sub, c), lambda k, pi: (pi, 0, 0, k, 0, 0))
    mat = pl.BlockSpec((FFT_KB, 2 * n2, 2 * n2), lambda k, pi: (k, 0, 0))
    return pl.pallas_call(
        _spec_body,
        grid=(n1 // FFT_KB, p),
        in_specs=[mat, mat, pl.BlockSpec((FFT_KB, 2, n2, c), lambda k, pi: (k, 0, 0, 0)), blk],
        out_specs=blk,
        out_shape=jax.ShapeDtypeStruct(a6.shape, BF16),
        compiler_params=_params("parallel", "parallel"),
        name="hy_spec",
    )(g_fwd, g_inv, hspec, a6)


def _long_conv(vv, h_raw, h_norm):
    b, l, c = vv.shape
    n1, n2 = _fft_split(2 * l)
    nh = n2 // FFT_SUB
    m1_data, m1_filt, m3, g_fwd, g_inv = _fft_tables(n1, n2)
    idx = np.arange(FFT_SUB * FFT_SUB)
    swapped = (idx % FFT_SUB) * FFT_SUB + idx // FFT_SUB
    perm = jnp.asarray(idx[None, :] == swapped[:, None], dtype=BF16)
    a_h = _dft_outer(m1_filt, perm, h_raw.reshape(1, n1, nh, FFT_SUB, c), "hy_dft1_filter")
    hspec = _spec_filter(g_fwd, a_h, 1.0 / h_norm)
    a = _dft_outer(m1_data, perm, vv.reshape(b // 2, n1, nh, FFT_SUB, c), "hy_dft1")
    y = _idft_outer(m3, perm, _spec(g_fwd, g_inv, hspec, a))
    return y.reshape(b, l, c)


def _hy_out_stage(cv_ref, vv_ref, x0_ref, bd_ref, w_ref, x_ref):
    y = ((cv_ref[...] + bd_ref[...] * vv_ref[...].astype(F32)) * x0_ref[...].astype(F32)).astype(BF16)
    return x_ref[...] + _dot(y, w_ref[...])


def _mixer_b(x, g, w_in, conv_w, conv_b, filt, bias_d, w_out):
    _, s, _ = x.shape
    x0, vv = _hy_in(x, g, w_in, conv_w, conv_b)
    h_raw, h_norm = _hy_filter(s, *filt)
    conv = _long_conv(vv, h_raw, h_norm)

    def specs(tm):
        tok = pl.BlockSpec((None, tm, D_MODEL), lambda bi, i: (bi, i, 0))
        return [tok, tok, tok, pl.BlockSpec((1, D_MODEL), lambda bi, i: (0, 0)), _resident(w_out)]

    return (_hy_out_stage, [conv, vv, x0, bias_d, _arr(w_out)], specs, lambda tm: [], "hy_out_xattn")


def _gelu(z):
    return 0.5 * z * (1.0 + lax.erf(z * (2.0 ** -0.5)))


def _sgu_body(x_ref, g_ref, win_ref, lng_ref, lnb_ref, ws_ref, bs_ref, wout_ref, o_ref, y_ref, z_ref):
    tm = x_ref.shape[0]
    x = x_ref[...]
    xn = _rms(x, g_ref[...]).astype(BF16)
    for c in range(2 * D_MODEL // FF_CHUNK):
        cols = slice(c * FF_CHUNK, (c + 1) * FF_CHUNK)
        z_ref[:, cols] = _gelu(_dot(xn, win_ref[:, cols]))
    zv = z_ref[:, D_MODEL:]
    zc = zv - jnp.mean(zv, axis=-1, keepdims=True)
    zv = zc * lax.rsqrt(jnp.mean(zc * zc, axis=-1, keepdims=True) + EPS) * lng_ref[...] + lnb_ref[...]
    zvb = zv.astype(BF16)
    n_chunks = tm // C_CHUNK
    for h in range(C_GROUPS):
        cols = slice(h * LANES, (h + 1) * LANES)
        v_all = jnp.concatenate([zvb[c * C_CHUNK:(c + 1) * C_CHUNK, cols] for c in range(n_chunks)], axis=1)
        sv = _dot(ws_ref[h], v_all) + bs_ref[:, h:h + 1]
        for c in range(n_chunks):
            rows = slice(c * C_CHUNK, (c + 1) * C_CHUNK)
            y_ref[rows, cols] = (z_ref[rows, cols] * sv[:, c * LANES:(c + 1) * LANES]).astype(BF16)
    o_ref[...] = x + _dot(y_ref[...], wout_ref[...])


def _mixer_c(x, g, w_in, ln_g, ln_b, w_s, b_s_t, w_out):
    b, s, _ = x.shape
    t = b * s
    tm = TM_SGU
    tok = pl.BlockSpec((tm, D_MODEL), lambda i: (i, 0))
    row = pl.BlockSpec((1, D_MODEL), lambda i: (0, 0))
    return pl.pallas_call(
        _sgu_body,
        grid=(t // tm,),
        in_specs=[tok, row, _resident(w_in), row, row, _resident(w_s),
                  pl.BlockSpec(b_s_t.shape, lambda i: (0, 0)), _resident(w_out)],
        out_specs=tok,
        out_shape=jax.ShapeDtypeStruct((t, D_MODEL), F32),
        scratch_shapes=[pltpu.VMEM((tm, D_MODEL), BF16), pltpu.VMEM((tm, 2 * D_MODEL), F32)],
        compiler_params=_params("parallel"),
        name="sgu",
    )(x.reshape(t, D_MODEL), g, _arr(w_in), ln_g, ln_b, _arr(w_s), b_s_t, _arr(w_out)).reshape(x.shape)


def _trunk(x, kvs, boff, w):
    b, s, _ = x.shape
    t = b * s
    for i in range(DEPTH):
        kind, j = i % N_MIXERS, i // N_MIXERS
        g = w["g_mix"][i][None, :]
        mixer_out = None
        if kind == 0:
            mixer_out = _mixer_a(x, g, w["a_w_in"][j], w["a_w_out"][j])
        elif kind == 1:
            filt = tuple(w[k][j] for k in ("b_f_w1", "b_f_b1", "b_f_w2", "b_f_b2", "b_f_w3", "b_f_b3",
                                          "b_f_wout", "b_f_freq"))
            mixer_out = _mixer_b(x, g, w["b_w_in"][j], w["b_conv_w"][j], w["b_conv_b"][j][None, :], filt,
                                 w["b_bias_d"][j][None, :], w["b_w_out"][j])
        else:
            x = _mixer_c(x, g, w["c_w_in"][j], w["c_ln_g"][j][None, :], w["c_ln_b"][j][None, :],
                         w["c_w_s"][j], w["c_b_s"][j].T, w["c_w_out"][j])
        x = _xattn(x, w["g_cross"][i][None, :], w["x_w_q"][i], kvs[i], boff, w["x_w_o"][i], mixer_out)
        g_final = w["g_final"][None, :] if i == DEPTH - 1 else None
        x = _ffn(x.reshape(t, D_MODEL), w["g_ffn"][i][None, :], w["f_w_gu"][i], w["f_w_down"][i],
                 g_final).reshape(b, s, D_MODEL)
    return x


_BF16_WEIGHTS = ("a_w_in", "a_w_out", "b_w_in", "b_w_out", "c_w_in", "c_w_s", "c_w_out",
                 "x_w_q", "x_w_kv", "x_w_o", "f_w_gu", "f_w_down")


def kernel(x_prompt, x_sample, mem_prompt, mem_sample, g_mix, g_cross, g_ffn, g_final, a_w_in, a_w_out, b_w_in, b_conv_w, b_conv_b, b_f_w1, b_f_b1, b_f_w2, b_f_b2, b_f_w3, b_f_b3, b_f_wout, b_f_freq, b_bias_d, b_w_out, c_w_in, c_ln_g, c_ln_b, c_w_s, c_b_s, c_w_out, x_w_q, x_w_kv, x_w_o, f_w_gu, f_w_down):
    w = dict(g_mix=g_mix, g_cross=g_cross, g_ffn=g_ffn, g_final=g_final,
             a_w_in=a_w_in, a_w_out=a_w_out, b_w_in=b_w_in, b_conv_w=b_conv_w, b_conv_b=b_conv_b,
             b_f_w1=b_f_w1, b_f_b1=b_f_b1, b_f_w2=b_f_w2, b_f_b2=b_f_b2, b_f_w3=b_f_w3,
             b_f_b3=b_f_b3, b_f_wout=b_f_wout, b_f_freq=b_f_freq, b_bias_d=b_bias_d,
             b_w_out=b_w_out, c_w_in=c_w_in, c_ln_g=c_ln_g, c_ln_b=c_ln_b, c_w_s=c_w_s,
             c_b_s=c_b_s, c_w_out=c_w_out, x_w_q=x_w_q, x_w_kv=x_w_kv, x_w_o=x_w_o,
             f_w_gu=f_w_gu, f_w_down=f_w_down)
    for name in _BF16_WEIGHTS:
        stack = w[name].astype(BF16)
        w[name] = [(stack, j) for j in range(stack.shape[0])]
    nb_prompt = mem_prompt.shape[0]
    mem = jnp.concatenate([mem_prompt, mem_sample], axis=0)
    mem2 = mem.reshape(mem.shape[0] * MEM_LEN, D_MODEL)
    kvs = [_kv_proj(mem2, w["x_w_kv"][i]).reshape(mem.shape[0], MEM_LEN, 2 * D_MODEL) for i in range(DEPTH)]
    y_prompt = _trunk(x_prompt, kvs, 0, w)
    y_sample = _trunk(x_sample, kvs, nb_prompt, w)
    return (y_prompt, y_sample)
```

```python
import functools
import math

import numpy as np
import jax
import jax.numpy as jnp
from jax import lax
from jax.experimental import pallas as pl
from jax.experimental.pallas import tpu as pltpu

F32 = jnp.float32
BF16 = jnp.bfloat16

D_MODEL = 1024
DEPTH = 4
N_MIXERS = 3

A_GROUPS = ((128, 1), (512, 4), (2048, 16))
A_N_GROUPS = len(A_GROUPS)
A_HEADS = 16
A_HEAD_DIM = D_MODEL // A_HEADS
A_HALF = 64
A_SUBTILE = 128
A_QUERIES_PER_STEP = 1024
A_Q_SCALE = A_HEAD_DIM ** -0.5 * math.log2(math.e)
A_REGROUP = 256
A_STEP = 4
ROPE_THETA = 10000.0

HY_EMB = 33
HY_BANDS = (HY_EMB - 1) // 2
HY_HID_PAD = 128
HY_DECAY_TARGET = 1e-2
HY_DECAY_STRONG_PCT = 0.3
HY_DECAY_WEAK_PCT = 1.5

C_CHUNK = 128
C_GROUPS = 8

MEM_LEN = 256
X_HEADS = 4
X_HEAD_DIM = D_MODEL // X_HEADS

D_FF = -(-8 * D_MODEL // (3 * 256)) * 256
FF_CHUNK = 256

EPS = 1e-6
NEG_INF = -1e30

LANES = 128
VMEM_LIMIT = 56 * 1024 * 1024

TM_FFN = 1024
TM_XATTN = 1024
TM_MIXER_XATTN = 512
TM_A_IN = 512
TM_HY_IN = 512
TM_SGU = 512
TM_KV = 256
TR_FILTER = 512


def _params(*sem):
    return pltpu.CompilerParams(dimension_semantics=sem, vmem_limit_bytes=VMEM_LIMIT)


def _resident(w):
    if isinstance(w, tuple):
        stack, layer = w
        nd = stack.ndim - 1
        return pl.BlockSpec((None,) + stack.shape[1:], lambda *_: (layer,) + (0,) * nd,
                            pipeline_mode=pl.Buffered(1))
    nd = w.ndim
    return pl.BlockSpec(w.shape, lambda *_: (0,) * nd, pipeline_mode=pl.Buffered(1))


def _arr(w):
    return w[0] if isinstance(w, tuple) else w


def _dot(a, b):
    return jnp.dot(a, b, preferred_element_type=F32)


def _dot_nt(a, b):
    return lax.dot_general(a, b, (((1,), (1,)), ((), ())), preferred_element_type=F32)


def _rms(x, g):
    return x * lax.rsqrt(jnp.mean(x * x, axis=-1, keepdims=True) + EPS) * g


def _ffn_body(*refs, final):
    if final:
        x_ref, g_ref, wgu_ref, wd_ref, gf_ref, o_ref, act_ref = refs
    else:
        x_ref, g_ref, wgu_ref, wd_ref, o_ref, act_ref = refs
    x = x_ref[...]
    xn = _rms(x, g_ref[...]).astype(BF16)
    for c in range(D_FF // FF_CHUNK):
        lo = c * FF_CHUNK
        gate = _dot(xn, wgu_ref[:, lo:lo + FF_CHUNK])
        up = _dot(xn, wgu_ref[:, D_FF + lo:D_FF + lo + FF_CHUNK])
        act_ref[:, lo:lo + FF_CHUNK] = (gate / (1.0 + jnp.exp(-gate)) * up).astype(BF16)
    y = x + _dot(act_ref[...], wd_ref[...])
    if final:
        y = _rms(y, gf_ref[...])
    o_ref[...] = y


def _ffn(x2, g, wgu, wd, g_final=None):
    t = x2.shape[0]
    tm = TM_FFN
    final = g_final is not None
    tok = pl.BlockSpec((tm, D_MODEL), lambda i: (i, 0))
    row = pl.BlockSpec((1, D_MODEL), lambda i: (0, 0))
    in_specs = [tok, row, _resident(wgu), _resident(wd)]
    args = [x2, g, _arr(wgu), _arr(wd)]
    if final:
        in_specs.append(row)
        args.append(g_final)
    return pl.pallas_call(
        functools.partial(_ffn_body, final=final),
        grid=(t // tm,),
        in_specs=in_specs,
        out_specs=tok,
        out_shape=jax.ShapeDtypeStruct((t, D_MODEL), F32),
        scratch_shapes=[pltpu.VMEM((tm, D_FF), BF16)],
        compiler_params=_params("parallel"),
        name="ffn_final" if final else "ffn",
    )(*args)


def _kv_body(m_ref, w_ref, o_ref):
    o_ref[...] = _dot(m_ref[...].astype(BF16), w_ref[...]).astype(BF16)


def _kv_proj(mem2, wkv):
    r = mem2.shape[0]
    tm = TM_KV
    return pl.pallas_call(
        _kv_body,
        grid=(r // tm,),
        in_specs=[pl.BlockSpec((tm, D_MODEL), lambda i: (i, 0)), _resident(wkv)],
        out_specs=pl.BlockSpec((tm, 2 * D_MODEL), lambda i: (i, 0)),
        out_shape=jax.ShapeDtypeStruct((r, 2 * D_MODEL), BF16),
        compiler_params=_params("parallel"),
        name="kv_proj",
    )(mem2, _arr(wkv))


def _xattn_stage(x, g_ref, wq_ref, k_ref, v_ref, wo_ref, y_ref):
    xn = _rms(x, g_ref[...]).astype(BF16)
    q = (_dot(xn, wq_ref[...]) * (X_HEAD_DIM ** -0.5)).astype(BF16)
    def scores(h):
        sl = slice(h * X_HEAD_DIM, (h + 1) * X_HEAD_DIM)
        return _dot_nt(q[:, sl], k_ref[:, sl])

    s_next = scores(0)
    for h in range(X_HEADS):
        sl = slice(h * X_HEAD_DIM, (h + 1) * X_HEAD_DIM)
        s = s_next
        if h + 1 < X_HEADS:
            s_next = scores(h + 1)
        p = jnp.exp(s - jnp.max(s, axis=-1, keepdims=True))
        inv = 1.0 / jnp.sum(p, axis=-1, keepdims=True)
        y_ref[:, sl] = (_dot(p.astype(BF16), v_ref[:, sl]) * inv).astype(BF16)
    return x + _dot(y_ref[...], wo_ref[...])


def _xattn_body(*refs, mixer_out, n_mixer_in, n_mixer_scratch):
    n_in = n_mixer_in + 6
    mixer_in, (x_ref, g_ref, wq_ref, k_ref, v_ref, wo_ref) = refs[:n_mixer_in], refs[n_mixer_in:n_in]
    o_ref = refs[n_in]
    mixer_scratch, y_ref = refs[n_in + 1:n_in + 1 + n_mixer_scratch], refs[n_in + 1 + n_mixer_scratch]
    x = x_ref[...] if mixer_out is None else mixer_out(*mixer_in, x_ref, *mixer_scratch)
    o_ref[...] = _xattn_stage(x, g_ref, wq_ref, k_ref, v_ref, wo_ref, y_ref)


def _xattn(x, g, wq, kv, boff, wo, mixer=None):
    b, s, _ = x.shape
    if mixer is None:
        stage, m_args, m_specs, m_scratch, tm, name = None, [], [], [], TM_XATTN, "xattn"
    else:
        stage, m_args, specs_fn, scratch_fn, name = mixer
        tm = TM_MIXER_XATTN
        m_specs, m_scratch = specs_fn(tm), scratch_fn(tm)
    tok = pl.BlockSpec((None, tm, D_MODEL), lambda bi, i: (bi, i, 0))
    return pl.pallas_call(
        functools.partial(_xattn_body, mixer_out=stage, n_mixer_in=len(m_args), n_mixer_scratch=len(m_scratch)),
        grid=(b, s // tm),
        in_specs=m_specs + [
            tok,
            pl.BlockSpec((1, D_MODEL), lambda bi, i: (0, 0)),
            _resident(wq),
            pl.BlockSpec((None, MEM_LEN, D_MODEL), lambda bi, i: (bi + boff, 0, 0)),
            pl.BlockSpec((None, MEM_LEN, D_MODEL), lambda bi, i: (bi + boff, 0, 1)),
            _resident(wo),
        ],
        out_specs=tok,
        out_shape=jax.ShapeDtypeStruct(x.shape, F32),
        scratch_shapes=m_scratch + [pltpu.VMEM((tm, D_MODEL), BF16)],
        compiler_params=_params("parallel", "parallel"),
        name=name,
    )(*m_args, x, g, _arr(wq), kv, kv, _arr(wo))


def _a_in_body(x_ref, g_ref, w_ref, cos_ref, sin_ref, *rest):
    o_refs, (xs_ref, xs2_ref, xp_ref) = rest[:A_N_GROUPS], rest[A_N_GROUPS:]
    tm = x_ref.shape[0]
    xn_f32 = _rms(x_ref[...], g_ref[...])
    reps = D_MODEL // LANES
    for cb in range(reps):
        xs_ref[cb] = xn_f32[:, cb * LANES:(cb + 1) * LANES]
    lane = lax.broadcasted_iota(jnp.int32, (tm, LANES), 1)
    low_half = (lane & (A_HEAD_DIM // 2)) == 0
    for gidx, (o_ref, (_, dil)) in enumerate(zip(o_refs, A_GROUPS)):
        rows = tm // dil
        if dil == 1:
            xn = xn_f32.astype(BF16)
            cos, sin = cos_ref[...], sin_ref[...]
        elif dil == A_STEP:
            for r in range(dil):
                for cb in range(reps):
                    piece = xs_ref[cb, pl.ds(r, rows, stride=dil), :]
                    xs2_ref[cb, r * rows:(r + 1) * rows, :] = piece
                    xp_ref[r * rows:(r + 1) * rows, cb * LANES:(cb + 1) * LANES] = piece.astype(BF16)
            xn = xp_ref[...]
        else:
            for r in range(dil):
                r_lo, r_hi = r % A_STEP, r // A_STEP
                for cb in range(reps):
                    xp_ref[r * rows:(r + 1) * rows, cb * LANES:(cb + 1) * LANES] = (
                        xs2_ref[cb, pl.ds(r_lo * (tm // A_STEP) + r_hi, rows, stride=A_STEP), :].astype(BF16))
            xn = xp_ref[...]
        if dil > 1:
            cos = jnp.concatenate([cos_ref[pl.ds(r, rows, stride=dil), :] for r in range(dil)], axis=0)
            sin = jnp.concatenate([sin_ref[pl.ds(r, rows, stride=dil), :] for r in range(dil)], axis=0)
        for part in range(3):
            lo = (gidx * 3 + part) * D_MODEL
            y = _dot(xn, w_ref[:, lo:lo + D_MODEL])
            if part == 2:
                o_ref[:, :, 2 * D_MODEL:] = y.reshape(dil, rows, D_MODEL).astype(BF16)
                continue
            c_t, s_t = (cos * A_Q_SCALE, sin * A_Q_SCALE) if part == 0 else (cos, sin)
            for cb in range(reps):
                yb = y[:, cb * LANES:(cb + 1) * LANES]
                partner = jnp.where(low_half, pltpu.roll(yb, LANES - A_HEAD_DIM // 2, 1),
                                    pltpu.roll(yb, A_HEAD_DIM // 2, 1))
                col = part * D_MODEL + cb * LANES
                o_ref[:, :, col:col + LANES] = (yb * c_t + partner * s_t).reshape(dil, rows, LANES).astype(BF16)


def _a_in(x, g, w, cos_t, sin_t):
    b, s, _ = x.shape
    tm = TM_A_IN
    return pl.pallas_call(
        _a_in_body,
        grid=(b, s // tm),
        in_specs=[
            pl.BlockSpec((None, tm, D_MODEL), lambda bi, i: (bi, i, 0)),
            pl.BlockSpec((1, D_MODEL), lambda bi, i: (0, 0)),
            _resident(w),
            pl.BlockSpec((tm, LANES), lambda bi, i: (i, 0)),
            pl.BlockSpec((tm, LANES), lambda bi, i: (i, 0)),
        ],
        out_specs=[pl.BlockSpec((None, dil, tm // dil, 3 * D_MODEL), lambda bi, i: (bi, 0, i, 0))
                   for _, dil in A_GROUPS],
        out_shape=[jax.ShapeDtypeStruct((b, dil, s // dil, 3 * D_MODEL), BF16) for _, dil in A_GROUPS],
        scratch_shapes=[pltpu.VMEM((D_MODEL // LANES, tm, LANES), F32), pltpu.VMEM((D_MODEL // LANES, tm, LANES), F32),
                        pltpu.VMEM((tm, D_MODEL), BF16)],
        compiler_params=_params("parallel", "parallel"),
        name="a_in",
    )(x, g, _arr(w), cos_t, sin_t)


def _attn_body(q_ref, kp_ref, k_ref, kn_ref, vp_ref, v_ref, vn_ref, o_ref, stat_ref, *, n_tiles):
    n_res, tq = q_ref.shape[0], q_ref.shape[1]
    sq = A_SUBTILE
    win = sq + 2 * A_HALF
    n_sub_tiles = tq // sq
    i = pl.program_id(2)
    qi = lax.broadcasted_iota(jnp.int32, (sq, win), 0)
    kj = lax.broadcasted_iota(jnp.int32, (sq, win), 1)
    band = (kj >= qi) & (kj <= qi + 2 * A_HALF)
    left = lax.broadcasted_iota(jnp.int32, (sq, LANES), 1) < A_HEAD_DIM
    for res, st in [(a, c) for a in range(n_res) for c in range(n_sub_tiles)]:
        if st == 0:
            kw = jnp.concatenate([kp_ref[res], k_ref[res], kn_ref[res]], axis=0)
            vw = jnp.concatenate([vp_ref[res], v_ref[res], vn_ref[res]], axis=0)
        q_res, o_res, stat_res = q_ref.at[res], o_ref.at[res], stat_ref.at[res]
        rows = slice(st * sq, (st + 1) * sq)
        mask = band
        if st == 0:
            mask = mask & (kj >= jnp.where(i == 0, A_HALF, 0))
        if st == n_sub_tiles - 1:
            mask = mask & (kj < jnp.where(i == n_tiles - 1, sq + A_HALF, win))
        mask2 = jnp.concatenate([mask, mask], axis=0)
        stat_res[rows, :] = jnp.zeros((sq, LANES), F32)

        def scores(hp, q_res=q_res, rows=rows, kw=kw, st=st):
            sl = slice(hp * LANES, (hp + 1) * LANES)
            q2 = q_res[rows, sl]
            zero = jnp.zeros_like(q2)
            qq = jnp.concatenate([jnp.where(left, q2, zero), jnp.where(left, zero, q2)], axis=0)
            return _dot_nt(qq, kw[st * sq:st * sq + win, sl])

        s_next = scores(0)
        for hp in range(A_HEADS // 2):
            sl = slice(hp * LANES, (hp + 1) * LANES)
            s = jnp.where(mask2, s_next, NEG_INF)
            if hp + 1 < A_HEADS // 2:
                s_next = scores(hp + 1)
            m = jnp.max(s, axis=-1, keepdims=True)
            p = jnp.exp2(s - m)
            l = jnp.sum(p, axis=-1, keepdims=True)
            r = _dot(p.astype(BF16), vw[st * sq:st * sq + win, sl])
            o_res[rows, sl] = jnp.where(left, r[:sq], r[sq:]).astype(BF16)
            for off, col in ((0, m), (A_HEADS, l)):
                stat_res[rows, off + 2 * hp:off + 2 * hp + 1] = col[:sq]
                stat_res[rows, off + 2 * hp + 1:off + 2 * hp + 2] = col[sq:]


def _attn_group(qkv):
    b, dil, n_sub, _ = qkv.shape
    tq = min(A_QUERIES_PER_STEP, n_sub)
    n_res = min(dil, A_QUERIES_PER_STEP // tq)
    n_tiles = n_sub // tq
    halo_per_tile = tq // A_HALF
    n_halo = n_sub // A_HALF

    def own(part):
        return pl.BlockSpec((None, n_res, tq, D_MODEL), lambda bi, r, i: (bi, r, i, part))

    def prev(part):
        return pl.BlockSpec((None, n_res, A_HALF, D_MODEL),
                            lambda bi, r, i: (bi, r, jnp.maximum(i * halo_per_tile - 1, 0), part))

    def nxt(part):
        return pl.BlockSpec((None, n_res, A_HALF, D_MODEL),
                            lambda bi, r, i: (bi, r, jnp.minimum((i + 1) * halo_per_tile, n_halo - 1), part))

    return pl.pallas_call(
        functools.partial(_attn_body, n_tiles=n_tiles),
        grid=(b, dil // n_res, n_tiles),
        in_specs=[own(0), prev(1), own(1), nxt(1), prev(2), own(2), nxt(2)],
        out_specs=[
            pl.BlockSpec((None, n_res, tq, D_MODEL), lambda bi, r, i: (bi, r, i, 0)),
            pl.BlockSpec((None, n_res, tq, LANES), lambda bi, r, i: (bi, r, i, 0)),
        ],
        out_shape=[
            jax.ShapeDtypeStruct((b, dil, n_sub, D_MODEL), BF16),
            jax.ShapeDtypeStruct((b, dil, n_sub, LANES), F32),
        ],
        compiler_params=_params("parallel", "parallel", "parallel"),
        name=f"attn_d{dil}",
    )(qkv, qkv, qkv, qkv, qkv, qkv, qkv)


def _a_out_stage(o1_ref, o2_ref, o3_ref, l1_ref, l2_ref, l3_ref, w_ref, sp_ref, pm2_ref, pm3_ref, x_ref,
                 ls2_ref, ls3_ref, y_ref):
    tm = x_ref.shape[0]
    for l_ref, ls_ref in ((l2_ref, ls2_ref), (l3_ref, ls3_ref)):
        dil = l_ref.shape[0]
        for r in range(dil):
            ls_ref[pl.ds(r, tm // dil, stride=dil), :] = l_ref[r]

    def token_order(o_ref, pm_ref, blk):
        dil = o_ref.shape[0]
        per = A_REGROUP // dil
        grouped = jnp.concatenate([o_ref[r, blk * per:(blk + 1) * per, :] for r in range(dil)], axis=0)
        return _dot(pm_ref[...], grouped)
    s1, s2, s3 = l1_ref[0], ls2_ref[...], ls3_ref[...]
    m = jnp.maximum(jnp.maximum(s1, s2), s3)
    e1, e2, e3 = jnp.exp2(s1 - m), jnp.exp2(s2 - m), jnp.exp2(s3 - m)
    den = sum(e * pltpu.roll(st, LANES - A_HEADS, 1) for e, st in ((e1, s1), (e2, s2), (e3, s3)))
    inv = 1.0 / den
    valid = lax.broadcasted_iota(jnp.int32, (tm, LANES), 1) < A_HEADS
    packed = jnp.zeros((tm, LANES), F32)
    for gidx, e in enumerate((e1, e2, e3)):
        wgt = jnp.where(valid, e * inv, 0.0)
        hi = wgt.astype(BF16).astype(F32)
        for term, val in enumerate((hi, wgt - hi)):
            shift = 2 * A_HEADS * gidx + A_HEADS * term
            packed = packed + (pltpu.roll(val, shift, 1) if shift else val)
    wall = _dot(packed.astype(BF16), sp_ref[...])
    for blk in range(tm // A_REGROUP):
        rows = slice(blk * A_REGROUP, (blk + 1) * A_REGROUP)
        y = (wall[rows, :D_MODEL] * o1_ref[0, rows, :].astype(F32)
             + wall[rows, D_MODEL:2 * D_MODEL] * token_order(o2_ref, pm2_ref, blk)
             + wall[rows, 2 * D_MODEL:] * token_order(o3_ref, pm3_ref, blk))
        y_ref[rows, :] = y.astype(BF16)
    return x_ref[...] + _dot(y_ref[...], w_ref[...])


def _a_out_mixer(outs, stats, w):
    col = np.arange(A_N_GROUPS * D_MODEL)
    row = np.arange(LANES)
    spread = jnp.asarray((row[:, None] // (2 * A_HEADS) == col[None, :] // D_MODEL)
                         & (row[:, None] % A_HEADS == (col[None, :] % D_MODEL) // A_HEAD_DIM)
                         & (row[:, None] < 2 * A_HEADS * A_N_GROUPS), dtype=BF16)

    def regroup_matrix(dil):
        t = np.arange(A_REGROUP)
        src = (t % dil) * (A_REGROUP // dil) + t // dil
        return jnp.asarray(np.arange(A_REGROUP)[None, :] == src[:, None], dtype=BF16)

    perms = [regroup_matrix(dil) for _, dil in A_GROUPS[1:]]

    def specs(tm):
        def grouped(arr):
            dil, width = arr.shape[1], arr.shape[3]
            return pl.BlockSpec((None, dil, tm // dil, width), lambda bi, i: (bi, 0, i, 0))

        return ([grouped(a) for a in outs] + [grouped(a) for a in stats]
                + [_resident(w), _resident(spread)] + [_resident(p) for p in perms])

    def scratch(tm):
        return [pltpu.VMEM((tm, LANES), F32), pltpu.VMEM((tm, LANES), F32), pltpu.VMEM((tm, D_MODEL), BF16)]

    return (_a_out_stage, [*outs, *stats, _arr(w), spread, *perms], specs, scratch, "a_out_xattn")


def _rope_tables(seq_len):
    inv = ROPE_THETA ** (-jnp.arange(0, A_HEAD_DIM, 2, dtype=F32) / A_HEAD_DIM)
    ang = jnp.arange(seq_len, dtype=F32)[:, None] * inv[None, :]
    cos, sin = jnp.cos(ang), jnp.sin(ang)
    reps = LANES // A_HEAD_DIM
    return (jnp.concatenate([cos, cos] * reps, axis=1),
            jnp.concatenate([-sin, sin] * reps, axis=1))


def _mixer_a(x, g, w_in, w_out):
    b, s, _ = x.shape
    cos_t, sin_t = _rope_tables(s)
    outs, stats = [], []
    for qkv in _a_in(x, g, w_in, cos_t, sin_t):
        o, l = _attn_group(qkv)
        outs.append(o)
        stats.append(l)
    return _a_out_mixer(outs, stats, w_out)


HALO = 8
HY_COL_CHUNK = 256


def _hy_in_body(xp_ref, x_ref, xn_ref, g_ref, w_ref, cw_ref, cb_ref, x0_ref, vv_ref, u_ref, *, n_tiles):
    tm = x_ref.shape[0]
    i = pl.program_id(1)
    xe = jnp.concatenate([xp_ref[...], x_ref[...], xn_ref[...]], axis=0)
    xn = _rms(xe, g_ref[...]).astype(BF16)
    row = lax.broadcasted_iota(jnp.int32, (tm + 2 * HALO, 1), 0)
    inside = ((row >= HALO) | (i > 0)) & ((row < tm + HALO) | (i < n_tiles - 1))
    cw = HY_COL_CHUNK
    for c in range(D_MODEL // cw):
        parts = []
        for k in range(3):
            cols = slice(k * D_MODEL + c * cw, k * D_MODEL + (c + 1) * cw)
            u_ref[k] = jnp.where(inside, _dot(xn, w_ref[:, cols]), 0.0)
            parts.append(u_ref[k, pl.ds(HALO - 1, tm), :] * cw_ref[0:1, cols]
                         + u_ref[k, pl.ds(HALO, tm), :] * cw_ref[1:2, cols]
                         + u_ref[k, pl.ds(HALO + 1, tm), :] * cw_ref[2:3, cols]
                         + cb_ref[:, cols])
        x0_ref[:, c * cw:(c + 1) * cw] = parts[0].astype(BF16)
        vv_ref[:, c * cw:(c + 1) * cw] = (parts[2] * parts[1]).astype(BF16)


def _hy_in(x, g, w, conv_w, conv_b):
    b, s, _ = x.shape
    tm = TM_HY_IN
    n_tiles = s // tm
    per = tm // HALO
    n_halo = s // HALO
    tok = pl.BlockSpec((None, tm, D_MODEL), lambda bi, i: (bi, i, 0))
    return pl.pallas_call(
        functools.partial(_hy_in_body, n_tiles=n_tiles),
        grid=(b, n_tiles),
        in_specs=[
            pl.BlockSpec((None, HALO, D_MODEL), lambda bi, i: (bi, jnp.maximum(i * per - 1, 0), 0)),
            tok,
            pl.BlockSpec((None, HALO, D_MODEL), lambda bi, i: (bi, jnp.minimum((i + 1) * per, n_halo - 1), 0)),
            pl.BlockSpec((1, D_MODEL), lambda bi, i: (0, 0)),
            _resident(w),
            pl.BlockSpec(conv_w.shape, lambda bi, i: (0, 0)),
            pl.BlockSpec(conv_b.shape, lambda bi, i: (0, 0)),
        ],
        out_specs=[tok, tok],
        out_shape=[jax.ShapeDtypeStruct(x.shape, BF16), jax.ShapeDtypeStruct(x.shape, BF16)],
        scratch_shapes=[pltpu.VMEM((3, tm + 2 * HALO, HY_COL_CHUNK), F32)],
        compiler_params=_params("parallel", "parallel"),
        name="hy_in",
    )(x, x, x, g, _arr(w), conv_w, conv_b)


def _hdot(a, b):
    return jnp.dot(a, b, precision=lax.Precision.HIGHEST, preferred_element_type=F32)


def _hy_filter_body(z_ref, t_ref, a_ref, b_ref, w1_ref, b1_ref, w2_ref, b2_ref, w3_ref, b3_ref,
                    wo_ref, fr_ref, dl_ref, h_ref, sum_ref):
    fr = fr_ref[...]
    hid = jnp.sin(fr * (_hdot(z_ref[...], w1_ref[...]) + b1_ref[...]))
    hid = jnp.sin(fr * (_hdot(hid, w2_ref[...]) + b2_ref[...]))
    hid = jnp.sin(fr * (_hdot(hid, w3_ref[...]) + b3_ref[...]))
    decay = jnp.exp(-t_ref[...] * dl_ref[...])
    hid = hid.astype(BF16)
    h_fwd = _dot(hid, wo_ref[:, :D_MODEL]) * decay
    h_bwd = _dot(hid, wo_ref[:, D_MODEL:]) * decay
    h = a_ref[...] * h_fwd + b_ref[...] * h_bwd
    h_ref[...] = h

    @pl.when(pl.program_id(0) == 0)
    def _():
        sum_ref[...] = jnp.zeros_like(sum_ref)

    sum_ref[...] += jnp.sum(jnp.abs(h), axis=0, keepdims=True)


def _hy_filter(seq_len, f_w1, f_b1, f_w2, f_b2, f_w3, f_b3, f_wout, f_freq):
    n = 2 * seq_len
    src = np.concatenate([np.arange(seq_len), [0], np.arange(seq_len - 1, 0, -1)])
    lag = jnp.asarray(src, dtype=F32)[:, None]
    t2 = lag / (seq_len - 1)
    w = 2.0 * math.pi * lag / seq_len
    f = jnp.linspace(1e-4, HY_BANDS - 1, HY_BANDS, dtype=F32)[None, :]
    z2 = jnp.concatenate([t2, jnp.cos(f * w), -jnp.sin(f * w),
                          jnp.zeros((n, HY_HID_PAD - HY_EMB), F32)], axis=-1)
    pos = np.arange(n)
    use_fwd = (pos < seq_len).astype(np.float32)[:, None]
    use_bwd = ((pos == 0) | (pos > seq_len)).astype(np.float32)[:, None]
    max_decay = math.log(HY_DECAY_TARGET) / HY_DECAY_STRONG_PCT
    min_decay = math.log(HY_DECAY_TARGET) / HY_DECAY_WEAK_PCT
    deltas = jnp.abs(jnp.linspace(min_decay, max_decay, D_MODEL, dtype=F32))[None, :]

    def pad2(m, rows):
        return jnp.pad(m, ((0, rows - m.shape[0]), (0, HY_HID_PAD - m.shape[1])))

    def padv(v):
        return jnp.pad(v, (0, HY_HID_PAD - v.shape[0]))[None, :]

    wo = jnp.pad(f_wout, ((0, HY_HID_PAD - f_wout.shape[0]), (0, 0))).astype(BF16)
    tr = TR_FILTER
    rowblk = lambda width: pl.BlockSpec((tr, width), lambda i: (i, 0))
    full = lambda shape: pl.BlockSpec(shape, lambda i: (0, 0))
    sq = (HY_HID_PAD, HY_HID_PAD)
    vec = (1, HY_HID_PAD)
    return pl.pallas_call(
        _hy_filter_body,
        grid=(n // tr,),
        in_specs=[rowblk(HY_HID_PAD), rowblk(1), rowblk(1), rowblk(1),
                  full(sq), full(vec), full(sq), full(vec), full(sq), full(vec),
                  full(wo.shape), full(vec), full((1, D_MODEL))],
        out_specs=[rowblk(D_MODEL), full((1, D_MODEL))],
        out_shape=[jax.ShapeDtypeStruct((n, D_MODEL), F32), jax.ShapeDtypeStruct((1, D_MODEL), F32)],
        compiler_params=_params("arbitrary"),
        name="hy_filter",
    )(z2, t2, jnp.asarray(use_fwd), jnp.asarray(use_bwd),
      pad2(f_w1, HY_HID_PAD), padv(f_b1), pad2(f_w2, HY_HID_PAD), padv(f_b2),
      pad2(f_w3, HY_HID_PAD), padv(f_b3), wo, padv(f_freq), deltas)


def _fft_split(n):
    n1 = 1 << ((n.bit_length() - 1 + 1) // 2)
    return n1, n // n1


def _fft_tables(n1, n2):
    n = n1 * n2
    h = n1 // 2
    idx = np.arange(n1)
    ang = -2.0 * np.pi * ((idx[:, None] * idx[None, :]) % n1) / n1
    fr, fi = np.cos(ang), np.sin(ang)
    m1_data = np.block([[fr[:, :h], -fi[:, :h]], [fi[:, :h], fr[:, :h]]])
    m1_filt = np.concatenate([fr, fi], axis=0)
    ifr, ifi = fr.T[:h] / n, -fi.T[:h] / n
    m3 = np.block([[ifr, -ifi], [ifi, ifr]])
    k1 = jnp.arange(n1, dtype=jnp.int32)[:, None, None]
    k2 = jnp.arange(n2, dtype=jnp.int32)[None, :, None]
    i2 = jnp.arange(n2, dtype=jnp.int32)[None, None, :]
    phase = (i2 * k1 + n1 * i2 * k2) % n
    ga = (-2.0 * math.pi / n) * phase.astype(F32)
    gr, gi = jnp.cos(ga), jnp.sin(ga)
    g_fwd = jnp.concatenate([jnp.concatenate([gr, -gi], axis=2), jnp.concatenate([gi, gr], axis=2)], axis=1)
    grt, git = jnp.swapaxes(gr, 1, 2), jnp.swapaxes(gi, 1, 2)
    g_inv = jnp.concatenate([jnp.concatenate([grt, git], axis=2), jnp.concatenate([-git, grt], axis=2)], axis=1)
    as_bf = lambda m: jnp.asarray(m, dtype=F32).astype(BF16)
    return as_bf(m1_data), as_bf(m1_filt), as_bf(m3), g_fwd.astype(BF16), g_inv.astype(BF16)


FFT_SUB = 16
FFT_TC = 512
FFT_PITCH = 24


def _dft_outer_body(m_ref, pm_ref, z_ref, o_ref, zs_ref, rb_ref):
    q, sub, tc = z_ref.shape
    r = m_ref.shape[0]
    nlb = tc // LANES
    pad = jnp.zeros((q, FFT_PITCH - sub, LANES), F32)
    for cb in range(nlb):
        rows = z_ref[:, :, cb * LANES:(cb + 1) * LANES].astype(F32)
        zs_ref[cb] = jnp.concatenate([rows, pad], axis=1).reshape(q * FFT_PITCH, LANES)
    hw = tc // 2
    n_groups = r // sub

    def transform(hf, j):
        z = jnp.concatenate([zs_ref[cb, pl.ds(j, q, stride=FFT_PITCH), :]
                             for cb in range(hf * nlb // 2, (hf + 1) * nlb // 2)], axis=1)
        rb_ref[hf, j] = _dot(m_ref[...], z.astype(BF16)).astype(BF16)

    def regroup(hf, g):
        part, i0 = divmod(g * sub, r // 2)
        grouped = rb_ref[hf, :, g * sub:(g + 1) * sub, :].reshape(sub * sub, hw)
        o_ref[part, i0:i0 + sub, :, hf * hw:(hf + 1) * hw] = (
            _dot(pm_ref[...], grouped).reshape(sub, sub, hw).astype(BF16))

    for j in range(sub):
        transform(0, j)
    for step in range(max(sub, n_groups)):
        if step < sub:
            transform(1, step)
        if step < n_groups:
            regroup(0, step)
    for g in range(n_groups):
        regroup(1, g)


def _dft_outer(mat, perm, z5, name):
    p, q, nh, sub, c = z5.shape
    r = mat.shape[0]
    tc = FFT_TC
    return pl.pallas_call(
        _dft_outer_body,
        grid=(p, nh, c // tc),
        in_specs=[pl.BlockSpec(mat.shape, lambda pi, h, ci: (0, 0)),
                  pl.BlockSpec(perm.shape, lambda pi, h, ci: (0, 0)),
                  pl.BlockSpec((None, q, None, sub, tc), lambda pi, h, ci: (pi, 0, h, 0, ci))],
        out_specs=pl.BlockSpec((None, None, 2, r // 2, sub, tc), lambda pi, h, ci: (pi, h, 0, 0, 0, ci)),
        out_shape=jax.ShapeDtypeStruct((p, nh, 2, r // 2, sub, c), BF16),
        scratch_shapes=[pltpu.VMEM((tc // LANES, q * FFT_PITCH, LANES), F32),
                        pltpu.VMEM((2, sub, r, tc // 2), BF16)],
        compiler_params=_params("parallel", "parallel", "parallel"),
        name=name,
    )(mat, perm, z5)


def _idft_outer_body(m_ref, pm_ref, b_ref, o_ref, tb_ref, ys_ref):
    _, n1, sub, tc = b_ref.shape
    q = m_ref.shape[0]
    nlb = tc // LANES
    hw = tc // 2
    n_groups = 2 * n1 // sub

    def regroup(hf, g):
        part, i0 = divmod(g * sub, n1)
        grouped = b_ref[part, i0:i0 + sub, :, hf * hw:(hf + 1) * hw].reshape(sub * sub, hw)
        tb_ref[hf, :, g * sub:(g + 1) * sub, :] = _dot(pm_ref[...], grouped).reshape(sub, sub, hw).astype(BF16)

    def transform(hf, j):
        res = _dot(m_ref[...], tb_ref[hf, j])
        for c in range(nlb // 2):
            ys_ref[hf * nlb // 2 + c, pl.ds(j, q, stride=FFT_PITCH), :] = res[:, c * LANES:(c + 1) * LANES]

    for g in range(n_groups):
        regroup(0, g)
    for step in range(max(sub, n_groups)):
        if step < n_groups:
            regroup(1, step)
        if step < sub:
            transform(0, step)
    for j in range(sub):
        transform(1, j)
    for cb in range(nlb):
        o_ref[:, :, cb * LANES:(cb + 1) * LANES] = ys_ref[cb].reshape(q, FFT_PITCH, LANES)[:, :sub, :]


def _idft_outer(mat, perm, b6):
    p, nh, _, n1, sub, c = b6.shape
    q = mat.shape[0]
    tc = FFT_TC
    return pl.pallas_call(
        _idft_outer_body,
        grid=(p, nh, c // tc),
        in_specs=[pl.BlockSpec(mat.shape, lambda pi, h, ci: (0, 0)),
                  pl.BlockSpec(perm.shape, lambda pi, h, ci: (0, 0)),
                  pl.BlockSpec((None, None, 2, n1, sub, tc), lambda pi, h, ci: (pi, h, 0, 0, 0, ci))],
        out_specs=pl.BlockSpec((None, q, None, sub, tc), lambda pi, h, ci: (pi, 0, h, 0, ci)),
        out_shape=jax.ShapeDtypeStruct((p, q, nh, sub, c), F32),
        scratch_shapes=[pltpu.VMEM((2, sub, 2 * n1, tc // 2), BF16),
                        pltpu.VMEM((tc // LANES, q * FFT_PITCH, LANES), F32)],
        compiler_params=_params("parallel", "parallel", "parallel"),
        name="hy_idft",
    )(mat, perm, b6)


FFT_KB = 4


def _stack_re_im(a_ref, kk):
    nh, _, _, sub, ct = a_ref.shape
    return jnp.concatenate([a_ref[:, 0, kk].reshape(nh * sub, ct), a_ref[:, 1, kk].reshape(nh * sub, ct)], axis=0)


def _spec_filter_body(g_ref, a_ref, sc_ref, h_ref):
    n2 = g_ref.shape[1] // 2
    for kk in range(g_ref.shape[0]):
        spec = _dot(g_ref[kk], _stack_re_im(a_ref, kk)) * sc_ref[...]
        h_ref[kk] = spec.reshape(2, n2, spec.shape[1])


def _spec_filter(g_fwd, a6, scale):
    _, nh, _, n1, sub, c = a6.shape
    n2 = nh * sub
    return pl.pallas_call(
        _spec_filter_body,
        grid=(n1 // FFT_KB,),
        in_specs=[pl.BlockSpec((FFT_KB, 2 * n2, 2 * n2), lambda k: (k, 0, 0)),
                  pl.BlockSpec((None, nh, 2, FFT_KB, sub, c), lambda k: (0, 0, 0, k, 0, 0)),
                  pl.BlockSpec((1, c), lambda k: (0, 0))],
        out_specs=pl.BlockSpec((FFT_KB, 2, n2, c), lambda k: (k, 0, 0, 0)),
        out_shape=jax.ShapeDtypeStruct((n1, 2, n2, c), F32),
        compiler_params=_params("parallel"),
        name="hy_spec_filter",
    )(g_fwd, a6, scale)


def _spec_body(gf_ref, gi_ref, h_ref, a_ref, o_ref):
    nh, _, kb, sub, ct = a_ref.shape
    n2 = nh * sub

    def forward(kk):
        return _dot(gf_ref[kk], _stack_re_im(a_ref, kk))

    spec_next = forward(0)
    for kk in range(kb):
        spec = spec_next
        if kk + 1 < kb:
            spec_next = forward(kk + 1)
        xr, xi = spec[:n2], spec[n2:]
        hr, hi = h_ref[kk, 0], h_ref[kk, 1]
        y = jnp.concatenate([xr * hr - xi * hi, xr * hi + xi * hr], axis=0).astype(BF16)
        back = _dot(gi_ref[kk], y)
        o_ref[:, 0, kk] = back[:n2].reshape(nh, sub, ct).astype(BF16)
        o_ref[:, 1, kk] = back[n2:].reshape(nh, sub, ct).astype(BF16)


def _spec(g_fwd, g_inv, hspec, a6):
    p, nh, _, n1, sub, c = a6.shape
    n2 = nh * sub
    blk = pl.BlockSpec((None, nh, 2, FFT_KB, sub, c), lambda k, pi: (pi, 0, 0, k, 0, 0))
    mat = pl.BlockSpec((FFT_KB, 2 * n2, 2 * n2), lambda k, pi: (k, 0, 0))
    return pl.pallas_call(
        _spec_body,
        grid=(n1 // FFT_KB, p),
        in_specs=[mat, mat, pl.BlockSpec((FFT_KB, 2, n2, c), lambda k, pi: (k, 0, 0, 0)), blk],
        out_specs=blk,
        out_shape=jax.ShapeDtypeStruct(a6.shape, BF16),
        compiler_params=_params("parallel", "parallel"),
        name="hy_spec",
    )(g_fwd, g_inv, hspec, a6)


def _long_conv(vv, h_raw, h_norm):
    b, l, c = vv.shape
    n1, n2 = _fft_split(2 * l)
    nh = n2 // FFT_SUB
    m1_data, m1_filt, m3, g_fwd, g_inv = _fft_tables(n1, n2)
    idx = np.arange(FFT_SUB * FFT_SUB)
    swapped = (idx % FFT_SUB) * FFT_SUB + idx // FFT_SUB
    perm = jnp.asarray(idx[None, :] == swapped[:, None], dtype=BF16)
    a_h = _dft_outer(m1_filt, perm, h_raw.reshape(1, n1, nh, FFT_SUB, c), "hy_dft1_filter")
    hspec = _spec_filter(g_fwd, a_h, 1.0 / h_norm)
    a = _dft_outer(m1_data, perm, vv.reshape(b // 2, n1, nh, FFT_SUB, c), "hy_dft1")
    y = _idft_outer(m3, perm, _spec(g_fwd, g_inv, hspec, a))
    return y.reshape(b, l, c)


def _hy_out_stage(cv_ref, vv_ref, x0_ref, bd_ref, w_ref, x_ref):
    y = ((cv_ref[...] + bd_ref[...] * vv_ref[...].astype(F32)) * x0_ref[...].astype(F32)).astype(BF16)
    return x_ref[...] + _dot(y, w_ref[...])


def _mixer_b(x, g, w_in, conv_w, conv_b, filt, bias_d, w_out):
    _, s, _ = x.shape
    x0, vv = _hy_in(x, g, w_in, conv_w, conv_b)
    h_raw, h_norm = _hy_filter(s, *filt)
    conv = _long_conv(vv, h_raw, h_norm)

    def specs(tm):
        tok = pl.BlockSpec((None, tm, D_MODEL), lambda bi, i: (bi, i, 0))
        return [tok, tok, tok, pl.BlockSpec((1, D_MODEL), lambda bi, i: (0, 0)), _resident(w_out)]

    return (_hy_out_stage, [conv, vv, x0, bias_d, _arr(w_out)], specs, lambda tm: [], "hy_out_xattn")


def _gelu(z):
    return 0.5 * z * (1.0 + lax.erf(z * (2.0 ** -0.5)))


def _sgu_body(x_ref, g_ref, win_ref, lng_ref, lnb_ref, ws_ref, bs_ref, wout_ref, o_ref, y_ref, z_ref):
    tm = x_ref.shape[0]
    x = x_ref[...]
    xn = _rms(x, g_ref[...]).astype(BF16)
    for c in range(2 * D_MODEL // FF_CHUNK):
        cols = slice(c * FF_CHUNK, (c + 1) * FF_CHUNK)
        z_ref[:, cols] = _gelu(_dot(xn, win_ref[:, cols]))
    zv = z_ref[:, D_MODEL:]
    zc = zv - jnp.mean(zv, axis=-1, keepdims=True)
    zv = zc * lax.rsqrt(jnp.mean(zc * zc, axis=-1, keepdims=True) + EPS) * lng_ref[...] + lnb_ref[...]
    zvb = zv.astype(BF16)
    n_chunks = tm // C_CHUNK
    for h in range(C_GROUPS):
        cols = slice(h * LANES, (h + 1) * LANES)
        v_all = jnp.concatenate([zvb[c * C_CHUNK:(c + 1) * C_CHUNK, cols] for c in range(n_chunks)], axis=1)
        sv = _dot(ws_ref[h], v_all) + bs_ref[:, h:h + 1]
        for c in range(n_chunks):
            rows = slice(c * C_CHUNK, (c + 1) * C_CHUNK)
            y_ref[rows, cols] = (z_ref[rows, cols] * sv[:, c * LANES:(c + 1) * LANES]).astype(BF16)
    o_ref[...] = x + _dot(y_ref[...], wout_ref[...])


def _mixer_c(x, g, w_in, ln_g, ln_b, w_s, b_s_t, w_out):
    b, s, _ = x.shape
    t = b * s
    tm = TM_SGU
    tok = pl.BlockSpec((tm, D_MODEL), lambda i: (i, 0))
    row = pl.BlockSpec((1, D_MODEL), lambda i: (0, 0))
    return pl.pallas_call(
        _sgu_body,
        grid=(t // tm,),
        in_specs=[tok, row, _resident(w_in), row, row, _resident(w_s),
                  pl.BlockSpec(b_s_t.shape, lambda i: (0, 0)), _resident(w_out)],
        out_specs=tok,
        out_shape=jax.ShapeDtypeStruct((t, D_MODEL), F32),
        scratch_shapes=[pltpu.VMEM((tm, D_MODEL), BF16), pltpu.VMEM((tm, 2 * D_MODEL), F32)],
        compiler_params=_params("parallel"),
        name="sgu",
    )(x.reshape(t, D_MODEL), g, _arr(w_in), ln_g, ln_b, _arr(w_s), b_s_t, _arr(w_out)).reshape(x.shape)


def _trunk(x, kvs, boff, w):
    b, s, _ = x.shape
    t = b * s
    for i in range(DEPTH):
        kind, j = i % N_MIXERS, i // N_MIXERS
        g = w["g_mix"][i][None, :]
        mixer_out = None
        if kind == 0:
            mixer_out = _mixer_a(x, g, w["a_w_in"][j], w["a_w_out"][j])
        elif kind == 1:
            filt = tuple(w[k][j] for k in ("b_f_w1", "b_f_b1", "b_f_w2", "b_f_b2", "b_f_w3", "b_f_b3",
                                          "b_f_wout", "b_f_freq"))
            mixer_out = _mixer_b(x, g, w["b_w_in"][j], w["b_conv_w"][j], w["b_conv_b"][j][None, :], filt,
                                 w["b_bias_d"][j][None, :], w["b_w_out"][j])
        else:
            x = _mixer_c(x, g, w["c_w_in"][j], w["c_ln_g"][j][None, :], w["c_ln_b"][j][None, :],
                         w["c_w_s"][j], w["c_b_s"][j].T, w["c_w_out"][j])
        x = _xattn(x, w["g_cross"][i][None, :], w["x_w_q"][i], kvs[i], boff, w["x_w_o"][i], mixer_out)
        g_final = w["g_final"][None, :] if i == DEPTH - 1 else None
        x = _ffn(x.reshape(t, D_MODEL), w["g_ffn"][i][None, :], w["f_w_gu"][i], w["f_w_down"][i],
                 g_final).reshape(b, s, D_MODEL)
    return x


_BF16_WEIGHTS = ("a_w_in", "a_w_out", "b_w_in", "b_w_out", "c_w_in", "c_w_s", "c_w_out",
                 "x_w_q", "x_w_kv", "x_w_o", "f_w_gu", "f_w_down")


def kernel(x_prompt, x_sample, mem_prompt, mem_sample, g_mix, g_cross, g_ffn, g_final, a_w_in, a_w_out, b_w_in, b_conv_w, b_conv_b, b_f_w1, b_f_b1, b_f_w2, b_f_b2, b_f_w3, b_f_b3, b_f_wout, b_f_freq, b_bias_d, b_w_out, c_w_in, c_ln_g, c_ln_b, c_w_s, c_b_s, c_w_out, x_w_q, x_w_kv, x_w_o, f_w_gu, f_w_down):
    w = dict(g_mix=g_mix, g_cross=g_cross, g_ffn=g_ffn, g_final=g_final,
             a_w_in=a_w_in, a_w_out=a_w_out, b_w_in=b_w_in, b_conv_w=b_conv_w, b_conv_b=b_conv_b,
             b_f_w1=b_f_w1, b_f_b1=b_f_b1, b_f_w2=b_f_w2, b_f_b2=b_f_b2, b_f_w3=b_f_w3,
             b_f_b3=b_f_b3, b_f_wout=b_f_wout, b_f_freq=b_f_freq, b_bias_d=b_bias_d,
             b_w_out=b_w_out, c_w_in=c_w_in, c_ln_g=c_ln_g, c_ln_b=c_ln_b, c_w_s=c_w_s,
             c_b_s=c_b_s, c_w_out=c_w_out, x_w_q=x_w_q, x_w_kv=x_w_kv, x_w_o=x_w_o,
             f_w_gu=f_w_gu, f_w_down=f_w_down)
    for name in _BF16_WEIGHTS:
        stack = w[name].astype(BF16)
        w[name] = [(stack, j) for j in range(stack.shape[0])]
    nb_prompt = mem_prompt.shape[0]
    mem = jnp.concatenate([mem_prompt, mem_sample], axis=0)
    mem2 = mem.reshape(mem.shape[0] * MEM_LEN, D_MODEL)
    kvs = [_kv_proj(mem2, w["x_w_kv"][i]).reshape(mem.shape[0], MEM_LEN, 2 * D_MODEL) for i in range(DEPTH)]
    y_prompt = _trunk(x_prompt, kvs, 0, w)
    y_sample = _trunk(x_sample, kvs, nb_prompt, w)
    return (y_prompt, y_sample)
```
